```python
import jax, jax.numpy as jnp
from jax import lax
import numpy as np

D_MODEL = 1024
BATCH = 4
SEQ = 4096
DEPTH = 1
DEC_BATCH = 32
DEC_SEQ = 1
PAST_LEN = 16384
PAGE_SIZE = 128

N_HEADS = 8
HEAD_DIM = 64
ATTN_WIDTH = N_HEADS * HEAD_DIM
N_KV = 2
Q_PER_KV = N_HEADS // N_KV
KV_COLS = N_KV * HEAD_DIM
CONV_WIDTH = D_MODEL - ATTN_WIDTH
CONV_K = 31
CMP_BLOCK = 32
SLC_BLOCK = 64
N_SEL = 16
WINDOW = 512
Q_BLOCK = 128
N_EXPERTS = 256
TOP_K = 8
D_EXPERT = 256
D_SHARED = 256
ROUTE_SCALE = 2.5
MOE_BLOCK = 64
EPS = 1e-6
FORCED = 1e4
IN_COLS = ATTN_WIDTH + 6 * KV_COLS + 3 * N_HEADS + 2 * CONV_WIDTH

kernel_name = "nsa_conformer_moe_adaln_step"


def rmsnorm(x, g):
    xf = x.astype(jnp.float32)
    y = xf * lax.rsqrt(jnp.mean(xf * xf, -1, keepdims=True) + EPS)
    return (y * g.astype(jnp.float32)).astype(x.dtype)


def layernorm(x, g, b):
    xf = x.astype(jnp.float32)
    mu = jnp.mean(xf, -1, keepdims=True)
    var = jnp.mean(jnp.square(xf - mu), -1, keepdims=True)
    y = (xf - mu) * lax.rsqrt(var + EPS)
    return (y * g.astype(jnp.float32) + b.astype(jnp.float32)).astype(x.dtype)


def alibi_slopes():
    h = jnp.arange(1, N_HEADS + 1, dtype=jnp.float32)
    return (2.0 ** (-8.0 * h / N_HEADS)).reshape(1, N_KV, Q_PER_KV, 1, 1)


def masked_softmax(s, mask):
    s = jnp.where(mask, s, -1e30)
    m = jnp.max(s, -1, keepdims=True)
    p = jnp.exp(s - m) * mask
    return p / jnp.maximum(jnp.sum(p, -1, keepdims=True), 1e-30)


def compress(k, w):
    b, L = k.shape[:2]
    kb = k.reshape(b, L // CMP_BLOCK, CMP_BLOCK, N_KV, HEAD_DIM)
    return jnp.einsum('bclgd,lde->bcge', kb, w)


def gather_blocks(pool_k, pool_v, phys):
    gi = jnp.arange(N_KV)[None, :, None, None]
    return pool_k[phys, :, gi], pool_v[phys, :, gi]


def nsa_attend(q, gates, q_pos, kc, vc, gather_sel, kw, vw, win_pos):
    b, tq = q.shape[:2]
    scale = HEAD_DIM ** -0.5
    slopes = alibi_slopes()
    qg = q.reshape(b, tq, N_KV, Q_PER_KV, HEAD_DIM).transpose(0, 2, 3, 1, 4)
    nc = kc.shape[1]
    cmp_pos = jnp.arange(nc) * CMP_BLOCK + (CMP_BLOCK - 1)
    dist_c = (q_pos[:, None] - cmp_pos[None, :]).astype(jnp.float32)
    s_c = jnp.einsum('bgrqd,bcgd->bgrqc', qg, kc).astype(jnp.float32) * scale - slopes * dist_c
    p_c = masked_softmax(s_c, dist_c >= 0)
    o_cmp = jnp.einsum('bgrqc,bcgd->bgrqd', p_c, vc.astype(jnp.float32))
    ratio = SLC_BLOCK // CMP_BLOCK
    ns = nc // ratio
    n_sel = min(N_SEL, ns)
    imp = p_c.sum(2).reshape(b, N_KV, tq, ns, ratio).sum(-1)
    blk = jnp.arange(ns)[None, :]
    cur = (q_pos // SLC_BLOCK)[:, None]
    valid = blk * SLC_BLOCK <= q_pos[:, None]
    forced = (blk == 0) | (blk == cur) | (blk == cur - 1)
    score = jnp.where(valid & forced, FORCED, jnp.where(valid, imp, -1.0))
    top_s, idx = lax.top_k(score, n_sel)
    k_sel, v_sel = gather_sel(idx)
    key_pos = idx[..., None] * SLC_BLOCK + jnp.arange(SLC_BLOCK)
    dist_s = (q_pos[:, None, None] - key_pos).astype(jnp.float32)
    mask_s = (top_s >= 0)[..., None] & (dist_s >= 0)
    s_s = jnp.einsum('bgrqd,bgqnld->bgrqnl', qg, k_sel).astype(jnp.float32) * scale
    s_s = s_s - slopes[..., None] * dist_s[:, :, None]
    p_s = masked_softmax(s_s.reshape(b, N_KV, Q_PER_KV, tq, -1), mask_s[:, :, None].reshape(b, N_KV, 1, tq, -1))
    o_slc = jnp.einsum('bgrqm,bgqmd->bgrqd', p_s, v_sel.reshape(b, N_KV, tq, -1, HEAD_DIM).astype(jnp.float32))
    dist_w = (q_pos[:, None] - win_pos[None, :]).astype(jnp.float32)
    mask_w = (dist_w >= 0) & (dist_w <= WINDOW) & (win_pos >= 0)[None, :]
    s_w = jnp.einsum('bgrqd,blgd->bgrql', qg, kw).astype(jnp.float32) * scale - slopes * dist_w
    p_w = masked_softmax(s_w, mask_w)
    o_win = jnp.einsum('bgrql,blgd->bgrqd', p_w, vw.astype(jnp.float32))
    g = gates.astype(jnp.float32).reshape(b, tq, N_KV, Q_PER_KV, 3).transpose(0, 2, 3, 1, 4)[..., None, :]
    o = o_cmp * g[..., 0] + o_slc * g[..., 1] + o_win * g[..., 2]
    return o.transpose(0, 3, 1, 2, 4).reshape(b, tq, ATTN_WIDTH).astype(q.dtype)


def nsa_prompt(q, kv, gates, w_ck, w_cv):
    kc_r, vc_r, ks, vs, kw, vw = kv
    b, t = q.shape[:2]
    kc = compress(kc_r, w_ck)
    vc = compress(vc_r, w_cv)
    ns = t // SLC_BLOCK
    pool_k = ks.reshape(b * ns, SLC_BLOCK, N_KV, HEAD_DIM)
    pool_v = vs.reshape(b * ns, SLC_BLOCK, N_KV, HEAD_DIM)
    base = (jnp.arange(b) * ns)[:, None, None, None]
    gather = lambda idx: gather_blocks(pool_k, pool_v, base + idx)
    pad = ((0, 0), (WINDOW, 0), (0, 0), (0, 0))
    kw_pad = jnp.pad(kw, pad)
    vw_pad = jnp.pad(vw, pad)
    span = Q_BLOCK + WINDOW
    n_qb = t // Q_BLOCK

    def one_block(args):
        i, qb, gb = args
        s0 = i * Q_BLOCK
        kwb = lax.dynamic_slice_in_dim(kw_pad, s0, span, axis=1)
        vwb = lax.dynamic_slice_in_dim(vw_pad, s0, span, axis=1)
        return nsa_attend(qb, gb, s0 + jnp.arange(Q_BLOCK), kc, vc, gather, kwb, vwb,
                          s0 - WINDOW + jnp.arange(span))

    qs = q.reshape(b, n_qb, Q_BLOCK, N_HEADS, HEAD_DIM).swapaxes(0, 1)
    gs = gates.reshape(b, n_qb, Q_BLOCK, N_HEADS, 3).swapaxes(0, 1)
    o = lax.map(one_block, (jnp.arange(n_qb), qs, gs))
    return o.swapaxes(0, 1).reshape(b, t, ATTN_WIDTH)


def nsa_sample(q, kv, gates, ck, cv, sk, sv, wk, wv, page_table, w_ck, w_cv):
    kc_new, vc_new, ks_new, vs_new, kw_new, vw_new = kv
    b, s = q.shape[:2]
    past = page_table.shape[1] * PAGE_SIZE
    pad = (-(past + s)) % SLC_BLOCK

    def full_rows(cache, new):
        rows = cache[page_table].reshape(b, past, N_KV, HEAD_DIM)
        return jnp.concatenate([rows, new, jnp.zeros((b, pad, N_KV, HEAD_DIM), new.dtype)], 1)

    kc = compress(full_rows(ck, kc_new), w_ck)
    vc = compress(full_rows(cv, vc_new), w_cv)
    sub = PAGE_SIZE // SLC_BLOCK
    ns_past = past // SLC_BLOCK
    ns_new = (s + pad) // SLC_BLOCK
    pool_k = sk.reshape(-1, SLC_BLOCK, N_KV, HEAD_DIM)
    pool_v = sv.reshape(-1, SLC_BLOCK, N_KV, HEAD_DIM)
    pad4 = ((0, 0), (0, pad), (0, 0), (0, 0))
    new_k = jnp.pad(ks_new, pad4).reshape(b * ns_new, SLC_BLOCK, N_KV, HEAD_DIM)
    new_v = jnp.pad(vs_new, pad4).reshape(b * ns_new, SLC_BLOCK, N_KV, HEAD_DIM)
    bidx = jnp.arange(b)[:, None, None, None]

    def gather(idx):
        in_past = (idx < ns_past)[..., None, None]
        jp = jnp.minimum(idx, ns_past - 1)
        phys = page_table[bidx, jp // sub] * sub + jp % sub
        jn = bidx * ns_new + jnp.clip(idx - ns_past, 0, ns_new - 1)
        kp, vp = gather_blocks(pool_k, pool_v, phys)
        kn, vn = gather_blocks(new_k, new_v, jn)
        return jnp.where(in_past, kp, kn), jnp.where(in_past, vp, vn)

    w_buf = wk.shape[1]
    kw_all = jnp.concatenate([wk, kw_new], 1)
    vw_all = jnp.concatenate([wv, vw_new], 1)
    win_pos = past - w_buf + jnp.arange(w_buf + s)
    o = nsa_attend(q, gates, past + jnp.arange(s), kc, vc, gather, kw_all, vw_all, win_pos)
    return o, kw_all[:, -w_buf:], vw_all[:, -w_buf:]


def conv_module(glu, prev, w_dw, b_dw, ln_g, ln_b):
    a, gt = jnp.split(glu, 2, axis=-1)
    u = a * jax.nn.sigmoid(gt)
    up = jnp.concatenate([prev, u], 1)
    y = lax.conv_general_dilated(up, w_dw[:, None, :], (1,), 'VALID',
                                 dimension_numbers=('NWC', 'WIO', 'NWC'),
                                 feature_group_count=CONV_WIDTH) + b_dw
    return jax.nn.silu(layernorm(y, ln_g, ln_b)), up[:, -(CONV_K - 1):]


def project(h, w_in):
    z = h @ w_in
    b, t = h.shape[:2]
    q = z[..., :ATTN_WIDTH].reshape(b, t, N_HEADS, HEAD_DIM)
    o = ATTN_WIDTH
    kv = []
    for _ in range(6):
        kv.append(z[..., o:o + KV_COLS].reshape(b, t, N_KV, HEAD_DIM))
        o += KV_COLS
    gates = jax.nn.sigmoid(z[..., o:o + 3 * N_HEADS].astype(jnp.float32)).reshape(b, t, N_HEADS, 3)
    glu = z[..., o + 3 * N_HEADS:]
    return q, kv, gates.astype(h.dtype), glu


def merge(o_attn, o_conv, g_oa, g_oc, w_out):
    return jnp.concatenate([rmsnorm(o_attn, g_oa), rmsnorm(o_conv, g_oc)], -1) @ w_out


def swiglu(x, wg, wu, wd):
    return (jax.nn.silu(x @ wg) * (x @ wu)) @ wd


def moe(h, router_w, router_b, w_g, w_u, w_d, ws_g, ws_u, ws_d):
    shp = h.shape
    x = h.reshape(-1, D_MODEL)
    n = x.shape[0]
    aff = jax.nn.sigmoid((x @ router_w).astype(jnp.float32))
    _, idx = lax.top_k(aff + router_b.astype(jnp.float32), TOP_K)
    wts = jnp.take_along_axis(aff, idx, -1)
    wts = ROUTE_SCALE * wts / jnp.sum(wts, -1, keepdims=True)
    flat_e = idx.reshape(-1)
    flat_tok = jnp.repeat(jnp.arange(n, dtype=jnp.int32), TOP_K)
    flat_w = wts.reshape(-1)
    order = jnp.argsort(flat_e)
    e_sorted = flat_e[order]
    counts = jnp.bincount(flat_e, length=N_EXPERTS)
    padded = (counts + MOE_BLOCK - 1) // MOE_BLOCK * MOE_BLOCK
    pad_end = jnp.cumsum(padded)
    pad_start = pad_end - padded
    start = jnp.cumsum(counts) - counts
    dest = pad_start[e_sorted] + jnp.arange(n * TOP_K) - start[e_sorted]
    n_blocks = -(-(n * TOP_K) // MOE_BLOCK) + N_EXPERTS
    rows = n_blocks * MOE_BLOCK
    row_tok = jnp.zeros((rows,), jnp.int32).at[dest].set(flat_tok[order])
    row_w = jnp.zeros((rows,), jnp.float32).at[dest].set(flat_w[order])
    blk_e = jnp.minimum(jnp.searchsorted(pad_end, jnp.arange(n_blocks) * MOE_BLOCK, side='right'), N_EXPERTS - 1)
    xb = x[row_tok].reshape(n_blocks, MOE_BLOCK, D_MODEL)
    yb = lax.map(lambda a: swiglu(a[0], w_g[a[1]], w_u[a[1]], w_d[a[1]]), (xb, blk_e))
    contrib = (yb.reshape(rows, D_MODEL).astype(jnp.float32) * row_w[:, None]).astype(x.dtype)
    y = jnp.zeros_like(x).at[row_tok].add(contrib) + swiglu(x, ws_g, ws_u, ws_d)
    return y.reshape(shp)


def modulation(c, w_ada, b_ada):
    return jnp.split((c @ w_ada + b_ada)[:, None, :], 6, axis=-1)


def setup_inputs(seed: int = 0) -> dict:
    key = jax.random.key(seed)
    ks = jax.random.split(key, 40)
    f = jnp.float32
    n_pages = PAST_LEN // PAGE_SIZE
    n_used = DEC_BATCH * n_pages
    n_pool = n_used + max(1, n_used // 4)
    w_buf = min(WINDOW, PAST_LEN)

    def nrm(k, shape, s):
        return jax.random.normal(k, shape, f) * s

    def gain(k, shape):
        return 1.0 + 0.05 * jax.random.normal(k, shape, f)

    page_table = jax.random.permutation(ks[0], n_pool)[:n_used].reshape(DEC_BATCH, n_pages).astype(jnp.int32)
    cshape = (DEPTH, n_pool, PAGE_SIZE, N_KV, HEAD_DIM)
    wshape = (DEPTH, DEC_BATCH, w_buf, N_KV, HEAD_DIM)
    return {
        "x_prompt": nrm(ks[1], (BATCH, SEQ, D_MODEL), 1.0),
        "x_sample": nrm(ks[2], (DEC_BATCH, DEC_SEQ, D_MODEL), 1.0),
        "cache_k_cmp": nrm(ks[3], cshape, 1.0),
        "cache_v_cmp": nrm(ks[4], cshape, 1.0),
        "cache_k_slc": nrm(ks[5], cshape, 1.0),
        "cache_v_slc": nrm(ks[6], cshape, 1.0),
        "state_k_win": nrm(ks[7], wshape, 1.0),
        "state_v_win": nrm(ks[8], wshape, 1.0),
        "state_conv": nrm(ks[9], (DEPTH, DEC_BATCH, CONV_K - 1, CONV_WIDTH), 0.5),
        "page_table": page_table,
        "c_prompt": nrm(ks[10], (BATCH, D_MODEL), 1.0),
        "c_sample": nrm(ks[11], (DEC_BATCH, D_MODEL), 1.0),
        "norm1_g": gain(ks[12], (DEPTH, D_MODEL)),
        "norm2_g": gain(ks[13], (DEPTH, D_MODEL)),
        "w_ada": nrm(ks[14], (DEPTH, D_MODEL, 6 * D_MODEL), 0.1 * D_MODEL ** -0.5),
        "b_ada": nrm(ks[15], (DEPTH, 6 * D_MODEL), 0.01),
        "w_in": nrm(ks[16], (DEPTH, D_MODEL, IN_COLS), D_MODEL ** -0.5),
        "w_cmp_k": nrm(ks[17], (DEPTH, CMP_BLOCK, HEAD_DIM, HEAD_DIM), (CMP_BLOCK * HEAD_DIM) ** -0.5),
        "w_cmp_v": nrm(ks[18], (DEPTH, CMP_BLOCK, HEAD_DIM, HEAD_DIM), (CMP_BLOCK * HEAD_DIM) ** -0.5),
        "w_dw": nrm(ks[19], (DEPTH, CONV_K, CONV_WIDTH), CONV_K ** -0.5),
        "b_dw": nrm(ks[20], (DEPTH, CONV_WIDTH), 0.01),
        "ln_conv_g": gain(ks[21], (DEPTH, CONV_WIDTH)),
        "ln_conv_b": nrm(ks[22], (DEPTH, CONV_WIDTH), 0.01),
        "g_out_attn": gain(ks[23], (DEPTH, ATTN_WIDTH)),
        "g_out_conv": gain(ks[24], (DEPTH, CONV_WIDTH)),
        "w_out": nrm(ks[25], (DEPTH, D_MODEL, D_MODEL), D_MODEL ** -0.5),
        "router_w": nrm(ks[26], (DEPTH, D_MODEL, N_EXPERTS), D_MODEL ** -0.5),
        "router_b": nrm(ks[27], (DEPTH, N_EXPERTS), 0.01),
        "w_exp_gate": nrm(ks[28], (DEPTH, N_EXPERTS, D_MODEL, D_EXPERT), D_MODEL ** -0.5),
        "w_exp_up": nrm(ks[29], (DEPTH, N_EXPERTS, D_MODEL, D_EXPERT), D_MODEL ** -0.5),
        "w_exp_down": nrm(ks[30], (DEPTH, N_EXPERTS, D_EXPERT, D_MODEL), D_EXPERT ** -0.5),
        "w_sh_gate": nrm(ks[31], (DEPTH, D_MODEL, D_SHARED), D_MODEL ** -0.5),
        "w_sh_up": nrm(ks[32], (DEPTH, D_MODEL, D_SHARED), D_MODEL ** -0.5),
        "w_sh_down": nrm(ks[33], (DEPTH, D_SHARED, D_MODEL), D_SHARED ** -0.5),
        "norm_f_g": gain(ks[34], (D_MODEL,)),
    }


def reference(x_prompt, x_sample, cache_k_cmp, cache_v_cmp, cache_k_slc, cache_v_slc,
              state_k_win, state_v_win, state_conv, page_table, c_prompt, c_sample,
              norm1_g, norm2_g, w_ada, b_ada, w_in, w_cmp_k, w_cmp_v, w_dw, b_dw,
              ln_conv_g, ln_conv_b, g_out_attn, g_out_conv, w_out, router_w, router_b,
              w_exp_gate, w_exp_up, w_exp_down, w_sh_gate, w_sh_up, w_sh_down, norm_f_g):
    xp, xs = x_prompt, x_sample
    st_p, st_s = [], []
    for l in range(DEPTH):
        conv_w = (w_dw[l], b_dw[l], ln_conv_g[l], ln_conv_b[l])
        out_w = (g_out_attn[l], g_out_conv[l], w_out[l])
        moe_w = (router_w[l], router_b[l], w_exp_gate[l], w_exp_up[l], w_exp_down[l],
                 w_sh_gate[l], w_sh_up[l], w_sh_down[l])
        m = modulation(c_prompt, w_ada[l], b_ada[l])
        h = rmsnorm(xp, norm1_g[l]) * (1 + m[1]) + m[0]
        q, kv, gates, glu = project(h, w_in[l])
        o_attn = nsa_prompt(q, kv, gates, w_cmp_k[l], w_cmp_v[l])
        zeros_prev = jnp.zeros((xp.shape[0], CONV_K - 1, CONV_WIDTH), glu.dtype)
        o_conv, conv_new = conv_module(glu, zeros_prev, *conv_w)
        xp = xp + m[2] * merge(o_attn, o_conv, *out_w)
        xp = xp + m[5] * moe(rmsnorm(xp, norm2_g[l]) * (1 + m[4]) + m[3], *moe_w)
        win = min(WINDOW, xp.shape[1])
        st_p.append((kv[0], kv[1], kv[2], kv[3], kv[4][:, -win:], kv[5][:, -win:], conv_new))
        m = modulation(c_sample, w_ada[l], b_ada[l])
        h = rmsnorm(xs, norm1_g[l]) * (1 + m[1]) + m[0]
        q, kv, gates, glu = project(h, w_in[l])
        o_attn, kw_buf, vw_buf = nsa_sample(q, kv, gates, cache_k_cmp[l], cache_v_cmp[l],
                                            cache_k_slc[l], cache_v_slc[l], state_k_win[l],
                                            state_v_win[l], page_table, w_cmp_k[l], w_cmp_v[l])
        o_conv, conv_new = conv_module(glu, state_conv[l], *conv_w)
        xs = xs + m[2] * merge(o_attn, o_conv, *out_w)
        xs = xs + m[5] * moe(rmsnorm(xs, norm2_g[l]) * (1 + m[4]) + m[3], *moe_w)
        st_s.append((kv[0], kv[1], kv[2], kv[3], kw_buf, vw_buf, conv_new))
    kc_p, vc_p, ks_p, vs_p, kw_p, vw_p, conv_p = [jnp.stack(a) for a in zip(*st_p)]
    kc_s, vc_s, ks_s, vs_s, kw_s, vw_s, conv_s = [jnp.stack(a) for a in zip(*st_s)]
    y_prompt = rmsnorm(xp, norm_f_g)
    y_sample = rmsnorm(xs, norm_f_g)
    return (y_prompt, y_sample, kc_p, vc_p, ks_p, vs_p, kw_p, vw_p, conv_p,
            kc_s, vc_s, ks_s, vs_s, kw_s, vw_s, conv_s)
```

```python
import functools

import jax
import jax.numpy as jnp
from jax import lax
from jax.experimental import pallas as pl
from jax.experimental.pallas import tpu as pltpu

F32 = jnp.float32
BF16 = jnp.bfloat16
I32 = jnp.int32

N_HEADS = 8
HEAD_DIM = 64
N_KV = 2
Q_PER_KV = N_HEADS // N_KV
ATTN_WIDTH = N_HEADS * HEAD_DIM
KV_COLS = N_KV * HEAD_DIM
CMP_BLOCK = 32
SLC_BLOCK = 64
CMP_PER_SLC = SLC_BLOCK // CMP_BLOCK
N_SEL = 16
WINDOW = 512
TOP_K = 8
ROUTE_SCALE = 2.5
EPS = 1e-6
FORCED = 1e4
NEG = -1e30
ATTN_SCALE = HEAD_DIM ** -0.5

LANES = 128
SUBLANES = 8
VMEM_LIMIT = 56 * 1024 * 1024

ROW_TILE = 256
Q_TILE = 128
KEY_TILE = 512
CMP_ROW_TILE = 512
MOE_ROWS = 256
CONV_HALO = 32


def _cparams(*sem):
    return pltpu.CompilerParams(dimension_semantics=sem, vmem_limit_bytes=VMEM_LIMIT)


def _dot(a, b):
    return jnp.dot(a, b, preferred_element_type=F32)


def _dot_nt(a, b):
    return lax.dot_general(a, b, (((1,), (1,)), ((), ())), preferred_element_type=F32)


def _split2(x):
    hi = x.astype(BF16)
    lo = (x - hi.astype(F32)).astype(BF16)
    return hi, lo


def _dot3(a, b):
    ah, al = _split2(a)
    bh, bl = _split2(b)
    return _dot(ah, bh) + (_dot(ah, bl) + _dot(al, bh))


def _dot3_nt(a, b):
    ah, al = _split2(a)
    bh, bl = _split2(b)
    return _dot_nt(ah, bh) + (_dot_nt(ah, bl) + _dot_nt(al, bh))


def _sigmoid(x):
    return 1.0 / (1.0 + jnp.exp(-x))


def _silu(x):
    return x * _sigmoid(x)


def _rms(x, g):
    return x * lax.rsqrt(jnp.mean(x * x, axis=-1, keepdims=True) + EPS) * g


def _alibi_slope_col(rows, rows_per_head, first_head, n_heads):
    r = lax.broadcasted_iota(I32, (rows, 1), 0) // rows_per_head
    out = jnp.zeros((rows, 1), F32)
    for k in range(n_heads):
        out = jnp.where(r == k, 2.0 ** (-8.0 * (first_head + k + 1) / N_HEADS), out)
    return out


def _softmax_rows(s, mask):
    m = jnp.max(s, axis=1, keepdims=True)
    p = jnp.where(mask, jnp.exp(s - m), 0.0)
    return p / jnp.maximum(jnp.sum(p, axis=1, keepdims=True), 1e-30)


def _modulation_kernel(c_ref, w_ref, b_ref, o_ref):
    o_ref[...] = _dot3(c_ref[...], w_ref[...]) + b_ref[...]


def _modulation(c, w, b):
    m, d = c.shape
    n = w.shape[1]
    tn = 768
    return pl.pallas_call(
        _modulation_kernel,
        grid=(n // tn,),
        in_specs=[pl.BlockSpec((m, d), lambda j: (0, 0)),
                  pl.BlockSpec((d, tn), lambda j: (0, j)),
                  pl.BlockSpec((1, tn), lambda j: (0, j))],
        out_specs=pl.BlockSpec((m, tn), lambda j: (0, j)),
        out_shape=jax.ShapeDtypeStruct((m, n), F32),
        compiler_params=_cparams("arbitrary"),
        name="modulation",
    )(c, w, b)


def _mod_spec(mod, tm, d):
    if mod.shape[1] == 1:
        return pl.BlockSpec((1, 1, d), lambda i, j: (i, 0, 0))
    return pl.BlockSpec((1, tm, d), lambda i, j: (i, j, 0))


def _in_proj_kernel(x_ref, shift_ref, scale_ref, g_ref, w_ref,
                    q_ref, kc_ref, vc_ref, ks_ref, vs_ref, kw_ref, vw_ref, gate_ref, u_ref):
    x = x_ref[0]
    h = _rms(x, g_ref[...]) * (1.0 + scale_ref[0]) + shift_ref[0]
    z = _dot(h.astype(BF16), w_ref[...])
    q_ref[0] = z[:, :ATTN_WIDTH]
    o = ATTN_WIDTH
    for ref in (kc_ref, vc_ref, ks_ref, vs_ref, kw_ref, vw_ref):
        ref[0] = z[:, o:o + KV_COLS]
        o += KV_COLS
    gate_ref[0] = _sigmoid(z[:, o:o + LANES])
    o += LANES
    cw = u_ref.shape[-1]
    u_ref[0] = z[:, o:o + cw] * _sigmoid(z[:, o + cw:o + 2 * cw])


def _in_proj(x, shift, scale, g, w_bf, tm):
    b, t, d = x.shape
    cw = (w_bf.shape[1] - ATTN_WIDTH - 6 * KV_COLS - LANES) // 2
    row = lambda n: pl.BlockSpec((1, tm, n), lambda i, j: (i, j, 0))
    sds = lambda n: jax.ShapeDtypeStruct((b, t, n), F32)
    return pl.pallas_call(
        _in_proj_kernel,
        grid=(b, t // tm),
        in_specs=[row(d), _mod_spec(shift, tm, d), _mod_spec(scale, tm, d),
                  pl.BlockSpec((1, d), lambda i, j: (0, 0)),
                  pl.BlockSpec(w_bf.shape, lambda i, j: (0, 0))],
        out_specs=[row(ATTN_WIDTH)] + [row(KV_COLS)] * 6 + [row(LANES), row(cw)],
        out_shape=[sds(ATTN_WIDTH)] + [sds(KV_COLS)] * 6 + [sds(LANES), sds(cw)],
        compiler_params=_cparams("arbitrary", "arbitrary"),
        name="in_proj",
    )(x, shift, scale, g, w_bf)


def _compress_kernel(k_ref, v_ref, wk_ref, wv_ref, ko_ref, vo_ref):
    ko_ref[...] = _dot3(k_ref[...], wk_ref[...])
    vo_ref[...] = _dot3(v_ref[...], wv_ref[...])


def _compress(k_rows, v_rows, wk, wv):
    r, kdim = k_rows.shape
    tr = min(CMP_ROW_TILE, r)
    assert r % tr == 0
    rows = pl.BlockSpec((tr, kdim), lambda i: (i, 0))
    wspec = pl.BlockSpec((kdim, KV_COLS), lambda i: (0, 0))
    ospec = pl.BlockSpec((tr, KV_COLS), lambda i: (i, 0))
    return pl.pallas_call(
        _compress_kernel,
        grid=(r // tr,),
        in_specs=[rows, rows, wspec, wspec],
        out_specs=[ospec, ospec],
        out_shape=[jax.ShapeDtypeStruct((r, KV_COLS), F32)] * 2,
        compiler_params=_cparams("arbitrary"),
        name="compress",
    )(k_rows, v_rows, wk, wv)


def _compress_weight(w):
    eye = jnp.eye(N_KV, dtype=w.dtype)
    big = jnp.einsum('lde,gh->lgdhe', w, eye)
    return big.reshape(CMP_BLOCK * KV_COLS, KV_COLS)


def _pair_sum(x):
    n = x.shape[-1]
    lane = lax.broadcasted_iota(I32, x.shape, x.ndim - 1)
    nxt = pltpu.roll(x, n - 1, x.ndim - 1)
    prv = pltpu.roll(x, 1, x.ndim - 1)
    return x + jnp.where((lane & 1) == 0, nxt, prv)


def _block_scores(imp, blk, q_pos, n_blocks_total):
    cur = q_pos // SLC_BLOCK
    valid = jnp.logical_and(blk * SLC_BLOCK <= q_pos, blk < n_blocks_total)
    forced = jnp.logical_or(blk == 0, jnp.logical_or(blk == cur, blk == cur - 1))
    return jnp.where(valid, jnp.where(forced, FORCED, imp), -1.0)


def _select_blocks(score, blk, n_sel):
    blk_f = blk.astype(F32)
    sel = jnp.zeros(score.shape, jnp.bool_)
    s = score
    picks = []
    for _ in range(n_sel):
        m = jnp.max(s, axis=1, keepdims=True)
        first = jnp.min(jnp.where(s == m, blk_f, 1e9), axis=1, keepdims=True)
        pick = blk_f == first
        picks.append((first, m))
        sel = jnp.logical_or(sel, pick)
        s = jnp.where(pick, -2.0, s)
    return jnp.logical_and(sel, score >= 0.0), picks


def _prompt_attn_kernel(q_ref, gate_ref, kc_ref, vc_ref, ks_ref, vs_ref, kw_ref, vw_ref, o_ref,
                        *, seq, n_sel):
    i = pl.program_id(1)
    tq = Q_TILE
    rows = Q_PER_KV * tq
    nc = kc_ref.shape[1]
    q_blk = q_ref[0] * ATTN_SCALE
    gates = gate_ref[0]
    q_pos_col = i * tq + lax.broadcasted_iota(I32, (tq, 1), 0)
    q_pos_rows = i * tq + (lax.broadcasted_iota(I32, (rows, 1), 0) & (tq - 1))
    q_pos_rows_f = q_pos_rows.astype(F32)
    span = WINDOW + tq
    w_start = pl.multiple_of(jnp.maximum(i * tq - WINDOW, 0), tq)
    pieces = []
    for g in range(N_KV):
        heads = [g * Q_PER_KV + r for r in range(Q_PER_KV)]
        qg = jnp.concatenate([q_blk[:, h * HEAD_DIM:(h + 1) * HEAD_DIM] for h in heads], axis=0)
        qg_bf = qg.astype(BF16)
        slope = _alibi_slope_col(rows, tq, g * Q_PER_KV, Q_PER_KV)
        gsl = slice(g * HEAD_DIM, (g + 1) * HEAD_DIM)

        kc = kc_ref[0][:, gsl]
        vc = vc_ref[0][:, gsl]
        cmp_pos = lax.broadcasted_iota(I32, (rows, nc), 1) * CMP_BLOCK + (CMP_BLOCK - 1)
        dist_c = (q_pos_rows - cmp_pos).astype(F32)
        mask_c = dist_c >= 0.0
        s_c = jnp.where(mask_c, _dot3_nt(qg, kc) - slope * dist_c, NEG)
        p_c = _softmax_rows(s_c, mask_c)
        o_cmp = _dot(p_c.astype(BF16), vc.astype(BF16))

        imp = p_c[0:tq]
        for r in range(1, Q_PER_KV):
            imp = imp + p_c[r * tq:(r + 1) * tq]
        blk = lax.broadcasted_iota(I32, (tq, nc), 1) >> 1
        score = _block_scores(_pair_sum(imp), blk, q_pos_col, seq // SLC_BLOCK)
        sel, _ = _select_blocks(score, blk, n_sel)
        sel_bf = jnp.where(sel, 1.0, 0.0).astype(BF16)

        n_tiles = ((i + 1) * tq + KEY_TILE - 1) // KEY_TILE

        def slc_step(t, carry, qg_bf=qg_bf, slope=slope, sel_bf=sel_bf, gsl=gsl):
            m_run, l_run, acc = carry
            k0 = pl.multiple_of(t * KEY_TILE, KEY_TILE)
            kt = ks_ref[0, pl.ds(k0, KEY_TILE), :][:, gsl]
            vt = vs_ref[0, pl.ds(k0, KEY_TILE), :][:, gsl]
            key_cmp = (k0 + lax.broadcasted_iota(I32, (nc, KEY_TILE), 1)) // CMP_BLOCK
            expand = jnp.where(key_cmp == lax.broadcasted_iota(I32, (nc, KEY_TILE), 0),
                               1.0, 0.0).astype(BF16)
            chosen = _dot(sel_bf, expand)
            chosen = jnp.concatenate([chosen] * Q_PER_KV, axis=0)
            dist = q_pos_rows_f - (k0 + lax.broadcasted_iota(I32, (rows, KEY_TILE), 1)).astype(F32)
            mask = jnp.logical_and(chosen > 0.5, dist >= 0.0)
            s = jnp.where(mask, _dot_nt(qg_bf, kt.astype(BF16)) - slope * dist, NEG)
            m_new = jnp.maximum(m_run, jnp.max(s, axis=1, keepdims=True))
            alpha = jnp.exp(m_run - m_new)
            p = jnp.where(mask, jnp.exp(s - m_new), 0.0)
            l_new = alpha * l_run + jnp.sum(p, axis=1, keepdims=True)
            acc_new = alpha * acc + _dot(p.astype(BF16), vt.astype(BF16))
            return m_new, l_new, acc_new

        init = (jnp.full((rows, 1), NEG, F32), jnp.zeros((rows, 1), F32),
                jnp.zeros((rows, HEAD_DIM), F32))
        _, l_s, acc_s = lax.fori_loop(0, n_tiles, slc_step, init)
        o_slc = acc_s / jnp.maximum(l_s, 1e-30)

        kw = kw_ref[0, pl.ds(w_start, span), :][:, gsl]
        vw = vw_ref[0, pl.ds(w_start, span), :][:, gsl]
        dist_w = q_pos_rows_f - (w_start + lax.broadcasted_iota(I32, (rows, span), 1)).astype(F32)
        mask_w = jnp.logical_and(dist_w >= 0.0, dist_w <= float(WINDOW))
        s_w = jnp.where(mask_w, _dot_nt(qg_bf, kw.astype(BF16)) - slope * dist_w, NEG)
        p_w = _softmax_rows(s_w, mask_w)
        o_win = _dot(p_w.astype(BF16), vw.astype(BF16))

        for r, h in enumerate(heads):
            rs = slice(r * tq, (r + 1) * tq)
            g0 = gates[:, 3 * h + 0:3 * h + 1]
            g1 = gates[:, 3 * h + 1:3 * h + 2]
            g2 = gates[:, 3 * h + 2:3 * h + 3]
            pieces.append(o_cmp[rs] * g0 + o_slc[rs] * g1 + o_win[rs] * g2)
    o_ref[0] = jnp.concatenate(pieces, axis=1)


def _prompt_attention(q, gates, kc, vc, ks, vs, kw, vw):
    b, t, _ = q.shape
    nc = kc.shape[1]
    assert t % KEY_TILE == 0 and t >= WINDOW + Q_TILE
    n_sel = min(N_SEL, t // SLC_BLOCK)
    qspec = lambda n: pl.BlockSpec((1, Q_TILE, n), lambda bi, i: (bi, i, 0))
    full = lambda r: pl.BlockSpec((1, r, KV_COLS), lambda bi, i: (bi, 0, 0))
    return pl.pallas_call(
        functools.partial(_prompt_attn_kernel, seq=t, n_sel=n_sel),
        grid=(b, t // Q_TILE),
        in_specs=[qspec(ATTN_WIDTH), qspec(LANES), full(nc), full(nc),
                  full(t), full(t), full(t), full(t)],
        out_specs=qspec(ATTN_WIDTH),
        out_shape=jax.ShapeDtypeStruct((b, t, ATTN_WIDTH), F32),
        compiler_params=_cparams("arbitrary", "arbitrary"),
        name="prompt_attention",
    )(q, gates, kc, vc, ks, vs, kw, vw)


def _merge_groups(per_group):
    row = lax.broadcasted_iota(I32, per_group[0].shape, 0) // Q_PER_KV
    out = per_group[0]
    for g in range(1, N_KV):
        out = jnp.where(row == g, per_group[g], out)
    return out


def _group_slice(x, g):
    return x[:, g * HEAD_DIM:(g + 1) * HEAD_DIM]


def _sample_select_kernel(pt_ref, q_ref, kcp_ref, vcp_ref, kcn_ref, vcn_ref, ocmp_ref, sel_ref,
                          kc_buf, vc_buf, *, past, n_new, n_sel):
    b = pl.program_id(0)
    n_pages = kc_buf.shape[0]
    cpp = kc_buf.shape[1] // KV_COLS
    n_past = n_pages * cpp

    def gather(p, _):
        page = pt_ref[b * n_pages + p]
        kc_buf[pl.ds(p, 1), :] = kcp_ref[pl.ds(page, 1), :]
        vc_buf[pl.ds(p, 1), :] = vcp_ref[pl.ds(page, 1), :]
        return 0

    lax.fori_loop(0, n_pages, gather, 0)

    q8 = q_ref[0] * ATTN_SCALE
    slope = _alibi_slope_col(N_HEADS, 1, 0, N_HEADS)
    page_lane = lax.broadcasted_iota(I32, (N_HEADS, n_pages), 1)
    new_lane = lax.broadcasted_iota(I32, (N_HEADS, LANES), 1)

    kc = kc_buf[...]
    vc = vc_buf[...]
    pad = jnp.zeros((LANES - n_new, KV_COLS), F32)
    kcn = jnp.concatenate([kcn_ref[0], pad], axis=0)
    vcn = jnp.concatenate([vcn_ref[0], pad], axis=0)
    cmp_idx = [page_lane * cpp + j for j in range(cpp)] + [n_past + new_lane]
    k_planes = [kc[:, j * KV_COLS:(j + 1) * KV_COLS] for j in range(cpp)] + [kcn]
    v_planes = [vc[:, j * KV_COLS:(j + 1) * KV_COLS] for j in range(cpp)] + [vcn]
    scores, masks = [], []
    for idx, kp in zip(cmp_idx, k_planes):
        dist = (past - (idx * CMP_BLOCK + (CMP_BLOCK - 1))).astype(F32)
        mask = jnp.logical_and(dist >= 0.0, idx < n_past + n_new)
        qk = _merge_groups([_dot3_nt(q8, _group_slice(kp, g)) for g in range(N_KV)])
        scores.append(jnp.where(mask, qk - slope * dist, NEG))
        masks.append(mask)
    m = scores[0].max(axis=1, keepdims=True)
    for s in scores[1:]:
        m = jnp.maximum(m, s.max(axis=1, keepdims=True))
    probs = [jnp.where(mk, jnp.exp(s - m), 0.0) for s, mk in zip(scores, masks)]
    denom = probs[0].sum(axis=1, keepdims=True)
    for p in probs[1:]:
        denom = denom + p.sum(axis=1, keepdims=True)
    denom = jnp.maximum(denom, 1e-30)
    probs = [p / denom for p in probs]

    o_cmp = jnp.zeros((N_HEADS, HEAD_DIM), F32)
    for p, vp in zip(probs, v_planes):
        p_bf = p.astype(BF16)
        o_cmp = o_cmp + _merge_groups(
            [_dot(p_bf, _group_slice(vp, g).astype(BF16)) for g in range(N_KV)])
    ocmp_ref[0] = o_cmp

    def group_rows(p):
        row = lax.broadcasted_iota(I32, p.shape, 0)
        out = jnp.zeros(p.shape, F32)
        for g in range(N_KV):
            tot = jnp.sum(jnp.where(row // Q_PER_KV == g, p, 0.0), axis=0, keepdims=True)
            out = jnp.where(row == g, tot, out)
        return out

    imps = [group_rows(p) for p in probs]
    spp = cpp // CMP_PER_SLC
    blk_imp, blk_idx = [], []
    for j in range(spp):
        tot = imps[j * CMP_PER_SLC]
        for r in range(1, CMP_PER_SLC):
            tot = tot + imps[j * CMP_PER_SLC + r]
        blk_imp.append(tot)
        blk_idx.append(page_lane * spp + j)
    blk_imp.append(_pair_sum(imps[-1]))
    blk_idx.append(n_pages * spp + (new_lane >> 1))
    imp_all = jnp.concatenate(blk_imp, axis=1)
    blk_all = jnp.concatenate(blk_idx, axis=1)
    n_blocks_total = (n_past + n_new) // CMP_PER_SLC
    q_pos = jnp.full((N_HEADS, 1), past, I32)
    score = _block_scores(imp_all, blk_all, q_pos, n_blocks_total)
    _, picks = _select_blocks(score, blk_all, n_sel)
    out_lane = lax.broadcasted_iota(I32, (N_HEADS, LANES), 1)
    out = jnp.full((N_HEADS, LANES), -1, I32)
    for j, (blk_f, top) in enumerate(picks):
        out = jnp.where(out_lane == j, jnp.where(top >= 0.0, blk_f.astype(I32), -1), out)
    sel_ref[0] = out


def _sample_select(page_table, q8, kc_pool, vc_pool, kc_new, vc_new, past):
    b, n_pages = page_table.shape
    n_new = kc_new.shape[1]
    n_sel = min(N_SEL, (past // CMP_BLOCK + n_new) // CMP_PER_SLC)
    width = kc_pool.shape[1]
    whole = pl.BlockSpec(memory_space=pltpu.VMEM)
    per_b = lambda r, n: pl.BlockSpec((1, r, n), lambda i, pt: (i, 0, 0))
    return pl.pallas_call(
        functools.partial(_sample_select_kernel, past=past, n_new=n_new, n_sel=n_sel),
        grid_spec=pltpu.PrefetchScalarGridSpec(
            num_scalar_prefetch=1,
            grid=(b,),
            in_specs=[per_b(N_HEADS, HEAD_DIM), whole, whole,
                      per_b(n_new, KV_COLS), per_b(n_new, KV_COLS)],
            out_specs=[per_b(N_HEADS, HEAD_DIM), per_b(N_HEADS, LANES)],
            scratch_shapes=[pltpu.VMEM((n_pages, width), F32), pltpu.VMEM((n_pages, width), F32)],
        ),
        out_shape=[jax.ShapeDtypeStruct((b, N_HEADS, HEAD_DIM), F32),
                   jax.ShapeDtypeStruct((b, N_HEADS, LANES), I32)],
        compiler_params=_cparams("arbitrary"),
        name="sample_select",
    )(page_table.reshape(-1), q8, kc_pool, vc_pool, kc_new, vc_new)


def _sample_attend_kernel(sel_ref, pt_ref, q_ref, k0_ref, k1_ref, v0_ref, v1_ref, ksn_ref, vsn_ref,
                          kwst_ref, vwst_ref, kwn_ref, vwn_ref, gate_ref, ocmp_ref, o_ref,
                          m_sc, l_sc, acc_sc, *, past, n_sel, ns_past):
    b = pl.program_id(0)
    n = pl.program_id(1)
    q8 = q_ref[0] * ATTN_SCALE
    q8_bf = q8.astype(BF16)
    slope = _alibi_slope_col(N_HEADS, 1, 0, N_HEADS)

    @pl.when(n == 0)
    def _():
        m_sc[...] = jnp.full(m_sc.shape, NEG, F32)
        l_sc[...] = jnp.zeros(l_sc.shape, F32)
        acc_sc[...] = jnp.zeros(acc_sc.shape, F32)

    lane = lax.broadcasted_iota(I32, (N_HEADS, SLC_BLOCK), 1)
    first_row = lax.broadcasted_iota(I32, (SLC_BLOCK, HEAD_DIM), 0) == 0
    qk_g, blk_g, v_g = [], [], []
    for g, (k_ref, v_ref) in enumerate(((k0_ref, v0_ref), (k1_ref, v1_ref))):
        blk = sel_ref[(b * N_KV + g) * n_sel + n]
        in_past = blk < ns_past
        k_new = jnp.where(first_row, _group_slice(ksn_ref[0], g), 0.0)
        v_new = jnp.where(first_row, _group_slice(vsn_ref[0], g), 0.0)
        kk = jnp.where(in_past, _group_slice(k_ref[0], g), k_new)
        vv = jnp.where(in_past, _group_slice(v_ref[0], g), v_new)
        qk_g.append(_dot_nt(q8_bf, kk.astype(BF16)))
        blk_g.append(jnp.full((N_HEADS, SLC_BLOCK), blk, I32))
        v_g.append(vv.astype(BF16))
    blk_rows = _merge_groups(blk_g)
    dist = (past - (blk_rows * SLC_BLOCK + lane)).astype(F32)
    mask = jnp.logical_and(dist >= 0.0, blk_rows >= 0)
    s = jnp.where(mask, _merge_groups(qk_g) - slope * dist, NEG)
    m_old = m_sc[...]
    m_new = jnp.maximum(m_old, jnp.max(s, axis=1, keepdims=True))
    alpha = jnp.exp(m_old - m_new)
    p = jnp.where(mask, jnp.exp(s - m_new), 0.0)
    p_bf = p.astype(BF16)
    l_sc[...] = alpha * l_sc[...] + jnp.sum(p, axis=1, keepdims=True)
    acc_sc[...] = alpha * acc_sc[...] + _merge_groups([_dot(p_bf, v) for v in v_g])
    m_sc[...] = m_new

    @pl.when(n == n_sel - 1)
    def _():
        o_slc = acc_sc[...] / jnp.maximum(l_sc[...], 1e-30)
        w_buf = kwst_ref.shape[1]
        kwst = kwst_ref[0]
        vwst = vwst_ref[0]
        wl = lax.broadcasted_iota(I32, (N_HEADS, w_buf), 1)
        win_pos = past - w_buf + wl
        dist_w = (past - win_pos).astype(F32)
        mask_w = jnp.logical_and(jnp.logical_and(dist_w >= 0.0, dist_w <= float(WINDOW)), win_pos >= 0)
        qk_w = _merge_groups([_dot_nt(q8_bf, _group_slice(kwst, g).astype(BF16)) for g in range(N_KV)])
        s_w = jnp.where(mask_w, qk_w - slope * dist_w, NEG)
        s_n = _merge_groups([jnp.sum(q8 * _group_slice(kwn_ref[0], g), axis=1, keepdims=True)
                             for g in range(N_KV)])
        m_w = jnp.maximum(jnp.max(s_w, axis=1, keepdims=True), s_n)
        p_w = jnp.where(mask_w, jnp.exp(s_w - m_w), 0.0)
        p_n = jnp.exp(s_n - m_w)
        den = jnp.maximum(jnp.sum(p_w, axis=1, keepdims=True) + p_n, 1e-30)
        pw_bf = p_w.astype(BF16)
        o_w = _merge_groups([_dot(pw_bf, _group_slice(vwst, g).astype(BF16)) for g in range(N_KV)])
        v_n = _merge_groups([jnp.broadcast_to(_group_slice(vwn_ref[0], g), (N_HEADS, HEAD_DIM))
                             for g in range(N_KV)])
        o_win = (o_w + p_n * v_n) / den
        gt = gate_ref[0]
        o_ref[0] = ocmp_ref[0] * gt[:, 0:1] + o_slc * gt[:, 1:2] + o_win * gt[:, 2:3]


def _sample_attend(sel, page_table, q8, k_blocks, v_blocks, ks_new, vs_new, kw_state, vw_state,
                   kw_new, vw_new, gates8, o_cmp, past):
    b, n_pages = page_table.shape
    n_sel = sel.shape[-1]
    ns_past = past // SLC_BLOCK
    blocks_per_page = (past // n_pages) // SLC_BLOCK
    w_buf = kw_state.shape[1]

    def cache_map(g):
        def index(i, n, sel_ref, pt_ref):
            blk = jnp.clip(sel_ref[(i * N_KV + g) * n_sel + n], 0, ns_past - 1)
            page = pt_ref[i * n_pages + blk // blocks_per_page]
            return (page * blocks_per_page + blk % blocks_per_page, 0, 0)
        return pl.BlockSpec((1, SLC_BLOCK, KV_COLS), index)

    per_b = lambda r, n_: pl.BlockSpec((1, r, n_), lambda i, n, s, p: (i, 0, 0))
    return pl.pallas_call(
        functools.partial(_sample_attend_kernel, past=past, n_sel=n_sel, ns_past=ns_past),
        grid_spec=pltpu.PrefetchScalarGridSpec(
            num_scalar_prefetch=2,
            grid=(b, n_sel),
            in_specs=[per_b(N_HEADS, HEAD_DIM), cache_map(0), cache_map(1), cache_map(0), cache_map(1),
                      per_b(1, KV_COLS), per_b(1, KV_COLS),
                      per_b(w_buf, KV_COLS), per_b(w_buf, KV_COLS),
                      per_b(1, KV_COLS), per_b(1, KV_COLS),
                      per_b(N_HEADS, 3), per_b(N_HEADS, HEAD_DIM)],
            out_specs=per_b(N_HEADS, HEAD_DIM),
            scratch_shapes=[pltpu.VMEM((N_HEADS, 1), F32), pltpu.VMEM((N_HEADS, 1), F32),
                            pltpu.VMEM((N_HEADS, HEAD_DIM), F32)],
        ),
        out_shape=jax.ShapeDtypeStruct((b, N_HEADS, HEAD_DIM), F32),
        compiler_params=_cparams("arbitrary", "arbitrary"),
        name="sample_attend",
    )(sel.reshape(-1), page_table.reshape(-1), q8, k_blocks, k_blocks, v_blocks, v_blocks,
      ks_new, vs_new, kw_state, vw_state, kw_new, vw_new, gates8, o_cmp)


def _layernorm_silu(y, g, b):
    mu = jnp.mean(y, axis=-1, keepdims=True)
    var = jnp.mean(jnp.square(y - mu), axis=-1, keepdims=True)
    return _silu((y - mu) * lax.rsqrt(var + EPS) * g + b)


def _conv_prompt_kernel(u_ref, w_ref, b_ref, g_ref, beta_ref, o_ref, buf):
    j = pl.program_id(1)
    tt = u_ref.shape[1]
    kw = w_ref.shape[0]

    @pl.when(j == 0)
    def _():
        buf[0:CONV_HALO, :] = jnp.zeros((CONV_HALO, buf.shape[1]), F32)

    buf[CONV_HALO:CONV_HALO + tt, :] = u_ref[0]
    w = w_ref[...]
    acc = jnp.zeros((tt, buf.shape[1]), F32)
    for k in range(kw):
        acc = acc + w[k:k + 1, :] * buf[pl.ds(CONV_HALO - (kw - 1) + k, tt), :]
    o_ref[0] = _layernorm_silu(acc + b_ref[...], g_ref[...], beta_ref[...])
    buf[0:CONV_HALO, :] = buf[tt:tt + CONV_HALO, :]


def _conv_prompt(u, w_dw, b_dw, ln_g, ln_b, tt):
    b, t, c = u.shape
    vec = pl.BlockSpec((1, c), lambda i, j: (0, 0))
    return pl.pallas_call(
        _conv_prompt_kernel,
        grid=(b, t // tt),
        in_specs=[pl.BlockSpec((1, tt, c), lambda i, j: (i, j, 0)),
                  pl.BlockSpec(w_dw.shape, lambda i, j: (0, 0)), vec, vec, vec],
        out_specs=pl.BlockSpec((1, tt, c), lambda i, j: (i, j, 0)),
        out_shape=jax.ShapeDtypeStruct((b, t, c), F32),
        scratch_shapes=[pltpu.VMEM((CONV_HALO + tt, c), F32)],
        compiler_params=_cparams("arbitrary", "arbitrary"),
        name="conv_prompt",
    )(u, w_dw, b_dw, ln_g, ln_b)


def _conv_sample_kernel(up_ref, w_ref, b_ref, g_ref, beta_ref, o_ref):
    y = jnp.sum(up_ref[...] * w_ref[...][None, :, :], axis=1)
    o_ref[...] = _layernorm_silu(y + b_ref[...], g_ref[...], beta_ref[...])


def _conv_sample(up, w_dw, b_dw, ln_g, ln_b):
    b, kw, c = up.shape
    return pl.pallas_call(
        _conv_sample_kernel,
        out_shape=jax.ShapeDtypeStruct((b, c), F32),
        name="conv_sample",
    )(up, w_dw, b_dw, ln_g, ln_b)


def _merge_router_kernel(oa_ref, oc_ref, x_ref, gate_ref, shift_ref, scale_ref, goa_ref, goc_ref,
                         wout_ref, g2_ref, rwh_ref, rwl_ref, rb_ref, cnt_in_ref,
                         x1_ref, h2_ref, eidx_ref, wts_ref, rank_ref, cnt_out_ref, run):
    first = jnp.logical_and(pl.program_id(0) == 0, pl.program_id(1) == 0)

    @pl.when(first)
    def _():
        run[...] = cnt_in_ref[...]

    a = _rms(oa_ref[0], goa_ref[...])
    c = _rms(oc_ref[0], goc_ref[...])
    cat = jnp.concatenate([a, c], axis=1).astype(BF16)
    x1 = x_ref[0] + gate_ref[0] * _dot(cat, wout_ref[...])
    x1_ref[0] = x1
    h2 = _rms(x1, g2_ref[...]) * (1.0 + scale_ref[0]) + shift_ref[0]
    h2_ref[0] = h2

    hh, hl = _split2(h2)
    logits = _dot(hh, rwh_ref[...]) + (_dot(hh, rwl_ref[...]) + _dot(hl, rwh_ref[...]))
    aff = _sigmoid(logits)
    tm, n_exp = aff.shape
    lane_f = lax.broadcasted_iota(I32, (tm, n_exp), 1).astype(F32)
    s = aff + rb_ref[...]
    hot = jnp.zeros((tm, n_exp), jnp.bool_)
    experts, weights = [], []
    for _ in range(TOP_K):
        m = jnp.max(s, axis=1, keepdims=True)
        e = jnp.min(jnp.where(s == m, lane_f, 1e9), axis=1, keepdims=True)
        pick = lane_f == e
        experts.append(e)
        weights.append(jnp.sum(jnp.where(pick, aff, 0.0), axis=1, keepdims=True))
        hot = jnp.logical_or(hot, pick)
        s = jnp.where(pick, NEG, s)
    total = weights[0]
    for w in weights[1:]:
        total = total + w

    hot_f = jnp.where(hot, 1.0, 0.0)
    r_i = lax.broadcasted_iota(I32, (tm, tm), 0)
    c_i = lax.broadcasted_iota(I32, (tm, tm), 1)
    lower = jnp.where(c_i < r_i, 1.0, 0.0).astype(BF16)
    before = _dot(lower, hot_f.astype(BF16)) + run[...]
    out_lane = lax.broadcasted_iota(I32, (tm, LANES), 1)
    e_out = jnp.zeros((tm, LANES), I32)
    w_out = jnp.zeros((tm, LANES), F32)
    r_out = jnp.zeros((tm, LANES), I32)
    for k in range(TOP_K):
        rank = jnp.sum(jnp.where(lane_f == experts[k], before, 0.0), axis=1, keepdims=True)
        e_out = jnp.where(out_lane == k, experts[k].astype(I32), e_out)
        w_out = jnp.where(out_lane == k, ROUTE_SCALE * weights[k] / total, w_out)
        r_out = jnp.where(out_lane == k, rank.astype(I32), r_out)
    eidx_ref[0] = e_out
    wts_ref[0] = w_out
    rank_ref[0] = r_out
    run[...] = run[...] + jnp.sum(hot_f, axis=0, keepdims=True)
    cnt_out_ref[...] = run[...]


def _merge_router(o_attn, o_conv, x, gate, shift, scale, goa, goc, wout_bf, g2, rw_hi, rw_lo, rb,
                  cnt_in, tm):
    b, t, d = x.shape
    n_exp = rw_hi.shape[1]
    row = lambda n: pl.BlockSpec((1, tm, n), lambda i, j: (i, j, 0))
    const = lambda shape: pl.BlockSpec(shape, lambda i, j: (0,) * len(shape))
    sds = lambda n, dt: jax.ShapeDtypeStruct((b, t, n), dt)
    return pl.pallas_call(
        _merge_router_kernel,
        grid=(b, t // tm),
        in_specs=[row(o_attn.shape[-1]), row(o_conv.shape[-1]), row(d),
                  _mod_spec(gate, tm, d), _mod_spec(shift, tm, d), _mod_spec(scale, tm, d),
                  const(goa.shape), const(goc.shape), const(wout_bf.shape), const(g2.shape),
                  const(rw_hi.shape), const(rw_lo.shape), const(rb.shape), const(cnt_in.shape)],
        out_specs=[row(d), row(d), row(LANES), row(LANES), row(LANES), const((1, n_exp))],
        out_shape=[sds(d, F32), sds(d, F32), sds(LANES, I32), sds(LANES, F32), sds(LANES, I32),
                   jax.ShapeDtypeStruct((1, n_exp), F32)],
        scratch_shapes=[pltpu.VMEM((1, n_exp), F32)],
        compiler_params=_cparams("arbitrary", "arbitrary"),
        name="merge_router",
    )(o_attn, o_conv, x, gate, shift, scale, goa, goc, wout_bf, g2, rw_hi, rw_lo, rb, cnt_in)


def _row_copy(src_hbm, dst, src_row, dst_row, sem, chunks):
    return pltpu.make_async_copy(src_hbm.at[pl.ds(src_row * chunks, chunks)],
                                 dst.at[pl.ds(dst_row * chunks, chunks)], sem)


def _dispatch_kernel(nact_ref, tok_ref, h_hbm, o_ref, sem, *, rows, chunks):
    j = pl.program_id(0)

    @pl.when(j < nact_ref[0])
    def _():
        def issue(r, _):
            _row_copy(h_hbm, o_ref, tok_ref[0, 0, r], r, sem, chunks).start()
            return 0

        def drain(r, _):
            _row_copy(h_hbm, o_ref, 0, r, sem, chunks).wait()
            return 0

        lax.fori_loop(0, rows, issue, 0)
        lax.fori_loop(0, rows, drain, 0)

    @pl.when(j >= nact_ref[0])
    def _():
        o_ref[...] = jnp.zeros(o_ref.shape, F32)


def _dispatch(n_active, row_tok, h_rows, rows, chunks):
    n_blocks = row_tok.shape[0]
    blk = lambda j, na: (jnp.minimum(j, na[0] - 1), 0, 0)
    return pl.pallas_call(
        functools.partial(_dispatch_kernel, rows=rows, chunks=chunks),
        grid_spec=pltpu.PrefetchScalarGridSpec(
            num_scalar_prefetch=1,
            grid=(n_blocks,),
            in_specs=[pl.BlockSpec((1, 1, rows), blk, memory_space=pltpu.SMEM),
                      pl.BlockSpec(memory_space=pl.ANY)],
            out_specs=pl.BlockSpec((rows * chunks, LANES), lambda j, na: (j, 0)),
            scratch_shapes=[pltpu.SemaphoreType.DMA(())],
        ),
        out_shape=jax.ShapeDtypeStruct((n_blocks * rows * chunks, LANES), F32),
        compiler_params=_cparams("arbitrary"),
        name="moe_dispatch",
    )(n_active, row_tok, h_rows)


def _expert_kernel(be_ref, nact_ref, x_ref, wg_ref, wu_ref, wd_ref, y_ref, *, rows, chunks):
    j = pl.program_id(0)

    @pl.when(j < nact_ref[0])
    def _():
        f = wg_ref.shape[2]
        gate = jnp.zeros((rows, f), F32)
        up = jnp.zeros((rows, f), F32)
        for c in range(chunks):
            xc = x_ref[pl.ds(c, rows, stride=chunks), :].astype(BF16)
            cs = slice(c * LANES, (c + 1) * LANES)
            gate = gate + _dot(xc, wg_ref[0, cs, :].astype(BF16))
            up = up + _dot(xc, wu_ref[0, cs, :].astype(BF16))
        h = (_silu(gate) * up).astype(BF16)
        y = _dot(h, wd_ref[0].astype(BF16))
        for c in range(chunks):
            y_ref[pl.ds(c, rows, stride=chunks), :] = y[:, c * LANES:(c + 1) * LANES]

    @pl.when(j >= nact_ref[0])
    def _():
        y_ref[...] = jnp.zeros(y_ref.shape, F32)


def _experts(blk_expert, n_active, xs, wg, wu, wd, rows, chunks):
    n_blocks = blk_expert.shape[0]
    d, f = wg.shape[1], wg.shape[2]
    last = lambda j, na: jnp.minimum(j, na[0] - 1)
    xspec = pl.BlockSpec((rows * chunks, LANES), lambda j, be, na: (last(j, na), 0))
    wspec = lambda s: pl.BlockSpec((1,) + s, lambda j, be, na: (be[last(j, na)], 0, 0))
    return pl.pallas_call(
        functools.partial(_expert_kernel, rows=rows, chunks=chunks),
        grid_spec=pltpu.PrefetchScalarGridSpec(
            num_scalar_prefetch=2,
            grid=(n_blocks,),
            in_specs=[xspec, wspec((d, f)), wspec((d, f)), wspec((f, d))],
            out_specs=pl.BlockSpec((rows * chunks, LANES), lambda j, be, na: (j, 0)),
        ),
        out_shape=jax.ShapeDtypeStruct(xs.shape, F32),
        compiler_params=_cparams("arbitrary"),
        name="moe_experts",
    )(blk_expert, n_active, xs, wg, wu, wd)


def _combine_kernel(dest_ref, w_ref, x1_ref, h2_ref, gate_ref, wsg_ref, wsu_ref, wsd_ref, gf_ref,
                    ys_hbm, o_ref, buf, sem, *, chunks):
    tm = x1_ref.shape[1]

    def issue(r, _):
        for k in range(TOP_K):
            _row_copy(ys_hbm, buf.at[k], dest_ref[0, 0, r * TOP_K + k], r, sem, chunks).start()
        return 0

    def drain(r, _):
        for k in range(TOP_K):
            _row_copy(ys_hbm, buf.at[k], 0, r, sem, chunks).wait()
        return 0

    lax.fori_loop(0, tm, issue, 0)
    h_bf = h2_ref[0].astype(BF16)
    hid = (_silu(_dot(h_bf, wsg_ref[...])) * _dot(h_bf, wsu_ref[...])).astype(BF16)
    shared = _dot(hid, wsd_ref[...])
    lax.fori_loop(0, tm, drain, 0)

    w = w_ref[0]
    cols = []
    for c in range(chunks):
        tot = jnp.zeros((tm, LANES), F32)
        for k in range(TOP_K):
            tot = tot + buf[k, pl.ds(c, tm, stride=chunks), :] * w[:, k:k + 1]
        cols.append(tot)
    routed = jnp.concatenate(cols, axis=1)
    x2 = x1_ref[0] + gate_ref[0] * (routed + shared)
    o_ref[0] = _rms(x2, gf_ref[...])


def _combine(dest, wts, x1, h2, gate, wsg_bf, wsu_bf, wsd_bf, gf, ys, tm, chunks):
    b, t, d = x1.shape
    nt = t // tm
    row = lambda n: pl.BlockSpec((1, tm, n), lambda i, j: (i, j, 0))
    const = lambda shape: pl.BlockSpec(shape, lambda i, j: (0,) * len(shape))
    return pl.pallas_call(
        functools.partial(_combine_kernel, chunks=chunks),
        grid=(b, nt),
        in_specs=[pl.BlockSpec((1, 1, tm * TOP_K), lambda i, j: (i * nt + j, 0, 0),
                               memory_space=pltpu.SMEM),
                  row(LANES), row(d), row(d), _mod_spec(gate, tm, d),
                  const(wsg_bf.shape), const(wsu_bf.shape), const(wsd_bf.shape), const(gf.shape),
                  pl.BlockSpec(memory_space=pl.ANY)],
        out_specs=row(d),
        out_shape=jax.ShapeDtypeStruct((b, t, d), F32),
        scratch_shapes=[pltpu.VMEM((TOP_K, tm * chunks, LANES), F32), pltpu.SemaphoreType.DMA(())],
        compiler_params=_cparams("arbitrary", "arbitrary"),
        name="moe_combine",
    )(dest, wts, x1, h2, gate, wsg_bf, wsu_bf, wsd_bf, gf, ys)


def _split_mod(mod, per_token):
    parts = jnp.split(mod, 6, axis=-1)
    if per_token:
        return [p[None] for p in parts]
    return [p[:, None, :] for p in parts]


def _padded_in_weight(w_in, conv_width):
    n_gate = 3 * N_HEADS
    o = ATTN_WIDTH + 6 * KV_COLS
    main = w_in[:, :o]
    gates = jnp.pad(w_in[:, o:o + n_gate], ((0, 0), (0, LANES - n_gate)))
    glu = w_in[:, o + n_gate:o + n_gate + 2 * conv_width]
    return jnp.concatenate([main, gates, glu], axis=1).astype(BF16)


def _cmp_rows(x):
    return x.reshape(x.shape[:-2] + (x.shape[-2] // CMP_BLOCK, CMP_BLOCK * KV_COLS))


def _kv5(x):
    return x.reshape(x.shape[:-1] + (N_KV, HEAD_DIM))[None]


def kernel(x_prompt, x_sample, cache_k_cmp, cache_v_cmp, cache_k_slc, cache_v_slc, state_k_win, state_v_win, state_conv, page_table, c_prompt, c_sample, norm1_g, norm2_g, w_ada, b_ada, w_in, w_cmp_k, w_cmp_v, w_dw, b_dw, ln_conv_g, ln_conv_b, g_out_attn, g_out_conv, w_out, router_w, router_b, w_exp_gate, w_exp_up, w_exp_down, w_sh_gate, w_sh_up, w_sh_down, norm_f_g):
    assert w_ada.shape[0] == 1, "single layer"
    bp, t, d = x_prompt.shape
    bs, s_new, _ = x_sample.shape
    assert s_new == 1
    n_pool, page = cache_k_cmp.shape[1], cache_k_cmp.shape[2]
    n_pages = page_table.shape[1]
    past = n_pages * page
    conv_width = state_conv.shape[-1]
    n_exp = router_w.shape[-1]
    chunks = d // LANES
    tm = min(ROW_TILE, t)

    w_in_bf = _padded_in_weight(w_in[0], conv_width)
    wck = _compress_weight(w_cmp_k[0])
    wcv = _compress_weight(w_cmp_v[0])
    wout_bf = w_out[0].astype(BF16)
    rw_hi = router_w[0].astype(BF16)
    rw_lo = (router_w[0] - rw_hi.astype(F32)).astype(BF16)
    wsg_bf, wsu_bf, wsd_bf = (w[0].astype(BF16) for w in (w_sh_gate, w_sh_up, w_sh_down))
    gf = norm_f_g[None, :]

    n_c = bp + bs
    c_all = jnp.concatenate([c_prompt, c_sample], axis=0)
    c_all = jnp.pad(c_all, ((0, (-n_c) % SUBLANES), (0, 0)))
    mod = _modulation(c_all, w_ada[0], b_ada)
    mp = _split_mod(mod[:bp], per_token=False)
    ms = _split_mod(mod[bp:n_c], per_token=True)

    (q_p, kc_p, vc_p, ks_p, vs_p, kw_p, vw_p, gate_p, u_p) = _in_proj(
        x_prompt, mp[0], mp[1], norm1_g, w_in_bf, tm)
    nc_p = t // CMP_BLOCK
    kcc, vcc = _compress(_cmp_rows(kc_p).reshape(bp * nc_p, -1), _cmp_rows(vc_p).reshape(bp * nc_p, -1),
                         wck, wcv)
    o_attn_p = _prompt_attention(q_p, gate_p, kcc.reshape(bp, nc_p, KV_COLS), vcc.reshape(bp, nc_p, KV_COLS),
                                 ks_p, vs_p, kw_p, vw_p)
    o_conv_p = _conv_prompt(u_p, w_dw[0], b_dw, ln_conv_g, ln_conv_b, tm)

    xs_row = x_sample.reshape(1, bs, d)
    (q_s, kc_s, vc_s, ks_s, vs_s, kw_s, vw_s, gate_s, u_s) = _in_proj(
        xs_row, ms[0], ms[1], norm1_g, w_in_bf, bs)
    q8 = q_s.reshape(bs, N_HEADS, HEAD_DIM)
    gates8 = gate_s[0, :, :3 * N_HEADS].reshape(bs, N_HEADS, 3)
    cpp = page // CMP_BLOCK
    kc_pool, vc_pool = _compress(cache_k_cmp[0].reshape(n_pool * cpp, -1),
                                 cache_v_cmp[0].reshape(n_pool * cpp, -1), wck, wcv)
    tail = (-(past + s_new)) % SLC_BLOCK
    n_new = (s_new + tail) // CMP_BLOCK
    tail_rows = lambda x: _cmp_rows(jnp.pad(x[0][:, None, :], ((0, 0), (0, tail), (0, 0)))).reshape(bs * n_new, -1)
    pad_rows = (-(bs * n_new)) % SUBLANES
    kc_new, vc_new = _compress(jnp.pad(tail_rows(kc_s), ((0, pad_rows), (0, 0))),
                               jnp.pad(tail_rows(vc_s), ((0, pad_rows), (0, 0))), wck, wcv)
    kc_new = kc_new[:bs * n_new].reshape(bs, n_new, KV_COLS)
    vc_new = vc_new[:bs * n_new].reshape(bs, n_new, KV_COLS)
    o_cmp_s, sel = _sample_select(page_table, q8, kc_pool.reshape(n_pool, cpp * KV_COLS),
                                  vc_pool.reshape(n_pool, cpp * KV_COLS), kc_new, vc_new, past)
    n_sel = min(N_SEL, (past // CMP_BLOCK + n_new) // CMP_PER_SLC)
    sel = sel[:, :N_KV, :n_sel]
    spp = page // SLC_BLOCK
    row3 = lambda x: x[0][:, None, :]
    o_attn_s = _sample_attend(
        sel, page_table, q8,
        cache_k_slc[0].reshape(n_pool * spp, SLC_BLOCK, KV_COLS),
        cache_v_slc[0].reshape(n_pool * spp, SLC_BLOCK, KV_COLS),
        row3(ks_s), row3(vs_s),
        state_k_win[0].reshape(bs, -1, KV_COLS), state_v_win[0].reshape(bs, -1, KV_COLS),
        row3(kw_s), row3(vw_s), gates8, o_cmp_s, past)
    o_attn_s = o_attn_s.reshape(1, bs, ATTN_WIDTH)
    up_s = jnp.concatenate([state_conv[0], u_s[0][:, None, :]], axis=1)
    o_conv_s = _conv_sample(up_s, w_dw[0], b_dw, ln_conv_g, ln_conv_b)[None]

    router = functools.partial(_merge_router, goa=g_out_attn, goc=g_out_conv, wout_bf=wout_bf, g2=norm2_g,
                               rw_hi=rw_hi, rw_lo=rw_lo, rb=router_b)
    x1_p, h2_p, e_p, w_p, r_p, cnt = router(o_attn_p, o_conv_p, x_prompt, mp[2], mp[3], mp[4],
                                            cnt_in=jnp.zeros((1, n_exp), F32), tm=tm)
    x1_s, h2_s, e_s, w_s, r_s, cnt = router(o_attn_s, o_conv_s, xs_row, ms[2], ms[3], ms[4],
                                            cnt_in=cnt, tm=bs)

    n_tok = bp * t + bs
    e_all = jnp.concatenate([e_p.reshape(-1, LANES), e_s.reshape(-1, LANES)], axis=0)[:, :TOP_K]
    r_all = jnp.concatenate([r_p.reshape(-1, LANES), r_s.reshape(-1, LANES)], axis=0)[:, :TOP_K]
    counts = cnt[0].astype(I32)
    padded = (counts + MOE_ROWS - 1) // MOE_ROWS * MOE_ROWS
    pad_end = jnp.cumsum(padded)
    pad_start = pad_end - padded
    dest = pad_start[e_all] + r_all
    n_blocks = -(-(n_tok * TOP_K) // MOE_ROWS) + n_exp
    blk_expert = jnp.minimum(jnp.searchsorted(pad_end, jnp.arange(n_blocks, dtype=I32) * MOE_ROWS,
                                              side='right'), n_exp - 1).astype(I32)
    n_active = (pad_end[-1:] // MOE_ROWS).astype(I32)
    row_tok = jnp.zeros((n_blocks * MOE_ROWS,), I32).at[dest.reshape(-1)].set(
        jnp.repeat(jnp.arange(n_tok, dtype=I32), TOP_K))

    h_rows = jnp.concatenate([h2_p.reshape(-1, d), h2_s.reshape(-1, d)], axis=0).reshape(n_tok * chunks, LANES)
    xs = _dispatch(n_active, row_tok.reshape(n_blocks, 1, MOE_ROWS), h_rows, MOE_ROWS, chunks)
    ys = _experts(blk_expert, n_active, xs, w_exp_gate[0], w_exp_up[0], w_exp_down[0], MOE_ROWS, chunks)

    comb = functools.partial(_combine, wsg_bf=wsg_bf, wsu_bf=wsu_bf, wsd_bf=wsd_bf, gf=gf, ys=ys, chunks=chunks)
    dest_p = dest[:bp * t].reshape(bp * (t // tm), 1, tm * TOP_K)
    dest_s = dest[bp * t:].reshape(1, 1, bs * TOP_K)
    y_prompt = comb(dest_p, w_p, x1_p, h2_p, mp[5], tm=tm)
    y_sample = comb(dest_s, w_s, x1_s, h2_s, ms[5], tm=bs).reshape(bs, 1, d)

    win = min(WINDOW, t)
    hist = state_conv.shape[2]
    out_p = [_kv5(a) for a in (kc_p, vc_p, ks_p, vs_p, kw_p[:, t - win:], vw_p[:, t - win:])]
    conv_p = u_p[:, t - hist:][None]
    out_s = [_kv5(a[0][:, None, :]) for a in (kc_s, vc_s, ks_s, vs_s)]
    w_buf = state_k_win.shape[2]
    kw_buf = jnp.concatenate([state_k_win, _kv5(kw_s[0][:, None, :])], axis=2)[:, :, -w_buf:]
    vw_buf = jnp.concatenate([state_v_win, _kv5(vw_s[0][:, None, :])], axis=2)[:, :, -w_buf:]
    conv_s = up_s[:, -hist:][None]
    return (y_prompt, y_sample, *out_p, conv_p, *out_s, kw_buf, vw_buf, conv_s)
```

```python
import functools

import jax
import jax.numpy as jnp
from jax import lax
from jax.experimental import pallas as pl
from jax.experimental.pallas import tpu as pltpu

F32 = jnp.float32
BF16 = jnp.bfloat16
I32 = jnp.int32

N_HEADS = 8
HEAD_DIM = 64
N_KV = 2
Q_PER_KV = N_HEADS // N_KV
ATTN_WIDTH = N_HEADS * HEAD_DIM
KV_COLS = N_KV * HEAD_DIM
CMP_BLOCK = 32
SLC_BLOCK = 64
CMP_PER_SLC = SLC_BLOCK // CMP_BLOCK
N_SEL = 16
WINDOW = 512
TOP_K = 8
ROUTE_SCALE = 2.5
EPS = 1e-6
FORCED = 1e4
NEG = -1e30
ATTN_SCALE = HEAD_DIM ** -0.5

LANES = 128
SUBLANES = 8
VMEM_LIMIT = 56 * 1024 * 1024

ROW_TILE = 256
Q_TILE = 128
KEY_TILE = 512
CMP_ROW_TILE = 512
MOE_ROWS = 256
CONV_HALO = 32
PAGES_PER_STEP = 8


def _cparams(*sem):
    return pltpu.CompilerParams(dimension_semantics=sem, vmem_limit_bytes=VMEM_LIMIT)


def _dot(a, b):
    return jnp.dot(a, b, preferred_element_type=F32)


def _dot_nt(a, b):
    return lax.dot_general(a, b, (((1,), (1,)), ((), ())), preferred_element_type=F32)


def _split2(x):
    hi = x.astype(BF16)
    lo = (x - hi.astype(F32)).astype(BF16)
    return hi, lo


def _dot3(a, b):
    ah, al = _split2(a)
    bh, bl = _split2(b)
    return _dot(ah, bh) + (_dot(ah, bl) + _dot(al, bh))


def _dot3_nt(a, b):
    ah, al = _split2(a)
    bh, bl = _split2(b)
    return _dot_nt(ah, bh) + (_dot_nt(ah, bl) + _dot_nt(al, bh))


def _sigmoid(x):
    return 1.0 / (1.0 + jnp.exp(-x))


def _silu(x):
    return x * _sigmoid(x)


def _rms(x, g):
    return x * lax.rsqrt(jnp.mean(x * x, axis=-1, keepdims=True) + EPS) * g


def _alibi_slope_col(rows, rows_per_head, first_head, n_heads):
    r = lax.broadcasted_iota(I32, (rows, 1), 0) // rows_per_head
    out = jnp.zeros((rows, 1), F32)
    for k in range(n_heads):
        out = jnp.where(r == k, 2.0 ** (-8.0 * (first_head + k + 1) / N_HEADS), out)
    return out


def _softmax_rows(s, mask):
    m = jnp.max(s, axis=1, keepdims=True)
    p = jnp.where(mask, jnp.exp(s - m), 0.0)
    return p / jnp.maximum(jnp.sum(p, axis=1, keepdims=True), 1e-30)


def _modulation_kernel(c_ref, w_ref, b_ref, o_ref):
    o_ref[...] = _dot3(c_ref[...], w_ref[...]) + b_ref[...]


def _modulation(c, w, b):
    m, d = c.shape
    n = w.shape[1]
    tn = 768
    return pl.pallas_call(
        _modulation_kernel,
        grid=(n // tn,),
        in_specs=[pl.BlockSpec((m, d), lambda j: (0, 0)),
                  pl.BlockSpec((d, tn), lambda j: (0, j)),
                  pl.BlockSpec((1, tn), lambda j: (0, j))],
        out_specs=pl.BlockSpec((m, tn), lambda j: (0, j)),
        out_shape=jax.ShapeDtypeStruct((m, n), F32),
        compiler_params=_cparams("arbitrary"),
        name="modulation",
    )(c, w, b)


def _mod_spec(mod, tm, d):
    if mod.shape[1] == 1:
        return pl.BlockSpec((1, 1, d), lambda i, j: (i, 0, 0))
    return pl.BlockSpec((1, tm, d), lambda i, j: (i, j, 0))


def _in_proj_kernel(x_ref, shift_ref, scale_ref, g_ref, w_ref,
                    q_ref, kc_ref, vc_ref, ks_ref, vs_ref, kw_ref, vw_ref, gate_ref, u_ref):
    x = x_ref[0]
    h = _rms(x, g_ref[...]) * (1.0 + scale_ref[0]) + shift_ref[0]
    z = _dot(h.astype(BF16), w_ref[...])
    q_ref[0] = z[:, :ATTN_WIDTH]
    o = ATTN_WIDTH
    for ref in (kc_ref, vc_ref, ks_ref, vs_ref, kw_ref, vw_ref):
        ref[0] = z[:, o:o + KV_COLS]
        o += KV_COLS
    gate_ref[0] = _sigmoid(z[:, o:o + LANES])
    o += LANES
    cw = u_ref.shape[-1]
    u_ref[0] = z[:, o:o + cw] * _sigmoid(z[:, o + cw:o + 2 * cw])


def _in_proj(x, shift, scale, g, w_bf, tm):
    b, t, d = x.shape
    cw = (w_bf.shape[1] - ATTN_WIDTH - 6 * KV_COLS - LANES) // 2
    row = lambda n: pl.BlockSpec((1, tm, n), lambda i, j: (i, j, 0))
    sds = lambda n: jax.ShapeDtypeStruct((b, t, n), F32)
    return pl.pallas_call(
        _in_proj_kernel,
        grid=(b, t // tm),
        in_specs=[row(d), _mod_spec(shift, tm, d), _mod_spec(scale, tm, d),
                  pl.BlockSpec((1, d), lambda i, j: (0, 0)),
                  pl.BlockSpec(w_bf.shape, lambda i, j: (0, 0))],
        out_specs=[row(ATTN_WIDTH)] + [row(KV_COLS)] * 6 + [row(LANES), row(cw)],
        out_shape=[sds(ATTN_WIDTH)] + [sds(KV_COLS)] * 6 + [sds(LANES), sds(cw)],
        compiler_params=_cparams("arbitrary", "arbitrary"),
        name="in_proj",
    )(x, shift, scale, g, w_bf)


def _compress_kernel(k_ref, v_ref, wk_ref, wv_ref, ko_ref, vo_ref):
    ko_ref[...] = _dot3(k_ref[...], wk_ref[...])
    vo_ref[...] = _dot3(v_ref[...], wv_ref[...])


def _compress(k_rows, v_rows, wk, wv):
    r, kdim = k_rows.shape
    tr = min(CMP_ROW_TILE, r)
    assert r % tr == 0
    rows = pl.BlockSpec((tr, kdim), lambda i: (i, 0))
    wspec = pl.BlockSpec((kdim, KV_COLS), lambda i: (0, 0))
    ospec = pl.BlockSpec((tr, KV_COLS), lambda i: (i, 0))
    return pl.pallas_call(
        _compress_kernel,
        grid=(r // tr,),
        in_specs=[rows, rows, wspec, wspec],
        out_specs=[ospec, ospec],
        out_shape=[jax.ShapeDtypeStruct((r, KV_COLS), F32)] * 2,
        compiler_params=_cparams("arbitrary"),
        name="compress",
    )(k_rows, v_rows, wk, wv)


def _compress_weight(w):
    eye = jnp.eye(N_KV, dtype=w.dtype)
    big = jnp.einsum('lde,gh->lgdhe', w, eye)
    return big.reshape(CMP_BLOCK * KV_COLS, KV_COLS)


def _pair_sum(x):
    n = x.shape[-1]
    lane = lax.broadcasted_iota(I32, x.shape, x.ndim - 1)
    nxt = pltpu.roll(x, n - 1, x.ndim - 1)
    prv = pltpu.roll(x, 1, x.ndim - 1)
    return x + jnp.where((lane & 1) == 0, nxt, prv)


def _block_scores(imp, blk, q_pos, n_blocks_total):
    cur = q_pos // SLC_BLOCK
    valid = jnp.logical_and(blk * SLC_BLOCK <= q_pos, blk < n_blocks_total)
    forced = jnp.logical_or(blk == 0, jnp.logical_or(blk == cur, blk == cur - 1))
    return jnp.where(valid, jnp.where(forced, FORCED, imp), -1.0)


def _select_blocks(score, blk, n_sel):
    blk_f = blk.astype(F32)
    sel = jnp.zeros(score.shape, jnp.bool_)
    s = score
    picks = []
    for _ in range(n_sel):
        m = jnp.max(s, axis=1, keepdims=True)
        first = jnp.min(jnp.where(s == m, blk_f, 1e9), axis=1, keepdims=True)
        pick = blk_f == first
        picks.append((first, m))
        sel = jnp.logical_or(sel, pick)
        s = jnp.where(pick, -2.0, s)
    return jnp.logical_and(sel, score >= 0.0), picks


def _prompt_attn_kernel(q_ref, gate_ref, kc_ref, vc_ref, ks_ref, vs_ref, kw_ref, vw_ref, o_ref,
                        *, seq, n_sel):
    i = pl.program_id(1)
    tq = Q_TILE
    rows = Q_PER_KV * tq
    nc = kc_ref.shape[1]
    q_blk = q_ref[0] * ATTN_SCALE
    gates = gate_ref[0]
    q_pos_col = i * tq + lax.broadcasted_iota(I32, (tq, 1), 0)
    q_pos_rows = i * tq + (lax.broadcasted_iota(I32, (rows, 1), 0) & (tq - 1))
    q_pos_rows_f = q_pos_rows.astype(F32)
    span = WINDOW + tq
    w_start = pl.multiple_of(jnp.maximum(i * tq - WINDOW, 0), tq)
    pieces = []
    for g in range(N_KV):
        heads = [g * Q_PER_KV + r for r in range(Q_PER_KV)]
        qg = jnp.concatenate([q_blk[:, h * HEAD_DIM:(h + 1) * HEAD_DIM] for h in heads], axis=0)
        qg_bf = qg.astype(BF16)
        slope = _alibi_slope_col(rows, tq, g * Q_PER_KV, Q_PER_KV)
        gsl = slice(g * HEAD_DIM, (g + 1) * HEAD_DIM)

        kc = kc_ref[0][:, gsl]
        vc = vc_ref[0][:, gsl]
        cmp_pos = lax.broadcasted_iota(I32, (rows, nc), 1) * CMP_BLOCK + (CMP_BLOCK - 1)
        dist_c = (q_pos_rows - cmp_pos).astype(F32)
        mask_c = dist_c >= 0.0
        s_c = jnp.where(mask_c, _dot3_nt(qg, kc) - slope * dist_c, NEG)
        p_c = _softmax_rows(s_c, mask_c)
        o_cmp = _dot(p_c.astype(BF16), vc.astype(BF16))

        imp = p_c[0:tq]
        for r in range(1, Q_PER_KV):
            imp = imp + p_c[r * tq:(r + 1) * tq]
        blk = lax.broadcasted_iota(I32, (tq, nc), 1) >> 1
        score = _block_scores(_pair_sum(imp), blk, q_pos_col, seq // SLC_BLOCK)
        sel, _ = _select_blocks(score, blk, n_sel)
        sel_bf = jnp.where(sel, 1.0, 0.0).astype(BF16)

        n_tiles = ((i + 1) * tq + KEY_TILE - 1) // KEY_TILE

        def slc_step(t, carry, qg_bf=qg_bf, slope=slope, sel_bf=sel_bf, gsl=gsl):
            m_run, l_run, acc = carry
            k0 = pl.multiple_of(t * KEY_TILE, KEY_TILE)
            kt = ks_ref[0, pl.ds(k0, KEY_TILE), :][:, gsl]
            vt = vs_ref[0, pl.ds(k0, KEY_TILE), :][:, gsl]
            key_cmp = (k0 + lax.broadcasted_iota(I32, (nc, KEY_TILE), 1)) // CMP_BLOCK
            expand = jnp.where(key_cmp == lax.broadcasted_iota(I32, (nc, KEY_TILE), 0),
                               1.0, 0.0).astype(BF16)
            chosen = _dot(sel_bf, expand)
            chosen = jnp.concatenate([chosen] * Q_PER_KV, axis=0)
            dist = q_pos_rows_f - (k0 + lax.broadcasted_iota(I32, (rows, KEY_TILE), 1)).astype(F32)
            mask = jnp.logical_and(chosen > 0.5, dist >= 0.0)
            s = jnp.where(mask, _dot_nt(qg_bf, kt.astype(BF16)) - slope * dist, NEG)
            m_new = jnp.maximum(m_run, jnp.max(s, axis=1, keepdims=True))
            alpha = jnp.exp(m_run - m_new)
            p = jnp.where(mask, jnp.exp(s - m_new), 0.0)
            l_new = alpha * l_run + jnp.sum(p, axis=1, keepdims=True)
            acc_new = alpha * acc + _dot(p.astype(BF16), vt.astype(BF16))
            return m_new, l_new, acc_new

        init = (jnp.full((rows, 1), NEG, F32), jnp.zeros((rows, 1), F32),
                jnp.zeros((rows, HEAD_DIM), F32))
        _, l_s, acc_s = lax.fori_loop(0, n_tiles, slc_step, init)
        o_slc = acc_s / jnp.maximum(l_s, 1e-30)

        kw = kw_ref[0, pl.ds(w_start, span), :][:, gsl]
        vw = vw_ref[0, pl.ds(w_start, span), :][:, gsl]
        dist_w = q_pos_rows_f - (w_start + lax.broadcasted_iota(I32, (rows, span), 1)).astype(F32)
        mask_w = jnp.logical_and(dist_w >= 0.0, dist_w <= float(WINDOW))
        s_w = jnp.where(mask_w, _dot_nt(qg_bf, kw.astype(BF16)) - slope * dist_w, NEG)
        p_w = _softmax_rows(s_w, mask_w)
        o_win = _dot(p_w.astype(BF16), vw.astype(BF16))

        for r, h in enumerate(heads):
            rs = slice(r * tq, (r + 1) * tq)
            g0 = gates[:, 3 * h + 0:3 * h + 1]
            g1 = gates[:, 3 * h + 1:3 * h + 2]
            g2 = gates[:, 3 * h + 2:3 * h + 3]
            pieces.append(o_cmp[rs] * g0 + o_slc[rs] * g1 + o_win[rs] * g2)
    o_ref[0] = jnp.concatenate(pieces, axis=1)


def _prompt_attention(q, gates, kc, vc, ks, vs, kw, vw):
    b, t, _ = q.shape
    nc = kc.shape[1]
    assert t % KEY_TILE == 0 and t >= WINDOW + Q_TILE
    n_sel = min(N_SEL, t // SLC_BLOCK)
    qspec = lambda n: pl.BlockSpec((1, Q_TILE, n), lambda bi, i: (bi, i, 0))
    full = lambda r: pl.BlockSpec((1, r, KV_COLS), lambda bi, i: (bi, 0, 0))
    return pl.pallas_call(
        functools.partial(_prompt_attn_kernel, seq=t, n_sel=n_sel),
        grid=(b, t // Q_TILE),
        in_specs=[qspec(ATTN_WIDTH), qspec(LANES), full(nc), full(nc),
                  full(t), full(t), full(t), full(t)],
        out_specs=qspec(ATTN_WIDTH),
        out_shape=jax.ShapeDtypeStruct((b, t, ATTN_WIDTH), F32),
        compiler_params=_cparams("arbitrary", "arbitrary"),
        name="prompt_attention",
    )(q, gates, kc, vc, ks, vs, kw, vw)


def _merge_groups(per_group):
    row = lax.broadcasted_iota(I32, per_group[0].shape, 0) // Q_PER_KV
    out = per_group[0]
    for g in range(1, N_KV):
        out = jnp.where(row == g, per_group[g], out)
    return out


def _group_slice(x, g):
    return x[:, g * HEAD_DIM:(g + 1) * HEAD_DIM]


def _matmul3_kernel(a_ref, b_ref, o_ref):
    o_ref[...] = _dot3(a_ref[...], b_ref[...])


def _matmul3(a, b):
    return pl.pallas_call(
        _matmul3_kernel,
        out_shape=jax.ShapeDtypeStruct((a.shape[0], b.shape[1]), F32),
        compiler_params=pltpu.CompilerParams(vmem_limit_bytes=VMEM_LIMIT),
        name="matmul3",
    )(a, b)


def _page_specs(n_pages, page):
    def spec(o):
        return pl.BlockSpec((1, N_KV, HEAD_DIM, page),
                            lambda i, j, pt: (pt[i * n_pages + j * PAGES_PER_STEP + o], 0, 0, 0))
    return [spec(o) for o in range(PAGES_PER_STEP)]


def _sample_scores_kernel(pt_ref, ut_ref, *refs):
    k_refs, o_ref = refs[:-1], refs[-1]
    for h in range(N_HEADS):
        g = h // Q_PER_KV
        u = ut_ref[0, h]
        rows = [jnp.sum(k_ref[0, g] * u, axis=0, keepdims=True) for k_ref in k_refs]
        o_ref[0, h] = jnp.concatenate(rows, axis=0)


def _sample_scores(page_table, ut, k_pages):
    b, n_pages = page_table.shape
    page = k_pages.shape[-1]
    assert n_pages % PAGES_PER_STEP == 0
    return pl.pallas_call(
        _sample_scores_kernel,
        grid_spec=pltpu.PrefetchScalarGridSpec(
            num_scalar_prefetch=1,
            grid=(b, n_pages // PAGES_PER_STEP),
            in_specs=[pl.BlockSpec((1, N_HEADS, HEAD_DIM, page), lambda i, j, pt: (i, 0, 0, 0))]
                     + _page_specs(n_pages, page),
            out_specs=pl.BlockSpec((1, N_HEADS, PAGES_PER_STEP, page), lambda i, j, pt: (i, 0, j, 0)),
        ),
        out_shape=jax.ShapeDtypeStruct((b, N_HEADS, n_pages, page), F32),
        compiler_params=_cparams("arbitrary", "arbitrary"),
        name="sample_scores",
    )(page_table.reshape(-1), ut, *([k_pages] * PAGES_PER_STEP))


def _max_all(x):
    return jnp.max(jnp.max(x, axis=0, keepdims=True), axis=1, keepdims=True)


def _min_all(x):
    return jnp.min(jnp.min(x, axis=0, keepdims=True), axis=1, keepdims=True)


def _sum_all(x):
    return jnp.sum(jnp.sum(x, axis=0, keepdims=True), axis=1, keepdims=True)


def _sample_select_kernel(s_ref, q_ref, kcn_ref, pexp_ref, pnew_ref, sel_ref, *, past, n_new, n_sel):
    n_pages, page = s_ref.shape[2], s_ref.shape[3]
    cpp = page // CMP_BLOCK
    n_past = n_pages * cpp
    n_blocks_total = (n_past + n_new) // CMP_PER_SLC
    lane = lax.broadcasted_iota(I32, (n_pages, page), 1)
    prow = lax.broadcasted_iota(I32, (n_pages, page), 0)
    dist = (past - ((prow * cpp + lane // CMP_BLOCK) * CMP_BLOCK + (CMP_BLOCK - 1))).astype(F32)
    mask = jnp.logical_and(lane % CMP_BLOCK == 0, dist >= 0.0)

    q8 = q_ref[0] * ATTN_SCALE
    slope = _alibi_slope_col(N_HEADS, 1, 0, N_HEADS)
    kcn = jnp.concatenate([kcn_ref[0], jnp.zeros((LANES - n_new, KV_COLS), F32)], axis=0)
    new_lane = lax.broadcasted_iota(I32, (N_HEADS, LANES), 1)
    dist_n = (past - ((n_past + new_lane) * CMP_BLOCK + (CMP_BLOCK - 1))).astype(F32)
    mask_n = jnp.logical_and(dist_n >= 0.0, new_lane < n_new)
    qk_n = _merge_groups([_dot3_nt(q8, _group_slice(kcn, g)) for g in range(N_KV)])
    s_new = jnp.where(mask_n, qk_n - slope * dist_n, NEG)

    probs, probs_new = [], []
    for h in range(N_HEADS):
        x = s_ref[0, h]
        for sh in (16, 8, 4, 2, 1):
            x = x + pltpu.roll(x, page - sh, 1)
        s = jnp.where(mask, x - 2.0 ** (-8.0 * (h + 1) / N_HEADS) * dist, NEG)
        sn = s_new[h:h + 1, :]
        mn = jnp.logical_and(dist_n[h:h + 1, :] >= 0.0, new_lane[h:h + 1, :] < n_new)
        m = jnp.maximum(_max_all(s), jnp.max(sn, axis=1, keepdims=True))
        p = jnp.where(mask, jnp.exp(s - m), 0.0)
        pn = jnp.where(mn, jnp.exp(sn - m), 0.0)
        den = jnp.maximum(_sum_all(p) + jnp.sum(pn, axis=1, keepdims=True), 1e-30)
        p = p / den
        probs.append(p)
        probs_new.append(pn / den)
        z = p
        for sh in (1, 2, 4, 8, 16):
            z = z + pltpu.roll(z, sh, 1)
        pexp_ref[0, h] = z
    pnew_ref[0] = jnp.concatenate(probs_new, axis=0)

    row1 = lax.broadcasted_iota(I32, (1, LANES), 1)
    blk = jnp.where(lane % SLC_BLOCK == 0, prow * (page // SLC_BLOCK) + lane // SLC_BLOCK, -1)
    blk_n = jnp.where(row1 < n_new, n_past // CMP_PER_SLC + (row1 >> 1), -1)
    blk_f = blk.astype(F32)
    blk_nf = blk_n.astype(F32)
    out_lane = lax.broadcasted_iota(I32, (N_HEADS, LANES), 1)
    out_row = lax.broadcasted_iota(I32, (N_HEADS, LANES), 0)
    out = jnp.full((N_HEADS, LANES), -1, I32)
    for g in range(N_KV):
        imp = probs[g * Q_PER_KV]
        imp_n = probs_new[g * Q_PER_KV]
        for r in range(1, Q_PER_KV):
            imp = imp + probs[g * Q_PER_KV + r]
            imp_n = imp_n + probs_new[g * Q_PER_KV + r]
        imp = imp + pltpu.roll(imp, page - CMP_BLOCK, 1)
        s_m = jnp.where(blk >= 0, _block_scores(imp, blk, past, n_blocks_total), -4.0)
        s_n = jnp.where(blk_n >= 0, _block_scores(_pair_sum(imp_n), blk_n, past, n_blocks_total), -4.0)
        for j in range(n_sel):
            top = jnp.maximum(_max_all(s_m), jnp.max(s_n, axis=1, keepdims=True))
            first = jnp.minimum(_min_all(jnp.where(s_m == top, blk_f, 1e9)),
                                jnp.min(jnp.where(s_n == top, blk_nf, 1e9), axis=1, keepdims=True))
            s_m = jnp.where(blk_f == first, -2.0, s_m)
            s_n = jnp.where(blk_nf == first, -2.0, s_n)
            pick = jnp.where(top >= 0.0, first.astype(I32), -1)
            out = jnp.where(jnp.logical_and(out_row == g, out_lane == j), pick, out)
    sel_ref[0] = out


def _sample_select(s_raw, q8, kc_new, past):
    b, _, n_pages, page = s_raw.shape
    n_new = kc_new.shape[1]
    assert CMP_PER_SLC == 2 and CMP_BLOCK == 32
    n_sel = min(N_SEL, (past // CMP_BLOCK + n_new) // CMP_PER_SLC)
    per_b = lambda *s: pl.BlockSpec((1,) + s, lambda i: (i,) + (0,) * len(s))
    return pl.pallas_call(
        functools.partial(_sample_select_kernel, past=past, n_new=n_new, n_sel=n_sel),
        grid=(b,),
        in_specs=[per_b(N_HEADS, n_pages, page), per_b(N_HEADS, HEAD_DIM), per_b(n_new, KV_COLS)],
        out_specs=[per_b(N_HEADS, n_pages, page), per_b(N_HEADS, LANES), per_b(N_HEADS, LANES)],
        out_shape=[jax.ShapeDtypeStruct((b, N_HEADS, n_pages, page), F32),
                   jax.ShapeDtypeStruct((b, N_HEADS, LANES), F32),
                   jax.ShapeDtypeStruct((b, N_HEADS, LANES), I32)],
        compiler_params=_cparams("arbitrary"),
        name="sample_select",
    )(s_raw, q8, kc_new)


def _sample_values_kernel(pt_ref, pe_ref, *refs):
    v_refs, y_ref = refs[:-1], refs[-1]

    @pl.when(pl.program_id(1) == 0)
    def _():
        y_ref[...] = jnp.zeros(y_ref.shape, F32)

    for g in range(N_KV):
        heads = range(g * Q_PER_KV, (g + 1) * Q_PER_KV)
        pe = [pe_ref[0, h] for h in heads]
        acc = [jnp.zeros(y_ref.shape[2:], F32) for _ in heads]
        for o, v_ref in enumerate(v_refs):
            v = v_ref[0, g]
            for r in range(Q_PER_KV):
                acc[r] = acc[r] + v * pe[r][o:o + 1, :]
        for r, h in enumerate(heads):
            y_ref[0, h] = y_ref[0, h] + acc[r]


def _sample_values(page_table, pexp, v_pages):
    b, n_pages = page_table.shape
    page = v_pages.shape[-1]
    return pl.pallas_call(
        _sample_values_kernel,
        grid_spec=pltpu.PrefetchScalarGridSpec(
            num_scalar_prefetch=1,
            grid=(b, n_pages // PAGES_PER_STEP),
            in_specs=[pl.BlockSpec((1, N_HEADS, PAGES_PER_STEP, page), lambda i, j, pt: (i, 0, j, 0))]
                     + _page_specs(n_pages, page),
            out_specs=pl.BlockSpec((1, N_HEADS, HEAD_DIM, page), lambda i, j, pt: (i, 0, 0, 0)),
        ),
        out_shape=jax.ShapeDtypeStruct((b, N_HEADS, HEAD_DIM, page), F32),
        compiler_params=_cparams("arbitrary", "arbitrary"),
        name="sample_values",
    )(page_table.reshape(-1), pexp, *([v_pages] * PAGES_PER_STEP))


def _new_token_terms(q8, k_row, v_row):
    s = _merge_groups([jnp.sum(q8 * _group_slice(k_row, g), axis=1, keepdims=True) for g in range(N_KV)])
    v = _merge_groups([jnp.broadcast_to(_group_slice(v_row, g), (N_HEADS, HEAD_DIM)) for g in range(N_KV)])
    return s, v


def _sample_attend_kernel(sel_ref, pt_ref, q_ref, k0_ref, k1_ref, v0_ref, v1_ref, ksn_ref, vsn_ref,
                          kw_ref, vw_ref, kwn_ref, vwn_ref, gate_ref, ocmp_ref, pnew_ref, vcn_ref, o_ref,
                          m_sc, l_sc, acc_sc, *, past, n_sel, ns_past):
    b = pl.program_id(0)
    n = pl.program_id(1)
    page = k0_ref.shape[-1]
    spp = page // SLC_BLOCK
    q8 = q_ref[0] * ATTN_SCALE
    q8_bf = q8.astype(BF16)
    slope = _alibi_slope_col(N_HEADS, 1, 0, N_HEADS)

    @pl.when(n == 0)
    def _():
        m_sc[...] = jnp.full(m_sc.shape, NEG, F32)
        l_sc[...] = jnp.zeros(l_sc.shape, F32)
        acc_sc[...] = jnp.zeros(acc_sc.shape, F32)

    lane = lax.broadcasted_iota(I32, (N_HEADS, page), 1)
    qk_g, blk_g, blk_c = [], [], []
    for g, k_ref in enumerate((k0_ref, k1_ref)):
        blk = sel_ref[(b * N_KV + g) * n_sel + n]
        qk_g.append(_dot(q8_bf, k_ref[0, 0].astype(BF16)))
        blk_g.append(jnp.full((N_HEADS, page), blk, I32))
        blk_c.append(jnp.full((N_HEADS, 1), blk, I32))
    blk_rows = _merge_groups(blk_g)
    blk_col = _merge_groups(blk_c)
    page_pos = blk_rows // spp
    dist = (past - (page_pos * page + lane)).astype(F32)
    in_block = (lane // SLC_BLOCK) == (blk_rows - page_pos * spp)
    cached = jnp.logical_and(blk_rows >= 0, blk_rows < ns_past)
    mask = jnp.logical_and(jnp.logical_and(in_block, cached), dist >= 0.0)
    s = jnp.where(mask, _merge_groups(qk_g) - slope * dist, NEG)
    is_new = blk_col >= ns_past
    s_n, v_n = _new_token_terms(q8, ksn_ref[0], vsn_ref[0])
    s_n = jnp.where(is_new, s_n, NEG)
    m_old = m_sc[...]
    m_new = jnp.maximum(m_old, jnp.maximum(jnp.max(s, axis=1, keepdims=True), s_n))
    alpha = jnp.exp(m_old - m_new)
    p = jnp.where(mask, jnp.exp(s - m_new), 0.0)
    p_n = jnp.where(is_new, jnp.exp(s_n - m_new), 0.0)
    p_bf = p.astype(BF16)
    pv = _merge_groups([_dot_nt(p_bf, v_ref[0, 0].astype(BF16)) for v_ref in (v0_ref, v1_ref)])
    l_sc[...] = alpha * l_sc[...] + jnp.sum(p, axis=1, keepdims=True) + p_n
    acc_sc[...] = alpha * acc_sc[...] + pv + p_n * v_n
    m_sc[...] = m_new

    @pl.when(n == n_sel - 1)
    def _():
        o_slc = acc_sc[...] / jnp.maximum(l_sc[...], 1e-30)
        w_buf = kw_ref.shape[-1]
        wl = lax.broadcasted_iota(I32, (N_HEADS, w_buf), 1)
        win_pos = past - w_buf + wl
        dist_w = (past - win_pos).astype(F32)
        mask_w = jnp.logical_and(jnp.logical_and(dist_w >= 0.0, dist_w <= float(WINDOW)), win_pos >= 0)
        qk_w = _merge_groups([_dot(q8_bf, kw_ref[0, g].astype(BF16)) for g in range(N_KV)])
        s_w = jnp.where(mask_w, qk_w - slope * dist_w, NEG)
        s_t, v_t = _new_token_terms(q8, kwn_ref[0], vwn_ref[0])
        m_w = jnp.maximum(jnp.max(s_w, axis=1, keepdims=True), s_t)
        p_w = jnp.where(mask_w, jnp.exp(s_w - m_w), 0.0)
        p_t = jnp.exp(s_t - m_w)
        den = jnp.maximum(jnp.sum(p_w, axis=1, keepdims=True) + p_t, 1e-30)
        pw_bf = p_w.astype(BF16)
        o_w = _merge_groups([_dot_nt(pw_bf, vw_ref[0, g].astype(BF16)) for g in range(N_KV)])
        o_win = (o_w + p_t * v_t) / den
        n_new = vcn_ref.shape[1]
        vcn = jnp.concatenate([vcn_ref[0], jnp.zeros((LANES - n_new, KV_COLS), F32)], axis=0).astype(BF16)
        pn_bf = pnew_ref[0].astype(BF16)
        o_cmp = ocmp_ref[0] + _merge_groups([_dot(pn_bf, _group_slice(vcn, g)) for g in range(N_KV)])
        gt = gate_ref[0]
        o_ref[0] = o_cmp * gt[:, 0:1] + o_slc * gt[:, 1:2] + o_win * gt[:, 2:3]


def _sample_attend(sel, page_table, q8, k_pages, v_pages, ks_new, vs_new, kw_state, vw_state,
                   kw_new, vw_new, gates8, o_cmp, p_new, vc_new, past):
    b, n_pages = page_table.shape
    n_sel = sel.shape[-1]
    page = k_pages.shape[-1]
    ns_past = past // SLC_BLOCK
    spp = page // SLC_BLOCK
    w_buf = kw_state.shape[-1]
    n_new = vc_new.shape[1]

    def cache_map(g):
        def index(i, n, sel_ref, pt_ref):
            blk = jnp.clip(sel_ref[(i * N_KV + g) * n_sel + n], 0, ns_past - 1)
            return (pt_ref[i * n_pages + blk // spp], g, 0, 0)
        return pl.BlockSpec((1, 1, HEAD_DIM, page), index)

    per_b = lambda *s: pl.BlockSpec((1,) + s, lambda i, n, sl, pt: (i,) + (0,) * len(s))
    return pl.pallas_call(
        functools.partial(_sample_attend_kernel, past=past, n_sel=n_sel, ns_past=ns_past),
        grid_spec=pltpu.PrefetchScalarGridSpec(
            num_scalar_prefetch=2,
            grid=(b, n_sel),
            in_specs=[per_b(N_HEADS, HEAD_DIM), cache_map(0), cache_map(1), cache_map(0), cache_map(1),
                      per_b(1, KV_COLS), per_b(1, KV_COLS),
                      per_b(N_KV, HEAD_DIM, w_buf), per_b(N_KV, HEAD_DIM, w_buf),
                      per_b(1, KV_COLS), per_b(1, KV_COLS),
                      per_b(N_HEADS, 3), per_b(N_HEADS, HEAD_DIM), per_b(N_HEADS, LANES),
                      per_b(n_new, KV_COLS)],
            out_specs=per_b(N_HEADS, HEAD_DIM),
            scratch_shapes=[pltpu.VMEM((N_HEADS, 1), F32), pltpu.VMEM((N_HEADS, 1), F32),
                            pltpu.VMEM((N_HEADS, HEAD_DIM), F32)],
        ),
        out_shape=jax.ShapeDtypeStruct((b, N_HEADS, HEAD_DIM), F32),
        compiler_params=_cparams("arbitrary", "arbitrary"),
        name="sample_attend",
    )(sel.reshape(-1), page_table.reshape(-1), q8, k_pages, k_pages, v_pages, v_pages,
      ks_new, vs_new, kw_state, vw_state, kw_new, vw_new, gates8, o_cmp, p_new, vc_new)


def _layernorm_silu(y, g, b):
    mu = jnp.mean(y, axis=-1, keepdims=True)
    var = jnp.mean(jnp.square(y - mu), axis=-1, keepdims=True)
    return _silu((y - mu) * lax.rsqrt(var + EPS) * g + b)


def _conv_prompt_kernel(u_ref, w_ref, b_ref, g_ref, beta_ref, o_ref, buf):
    j = pl.program_id(1)
    tt = u_ref.shape[1]
    kw = w_ref.shape[0]

    @pl.when(j == 0)
    def _():
        buf[0:CONV_HALO, :] = jnp.zeros((CONV_HALO, buf.shape[1]), F32)

    buf[CONV_HALO:CONV_HALO + tt, :] = u_ref[0]
    w = w_ref[...]
    acc = jnp.zeros((tt, buf.shape[1]), F32)
    for k in range(kw):
        acc = acc + w[k:k + 1, :] * buf[pl.ds(CONV_HALO - (kw - 1) + k, tt), :]
    o_ref[0] = _layernorm_silu(acc + b_ref[...], g_ref[...], beta_ref[...])
    buf[0:CONV_HALO, :] = buf[tt:tt + CONV_HALO, :]


def _conv_prompt(u, w_dw, b_dw, ln_g, ln_b, tt):
    b, t, c = u.shape
    vec = pl.BlockSpec((1, c), lambda i, j: (0, 0))
    return pl.pallas_call(
        _conv_prompt_kernel,
        grid=(b, t // tt),
        in_specs=[pl.BlockSpec((1, tt, c), lambda i, j: (i, j, 0)),
                  pl.BlockSpec(w_dw.shape, lambda i, j: (0, 0)), vec, vec, vec],
        out_specs=pl.BlockSpec((1, tt, c), lambda i, j: (i, j, 0)),
        out_shape=jax.ShapeDtypeStruct((b, t, c), F32),
        scratch_shapes=[pltpu.VMEM((CONV_HALO + tt, c), F32)],
        compiler_params=_cparams("arbitrary", "arbitrary"),
        name="conv_prompt",
    )(u, w_dw, b_dw, ln_g, ln_b)


def _conv_sample_kernel(up_ref, w_ref, b_ref, g_ref, beta_ref, o_ref):
    y = jnp.sum(up_ref[...] * w_ref[...][None, :, :], axis=1)
    o_ref[...] = _layernorm_silu(y + b_ref[...], g_ref[...], beta_ref[...])


def _conv_sample(up, w_dw, b_dw, ln_g, ln_b):
    b, kw, c = up.shape
    return pl.pallas_call(
        _conv_sample_kernel,
        out_shape=jax.ShapeDtypeStruct((b, c), F32),
        name="conv_sample",
    )(up, w_dw, b_dw, ln_g, ln_b)


def _merge_router_kernel(oa_ref, oc_ref, x_ref, gate_ref, shift_ref, scale_ref, goa_ref, goc_ref,
                         wout_ref, g2_ref, rwh_ref, rwl_ref, rb_ref, cnt_in_ref,
                         x1_ref, h2_ref, eidx_ref, wts_ref, rank_ref, cnt_out_ref, run):
    first = jnp.logical_and(pl.program_id(0) == 0, pl.program_id(1) == 0)

    @pl.when(first)
    def _():
        run[...] = cnt_in_ref[...]

    a = _rms(oa_ref[0], goa_ref[...])
    c = _rms(oc_ref[0], goc_ref[...])
    cat = jnp.concatenate([a, c], axis=1).astype(BF16)
    x1 = x_ref[0] + gate_ref[0] * _dot(cat, wout_ref[...])
    x1_ref[0] = x1
    h2 = _rms(x1, g2_ref[...]) * (1.0 + scale_ref[0]) + shift_ref[0]
    h2_ref[0] = h2

    hh, hl = _split2(h2)
    logits = _dot(hh, rwh_ref[...]) + (_dot(hh, rwl_ref[...]) + _dot(hl, rwh_ref[...]))
    aff = _sigmoid(logits)
    tm, n_exp = aff.shape
    lane_f = lax.broadcasted_iota(I32, (tm, n_exp), 1).astype(F32)
    s = aff + rb_ref[...]
    hot = jnp.zeros((tm, n_exp), jnp.bool_)
    experts, weights = [], []
    for _ in range(TOP_K):
        m = jnp.max(s, axis=1, keepdims=True)
        e = jnp.min(jnp.where(s == m, lane_f, 1e9), axis=1, keepdims=True)
        pick = lane_f == e
        experts.append(e)
        weights.append(jnp.sum(jnp.where(pick, aff, 0.0), axis=1, keepdims=True))
        hot = jnp.logical_or(hot, pick)
        s = jnp.where(pick, NEG, s)
    total = weights[0]
    for w in weights[1:]:
        total = total + w

    hot_f = jnp.where(hot, 1.0, 0.0)
    r_i = lax.broadcasted_iota(I32, (tm, tm), 0)
    c_i = lax.broadcasted_iota(I32, (tm, tm), 1)
    lower = jnp.where(c_i < r_i, 1.0, 0.0).astype(BF16)
    before = _dot(lower, hot_f.astype(BF16)) + run[...]
    out_lane = lax.broadcasted_iota(I32, (tm, LANES), 1)
    e_out = jnp.zeros((tm, LANES), I32)
    w_out = jnp.zeros((tm, LANES), F32)
    r_out = jnp.zeros((tm, LANES), I32)
    for k in range(TOP_K):
        rank = jnp.sum(jnp.where(lane_f == experts[k], before, 0.0), axis=1, keepdims=True)
        e_out = jnp.where(out_lane == k, experts[k].astype(I32), e_out)
        w_out = jnp.where(out_lane == k, ROUTE_SCALE * weights[k] / total, w_out)
        r_out = jnp.where(out_lane == k, rank.astype(I32), r_out)
    eidx_ref[0] = e_out
    wts_ref[0] = w_out
    rank_ref[0] = r_out
    run[...] = run[...] + jnp.sum(hot_f, axis=0, keepdims=True)
    cnt_out_ref[...] = run[...]


def _merge_router(o_attn, o_conv, x, gate, shift, scale, goa, goc, wout_bf, g2, rw_hi, rw_lo, rb,
                  cnt_in, tm):
    b, t, d = x.shape
    n_exp = rw_hi.shape[1]
    row = lambda n: pl.BlockSpec((1, tm, n), lambda i, j: (i, j, 0))
    const = lambda shape: pl.BlockSpec(shape, lambda i, j: (0,) * len(shape))
    sds = lambda n, dt: jax.ShapeDtypeStruct((b, t, n), dt)
    return pl.pallas_call(
        _merge_router_kernel,
        grid=(b, t // tm),
        in_specs=[row(o_attn.shape[-1]), row(o_conv.shape[-1]), row(d),
                  _mod_spec(gate, tm, d), _mod_spec(shift, tm, d), _mod_spec(scale, tm, d),
                  const(goa.shape), const(goc.shape), const(wout_bf.shape), const(g2.shape),
                  const(rw_hi.shape), const(rw_lo.shape), const(rb.shape), const(cnt_in.shape)],
        out_specs=[row(d), row(d), row(LANES), row(LANES), row(LANES), const((1, n_exp))],
        out_shape=[sds(d, F32), sds(d, F32), sds(LANES, I32), sds(LANES, F32), sds(LANES, I32),
                   jax.ShapeDtypeStruct((1, n_exp), F32)],
        scratch_shapes=[pltpu.VMEM((1, n_exp), F32)],
        compiler_params=_cparams("arbitrary", "arbitrary"),
        name="merge_router",
    )(o_attn, o_conv, x, gate, shift, scale, goa, goc, wout_bf, g2, rw_hi, rw_lo, rb, cnt_in)


def _row_copy(src_hbm, dst, src_row, dst_row, sem, chunks):
    return pltpu.make_async_copy(src_hbm.at[pl.ds(src_row * chunks, chunks)],
                                 dst.at[pl.ds(dst_row * chunks, chunks)], sem)


def _slots_kernel(e_ref, r_ref, start_ref, o_ref):
    e = e_ref[0]
    r = r_ref[0]
    start = start_ref[...]
    tm = e.shape[0]
    lane_e = lax.broadcasted_iota(I32, (tm, start.shape[1]), 1)
    out_lane = lax.broadcasted_iota(I32, (tm, LANES), 1)
    out = jnp.zeros((tm, LANES), I32)
    for k in range(TOP_K):
        base = jnp.sum(jnp.where(lane_e == e[:, k:k + 1], start, 0.0), axis=1, keepdims=True)
        out = jnp.where(out_lane == k, base.astype(I32) + r[:, k:k + 1], out)
    o_ref[0] = out


def _slots(e_idx, rank, start, tm):
    b, t, _ = e_idx.shape
    row = pl.BlockSpec((1, tm, LANES), lambda i, j: (i, j, 0))
    return pl.pallas_call(
        _slots_kernel,
        grid=(b, t // tm),
        in_specs=[row, row, pl.BlockSpec(start.shape, lambda i, j: (0, 0))],
        out_specs=row,
        out_shape=jax.ShapeDtypeStruct((b, t, LANES), I32),
        compiler_params=_cparams("arbitrary", "arbitrary"),
        name="moe_slots",
    )(e_idx, rank, start)


def _dispatch_kernel(cnt_ref, end_ref, dest_ref, h_hbm, xs_hbm, zbuf, zsem, sem,
                     *, tokens, rows, chunks, n_blocks):
    j = pl.program_id(0)
    n_exp = cnt_ref.shape[0]
    blk_rows = rows * chunks

    def zero_block(blk):
        return pltpu.make_async_copy(zbuf, xs_hbm.at[pl.ds(blk * blk_rows, blk_rows)], zsem)

    @pl.when(j == 0)
    def _():
        zbuf[...] = jnp.zeros(zbuf.shape, F32)
        n_active = end_ref[n_exp - 1] // rows

        def zero_tail(e, issued):
            partial = cnt_ref[e] % rows != 0

            @pl.when(partial)
            def _():
                zero_block(end_ref[e] // rows - 1).start()

            return issued + partial.astype(I32)

        def zero_unused(blk, _):
            zero_block(blk).start()
            return 0

        def drain_zero(_, c):
            zero_block(0).wait()
            return c

        issued = lax.fori_loop(0, n_exp, zero_tail, 0)
        lax.fori_loop(n_active, n_blocks, zero_unused, 0)
        lax.fori_loop(0, issued + (n_blocks - n_active), drain_zero, 0)

    def issue(r, _):
        for k in range(TOP_K):
            _row_copy(h_hbm, xs_hbm, j * tokens + r, dest_ref[0, 0, r * TOP_K + k], sem, chunks).start()
        return 0

    def drain(r, _):
        for k in range(TOP_K):
            _row_copy(h_hbm, xs_hbm, 0, 0, sem, chunks).wait()
        return 0

    lax.fori_loop(0, tokens, issue, 0)
    lax.fori_loop(0, tokens, drain, 0)


def _dispatch(counts, pad_end, dest, h_rows, tokens, rows, chunks, n_blocks):
    n_tiles = dest.shape[0]
    return pl.pallas_call(
        functools.partial(_dispatch_kernel, tokens=tokens, rows=rows, chunks=chunks, n_blocks=n_blocks),
        grid_spec=pltpu.PrefetchScalarGridSpec(
            num_scalar_prefetch=2,
            grid=(n_tiles,),
            in_specs=[pl.BlockSpec((1, 1, tokens * TOP_K), lambda j, c, e: (j, 0, 0),
                                   memory_space=pltpu.SMEM),
                      pl.BlockSpec(memory_space=pl.ANY)],
            out_specs=pl.BlockSpec(memory_space=pl.ANY),
            scratch_shapes=[pltpu.VMEM((rows * chunks, LANES), F32),
                            pltpu.SemaphoreType.DMA(()), pltpu.SemaphoreType.DMA(())],
        ),
        out_shape=jax.ShapeDtypeStruct((n_blocks * rows * chunks, LANES), F32),
        compiler_params=_cparams("arbitrary"),
        name="moe_dispatch",
    )(counts, pad_end, dest, h_rows)


def _expert_kernel(be_ref, nact_ref, x_ref, wg_ref, wu_ref, wd_ref, y_ref, *, rows, chunks):
    j = pl.program_id(0)

    @pl.when(j < nact_ref[0])
    def _():
        f = wg_ref.shape[2]
        gate = jnp.zeros((rows, f), F32)
        up = jnp.zeros((rows, f), F32)
        for c in range(chunks):
            xc = x_ref[pl.ds(c, rows, stride=chunks), :].astype(BF16)
            cs = slice(c * LANES, (c + 1) * LANES)
            gate = gate + _dot(xc, wg_ref[0, cs, :].astype(BF16))
            up = up + _dot(xc, wu_ref[0, cs, :].astype(BF16))
        h = (_silu(gate) * up).astype(BF16)
        y = _dot(h, wd_ref[0].astype(BF16))
        for c in range(chunks):
            y_ref[pl.ds(c, rows, stride=chunks), :] = y[:, c * LANES:(c + 1) * LANES]

    @pl.when(j >= nact_ref[0])
    def _():
        y_ref[...] = jnp.zeros(y_ref.shape, F32)


def _experts(blk_expert, n_active, xs, wg, wu, wd, rows, chunks):
    n_blocks = blk_expert.shape[0]
    d, f = wg.shape[1], wg.shape[2]
    last = lambda j, na: jnp.minimum(j, na[0] - 1)
    xspec = pl.BlockSpec((rows * chunks, LANES), lambda j, be, na: (last(j, na), 0))
    wspec = lambda s: pl.BlockSpec((1,) + s, lambda j, be, na: (be[last(j, na)], 0, 0))
    return pl.pallas_call(
        functools.partial(_expert_kernel, rows=rows, chunks=chunks),
        grid_spec=pltpu.PrefetchScalarGridSpec(
            num_scalar_prefetch=2,
            grid=(n_blocks,),
            in_specs=[xspec, wspec((d, f)), wspec((d, f)), wspec((f, d))],
            out_specs=pl.BlockSpec((rows * chunks, LANES), lambda j, be, na: (j, 0)),
        ),
        out_shape=jax.ShapeDtypeStruct(xs.shape, F32),
        compiler_params=_cparams("arbitrary"),
        name="moe_experts",
    )(blk_expert, n_active, xs, wg, wu, wd)


def _combine_kernel(dest_ref, w_ref, x1_ref, h2_ref, gate_ref, wsg_ref, wsu_ref, wsd_ref, gf_ref,
                    ys_hbm, o_ref, buf, sem, *, chunks):
    tm = x1_ref.shape[1]

    def issue(r, _):
        for k in range(TOP_K):
            _row_copy(ys_hbm, buf.at[k], dest_ref[0, 0, r * TOP_K + k], r, sem, chunks).start()
        return 0

    def drain(r, _):
        for k in range(TOP_K):
            _row_copy(ys_hbm, buf.at[k], 0, r, sem, chunks).wait()
        return 0

    lax.fori_loop(0, tm, issue, 0)
    h_bf = h2_ref[0].astype(BF16)
    hid = (_silu(_dot(h_bf, wsg_ref[...])) * _dot(h_bf, wsu_ref[...])).astype(BF16)
    shared = _dot(hid, wsd_ref[...])
    lax.fori_loop(0, tm, drain, 0)

    w = w_ref[0]
    cols = []
    for c in range(chunks):
        tot = jnp.zeros((tm, LANES), F32)
        for k in range(TOP_K):
            tot = tot + buf[k, pl.ds(c, tm, stride=chunks), :] * w[:, k:k + 1]
        cols.append(tot)
    routed = jnp.concatenate(cols, axis=1)
    x2 = x1_ref[0] + gate_ref[0] * (routed + shared)
    o_ref[0] = _rms(x2, gf_ref[...])


def _combine(dest, wts, x1, h2, gate, wsg_bf, wsu_bf, wsd_bf, gf, ys, tm, chunks):
    b, t, d = x1.shape
    nt = t // tm
    row = lambda n: pl.BlockSpec((1, tm, n), lambda i, j: (i, j, 0))
    const = lambda shape: pl.BlockSpec(shape, lambda i, j: (0,) * len(shape))
    return pl.pallas_call(
        functools.partial(_combine_kernel, chunks=chunks),
        grid=(b, nt),
        in_specs=[pl.BlockSpec((1, 1, tm * TOP_K), lambda i, j: (i * nt + j, 0, 0),
                               memory_space=pltpu.SMEM),
                  row(LANES), row(d), row(d), _mod_spec(gate, tm, d),
                  const(wsg_bf.shape), const(wsu_bf.shape), const(wsd_bf.shape), const(gf.shape),
                  pl.BlockSpec(memory_space=pl.ANY)],
        out_specs=row(d),
        out_shape=jax.ShapeDtypeStruct((b, t, d), F32),
        scratch_shapes=[pltpu.VMEM((TOP_K, tm * chunks, LANES), F32), pltpu.SemaphoreType.DMA(())],
        compiler_params=_cparams("arbitrary", "arbitrary"),
        name="moe_combine",
    )(dest, wts, x1, h2, gate, wsg_bf, wsu_bf, wsd_bf, gf, ys)


def _split_mod(mod, per_token):
    parts = jnp.split(mod, 6, axis=-1)
    if per_token:
        return [p[None] for p in parts]
    return [p[:, None, :] for p in parts]


def _padded_in_weight(w_in, conv_width):
    n_gate = 3 * N_HEADS
    o = ATTN_WIDTH + 6 * KV_COLS
    main = w_in[:, :o]
    gates = jnp.pad(w_in[:, o:o + n_gate], ((0, 0), (0, LANES - n_gate)))
    glu = w_in[:, o + n_gate:o + n_gate + 2 * conv_width]
    return jnp.concatenate([main, gates, glu], axis=1).astype(BF16)


def _cmp_rows(x):
    return x.reshape(x.shape[:-2] + (x.shape[-2] // CMP_BLOCK, CMP_BLOCK * KV_COLS))


def _largest_tile(n, cap):
    best = [k for k in range(SUBLANES, cap + 1, SUBLANES) if n % k == 0]
    assert best, (n, cap)
    return best[-1]


def _kv5(x):
    return x.reshape(x.shape[:-1] + (N_KV, HEAD_DIM))[None]


def kernel(x_prompt, x_sample, cache_k_cmp, cache_v_cmp, cache_k_slc, cache_v_slc, state_k_win, state_v_win, state_conv, page_table, c_prompt, c_sample, norm1_g, norm2_g, w_ada, b_ada, w_in, w_cmp_k, w_cmp_v, w_dw, b_dw, ln_conv_g, ln_conv_b, g_out_attn, g_out_conv, w_out, router_w, router_b, w_exp_gate, w_exp_up, w_exp_down, w_sh_gate, w_sh_up, w_sh_down, norm_f_g):
    assert w_ada.shape[0] == 1, "single layer"
    bp, t, d = x_prompt.shape
    bs, s_new, _ = x_sample.shape
    assert s_new == 1
    n_pool, page = cache_k_cmp.shape[1], cache_k_cmp.shape[2]
    n_pages = page_table.shape[1]
    past = n_pages * page
    conv_width = state_conv.shape[-1]
    n_exp = router_w.shape[-1]
    chunks = d // LANES
    tm = min(ROW_TILE, t)

    w_in_bf = _padded_in_weight(w_in[0], conv_width)
    wck = _compress_weight(w_cmp_k[0])
    wcv = _compress_weight(w_cmp_v[0])
    wout_bf = w_out[0].astype(BF16)
    rw_hi = router_w[0].astype(BF16)
    rw_lo = (router_w[0] - rw_hi.astype(F32)).astype(BF16)
    wsg_bf, wsu_bf, wsd_bf = (w[0].astype(BF16) for w in (w_sh_gate, w_sh_up, w_sh_down))
    gf = norm_f_g[None, :]

    n_c = bp + bs
    c_all = jnp.concatenate([c_prompt, c_sample], axis=0)
    c_all = jnp.pad(c_all, ((0, (-n_c) % SUBLANES), (0, 0)))
    mod = _modulation(c_all, w_ada[0], b_ada)
    mp = _split_mod(mod[:bp], per_token=False)
    ms = _split_mod(mod[bp:n_c], per_token=True)

    (q_p, kc_p, vc_p, ks_p, vs_p, kw_p, vw_p, gate_p, u_p) = _in_proj(
        x_prompt, mp[0], mp[1], norm1_g, w_in_bf, tm)
    nc_p = t // CMP_BLOCK
    kcc, vcc = _compress(_cmp_rows(kc_p).reshape(bp * nc_p, -1), _cmp_rows(vc_p).reshape(bp * nc_p, -1),
                         wck, wcv)
    o_attn_p = _prompt_attention(q_p, gate_p, kcc.reshape(bp, nc_p, KV_COLS), vcc.reshape(bp, nc_p, KV_COLS),
                                 ks_p, vs_p, kw_p, vw_p)
    o_conv_p = _conv_prompt(u_p, w_dw[0], b_dw, ln_conv_g, ln_conv_b, tm)

    xs_row = x_sample.reshape(1, bs, d)
    (q_s, kc_s, vc_s, ks_s, vs_s, kw_s, vw_s, gate_s, u_s) = _in_proj(
        xs_row, ms[0], ms[1], norm1_g, w_in_bf, bs)
    q8 = q_s.reshape(bs, N_HEADS, HEAD_DIM)
    gates8 = gate_s[0, :, :3 * N_HEADS].reshape(bs, N_HEADS, 3)
    pages_t = lambda c: jnp.transpose(c[0], (0, 2, 3, 1))
    tail = (-(past + s_new)) % SLC_BLOCK
    n_new = (s_new + tail) // CMP_BLOCK
    tail_rows = lambda x: _cmp_rows(jnp.pad(x[0][:, None, :], ((0, 0), (0, tail), (0, 0)))).reshape(bs * n_new, -1)
    pad_rows = (-(bs * n_new)) % SUBLANES
    kc_new, vc_new = _compress(jnp.pad(tail_rows(kc_s), ((0, pad_rows), (0, 0))),
                               jnp.pad(tail_rows(vc_s), ((0, pad_rows), (0, 0))), wck, wcv)
    kc_new = kc_new[:bs * n_new].reshape(bs, n_new, KV_COLS)
    vc_new = vc_new[:bs * n_new].reshape(bs, n_new, KV_COLS)
    reps = page // CMP_BLOCK
    wk_fold = jnp.transpose(w_cmp_k[0], (2, 1, 0)).reshape(HEAD_DIM, HEAD_DIM * CMP_BLOCK)
    ut = _matmul3(q8.reshape(bs * N_HEADS, HEAD_DIM) * ATTN_SCALE, wk_fold)
    ut = jnp.tile(ut.reshape(bs, N_HEADS, HEAD_DIM, CMP_BLOCK), (1, 1, 1, reps))
    s_raw = _sample_scores(page_table, ut, pages_t(cache_k_cmp))
    p_exp, p_new, sel = _sample_select(s_raw, q8, kc_new, past)
    y_acc = _sample_values(page_table, p_exp, pages_t(cache_v_cmp))
    wv_fold = jnp.tile(jnp.transpose(w_cmp_v[0], (1, 0, 2)), (1, reps, 1)).reshape(HEAD_DIM * page, HEAD_DIM)
    o_cmp_s = _matmul3(y_acc.reshape(bs * N_HEADS, HEAD_DIM * page), wv_fold).reshape(bs, N_HEADS, HEAD_DIM)
    n_sel = min(N_SEL, (past // CMP_BLOCK + n_new) // CMP_PER_SLC)
    sel = sel[:, :N_KV, :n_sel]
    row3 = lambda x: x[0][:, None, :]
    o_attn_s = _sample_attend(
        sel, page_table, q8, pages_t(cache_k_slc), pages_t(cache_v_slc), row3(ks_s), row3(vs_s),
        pages_t(state_k_win), pages_t(state_v_win), row3(kw_s), row3(vw_s), gates8, o_cmp_s, p_new, vc_new,
        past)
    o_attn_s = o_attn_s.reshape(1, bs, ATTN_WIDTH)
    up_s = jnp.concatenate([state_conv[0], u_s[0][:, None, :]], axis=1)
    o_conv_s = _conv_sample(up_s, w_dw[0], b_dw, ln_conv_g, ln_conv_b)[None]

    router = functools.partial(_merge_router, goa=g_out_attn, goc=g_out_conv, wout_bf=wout_bf, g2=norm2_g,
                               rw_hi=rw_hi, rw_lo=rw_lo, rb=router_b)
    x1_p, h2_p, e_p, w_p, r_p, cnt = router(o_attn_p, o_conv_p, x_prompt, mp[2], mp[3], mp[4],
                                            cnt_in=jnp.zeros((1, n_exp), F32), tm=tm)
    x1_s, h2_s, e_s, w_s, r_s, cnt = router(o_attn_s, o_conv_s, xs_row, ms[2], ms[3], ms[4],
                                            cnt_in=cnt, tm=bs)

    n_tok = bp * t + bs
    counts = cnt[0].astype(I32)
    padded = (counts + MOE_ROWS - 1) // MOE_ROWS * MOE_ROWS
    pad_end = jnp.cumsum(padded)
    pad_start = (pad_end - padded).astype(F32)[None, :]
    n_blocks = -(-(n_tok * TOP_K) // MOE_ROWS) + n_exp
    blk_expert = jnp.minimum(jnp.searchsorted(pad_end, jnp.arange(n_blocks, dtype=I32) * MOE_ROWS,
                                              side='right'), n_exp - 1).astype(I32)
    n_active = (pad_end[-1:] // MOE_ROWS).astype(I32)
    dest_p = _slots(e_p, r_p, pad_start, tm)[:, :, :TOP_K]
    dest_s = _slots(e_s, r_s, pad_start, bs)[:, :, :TOP_K]
    dest = jnp.concatenate([dest_p.reshape(-1, TOP_K), dest_s.reshape(-1, TOP_K)], axis=0)

    tile = _largest_tile(n_tok, 512)
    h_rows = jnp.concatenate([h2_p.reshape(-1, d), h2_s.reshape(-1, d)], axis=0).reshape(n_tok * chunks, LANES)
    xs = _dispatch(counts, pad_end.astype(I32), dest.reshape(n_tok // tile, 1, tile * TOP_K), h_rows,
                   tile, MOE_ROWS, chunks, n_blocks)
    ys = _experts(blk_expert, n_active, xs, w_exp_gate[0], w_exp_up[0], w_exp_down[0], MOE_ROWS, chunks)

    comb = functools.partial(_combine, wsg_bf=wsg_bf, wsu_bf=wsu_bf, wsd_bf=wsd_bf, gf=gf, ys=ys, chunks=chunks)
    y_prompt = comb(dest_p.reshape(bp * (t // tm), 1, tm * TOP_K), w_p, x1_p, h2_p, mp[5], tm=tm)
    y_sample = comb(dest_s.reshape(1, 1, bs * TOP_K), w_s, x1_s, h2_s, ms[5], tm=bs).reshape(bs, 1, d)

    win = min(WINDOW, t)
    hist = state_conv.shape[2]
    out_p = [_kv5(a) for a in (kc_p, vc_p, ks_p, vs_p, kw_p[:, t - win:], vw_p[:, t - win:])]
    conv_p = u_p[:, t - hist:][None]
    out_s = [_kv5(a[0][:, None, :]) for a in (kc_s, vc_s, ks_s, vs_s)]
    w_buf = state_k_win.shape[2]
    kw_buf = jnp.concatenate([state_k_win, _kv5(kw_s[0][:, None, :])], axis=2)[:, :, -w_buf:]
    vw_buf = jnp.concatenate([state_v_win, _kv5(vw_s[0][:, None, :])], axis=2)[:, :, -w_buf:]
    conv_s = up_s[:, -hist:][None]
    return (y_prompt, y_sample, *out_p, conv_p, *out_s, kw_buf, vw_buf, conv_s)
```

```python
import functools

import jax
import jax.numpy as jnp
from jax import lax
from jax.experimental import pallas as pl
from jax.experimental.pallas import tpu as pltpu

F32 = jnp.float32
BF16 = jnp.bfloat16
I32 = jnp.int32

N_HEADS = 8
HEAD_DIM = 64
N_KV = 2
Q_PER_KV = N_HEADS // N_KV
ATTN_WIDTH = N_HEADS * HEAD_DIM
KV_COLS = N_KV * HEAD_DIM
CMP_BLOCK = 32
SLC_BLOCK = 64
CMP_PER_SLC = SLC_BLOCK // CMP_BLOCK
N_SEL = 16
WINDOW = 512
TOP_K = 8
ROUTE_SCALE = 2.5
EPS = 1e-6
FORCED = 1e4
NEG = -1e30
ATTN_SCALE = HEAD_DIM ** -0.5

LANES = 128
SUBLANES = 8
VMEM_LIMIT = 56 * 1024 * 1024

ROW_TILE = 256
Q_TILE = 128
KEY_TILE = 512
CMP_ROW_TILE = 512
MOE_ROWS = 256
CONV_HALO = 32
PAGES_PER_STEP = 8


def _cparams(*sem):
    return pltpu.CompilerParams(dimension_semantics=sem, vmem_limit_bytes=VMEM_LIMIT)


def _dot(a, b):
    return jnp.dot(a, b, preferred_element_type=F32)


def _dot_nt(a, b):
    return lax.dot_general(a, b, (((1,), (1,)), ((), ())), preferred_element_type=F32)


def _split2(x):
    hi = x.astype(BF16)
    lo = (x - hi.astype(F32)).astype(BF16)
    return hi, lo


def _dot3(a, b):
    ah, al = _split2(a)
    bh, bl = _split2(b)
    return _dot(ah, bh) + (_dot(ah, bl) + _dot(al, bh))


def _dot3_nt(a, b):
    ah, al = _split2(a)
    bh, bl = _split2(b)
    return _dot_nt(ah, bh) + (_dot_nt(ah, bl) + _dot_nt(al, bh))


def _sigmoid(x):
    return 1.0 / (1.0 + jnp.exp(-x))


def _silu(x):
    return x * _sigmoid(x)


def _rms(x, g):
    return x * lax.rsqrt(jnp.mean(x * x, axis=-1, keepdims=True) + EPS) * g


def _alibi_slope_col(rows, rows_per_head, first_head, n_heads):
    r = lax.broadcasted_iota(I32, (rows, 1), 0) // rows_per_head
    out = jnp.zeros((rows, 1), F32)
    for k in range(n_heads):
        out = jnp.where(r == k, 2.0 ** (-8.0 * (first_head + k + 1) / N_HEADS), out)
    return out


def _softmax_rows(s, mask):
    m = jnp.max(s, axis=1, keepdims=True)
    p = jnp.where(mask, jnp.exp(s - m), 0.0)
    return p / jnp.maximum(jnp.sum(p, axis=1, keepdims=True), 1e-30)


def _modulation_kernel(c_ref, w_ref, b_ref, o_ref):
    o_ref[...] = _dot3(c_ref[...], w_ref[...]) + b_ref[...]


def _modulation(c, w, b):
    m, d = c.shape
    n = w.shape[1]
    tn = 768
    return pl.pallas_call(
        _modulation_kernel,
        grid=(n // tn,),
        in_specs=[pl.BlockSpec((m, d), lambda j: (0, 0)),
                  pl.BlockSpec((d, tn), lambda j: (0, j)),
                  pl.BlockSpec((1, tn), lambda j: (0, j))],
        out_specs=pl.BlockSpec((m, tn), lambda j: (0, j)),
        out_shape=jax.ShapeDtypeStruct((m, n), F32),
        compiler_params=_cparams("arbitrary"),
        name="modulation",
    )(c, w, b)


def _mod_spec(mod, tm, d):
    if mod.shape[1] == 1:
        return pl.BlockSpec((1, 1, d), lambda i, j: (i, 0, 0))
    return pl.BlockSpec((1, tm, d), lambda i, j: (i, j, 0))


def _in_proj_kernel(x_ref, shift_ref, scale_ref, g_ref, w_ref,
                    q_ref, kc_ref, vc_ref, ks_ref, vs_ref, kw_ref, vw_ref, gate_ref, u_ref):
    x = x_ref[0]
    h = _rms(x, g_ref[...]) * (1.0 + scale_ref[0]) + shift_ref[0]
    z = _dot(h.astype(BF16), w_ref[...])
    q_ref[0] = z[:, :ATTN_WIDTH]
    o = ATTN_WIDTH
    for ref in (kc_ref, vc_ref, ks_ref, vs_ref, kw_ref, vw_ref):
        ref[0] = z[:, o:o + KV_COLS]
        o += KV_COLS
    gate_ref[0] = _sigmoid(z[:, o:o + LANES])
    o += LANES
    cw = u_ref.shape[-1]
    u_ref[0] = z[:, o:o + cw] * _sigmoid(z[:, o + cw:o + 2 * cw])


def _in_proj(x, shift, scale, g, w_bf, tm):
    b, t, d = x.shape
    cw = (w_bf.shape[1] - ATTN_WIDTH - 6 * KV_COLS - LANES) // 2
    row = lambda n: pl.BlockSpec((1, tm, n), lambda i, j: (i, j, 0))
    sds = lambda n: jax.ShapeDtypeStruct((b, t, n), F32)
    return pl.pallas_call(
        _in_proj_kernel,
        grid=(b, t // tm),
        in_specs=[row(d), _mod_spec(shift, tm, d), _mod_spec(scale, tm, d),
                  pl.BlockSpec((1, d), lambda i, j: (0, 0)),
                  pl.BlockSpec(w_bf.shape, lambda i, j: (0, 0))],
        out_specs=[row(ATTN_WIDTH)] + [row(KV_COLS)] * 6 + [row(LANES), row(cw)],
        out_shape=[sds(ATTN_WIDTH)] + [sds(KV_COLS)] * 6 + [sds(LANES), sds(cw)],
        compiler_params=_cparams("arbitrary", "arbitrary"),
        name="in_proj",
    )(x, shift, scale, g, w_bf)


def _compress_kernel(k_ref, v_ref, wk_ref, wv_ref, ko_ref, vo_ref):
    ko_ref[...] = _dot3(k_ref[...], wk_ref[...])
    vo_ref[...] = _dot3(v_ref[...], wv_ref[...])


def _compress(k_rows, v_rows, wk, wv):
    r, kdim = k_rows.shape
    tr = min(CMP_ROW_TILE, r)
    assert r % tr == 0
    rows = pl.BlockSpec((tr, kdim), lambda i: (i, 0))
    wspec = pl.BlockSpec((kdim, KV_COLS), lambda i: (0, 0))
    ospec = pl.BlockSpec((tr, KV_COLS), lambda i: (i, 0))
    return pl.pallas_call(
        _compress_kernel,
        grid=(r // tr,),
        in_specs=[rows, rows, wspec, wspec],
        out_specs=[ospec, ospec],
        out_shape=[jax.ShapeDtypeStruct((r, KV_COLS), F32)] * 2,
        compiler_params=_cparams("arbitrary"),
        name="compress",
    )(k_rows, v_rows, wk, wv)


def _compress_weight(w):
    eye = jnp.eye(N_KV, dtype=w.dtype)
    big = jnp.einsum('lde,gh->lgdhe', w, eye)
    return big.reshape(CMP_BLOCK * KV_COLS, KV_COLS)


def _pair_sum(x):
    n = x.shape[-1]
    lane = lax.broadcasted_iota(I32, x.shape, x.ndim - 1)
    nxt = pltpu.roll(x, n - 1, x.ndim - 1)
    prv = pltpu.roll(x, 1, x.ndim - 1)
    return x + jnp.where((lane & 1) == 0, nxt, prv)


def _block_scores(imp, blk, q_pos, n_blocks_total):
    cur = q_pos // SLC_BLOCK
    valid = jnp.logical_and(blk * SLC_BLOCK <= q_pos, blk < n_blocks_total)
    forced = jnp.logical_or(blk == 0, jnp.logical_or(blk == cur, blk == cur - 1))
    return jnp.where(valid, jnp.where(forced, FORCED, imp), -1.0)


def _select_blocks(score, blk, n_sel):
    blk_f = blk.astype(F32)
    sel = jnp.zeros(score.shape, jnp.bool_)
    s = score
    picks = []
    for _ in range(n_sel):
        m = jnp.max(s, axis=1, keepdims=True)
        first = jnp.min(jnp.where(s == m, blk_f, 1e9), axis=1, keepdims=True)
        pick = blk_f == first
        picks.append((first, m))
        sel = jnp.logical_or(sel, pick)
        s = jnp.where(pick, -2.0, s)
    return jnp.logical_and(sel, score >= 0.0), picks


def _prompt_attn_kernel(q_ref, gate_ref, kc_ref, vc_ref, ks_ref, vs_ref, kw_ref, vw_ref, o_ref,
                        *, seq, n_sel):
    i = pl.program_id(1)
    tq = Q_TILE
    rows = Q_PER_KV * tq
    nc = kc_ref.shape[1]
    q_blk = q_ref[0] * ATTN_SCALE
    gates = gate_ref[0]
    q_pos_col = i * tq + lax.broadcasted_iota(I32, (tq, 1), 0)
    q_pos_rows = i * tq + (lax.broadcasted_iota(I32, (rows, 1), 0) & (tq - 1))
    q_pos_rows_f = q_pos_rows.astype(F32)
    span = WINDOW + tq
    w_start = pl.multiple_of(jnp.maximum(i * tq - WINDOW, 0), tq)
    pieces = []
    for g in range(N_KV):
        heads = [g * Q_PER_KV + r for r in range(Q_PER_KV)]
        qg = jnp.concatenate([q_blk[:, h * HEAD_DIM:(h + 1) * HEAD_DIM] for h in heads], axis=0)
        qg_bf = qg.astype(BF16)
        slope = _alibi_slope_col(rows, tq, g * Q_PER_KV, Q_PER_KV)
        gsl = slice(g * HEAD_DIM, (g + 1) * HEAD_DIM)

        kc = kc_ref[0][:, gsl]
        vc = vc_ref[0][:, gsl]
        cmp_pos = lax.broadcasted_iota(I32, (rows, nc), 1) * CMP_BLOCK + (CMP_BLOCK - 1)
        dist_c = (q_pos_rows - cmp_pos).astype(F32)
        mask_c = dist_c >= 0.0
        s_c = jnp.where(mask_c, _dot3_nt(qg, kc) - slope * dist_c, NEG)
        p_c = _softmax_rows(s_c, mask_c)
        o_cmp = _dot(p_c.astype(BF16), vc.astype(BF16))

        imp = p_c[0:tq]
        for r in range(1, Q_PER_KV):
            imp = imp + p_c[r * tq:(r + 1) * tq]
        blk = lax.broadcasted_iota(I32, (tq, nc), 1) >> 1
        score = _block_scores(_pair_sum(imp), blk, q_pos_col, seq // SLC_BLOCK)
        sel, _ = _select_blocks(score, blk, n_sel)
        sel_bf = jnp.where(sel, 1.0, 0.0).astype(BF16)

        n_tiles = ((i + 1) * tq + KEY_TILE - 1) // KEY_TILE

        def slc_step(t, carry, qg_bf=qg_bf, slope=slope, sel_bf=sel_bf, gsl=gsl):
            m_run, l_run, acc = carry
            k0 = pl.multiple_of(t * KEY_TILE, KEY_TILE)
            kt = ks_ref[0, pl.ds(k0, KEY_TILE), :][:, gsl]
            vt = vs_ref[0, pl.ds(k0, KEY_TILE), :][:, gsl]
            key_cmp = (k0 + lax.broadcasted_iota(I32, (nc, KEY_TILE), 1)) // CMP_BLOCK
            expand = jnp.where(key_cmp == lax.broadcasted_iota(I32, (nc, KEY_TILE), 0),
                               1.0, 0.0).astype(BF16)
            chosen = _dot(sel_bf, expand)
            chosen = jnp.concatenate([chosen] * Q_PER_KV, axis=0)
            dist = q_pos_rows_f - (k0 + lax.broadcasted_iota(I32, (rows, KEY_TILE), 1)).astype(F32)
            mask = jnp.logical_and(chosen > 0.5, dist >= 0.0)
            s = jnp.where(mask, _dot_nt(qg_bf, kt.astype(BF16)) - slope * dist, NEG)
            m_new = jnp.maximum(m_run, jnp.max(s, axis=1, keepdims=True))
            alpha = jnp.exp(m_run - m_new)
            p = jnp.where(mask, jnp.exp(s - m_new), 0.0)
            l_new = alpha * l_run + jnp.sum(p, axis=1, keepdims=True)
            acc_new = alpha * acc + _dot(p.astype(BF16), vt.astype(BF16))
            return m_new, l_new, acc_new

        init = (jnp.full((rows, 1), NEG, F32), jnp.zeros((rows, 1), F32),
                jnp.zeros((rows, HEAD_DIM), F32))
        _, l_s, acc_s = lax.fori_loop(0, n_tiles, slc_step, init)
        o_slc = acc_s / jnp.maximum(l_s, 1e-30)

        kw = kw_ref[0, pl.ds(w_start, span), :][:, gsl]
        vw = vw_ref[0, pl.ds(w_start, span), :][:, gsl]
        dist_w = q_pos_rows_f - (w_start + lax.broadcasted_iota(I32, (rows, span), 1)).astype(F32)
        mask_w = jnp.logical_and(dist_w >= 0.0, dist_w <= float(WINDOW))
        s_w = jnp.where(mask_w, _dot_nt(qg_bf, kw.astype(BF16)) - slope * dist_w, NEG)
        p_w = _softmax_rows(s_w, mask_w)
        o_win = _dot(p_w.astype(BF16), vw.astype(BF16))

        for r, h in enumerate(heads):
            rs = slice(r * tq, (r + 1) * tq)
            g0 = gates[:, 3 * h + 0:3 * h + 1]
            g1 = gates[:, 3 * h + 1:3 * h + 2]
            g2 = gates[:, 3 * h + 2:3 * h + 3]
            pieces.append(o_cmp[rs] * g0 + o_slc[rs] * g1 + o_win[rs] * g2)
    o_ref[0] = jnp.concatenate(pieces, axis=1)


def _prompt_attention(q, gates, kc, vc, ks, vs, kw, vw):
    b, t, _ = q.shape
    nc = kc.shape[1]
    assert t % KEY_TILE == 0 and t >= WINDOW + Q_TILE
    n_sel = min(N_SEL, t // SLC_BLOCK)
    qspec = lambda n: pl.BlockSpec((1, Q_TILE, n), lambda bi, i: (bi, i, 0))
    full = lambda r: pl.BlockSpec((1, r, KV_COLS), lambda bi, i: (bi, 0, 0))
    return pl.pallas_call(
        functools.partial(_prompt_attn_kernel, seq=t, n_sel=n_sel),
        grid=(b, t // Q_TILE),
        in_specs=[qspec(ATTN_WIDTH), qspec(LANES), full(nc), full(nc),
                  full(t), full(t), full(t), full(t)],
        out_specs=qspec(ATTN_WIDTH),
        out_shape=jax.ShapeDtypeStruct((b, t, ATTN_WIDTH), F32),
        compiler_params=_cparams("arbitrary", "arbitrary"),
        name="prompt_attention",
    )(q, gates, kc, vc, ks, vs, kw, vw)


def _merge_groups(per_group):
    row = lax.broadcasted_iota(I32, per_group[0].shape, 0) // Q_PER_KV
    out = per_group[0]
    for g in range(1, N_KV):
        out = jnp.where(row == g, per_group[g], out)
    return out


def _group_slice(x, g):
    return x[:, g * HEAD_DIM:(g + 1) * HEAD_DIM]


def _matmul3_kernel(a_ref, b_ref, o_ref):
    o_ref[...] = _dot3(a_ref[...], b_ref[...])


def _matmul3(a, b):
    return pl.pallas_call(
        _matmul3_kernel,
        out_shape=jax.ShapeDtypeStruct((a.shape[0], b.shape[1]), F32),
        compiler_params=pltpu.CompilerParams(vmem_limit_bytes=VMEM_LIMIT),
        name="matmul3",
    )(a, b)


def _page_specs(n_pages, page):
    def spec(o):
        return pl.BlockSpec((1, N_KV, HEAD_DIM, page),
                            lambda i, j, pt: (pt[i * n_pages + j * PAGES_PER_STEP + o], 0, 0, 0))
    return [spec(o) for o in range(PAGES_PER_STEP)]


def _sample_scores_kernel(pt_ref, ut_ref, *refs):
    k_refs, o_ref = refs[:-1], refs[-1]
    for h in range(N_HEADS):
        g = h // Q_PER_KV
        u = ut_ref[0, h]
        rows = [jnp.sum(k_ref[0, g] * u, axis=0, keepdims=True) for k_ref in k_refs]
        o_ref[0, h] = jnp.concatenate(rows, axis=0)


def _sample_scores(page_table, ut, k_pages):
    b, n_pages = page_table.shape
    page = k_pages.shape[-1]
    assert n_pages % PAGES_PER_STEP == 0
    return pl.pallas_call(
        _sample_scores_kernel,
        grid_spec=pltpu.PrefetchScalarGridSpec(
            num_scalar_prefetch=1,
            grid=(b, n_pages // PAGES_PER_STEP),
            in_specs=[pl.BlockSpec((1, N_HEADS, HEAD_DIM, page), lambda i, j, pt: (i, 0, 0, 0))]
                     + _page_specs(n_pages, page),
            out_specs=pl.BlockSpec((1, N_HEADS, PAGES_PER_STEP, page), lambda i, j, pt: (i, 0, j, 0)),
        ),
        out_shape=jax.ShapeDtypeStruct((b, N_HEADS, n_pages, page), F32),
        compiler_params=_cparams("arbitrary", "arbitrary"),
        name="sample_scores",
    )(page_table.reshape(-1), ut, *([k_pages] * PAGES_PER_STEP))


def _max_all(x):
    return jnp.max(jnp.max(x, axis=0, keepdims=True), axis=1, keepdims=True)


def _min_all(x):
    return jnp.min(jnp.min(x, axis=0, keepdims=True), axis=1, keepdims=True)


def _sum_all(x):
    return jnp.sum(jnp.sum(x, axis=0, keepdims=True), axis=1, keepdims=True)


def _sample_select_kernel(s_ref, q_ref, kcn_ref, pexp_ref, pnew_ref, sel_ref, *, past, n_new, n_sel):
    n_pages, page = s_ref.shape[2], s_ref.shape[3]
    cpp = page // CMP_BLOCK
    n_past = n_pages * cpp
    n_blocks_total = (n_past + n_new) // CMP_PER_SLC
    lane = lax.broadcasted_iota(I32, (n_pages, page), 1)
    prow = lax.broadcasted_iota(I32, (n_pages, page), 0)
    dist = (past - ((prow * cpp + lane // CMP_BLOCK) * CMP_BLOCK + (CMP_BLOCK - 1))).astype(F32)
    mask = jnp.logical_and(lane % CMP_BLOCK == 0, dist >= 0.0)

    q8 = q_ref[0] * ATTN_SCALE
    slope = _alibi_slope_col(N_HEADS, 1, 0, N_HEADS)
    kcn = jnp.concatenate([kcn_ref[0], jnp.zeros((LANES - n_new, KV_COLS), F32)], axis=0)
    new_lane = lax.broadcasted_iota(I32, (N_HEADS, LANES), 1)
    dist_n = (past - ((n_past + new_lane) * CMP_BLOCK + (CMP_BLOCK - 1))).astype(F32)
    mask_n = jnp.logical_and(dist_n >= 0.0, new_lane < n_new)
    qk_n = _merge_groups([_dot3_nt(q8, _group_slice(kcn, g)) for g in range(N_KV)])
    s_new = jnp.where(mask_n, qk_n - slope * dist_n, NEG)

    probs, probs_new = [], []
    for h in range(N_HEADS):
        x = s_ref[0, h]
        for sh in (16, 8, 4, 2, 1):
            x = x + pltpu.roll(x, page - sh, 1)
        s = jnp.where(mask, x - 2.0 ** (-8.0 * (h + 1) / N_HEADS) * dist, NEG)
        sn = s_new[h:h + 1, :]
        mn = jnp.logical_and(dist_n[h:h + 1, :] >= 0.0, new_lane[h:h + 1, :] < n_new)
        m = jnp.maximum(_max_all(s), jnp.max(sn, axis=1, keepdims=True))
        p = jnp.where(mask, jnp.exp(s - m), 0.0)
        pn = jnp.where(mn, jnp.exp(sn - m), 0.0)
        den = jnp.maximum(_sum_all(p) + jnp.sum(pn, axis=1, keepdims=True), 1e-30)
        p = p / den
        probs.append(p)
        probs_new.append(pn / den)
        z = p
        for sh in (1, 2, 4, 8, 16):
            z = z + pltpu.roll(z, sh, 1)
        pexp_ref[0, h] = z
    pnew_ref[0] = jnp.concatenate(probs_new, axis=0)

    row1 = lax.broadcasted_iota(I32, (1, LANES), 1)
    blk = jnp.where(lane % SLC_BLOCK == 0, prow * (page // SLC_BLOCK) + lane // SLC_BLOCK, -1)
    blk_n = jnp.where(row1 < n_new, n_past // CMP_PER_SLC + (row1 >> 1), -1)
    blk_f = blk.astype(F32)
    blk_nf = blk_n.astype(F32)
    out_lane = lax.broadcasted_iota(I32, (N_HEADS, LANES), 1)
    out_row = lax.broadcasted_iota(I32, (N_HEADS, LANES), 0)
    out = jnp.full((N_HEADS, LANES), -1, I32)
    for g in range(N_KV):
        imp = probs[g * Q_PER_KV]
        imp_n = probs_new[g * Q_PER_KV]
        for r in range(1, Q_PER_KV):
            imp = imp + probs[g * Q_PER_KV + r]
            imp_n = imp_n + probs_new[g * Q_PER_KV + r]
        imp = imp + pltpu.roll(imp, page - CMP_BLOCK, 1)
        s_m = jnp.where(blk >= 0, _block_scores(imp, blk, past, n_blocks_total), -4.0)
        s_n = jnp.where(blk_n >= 0, _block_scores(_pair_sum(imp_n), blk_n, past, n_blocks_total), -4.0)
        for j in range(n_sel):
            top = jnp.maximum(_max_all(s_m), jnp.max(s_n, axis=1, keepdims=True))
            first = jnp.minimum(_min_all(jnp.where(s_m == top, blk_f, 1e9)),
                                jnp.min(jnp.where(s_n == top, blk_nf, 1e9), axis=1, keepdims=True))
            s_m = jnp.where(blk_f == first, -2.0, s_m)
            s_n = jnp.where(blk_nf == first, -2.0, s_n)
            pick = jnp.where(top >= 0.0, first.astype(I32), -1)
            out = jnp.where(jnp.logical_and(out_row == g, out_lane == j), pick, out)
    sel_ref[0] = out


def _sample_select(s_raw, q8, kc_new, past):
    b, _, n_pages, page = s_raw.shape
    n_new = kc_new.shape[1]
    assert CMP_PER_SLC == 2 and CMP_BLOCK == 32
    n_sel = min(N_SEL, (past // CMP_BLOCK + n_new) // CMP_PER_SLC)
    per_b = lambda *s: pl.BlockSpec((1,) + s, lambda i: (i,) + (0,) * len(s))
    return pl.pallas_call(
        functools.partial(_sample_select_kernel, past=past, n_new=n_new, n_sel=n_sel),
        grid=(b,),
        in_specs=[per_b(N_HEADS, n_pages, page), per_b(N_HEADS, HEAD_DIM), per_b(n_new, KV_COLS)],
        out_specs=[per_b(N_HEADS, n_pages, page), per_b(N_HEADS, LANES), per_b(N_HEADS, LANES)],
        out_shape=[jax.ShapeDtypeStruct((b, N_HEADS, n_pages, page), F32),
                   jax.ShapeDtypeStruct((b, N_HEADS, LANES), F32),
                   jax.ShapeDtypeStruct((b, N_HEADS, LANES), I32)],
        compiler_params=_cparams("arbitrary"),
        name="sample_select",
    )(s_raw, q8, kc_new)


def _sample_values_kernel(pt_ref, pe_ref, *refs):
    v_refs, y_ref = refs[:-1], refs[-1]

    @pl.when(pl.program_id(1) == 0)
    def _():
        y_ref[...] = jnp.zeros(y_ref.shape, F32)

    for g in range(N_KV):
        heads = range(g * Q_PER_KV, (g + 1) * Q_PER_KV)
        pe = [pe_ref[0, h] for h in heads]
        acc = [jnp.zeros(y_ref.shape[2:], F32) for _ in heads]
        for o, v_ref in enumerate(v_refs):
            v = v_ref[0, g]
            for r in range(Q_PER_KV):
                acc[r] = acc[r] + v * pe[r][o:o + 1, :]
        for r, h in enumerate(heads):
            y_ref[0, h] = y_ref[0, h] + acc[r]


def _sample_values(page_table, pexp, v_pages):
    b, n_pages = page_table.shape
    page = v_pages.shape[-1]
    return pl.pallas_call(
        _sample_values_kernel,
        grid_spec=pltpu.PrefetchScalarGridSpec(
            num_scalar_prefetch=1,
            grid=(b, n_pages // PAGES_PER_STEP),
            in_specs=[pl.BlockSpec((1, N_HEADS, PAGES_PER_STEP, page), lambda i, j, pt: (i, 0, j, 0))]
                     + _page_specs(n_pages, page),
            out_specs=pl.BlockSpec((1, N_HEADS, HEAD_DIM, page), lambda i, j, pt: (i, 0, 0, 0)),
        ),
        out_shape=jax.ShapeDtypeStruct((b, N_HEADS, HEAD_DIM, page), F32),
        compiler_params=_cparams("arbitrary", "arbitrary"),
        name="sample_values",
    )(page_table.reshape(-1), pexp, *([v_pages] * PAGES_PER_STEP))


def _new_token_terms(q8, k_row, v_row):
    s = _merge_groups([jnp.sum(q8 * _group_slice(k_row, g), axis=1, keepdims=True) for g in range(N_KV)])
    v = _merge_groups([jnp.broadcast_to(_group_slice(v_row, g), (N_HEADS, HEAD_DIM)) for g in range(N_KV)])
    return s, v


def _sample_attend_kernel(sel_ref, pt_ref, q_ref, k0_ref, k1_ref, v0_ref, v1_ref, ksn_ref, vsn_ref,
                          kw_ref, vw_ref, kwn_ref, vwn_ref, gate_ref, ocmp_ref, pnew_ref, vcn_ref, o_ref,
                          m_sc, l_sc, acc_sc, *, past, n_sel, ns_past):
    b = pl.program_id(0)
    n = pl.program_id(1)
    page = k0_ref.shape[-1]
    spp = page // SLC_BLOCK
    q8 = q_ref[0] * ATTN_SCALE
    q8_bf = q8.astype(BF16)
    slope = _alibi_slope_col(N_HEADS, 1, 0, N_HEADS)

    @pl.when(n == 0)
    def _():
        m_sc[...] = jnp.full(m_sc.shape, NEG, F32)
        l_sc[...] = jnp.zeros(l_sc.shape, F32)
        acc_sc[...] = jnp.zeros(acc_sc.shape, F32)

    lane = lax.broadcasted_iota(I32, (N_HEADS, page), 1)
    qk_g, blk_g, blk_c = [], [], []
    for g, k_ref in enumerate((k0_ref, k1_ref)):
        blk = sel_ref[(b * N_KV + g) * n_sel + n]
        qk_g.append(_dot(q8_bf, k_ref[0, 0].astype(BF16)))
        blk_g.append(jnp.full((N_HEADS, page), blk, I32))
        blk_c.append(jnp.full((N_HEADS, 1), blk, I32))
    blk_rows = _merge_groups(blk_g)
    blk_col = _merge_groups(blk_c)
    page_pos = blk_rows // spp
    dist = (past - (page_pos * page + lane)).astype(F32)
    in_block = (lane // SLC_BLOCK) == (blk_rows - page_pos * spp)
    cached = jnp.logical_and(blk_rows >= 0, blk_rows < ns_past)
    mask = jnp.logical_and(jnp.logical_and(in_block, cached), dist >= 0.0)
    s = jnp.where(mask, _merge_groups(qk_g) - slope * dist, NEG)
    is_new = blk_col >= ns_past
    s_n, v_n = _new_token_terms(q8, ksn_ref[0], vsn_ref[0])
    s_n = jnp.where(is_new, s_n, NEG)
    m_old = m_sc[...]
    m_new = jnp.maximum(m_old, jnp.maximum(jnp.max(s, axis=1, keepdims=True), s_n))
    alpha = jnp.exp(m_old - m_new)
    p = jnp.where(mask, jnp.exp(s - m_new), 0.0)
    p_n = jnp.where(is_new, jnp.exp(s_n - m_new), 0.0)
    p_bf = p.astype(BF16)
    pv = _merge_groups([_dot_nt(p_bf, v_ref[0, 0].astype(BF16)) for v_ref in (v0_ref, v1_ref)])
    l_sc[...] = alpha * l_sc[...] + jnp.sum(p, axis=1, keepdims=True) + p_n
    acc_sc[...] = alpha * acc_sc[...] + pv + p_n * v_n
    m_sc[...] = m_new

    @pl.when(n == n_sel - 1)
    def _():
        o_slc = acc_sc[...] / jnp.maximum(l_sc[...], 1e-30)
        w_buf = kw_ref.shape[-1]
        wl = lax.broadcasted_iota(I32, (N_HEADS, w_buf), 1)
        win_pos = past - w_buf + wl
        dist_w = (past - win_pos).astype(F32)
        mask_w = jnp.logical_and(jnp.logical_and(dist_w >= 0.0, dist_w <= float(WINDOW)), win_pos >= 0)
        qk_w = _merge_groups([_dot(q8_bf, kw_ref[0, g].astype(BF16)) for g in range(N_KV)])
        s_w = jnp.where(mask_w, qk_w - slope * dist_w, NEG)
        s_t, v_t = _new_token_terms(q8, kwn_ref[0], vwn_ref[0])
        m_w = jnp.maximum(jnp.max(s_w, axis=1, keepdims=True), s_t)
        p_w = jnp.where(mask_w, jnp.exp(s_w - m_w), 0.0)
        p_t = jnp.exp(s_t - m_w)
        den = jnp.maximum(jnp.sum(p_w, axis=1, keepdims=True) + p_t, 1e-30)
        pw_bf = p_w.astype(BF16)
        o_w = _merge_groups([_dot_nt(pw_bf, vw_ref[0, g].astype(BF16)) for g in range(N_KV)])
        o_win = (o_w + p_t * v_t) / den
        n_new = vcn_ref.shape[1]
        vcn = jnp.concatenate([vcn_ref[0], jnp.zeros((LANES - n_new, KV_COLS), F32)], axis=0).astype(BF16)
        pn_bf = pnew_ref[0].astype(BF16)
        o_cmp = ocmp_ref[0] + _merge_groups([_dot(pn_bf, _group_slice(vcn, g)) for g in range(N_KV)])
        gt = gate_ref[0]
        o_ref[0] = o_cmp * gt[:, 0:1] + o_slc * gt[:, 1:2] + o_win * gt[:, 2:3]


def _sample_attend(sel, page_table, q8, k_pages, v_pages, ks_new, vs_new, kw_state, vw_state,
                   kw_new, vw_new, gates8, o_cmp, p_new, vc_new, past):
    b, n_pages = page_table.shape
    n_sel = sel.shape[-1]
    page = k_pages.shape[-1]
    ns_past = past // SLC_BLOCK
    spp = page // SLC_BLOCK
    w_buf = kw_state.shape[-1]
    n_new = vc_new.shape[1]

    def cache_map(g):
        def index(i, n, sel_ref, pt_ref):
            blk = jnp.clip(sel_ref[(i * N_KV + g) * n_sel + n], 0, ns_past - 1)
            return (pt_ref[i * n_pages + blk // spp], g, 0, 0)
        return pl.BlockSpec((1, 1, HEAD_DIM, page), index)

    per_b = lambda *s: pl.BlockSpec((1,) + s, lambda i, n, sl, pt: (i,) + (0,) * len(s))
    return pl.pallas_call(
        functools.partial(_sample_attend_kernel, past=past, n_sel=n_sel, ns_past=ns_past),
        grid_spec=pltpu.PrefetchScalarGridSpec(
            num_scalar_prefetch=2,
            grid=(b, n_sel),
            in_specs=[per_b(N_HEADS, HEAD_DIM), cache_map(0), cache_map(1), cache_map(0), cache_map(1),
                      per_b(1, KV_COLS), per_b(1, KV_COLS),
                      per_b(N_KV, HEAD_DIM, w_buf), per_b(N_KV, HEAD_DIM, w_buf),
                      per_b(1, KV_COLS), per_b(1, KV_COLS),
                      per_b(N_HEADS, 3), per_b(N_HEADS, HEAD_DIM), per_b(N_HEADS, LANES),
                      per_b(n_new, KV_COLS)],
            out_specs=per_b(N_HEADS, HEAD_DIM),
            scratch_shapes=[pltpu.VMEM((N_HEADS, 1), F32), pltpu.VMEM((N_HEADS, 1), F32),
                            pltpu.VMEM((N_HEADS, HEAD_DIM), F32)],
        ),
        out_shape=jax.ShapeDtypeStruct((b, N_HEADS, HEAD_DIM), F32),
        compiler_params=_cparams("arbitrary", "arbitrary"),
        name="sample_attend",
    )(sel.reshape(-1), page_table.reshape(-1), q8, k_pages, k_pages, v_pages, v_pages,
      ks_new, vs_new, kw_state, vw_state, kw_new, vw_new, gates8, o_cmp, p_new, vc_new)


def _layernorm_silu(y, g, b):
    mu = jnp.mean(y, axis=-1, keepdims=True)
    var = jnp.mean(jnp.square(y - mu), axis=-1, keepdims=True)
    return _silu((y - mu) * lax.rsqrt(var + EPS) * g + b)


def _conv_prompt_kernel(u_ref, w_ref, b_ref, g_ref, beta_ref, o_ref, buf):
    j = pl.program_id(1)
    tt = u_ref.shape[1]
    kw = w_ref.shape[0]

    @pl.when(j == 0)
    def _():
        buf[0:CONV_HALO, :] = jnp.zeros((CONV_HALO, buf.shape[1]), F32)

    buf[CONV_HALO:CONV_HALO + tt, :] = u_ref[0]
    w = w_ref[...]
    acc = jnp.zeros((tt, buf.shape[1]), F32)
    for k in range(kw):
        acc = acc + w[k:k + 1, :] * buf[pl.ds(CONV_HALO - (kw - 1) + k, tt), :]
    o_ref[0] = _layernorm_silu(acc + b_ref[...], g_ref[...], beta_ref[...])
    buf[0:CONV_HALO, :] = buf[tt:tt + CONV_HALO, :]


def _conv_prompt(u, w_dw, b_dw, ln_g, ln_b, tt):
    b, t, c = u.shape
    vec = pl.BlockSpec((1, c), lambda i, j: (0, 0))
    return pl.pallas_call(
        _conv_prompt_kernel,
        grid=(b, t // tt),
        in_specs=[pl.BlockSpec((1, tt, c), lambda i, j: (i, j, 0)),
                  pl.BlockSpec(w_dw.shape, lambda i, j: (0, 0)), vec, vec, vec],
        out_specs=pl.BlockSpec((1, tt, c), lambda i, j: (i, j, 0)),
        out_shape=jax.ShapeDtypeStruct((b, t, c), F32),
        scratch_shapes=[pltpu.VMEM((CONV_HALO + tt, c), F32)],
        compiler_params=_cparams("arbitrary", "arbitrary"),
        name="conv_prompt",
    )(u, w_dw, b_dw, ln_g, ln_b)


def _conv_sample_kernel(up_ref, w_ref, b_ref, g_ref, beta_ref, o_ref):
    y = jnp.sum(up_ref[...] * w_ref[...][None, :, :], axis=1)
    o_ref[...] = _layernorm_silu(y + b_ref[...], g_ref[...], beta_ref[...])


def _conv_sample(up, w_dw, b_dw, ln_g, ln_b):
    b, kw, c = up.shape
    return pl.pallas_call(
        _conv_sample_kernel,
        out_shape=jax.ShapeDtypeStruct((b, c), F32),
        name="conv_sample",
    )(up, w_dw, b_dw, ln_g, ln_b)


def _merge_router_kernel(oa_ref, oc_ref, x_ref, gate_ref, shift_ref, scale_ref, goa_ref, goc_ref,
                         wout_ref, g2_ref, rwh_ref, rwl_ref, rb_ref, cnt_in_ref,
                         x1_ref, h2_ref, eidx_ref, wts_ref, rank_ref, cnt_out_ref, run):
    first = jnp.logical_and(pl.program_id(0) == 0, pl.program_id(1) == 0)

    @pl.when(first)
    def _():
        run[...] = cnt_in_ref[...]

    a = _rms(oa_ref[0], goa_ref[...])
    c = _rms(oc_ref[0], goc_ref[...])
    cat = jnp.concatenate([a, c], axis=1).astype(BF16)
    x1 = x_ref[0] + gate_ref[0] * _dot(cat, wout_ref[...])
    x1_ref[0] = x1
    h2 = _rms(x1, g2_ref[...]) * (1.0 + scale_ref[0]) + shift_ref[0]
    h2_ref[0] = h2

    hh, hl = _split2(h2)
    logits = _dot(hh, rwh_ref[...]) + (_dot(hh, rwl_ref[...]) + _dot(hl, rwh_ref[...]))
    aff = _sigmoid(logits)
    tm, n_exp = aff.shape
    lane_f = lax.broadcasted_iota(I32, (tm, n_exp), 1).astype(F32)
    s = aff + rb_ref[...]
    hot = jnp.zeros((tm, n_exp), jnp.bool_)
    experts, weights = [], []
    for _ in range(TOP_K):
        m = jnp.max(s, axis=1, keepdims=True)
        e = jnp.min(jnp.where(s == m, lane_f, 1e9), axis=1, keepdims=True)
        pick = lane_f == e
        experts.append(e)
        weights.append(jnp.sum(jnp.where(pick, aff, 0.0), axis=1, keepdims=True))
        hot = jnp.logical_or(hot, pick)
        s = jnp.where(pick, NEG, s)
    total = weights[0]
    for w in weights[1:]:
        total = total + w

    hot_f = jnp.where(hot, 1.0, 0.0)
    r_i = lax.broadcasted_iota(I32, (tm, tm), 0)
    c_i = lax.broadcasted_iota(I32, (tm, tm), 1)
    lower = jnp.where(c_i < r_i, 1.0, 0.0).astype(BF16)
    before = _dot(lower, hot_f.astype(BF16)) + run[...]
    out_lane = lax.broadcasted_iota(I32, (tm, LANES), 1)
    e_out = jnp.zeros((tm, LANES), I32)
    w_out = jnp.zeros((tm, LANES), F32)
    r_out = jnp.zeros((tm, LANES), I32)
    for k in range(TOP_K):
        rank = jnp.sum(jnp.where(lane_f == experts[k], before, 0.0), axis=1, keepdims=True)
        e_out = jnp.where(out_lane == k, experts[k].astype(I32), e_out)
        w_out = jnp.where(out_lane == k, ROUTE_SCALE * weights[k] / total, w_out)
        r_out = jnp.where(out_lane == k, rank.astype(I32), r_out)
    eidx_ref[0] = e_out
    wts_ref[0] = w_out
    rank_ref[0] = r_out
    run[...] = run[...] + jnp.sum(hot_f, axis=0, keepdims=True)
    cnt_out_ref[...] = run[...]


def _merge_router(o_attn, o_conv, x, gate, shift, scale, goa, goc, wout_bf, g2, rw_hi, rw_lo, rb,
                  cnt_in, tm):
    b, t, d = x.shape
    n_exp = rw_hi.shape[1]
    row = lambda n: pl.BlockSpec((1, tm, n), lambda i, j: (i, j, 0))
    const = lambda shape: pl.BlockSpec(shape, lambda i, j: (0,) * len(shape))
    sds = lambda n, dt: jax.ShapeDtypeStruct((b, t, n), dt)
    return pl.pallas_call(
        _merge_router_kernel,
        grid=(b, t // tm),
        in_specs=[row(o_attn.shape[-1]), row(o_conv.shape[-1]), row(d),
                  _mod_spec(gate, tm, d), _mod_spec(shift, tm, d), _mod_spec(scale, tm, d),
                  const(goa.shape), const(goc.shape), const(wout_bf.shape), const(g2.shape),
                  const(rw_hi.shape), const(rw_lo.shape), const(rb.shape), const(cnt_in.shape)],
        out_specs=[row(d), row(d), row(LANES), row(LANES), row(LANES), const((1, n_exp))],
        out_shape=[sds(d, F32), sds(d, F32), sds(LANES, I32), sds(LANES, F32), sds(LANES, I32),
                   jax.ShapeDtypeStruct((1, n_exp), F32)],
        scratch_shapes=[pltpu.VMEM((1, n_exp), F32)],
        compiler_params=_cparams("arbitrary", "arbitrary"),
        name="merge_router",
    )(o_attn, o_conv, x, gate, shift, scale, goa, goc, wout_bf, g2, rw_hi, rw_lo, rb, cnt_in)


def _row_copy(src_hbm, dst, src_row, dst_row, sem, chunks):
    return pltpu.make_async_copy(src_hbm.at[pl.ds(src_row * chunks, chunks)],
                                 dst.at[pl.ds(dst_row * chunks, chunks)], sem)


def _slots_kernel(e_ref, r_ref, start_ref, o_ref):
    e = e_ref[0]
    r = r_ref[0]
    start = start_ref[...]
    tm = e.shape[0]
    lane_e = lax.broadcasted_iota(I32, (tm, start.shape[1]), 1)
    out_lane = lax.broadcasted_iota(I32, (tm, LANES), 1)
    out = jnp.zeros((tm, LANES), I32)
    for k in range(TOP_K):
        base = jnp.sum(jnp.where(lane_e == e[:, k:k + 1], start, 0.0), axis=1, keepdims=True)
        out = jnp.where(out_lane == k, base.astype(I32) + r[:, k:k + 1], out)
    o_ref[0] = out


def _slots(e_idx, rank, start, tm):
    b, t, _ = e_idx.shape
    row = pl.BlockSpec((1, tm, LANES), lambda i, j: (i, j, 0))
    return pl.pallas_call(
        _slots_kernel,
        grid=(b, t // tm),
        in_specs=[row, row, pl.BlockSpec(start.shape, lambda i, j: (0, 0))],
        out_specs=row,
        out_shape=jax.ShapeDtypeStruct((b, t, LANES), I32),
        compiler_params=_cparams("arbitrary", "arbitrary"),
        name="moe_slots",
    )(e_idx, rank, start)


def _dispatch_kernel(cnt_ref, end_ref, dest_ref, h_ref, xs_hbm, zbuf, zsem, sem,
                     *, tokens, rows, chunks, n_blocks):
    j = pl.program_id(0)
    n_exp = cnt_ref.shape[0]
    blk_rows = rows * chunks

    def zero_block(blk):
        return pltpu.make_async_copy(zbuf, xs_hbm.at[pl.ds(blk * blk_rows, blk_rows)], zsem)

    @pl.when(j == 0)
    def _():
        zbuf[...] = jnp.zeros(zbuf.shape, F32)
        n_active = end_ref[n_exp - 1] // rows

        def zero_tail(e, issued):
            partial = cnt_ref[e] % rows != 0

            @pl.when(partial)
            def _():
                zero_block(end_ref[e] // rows - 1).start()

            return issued + partial.astype(I32)

        def zero_unused(blk, _):
            zero_block(blk).start()
            return 0

        def drain_zero(_, c):
            zero_block(0).wait()
            return c

        issued = lax.fori_loop(0, n_exp, zero_tail, 0)
        lax.fori_loop(n_active, n_blocks, zero_unused, 0)
        lax.fori_loop(0, issued + (n_blocks - n_active), drain_zero, 0)

    def issue(r, _):
        for k in range(TOP_K):
            _row_copy(h_ref, xs_hbm, r, dest_ref[0, 0, r * TOP_K + k], sem, chunks).start()
        return 0

    def drain(r, _):
        for k in range(TOP_K):
            _row_copy(h_ref, xs_hbm, 0, 0, sem, chunks).wait()
        return 0

    lax.fori_loop(0, tokens, issue, 0)
    lax.fori_loop(0, tokens, drain, 0)


def _dispatch(counts, pad_end, dest, h_rows, tokens, rows, chunks, n_blocks):
    n_tiles = dest.shape[0]
    return pl.pallas_call(
        functools.partial(_dispatch_kernel, tokens=tokens, rows=rows, chunks=chunks, n_blocks=n_blocks),
        grid_spec=pltpu.PrefetchScalarGridSpec(
            num_scalar_prefetch=2,
            grid=(n_tiles,),
            in_specs=[pl.BlockSpec((1, 1, tokens * TOP_K), lambda j, c, e: (j, 0, 0),
                                   memory_space=pltpu.SMEM),
                      pl.BlockSpec((tokens * chunks, LANES), lambda j, c, e: (j, 0))],
            out_specs=pl.BlockSpec(memory_space=pl.ANY),
            scratch_shapes=[pltpu.VMEM((rows * chunks, LANES), F32),
                            pltpu.SemaphoreType.DMA(()), pltpu.SemaphoreType.DMA(())],
        ),
        out_shape=jax.ShapeDtypeStruct((n_blocks * rows * chunks, LANES), F32),
        compiler_params=_cparams("arbitrary"),
        name="moe_dispatch",
    )(counts, pad_end, dest, h_rows)


def _expert_kernel(be_ref, nact_ref, x_ref, wg_ref, wu_ref, wd_ref, y_ref, *, rows, chunks):
    j = pl.program_id(0)

    @pl.when(j < nact_ref[0])
    def _():
        f = wg_ref.shape[2]
        gate = jnp.zeros((rows, f), F32)
        up = jnp.zeros((rows, f), F32)
        for c in range(chunks):
            xc = x_ref[pl.ds(c, rows, stride=chunks), :].astype(BF16)
            cs = slice(c * LANES, (c + 1) * LANES)
            gate = gate + _dot(xc, wg_ref[0, cs, :].astype(BF16))
            up = up + _dot(xc, wu_ref[0, cs, :].astype(BF16))
        h = (_silu(gate) * up).astype(BF16)
        y = _dot(h, wd_ref[0].astype(BF16))
        for c in range(chunks):
            y_ref[pl.ds(c, rows, stride=chunks), :] = y[:, c * LANES:(c + 1) * LANES]

    @pl.when(j >= nact_ref[0])
    def _():
        y_ref[...] = jnp.zeros(y_ref.shape, F32)


def _experts(blk_expert, n_active, xs, wg, wu, wd, rows, chunks):
    n_blocks = blk_expert.shape[0]
    d, f = wg.shape[1], wg.shape[2]
    last = lambda j, na: jnp.minimum(j, na[0] - 1)
    xspec = pl.BlockSpec((rows * chunks, LANES), lambda j, be, na: (last(j, na), 0))
    wspec = lambda s: pl.BlockSpec((1,) + s, lambda j, be, na: (be[last(j, na)], 0, 0))
    return pl.pallas_call(
        functools.partial(_expert_kernel, rows=rows, chunks=chunks),
        grid_spec=pltpu.PrefetchScalarGridSpec(
            num_scalar_prefetch=2,
            grid=(n_blocks,),
            in_specs=[xspec, wspec((d, f)), wspec((d, f)), wspec((f, d))],
            out_specs=pl.BlockSpec((rows * chunks, LANES), lambda j, be, na: (j, 0)),
        ),
        out_shape=jax.ShapeDtypeStruct(xs.shape, F32),
        compiler_params=_cparams("arbitrary"),
        name="moe_experts",
    )(blk_expert, n_active, xs, wg, wu, wd)


def _combine_kernel(dest_ref, w_ref, x1_ref, h2_ref, gate_ref, wsg_ref, wsu_ref, wsd_ref, gf_ref,
                    ys_hbm, o_ref, buf, sem, *, chunks):
    tm = x1_ref.shape[1]

    def issue(r, _):
        for k in range(TOP_K):
            _row_copy(ys_hbm, buf.at[k], dest_ref[0, 0, r * TOP_K + k], r, sem, chunks).start()
        return 0

    def drain(r, _):
        for k in range(TOP_K):
            _row_copy(ys_hbm, buf.at[k], 0, r, sem, chunks).wait()
        return 0

    lax.fori_loop(0, tm, issue, 0)
    h_bf = h2_ref[0].astype(BF16)
    hid = (_silu(_dot(h_bf, wsg_ref[...])) * _dot(h_bf, wsu_ref[...])).astype(BF16)
    shared = _dot(hid, wsd_ref[...])
    lax.fori_loop(0, tm, drain, 0)

    w = w_ref[0]
    cols = []
    for c in range(chunks):
        tot = jnp.zeros((tm, LANES), F32)
        for k in range(TOP_K):
            tot = tot + buf[k, pl.ds(c, tm, stride=chunks), :] * w[:, k:k + 1]
        cols.append(tot)
    routed = jnp.concatenate(cols, axis=1)
    x2 = x1_ref[0] + gate_ref[0] * (routed + shared)
    o_ref[0] = _rms(x2, gf_ref[...])


def _combine(dest, wts, x1, h2, gate, wsg_bf, wsu_bf, wsd_bf, gf, ys, tm, chunks):
    b, t, d = x1.shape
    nt = t // tm
    row = lambda n: pl.BlockSpec((1, tm, n), lambda i, j: (i, j, 0))
    const = lambda shape: pl.BlockSpec(shape, lambda i, j: (0,) * len(shape))
    return pl.pallas_call(
        functools.partial(_combine_kernel, chunks=chunks),
        grid=(b, nt),
        in_specs=[pl.BlockSpec((1, 1, tm * TOP_K), lambda i, j: (i * nt + j, 0, 0),
                               memory_space=pltpu.SMEM),
                  row(LANES), row(d), row(d), _mod_spec(gate, tm, d),
                  const(wsg_bf.shape), const(wsu_bf.shape), const(wsd_bf.shape), const(gf.shape),
                  pl.BlockSpec(memory_space=pl.ANY)],
        out_specs=row(d),
        out_shape=jax.ShapeDtypeStruct((b, t, d), F32),
        scratch_shapes=[pltpu.VMEM((TOP_K, tm * chunks, LANES), F32), pltpu.SemaphoreType.DMA(())],
        compiler_params=_cparams("arbitrary", "arbitrary"),
        name="moe_combine",
    )(dest, wts, x1, h2, gate, wsg_bf, wsu_bf, wsd_bf, gf, ys)


def _split_mod(mod, per_token):
    parts = jnp.split(mod, 6, axis=-1)
    if per_token:
        return [p[None] for p in parts]
    return [p[:, None, :] for p in parts]


def _padded_in_weight(w_in, conv_width):
    n_gate = 3 * N_HEADS
    o = ATTN_WIDTH + 6 * KV_COLS
    main = w_in[:, :o]
    gates = jnp.pad(w_in[:, o:o + n_gate], ((0, 0), (0, LANES - n_gate)))
    glu = w_in[:, o + n_gate:o + n_gate + 2 * conv_width]
    return jnp.concatenate([main, gates, glu], axis=1).astype(BF16)


def _cmp_rows(x):
    return x.reshape(x.shape[:-2] + (x.shape[-2] // CMP_BLOCK, CMP_BLOCK * KV_COLS))


def _largest_tile(n, cap):
    best = [k for k in range(SUBLANES, cap + 1, SUBLANES) if n % k == 0]
    assert best, (n, cap)
    return best[-1]


def _kv5(x):
    return x.reshape(x.shape[:-1] + (N_KV, HEAD_DIM))[None]


def kernel(x_prompt, x_sample, cache_k_cmp, cache_v_cmp, cache_k_slc, cache_v_slc, state_k_win, state_v_win, state_conv, page_table, c_prompt, c_sample, norm1_g, norm2_g, w_ada, b_ada, w_in, w_cmp_k, w_cmp_v, w_dw, b_dw, ln_conv_g, ln_conv_b, g_out_attn, g_out_conv, w_out, router_w, router_b, w_exp_gate, w_exp_up, w_exp_down, w_sh_gate, w_sh_up, w_sh_down, norm_f_g):
    assert w_ada.shape[0] == 1, "single layer"
    bp, t, d = x_prompt.shape
    bs, s_new, _ = x_sample.shape
    assert s_new == 1
    n_pool, page = cache_k_cmp.shape[1], cache_k_cmp.shape[2]
    n_pages = page_table.shape[1]
    past = n_pages * page
    conv_width = state_conv.shape[-1]
    n_exp = router_w.shape[-1]
    chunks = d // LANES
    tm = min(ROW_TILE, t)

    w_in_bf = _padded_in_weight(w_in[0], conv_width)
    wck = _compress_weight(w_cmp_k[0])
    wcv = _compress_weight(w_cmp_v[0])
    wout_bf = w_out[0].astype(BF16)
    rw_hi = router_w[0].astype(BF16)
    rw_lo = (router_w[0] - rw_hi.astype(F32)).astype(BF16)
    wsg_bf, wsu_bf, wsd_bf = (w[0].astype(BF16) for w in (w_sh_gate, w_sh_up, w_sh_down))
    gf = norm_f_g[None, :]

    n_c = bp + bs
    c_all = jnp.concatenate([c_prompt, c_sample], axis=0)
    c_all = jnp.pad(c_all, ((0, (-n_c) % SUBLANES), (0, 0)))
    mod = _modulation(c_all, w_ada[0], b_ada)
    mp = _split_mod(mod[:bp], per_token=False)
    ms = _split_mod(mod[bp:n_c], per_token=True)

    (q_p, kc_p, vc_p, ks_p, vs_p, kw_p, vw_p, gate_p, u_p) = _in_proj(
        x_prompt, mp[0], mp[1], norm1_g, w_in_bf, tm)
    nc_p = t // CMP_BLOCK
    kcc, vcc = _compress(_cmp_rows(kc_p).reshape(bp * nc_p, -1), _cmp_rows(vc_p).reshape(bp * nc_p, -1),
                         wck, wcv)
    o_attn_p = _prompt_attention(q_p, gate_p, kcc.reshape(bp, nc_p, KV_COLS), vcc.reshape(bp, nc_p, KV_COLS),
                                 ks_p, vs_p, kw_p, vw_p)
    o_conv_p = _conv_prompt(u_p, w_dw[0], b_dw, ln_conv_g, ln_conv_b, tm)

    xs_row = x_sample.reshape(1, bs, d)
    (q_s, kc_s, vc_s, ks_s, vs_s, kw_s, vw_s, gate_s, u_s) = _in_proj(
        xs_row, ms[0], ms[1], norm1_g, w_in_bf, bs)
    q8 = q_s.reshape(bs, N_HEADS, HEAD_DIM)
    gates8 = gate_s[0, :, :3 * N_HEADS].reshape(bs, N_HEADS, 3)
    pages_t = lambda c: jnp.transpose(c[0], (0, 2, 3, 1))
    tail = (-(past + s_new)) % SLC_BLOCK
    n_new = (s_new + tail) // CMP_BLOCK
    tail_rows = lambda x: _cmp_rows(jnp.pad(x[0][:, None, :], ((0, 0), (0, tail), (0, 0)))).reshape(bs * n_new, -1)
    pad_rows = (-(bs * n_new)) % SUBLANES
    kc_new, vc_new = _compress(jnp.pad(tail_rows(kc_s), ((0, pad_rows), (0, 0))),
                               jnp.pad(tail_rows(vc_s), ((0, pad_rows), (0, 0))), wck, wcv)
    kc_new = kc_new[:bs * n_new].reshape(bs, n_new, KV_COLS)
    vc_new = vc_new[:bs * n_new].reshape(bs, n_new, KV_COLS)
    reps = page // CMP_BLOCK
    wk_fold = jnp.transpose(w_cmp_k[0], (2, 1, 0)).reshape(HEAD_DIM, HEAD_DIM * CMP_BLOCK)
    ut = _matmul3(q8.reshape(bs * N_HEADS, HEAD_DIM) * ATTN_SCALE, wk_fold)
    ut = jnp.tile(ut.reshape(bs, N_HEADS, HEAD_DIM, CMP_BLOCK), (1, 1, 1, reps))
    s_raw = _sample_scores(page_table, ut, pages_t(cache_k_cmp))
    p_exp, p_new, sel = _sample_select(s_raw, q8, kc_new, past)
    y_acc = _sample_values(page_table, p_exp, pages_t(cache_v_cmp))
    wv_fold = jnp.tile(jnp.transpose(w_cmp_v[0], (1, 0, 2)), (1, reps, 1)).reshape(HEAD_DIM * page, HEAD_DIM)
    o_cmp_s = _matmul3(y_acc.reshape(bs * N_HEADS, HEAD_DIM * page), wv_fold).reshape(bs, N_HEADS, HEAD_DIM)
    n_sel = min(N_SEL, (past // CMP_BLOCK + n_new) // CMP_PER_SLC)
    sel = sel[:, :N_KV, :n_sel]
    row3 = lambda x: x[0][:, None, :]
    o_attn_s = _sample_attend(
        sel, page_table, q8, pages_t(cache_k_slc), pages_t(cache_v_slc), row3(ks_s), row3(vs_s),
        pages_t(state_k_win), pages_t(state_v_win), row3(kw_s), row3(vw_s), gates8, o_cmp_s, p_new, vc_new,
        past)
    o_attn_s = o_attn_s.reshape(1, bs, ATTN_WIDTH)
    up_s = jnp.concatenate([state_conv[0], u_s[0][:, None, :]], axis=1)
    o_conv_s = _conv_sample(up_s, w_dw[0], b_dw, ln_conv_g, ln_conv_b)[None]

    router = functools.partial(_merge_router, goa=g_out_attn, goc=g_out_conv, wout_bf=wout_bf, g2=norm2_g,
                               rw_hi=rw_hi, rw_lo=rw_lo, rb=router_b)
    x1_p, h2_p, e_p, w_p, r_p, cnt = router(o_attn_p, o_conv_p, x_prompt, mp[2], mp[3], mp[4],
                                            cnt_in=jnp.zeros((1, n_exp), F32), tm=tm)
    x1_s, h2_s, e_s, w_s, r_s, cnt = router(o_attn_s, o_conv_s, xs_row, ms[2], ms[3], ms[4],
                                            cnt_in=cnt, tm=bs)

    n_tok = bp * t + bs
    counts = cnt[0].astype(I32)
    padded = (counts + MOE_ROWS - 1) // MOE_ROWS * MOE_ROWS
    pad_end = jnp.cumsum(padded)
    pad_start = (pad_end - padded).astype(F32)[None, :]
    n_blocks = -(-(n_tok * TOP_K) // MOE_ROWS) + n_exp
    blk_expert = jnp.minimum(jnp.searchsorted(pad_end, jnp.arange(n_blocks, dtype=I32) * MOE_ROWS,
                                              side='right'), n_exp - 1).astype(I32)
    n_active = (pad_end[-1:] // MOE_ROWS).astype(I32)
    dest_p = _slots(e_p, r_p, pad_start, tm)[:, :, :TOP_K]
    dest_s = _slots(e_s, r_s, pad_start, bs)[:, :, :TOP_K]
    dest = jnp.concatenate([dest_p.reshape(-1, TOP_K), dest_s.reshape(-1, TOP_K)], axis=0)

    tile = _largest_tile(n_tok, 512)
    h_rows = jnp.concatenate([h2_p.reshape(-1, d), h2_s.reshape(-1, d)], axis=0).reshape(n_tok * chunks, LANES)
    xs = _dispatch(counts, pad_end.astype(I32), dest.reshape(n_tok // tile, 1, tile * TOP_K), h_rows,
                   tile, MOE_ROWS, chunks, n_blocks)
    ys = _experts(blk_expert, n_active, xs, w_exp_gate[0], w_exp_up[0], w_exp_down[0], MOE_ROWS, chunks)

    comb = functools.partial(_combine, wsg_bf=wsg_bf, wsu_bf=wsu_bf, wsd_bf=wsd_bf, gf=gf, ys=ys, chunks=chunks)
    y_prompt = comb(dest_p.reshape(bp * (t // tm), 1, tm * TOP_K), w_p, x1_p, h2_p, mp[5], tm=tm)
    y_sample = comb(dest_s.reshape(1, 1, bs * TOP_K), w_s, x1_s, h2_s, ms[5], tm=bs).reshape(bs, 1, d)

    win = min(WINDOW, t)
    hist = state_conv.shape[2]
    out_p = [_kv5(a) for a in (kc_p, vc_p, ks_p, vs_p, kw_p[:, t - win:], vw_p[:, t - win:])]
    conv_p = u_p[:, t - hist:][None]
    out_s = [_kv5(a[0][:, None, :]) for a in (kc_s, vc_s, ks_s, vs_s)]
    w_buf = state_k_win.shape[2]
    kw_buf = jnp.concatenate([state_k_win, _kv5(kw_s[0][:, None, :])], axis=2)[:, :, -w_buf:]
    vw_buf = jnp.concatenate([state_v_win, _kv5(vw_s[0][:, None, :])], axis=2)[:, :, -w_buf:]
    conv_s = up_s[:, -hist:][None]
    return (y_prompt, y_sample, *out_p, conv_p, *out_s, kw_buf, vw_buf, conv_s)
```

```python
import functools

import jax
import jax.numpy as jnp
from jax import lax
from jax.experimental import pallas as pl
from jax.experimental.pallas import tpu as pltpu

F32 = jnp.float32
BF16 = jnp.bfloat16
I32 = jnp.int32

N_HEADS = 8
HEAD_DIM = 64
N_KV = 2
Q_PER_KV = N_HEADS // N_KV
ATTN_WIDTH = N_HEADS * HEAD_DIM
KV_COLS = N_KV * HEAD_DIM
CMP_BLOCK = 32
SLC_BLOCK = 64
CMP_PER_SLC = SLC_BLOCK // CMP_BLOCK
N_SEL = 16
WINDOW = 512
TOP_K = 8
ROUTE_SCALE = 2.5
EPS = 1e-6
FORCED = 1e4
NEG = -1e30
ATTN_SCALE = HEAD_DIM ** -0.5

LANES = 128
SUBLANES = 8
VMEM_LIMIT = 56 * 1024 * 1024

ROW_TILE = 256
Q_TILE = 128
KEY_TILE = 1024
CMP_ROW_TILE = 512
MOE_ROWS = 256
CONV_HALO = 32
PAGES_PER_STEP = 16


def _cparams(*sem):
    return pltpu.CompilerParams(dimension_semantics=sem, vmem_limit_bytes=VMEM_LIMIT)


def _dot(a, b):
    return jnp.dot(a, b, preferred_element_type=F32)


def _dot_nt(a, b):
    return lax.dot_general(a, b, (((1,), (1,)), ((), ())), preferred_element_type=F32)


def _dot_tn(a, b):
    return lax.dot_general(a, b, (((0,), (0,)), ((), ())), preferred_element_type=F32)


def _split2(x):
    hi = x.astype(BF16)
    lo = (x - hi.astype(F32)).astype(BF16)
    return hi, lo


def _dot3(a, b):
    ah, al = _split2(a)
    bh, bl = _split2(b)
    return _dot(ah, bh) + (_dot(ah, bl) + _dot(al, bh))


def _dot3_nt(a, b):
    ah, al = _split2(a)
    bh, bl = _split2(b)
    return _dot_nt(ah, bh) + (_dot_nt(ah, bl) + _dot_nt(al, bh))


def _sigmoid(x):
    return 1.0 / (1.0 + jnp.exp(-x))


def _silu(x):
    return x * _sigmoid(x)


def _rms(x, g):
    return x * lax.rsqrt(jnp.mean(x * x, axis=-1, keepdims=True) + EPS) * g


def _alibi_slope_col(rows, rows_per_head, first_head, n_heads):
    r = lax.broadcasted_iota(I32, (rows, 1), 0) // rows_per_head
    out = jnp.zeros((rows, 1), F32)
    for k in range(n_heads):
        out = jnp.where(r == k, 2.0 ** (-8.0 * (first_head + k + 1) / N_HEADS), out)
    return out


def _modulation_kernel(c_ref, w_ref, b_ref, o_ref):
    o_ref[...] = _dot3(c_ref[...], w_ref[...]) + b_ref[...]


def _modulation(c, w, b):
    m, d = c.shape
    n = w.shape[1]
    tn = 768
    return pl.pallas_call(
        _modulation_kernel,
        grid=(n // tn,),
        in_specs=[pl.BlockSpec((m, d), lambda j: (0, 0)),
                  pl.BlockSpec((d, tn), lambda j: (0, j)),
                  pl.BlockSpec((1, tn), lambda j: (0, j))],
        out_specs=pl.BlockSpec((m, tn), lambda j: (0, j)),
        out_shape=jax.ShapeDtypeStruct((m, n), F32),
        compiler_params=_cparams("arbitrary"),
        name="modulation",
    )(c, w, b)


def _mod_spec(mod, tm, d):
    if mod.shape[1] == 1:
        return pl.BlockSpec((1, 1, d), lambda i, j: (i, 0, 0))
    return pl.BlockSpec((1, tm, d), lambda i, j: (i, j, 0))


def _in_proj_kernel(x_ref, shift_ref, scale_ref, g_ref, w_ref,
                    q_ref, kc_ref, vc_ref, ks_ref, vs_ref, kw_ref, vw_ref, gate_ref, u_ref):
    x = x_ref[0]
    h = _rms(x, g_ref[...]) * (1.0 + scale_ref[0]) + shift_ref[0]
    z = _dot(h.astype(BF16), w_ref[...])
    q_ref[0] = z[:, :ATTN_WIDTH]
    o = ATTN_WIDTH
    for ref in (kc_ref, vc_ref, ks_ref, vs_ref, kw_ref, vw_ref):
        ref[0] = z[:, o:o + KV_COLS]
        o += KV_COLS
    gate_ref[0] = _sigmoid(z[:, o:o + LANES])
    o += LANES
    cw = u_ref.shape[-1]
    u_ref[0] = z[:, o:o + cw] * _sigmoid(z[:, o + cw:o + 2 * cw])


def _in_proj(x, shift, scale, g, w_bf, tm):
    b, t, d = x.shape
    cw = (w_bf.shape[1] - ATTN_WIDTH - 6 * KV_COLS - LANES) // 2
    row = lambda n: pl.BlockSpec((1, tm, n), lambda i, j: (i, j, 0))
    sds = lambda n: jax.ShapeDtypeStruct((b, t, n), F32)
    return pl.pallas_call(
        _in_proj_kernel,
        grid=(b, t // tm),
        in_specs=[row(d), _mod_spec(shift, tm, d), _mod_spec(scale, tm, d),
                  pl.BlockSpec((1, d), lambda i, j: (0, 0)),
                  pl.BlockSpec(w_bf.shape, lambda i, j: (0, 0))],
        out_specs=[row(ATTN_WIDTH)] + [row(KV_COLS)] * 6 + [row(LANES), row(cw)],
        out_shape=[sds(ATTN_WIDTH)] + [sds(KV_COLS)] * 6 + [sds(LANES), sds(cw)],
        compiler_params=_cparams("arbitrary", "arbitrary"),
        name="in_proj",
    )(x, shift, scale, g, w_bf)


def _compress_kernel(k_ref, v_ref, wk_ref, wv_ref, ko_ref, vo_ref):
    ko_ref[...] = _dot3(k_ref[...], wk_ref[...])
    vo_ref[...] = _dot3(v_ref[...], wv_ref[...])


def _compress(k_rows, v_rows, wk, wv):
    r, kdim = k_rows.shape
    tr = min(CMP_ROW_TILE, r)
    assert r % tr == 0
    rows = pl.BlockSpec((tr, kdim), lambda i: (i, 0))
    wspec = pl.BlockSpec((kdim, KV_COLS), lambda i: (0, 0))
    ospec = pl.BlockSpec((tr, KV_COLS), lambda i: (i, 0))
    return pl.pallas_call(
        _compress_kernel,
        grid=(r // tr,),
        in_specs=[rows, rows, wspec, wspec],
        out_specs=[ospec, ospec],
        out_shape=[jax.ShapeDtypeStruct((r, KV_COLS), F32)] * 2,
        compiler_params=_cparams("arbitrary"),
        name="compress",
    )(k_rows, v_rows, wk, wv)


def _compress_weight(w):
    eye = jnp.eye(N_KV, dtype=w.dtype)
    big = jnp.einsum('lde,gh->lgdhe', w, eye)
    return big.reshape(CMP_BLOCK * KV_COLS, KV_COLS)


def _pair_sum(x, axis):
    n = x.shape[axis]
    idx = lax.broadcasted_iota(I32, x.shape, axis)
    nxt = pltpu.roll(x, n - 1, axis)
    prv = pltpu.roll(x, 1, axis)
    return x + jnp.where((idx & 1) == 0, nxt, prv)


def _block_scores(imp, blk, q_pos, n_blocks_total):
    cur = q_pos // SLC_BLOCK
    valid = jnp.logical_and(blk * SLC_BLOCK <= q_pos, blk < n_blocks_total)
    forced = jnp.logical_or(blk == 0, jnp.logical_or(blk == cur, blk == cur - 1))
    return jnp.where(valid, jnp.where(forced, FORCED, imp), -1.0)


def _select_blocks(score, blk, n_sel):
    blk_f = blk.astype(F32)
    s = score
    for _ in range(n_sel):
        m = jnp.max(s, axis=0, keepdims=True)
        first = jnp.min(jnp.where(s == m, blk_f, 1e9), axis=0, keepdims=True)
        s = jnp.where(blk_f == first, -2.0, s)
    return jnp.where(jnp.logical_and(s == -2.0, score >= 0.0), 1.0, 0.0)


def _prompt_attn_kernel(q_ref, gate_ref, kc_ref, vc_ref, ks_ref, vs_ref, kw_ref, vw_ref, o_ref,
                        *, seq, n_sel):
    i = pl.program_id(1)
    tq = Q_TILE
    nc = kc_ref.shape[1]
    q_blk = q_ref[0] * ATTN_SCALE
    gates = gate_ref[0]
    slopes = [2.0 ** (-8.0 * (h + 1) / N_HEADS) for h in range(N_HEADS)]
    group_heads = [list(range(g * Q_PER_KV, (g + 1) * Q_PER_KV)) for g in range(N_KV)]
    gsl = [slice(g * HEAD_DIM, (g + 1) * HEAD_DIM) for g in range(N_KV)]
    rsl = [slice(r * tq, (r + 1) * tq) for r in range(Q_PER_KV)]
    q_pos_col = i * tq + lax.broadcasted_iota(I32, (tq, 1), 0)
    q_pos_row = i * tq + lax.broadcasted_iota(I32, (1, tq), 1)
    qg = [jnp.concatenate([q_blk[:, h * HEAD_DIM:(h + 1) * HEAD_DIM] for h in hs], axis=0)
          for hs in group_heads]
    qg_bf = [x.astype(BF16) for x in qg]

    cmp_row = lax.broadcasted_iota(I32, (nc, tq), 0)
    dist_c = (q_pos_row - (cmp_row * CMP_BLOCK + (CMP_BLOCK - 1))).astype(F32)
    mask_c = dist_c >= 0.0
    blk = cmp_row >> 1
    o_cmp = [None] * N_HEADS
    sel_bf = []
    for g, hs in enumerate(group_heads):
        vc_bf = vc_ref[0][:, gsl[g]].astype(BF16)
        qk = _dot3_nt(kc_ref[0][:, gsl[g]], qg[g])
        imp = jnp.zeros((nc, tq), F32)
        for r, h in enumerate(hs):
            s = jnp.where(mask_c, qk[:, rsl[r]] - slopes[h] * dist_c, NEG)
            m = jnp.max(s, axis=0, keepdims=True)
            p = jnp.where(mask_c, jnp.exp(s - m), 0.0)
            p = p / jnp.maximum(jnp.sum(p, axis=0, keepdims=True), 1e-30)
            o_cmp[h] = _dot_tn(p.astype(BF16), vc_bf)
            imp = imp + p
        score = _block_scores(_pair_sum(imp, 0), blk, q_pos_row, seq // SLC_BLOCK)
        sel_bf.append(_select_blocks(score, blk, n_sel).astype(BF16))

    span = WINDOW + tq
    w_start = pl.multiple_of(jnp.maximum(i * tq - WINDOW, 0), tq)
    dist_w = (q_pos_col - (w_start + lax.broadcasted_iota(I32, (tq, span), 1))).astype(F32)
    bias_w = jnp.where(jnp.logical_and(dist_w >= 0.0, dist_w <= float(WINDOW)), 0.0, NEG)
    o_win = [None] * N_HEADS
    for g, hs in enumerate(group_heads):
        kw_bf = kw_ref[0, pl.ds(w_start, span), :][:, gsl[g]].astype(BF16)
        vw_bf = vw_ref[0, pl.ds(w_start, span), :][:, gsl[g]].astype(BF16)
        s_all = _dot_nt(qg_bf[g], kw_bf)
        probs, sums = [], []
        for r, h in enumerate(hs):
            s = s_all[rsl[r]] + (bias_w - slopes[h] * dist_w)
            p = jnp.exp(s - jnp.max(s, axis=1, keepdims=True))
            sums.append(jnp.sum(p, axis=1, keepdims=True))
            probs.append(p.astype(BF16))
        o_all = _dot(jnp.concatenate(probs, axis=0), vw_bf)
        for r, h in enumerate(hs):
            o_win[h] = o_all[rsl[r]] / sums[r]

    n_tiles = ((i + 1) * tq + KEY_TILE - 1) // KEY_TILE

    def slc_step(t, carry):
        ms, ls, accs = (list(c) for c in carry)
        k0 = pl.multiple_of(t * KEY_TILE, KEY_TILE)
        dist = (q_pos_col - (k0 + lax.broadcasted_iota(I32, (tq, KEY_TILE), 1))).astype(F32)
        causal = dist >= 0.0
        key_cmp = (k0 + lax.broadcasted_iota(I32, (nc, KEY_TILE), 1)) // CMP_BLOCK
        expand = jnp.where(key_cmp == lax.broadcasted_iota(I32, (nc, KEY_TILE), 0), 1.0, 0.0).astype(BF16)
        for g, hs in enumerate(group_heads):
            kt_bf = ks_ref[0, pl.ds(k0, KEY_TILE), :][:, gsl[g]].astype(BF16)
            vt_bf = vs_ref[0, pl.ds(k0, KEY_TILE), :][:, gsl[g]].astype(BF16)
            chosen = _dot_tn(sel_bf[g], expand)
            bias = jnp.where(jnp.logical_and(causal, chosen > 0.5), 0.0, NEG)
            s_all = _dot_nt(qg_bf[g], kt_bf)
            probs, alphas = [], []
            for r, h in enumerate(hs):
                s = s_all[rsl[r]] + (bias - slopes[h] * dist)
                m_new = jnp.maximum(ms[h], jnp.max(s, axis=1, keepdims=True))
                alpha = jnp.exp(ms[h] - m_new)
                p = jnp.exp(s - m_new)
                ls[h] = alpha * ls[h] + jnp.sum(p, axis=1, keepdims=True)
                ms[h] = m_new
                alphas.append(alpha)
                probs.append(p.astype(BF16))
            pv = _dot(jnp.concatenate(probs, axis=0), vt_bf)
            for r, h in enumerate(hs):
                accs[h] = alphas[r] * accs[h] + pv[rsl[r]]
        return tuple(ms), tuple(ls), tuple(accs)

    init = (tuple(jnp.full((tq, 1), NEG, F32) for _ in range(N_HEADS)),
            tuple(jnp.zeros((tq, 1), F32) for _ in range(N_HEADS)),
            tuple(jnp.zeros((tq, HEAD_DIM), F32) for _ in range(N_HEADS)))
    _, l_s, acc_s = lax.fori_loop(0, n_tiles, slc_step, init)

    pieces = []
    for h in range(N_HEADS):
        o_slc = acc_s[h] / jnp.maximum(l_s[h], 1e-30)
        pieces.append(o_cmp[h] * gates[:, 3 * h + 0:3 * h + 1] + o_slc * gates[:, 3 * h + 1:3 * h + 2]
                      + o_win[h] * gates[:, 3 * h + 2:3 * h + 3])
    o_ref[0] = jnp.concatenate(pieces, axis=1)


def _prompt_attention(q, gates, kc, vc, ks, vs, kw, vw):
    b, t, _ = q.shape
    nc = kc.shape[1]
    assert t % KEY_TILE == 0 and t >= WINDOW + Q_TILE
    n_sel = min(N_SEL, t // SLC_BLOCK)
    qspec = lambda n: pl.BlockSpec((1, Q_TILE, n), lambda bi, i: (bi, i, 0))
    full = lambda r: pl.BlockSpec((1, r, KV_COLS), lambda bi, i: (bi, 0, 0))
    return pl.pallas_call(
        functools.partial(_prompt_attn_kernel, seq=t, n_sel=n_sel),
        grid=(b, t // Q_TILE),
        in_specs=[qspec(ATTN_WIDTH), qspec(LANES), full(nc), full(nc),
                  full(t), full(t), full(t), full(t)],
        out_specs=qspec(ATTN_WIDTH),
        out_shape=jax.ShapeDtypeStruct((b, t, ATTN_WIDTH), F32),
        compiler_params=_cparams("arbitrary", "arbitrary"),
        name="prompt_attention",
    )(q, gates, kc, vc, ks, vs, kw, vw)


def _merge_groups(per_group):
    row = lax.broadcasted_iota(I32, per_group[0].shape, 0) // Q_PER_KV
    out = per_group[0]
    for g in range(1, N_KV):
        out = jnp.where(row == g, per_group[g], out)
    return out


def _group_slice(x, g):
    return x[:, g * HEAD_DIM:(g + 1) * HEAD_DIM]


def _matmul3_kernel(a_ref, b_ref, o_ref):
    o_ref[...] = _dot3(a_ref[...], b_ref[...])


def _matmul3(a, b):
    return pl.pallas_call(
        _matmul3_kernel,
        out_shape=jax.ShapeDtypeStruct((a.shape[0], b.shape[1]), F32),
        compiler_params=pltpu.CompilerParams(vmem_limit_bytes=VMEM_LIMIT),
        name="matmul3",
    )(a, b)


def _page_specs(n_pages, page):
    def spec(o):
        return pl.BlockSpec((1, N_KV, HEAD_DIM, page),
                            lambda i, j, pt: (pt[i * n_pages + j * PAGES_PER_STEP + o], 0, 0, 0))
    return [spec(o) for o in range(PAGES_PER_STEP)]


def _sample_scores_kernel(pt_ref, ut_ref, *refs):
    k_refs, o_ref = refs[:-1], refs[-1]
    for h in range(N_HEADS):
        g = h // Q_PER_KV
        u = ut_ref[0, h]
        rows = [jnp.sum(k_ref[0, g] * u, axis=0, keepdims=True) for k_ref in k_refs]
        o_ref[0, h] = jnp.concatenate(rows, axis=0)


def _sample_scores(page_table, ut, k_pages):
    b, n_pages = page_table.shape
    page = k_pages.shape[-1]
    assert n_pages % PAGES_PER_STEP == 0
    return pl.pallas_call(
        _sample_scores_kernel,
        grid_spec=pltpu.PrefetchScalarGridSpec(
            num_scalar_prefetch=1,
            grid=(b, n_pages // PAGES_PER_STEP),
            in_specs=[pl.BlockSpec((1, N_HEADS, HEAD_DIM, page), lambda i, j, pt: (i, 0, 0, 0))]
                     + _page_specs(n_pages, page),
            out_specs=pl.BlockSpec((1, N_HEADS, PAGES_PER_STEP, page), lambda i, j, pt: (i, 0, j, 0)),
        ),
        out_shape=jax.ShapeDtypeStruct((b, N_HEADS, n_pages, page), F32),
        compiler_params=_cparams("arbitrary", "arbitrary"),
        name="sample_scores",
    )(page_table.reshape(-1), ut, *([k_pages] * PAGES_PER_STEP))


def _max_all(x):
    return jnp.max(jnp.max(x, axis=0, keepdims=True), axis=1, keepdims=True)


def _min_all(x):
    return jnp.min(jnp.min(x, axis=0, keepdims=True), axis=1, keepdims=True)


def _sum_all(x):
    return jnp.sum(jnp.sum(x, axis=0, keepdims=True), axis=1, keepdims=True)


def _sample_select_kernel(s_ref, q_ref, kcn_ref, pexp_ref, pnew_ref, sel_ref, *, past, n_new, n_sel):
    n_pages, page = s_ref.shape[2], s_ref.shape[3]
    cpp = page // CMP_BLOCK
    n_past = n_pages * cpp
    n_blocks_total = (n_past + n_new) // CMP_PER_SLC
    lane = lax.broadcasted_iota(I32, (n_pages, page), 1)
    prow = lax.broadcasted_iota(I32, (n_pages, page), 0)
    dist = (past - ((prow * cpp + lane // CMP_BLOCK) * CMP_BLOCK + (CMP_BLOCK - 1))).astype(F32)
    mask = jnp.logical_and(lane % CMP_BLOCK == 0, dist >= 0.0)

    q8 = q_ref[0] * ATTN_SCALE
    slope = _alibi_slope_col(N_HEADS, 1, 0, N_HEADS)
    kcn = jnp.concatenate([kcn_ref[0], jnp.zeros((LANES - n_new, KV_COLS), F32)], axis=0)
    new_lane = lax.broadcasted_iota(I32, (N_HEADS, LANES), 1)
    dist_n = (past - ((n_past + new_lane) * CMP_BLOCK + (CMP_BLOCK - 1))).astype(F32)
    mask_n = jnp.logical_and(dist_n >= 0.0, new_lane < n_new)
    qk_n = _merge_groups([_dot3_nt(q8, _group_slice(kcn, g)) for g in range(N_KV)])
    s_new = jnp.where(mask_n, qk_n - slope * dist_n, NEG)

    probs, probs_new = [], []
    for h in range(N_HEADS):
        x = s_ref[0, h]
        for sh in (16, 8, 4, 2, 1):
            x = x + pltpu.roll(x, page - sh, 1)
        s = jnp.where(mask, x - 2.0 ** (-8.0 * (h + 1) / N_HEADS) * dist, NEG)
        sn = s_new[h:h + 1, :]
        mn = jnp.logical_and(dist_n[h:h + 1, :] >= 0.0, new_lane[h:h + 1, :] < n_new)
        m = jnp.maximum(_max_all(s), jnp.max(sn, axis=1, keepdims=True))
        p = jnp.where(mask, jnp.exp(s - m), 0.0)
        pn = jnp.where(mn, jnp.exp(sn - m), 0.0)
        den = jnp.maximum(_sum_all(p) + jnp.sum(pn, axis=1, keepdims=True), 1e-30)
        p = p / den
        probs.append(p)
        probs_new.append(pn / den)
        z = p
        for sh in (1, 2, 4, 8, 16):
            z = z + pltpu.roll(z, sh, 1)
        pexp_ref[0, h] = z
    pnew_ref[0] = jnp.concatenate(probs_new, axis=0)

    row1 = lax.broadcasted_iota(I32, (1, LANES), 1)
    blk = jnp.where(lane % SLC_BLOCK == 0, prow * (page // SLC_BLOCK) + lane // SLC_BLOCK, -1)
    blk_n = jnp.where(row1 < n_new, n_past // CMP_PER_SLC + (row1 >> 1), -1)
    blk_f = blk.astype(F32)
    blk_nf = blk_n.astype(F32)
    out_lane = lax.broadcasted_iota(I32, (N_HEADS, LANES), 1)
    out_row = lax.broadcasted_iota(I32, (N_HEADS, LANES), 0)
    out = jnp.full((N_HEADS, LANES), -1, I32)
    for g in range(N_KV):
        imp = probs[g * Q_PER_KV]
        imp_n = probs_new[g * Q_PER_KV]
        for r in range(1, Q_PER_KV):
            imp = imp + probs[g * Q_PER_KV + r]
            imp_n = imp_n + probs_new[g * Q_PER_KV + r]
        imp = imp + pltpu.roll(imp, page - CMP_BLOCK, 1)
        s_m = jnp.where(blk >= 0, _block_scores(imp, blk, past, n_blocks_total), -4.0)
        s_n = jnp.where(blk_n >= 0, _block_scores(_pair_sum(imp_n, 1), blk_n, past, n_blocks_total), -4.0)
        for j in range(n_sel):
            top = jnp.maximum(_max_all(s_m), jnp.max(s_n, axis=1, keepdims=True))
            first = jnp.minimum(_min_all(jnp.where(s_m == top, blk_f, 1e9)),
                                jnp.min(jnp.where(s_n == top, blk_nf, 1e9), axis=1, keepdims=True))
            s_m = jnp.where(blk_f == first, -2.0, s_m)
            s_n = jnp.where(blk_nf == first, -2.0, s_n)
            pick = jnp.where(top >= 0.0, first.astype(I32), -1)
            out = jnp.where(jnp.logical_and(out_row == g, out_lane == j), pick, out)
    sel_ref[0] = out


def _sample_select(s_raw, q8, kc_new, past):
    b, _, n_pages, page = s_raw.shape
    n_new = kc_new.shape[1]
    assert CMP_PER_SLC == 2 and CMP_BLOCK == 32
    n_sel = min(N_SEL, (past // CMP_BLOCK + n_new) // CMP_PER_SLC)
    per_b = lambda *s: pl.BlockSpec((1,) + s, lambda i: (i,) + (0,) * len(s))
    return pl.pallas_call(
        functools.partial(_sample_select_kernel, past=past, n_new=n_new, n_sel=n_sel),
        grid=(b,),
        in_specs=[per_b(N_HEADS, n_pages, page), per_b(N_HEADS, HEAD_DIM), per_b(n_new, KV_COLS)],
        out_specs=[per_b(N_HEADS, n_pages, page), per_b(N_HEADS, LANES), per_b(N_HEADS, LANES)],
        out_shape=[jax.ShapeDtypeStruct((b, N_HEADS, n_pages, page), F32),
                   jax.ShapeDtypeStruct((b, N_HEADS, LANES), F32),
                   jax.ShapeDtypeStruct((b, N_HEADS, LANES), I32)],
        compiler_params=_cparams("arbitrary"),
        name="sample_select",
    )(s_raw, q8, kc_new)


def _sample_values_kernel(pt_ref, pe_ref, *refs):
    v_refs, y_ref = refs[:-1], refs[-1]

    @pl.when(pl.program_id(1) == 0)
    def _():
        y_ref[...] = jnp.zeros(y_ref.shape, F32)

    for g in range(N_KV):
        heads = range(g * Q_PER_KV, (g + 1) * Q_PER_KV)
        pe = [pe_ref[0, h] for h in heads]
        acc = [jnp.zeros(y_ref.shape[2:], F32) for _ in heads]
        for o, v_ref in enumerate(v_refs):
            v = v_ref[0, g]
            for r in range(Q_PER_KV):
                acc[r] = acc[r] + v * pe[r][o:o + 1, :]
        for r, h in enumerate(heads):
            y_ref[0, h] = y_ref[0, h] + acc[r]


def _sample_values(page_table, pexp, v_pages):
    b, n_pages = page_table.shape
    page = v_pages.shape[-1]
    return pl.pallas_call(
        _sample_values_kernel,
        grid_spec=pltpu.PrefetchScalarGridSpec(
            num_scalar_prefetch=1,
            grid=(b, n_pages // PAGES_PER_STEP),
            in_specs=[pl.BlockSpec((1, N_HEADS, PAGES_PER_STEP, page), lambda i, j, pt: (i, 0, j, 0))]
                     + _page_specs(n_pages, page),
            out_specs=pl.BlockSpec((1, N_HEADS, HEAD_DIM, page), lambda i, j, pt: (i, 0, 0, 0)),
        ),
        out_shape=jax.ShapeDtypeStruct((b, N_HEADS, HEAD_DIM, page), F32),
        compiler_params=_cparams("arbitrary", "arbitrary"),
        name="sample_values",
    )(page_table.reshape(-1), pexp, *([v_pages] * PAGES_PER_STEP))


def _new_token_terms(q8, k_row, v_row):
    s = _merge_groups([jnp.sum(q8 * _group_slice(k_row, g), axis=1, keepdims=True) for g in range(N_KV)])
    v = _merge_groups([jnp.broadcast_to(_group_slice(v_row, g), (N_HEADS, HEAD_DIM)) for g in range(N_KV)])
    return s, v


def _sample_attend_kernel(sel_ref, pt_ref, q_ref, k0_ref, k1_ref, v0_ref, v1_ref, ksn_ref, vsn_ref,
                          kw_ref, vw_ref, kwn_ref, vwn_ref, gate_ref, ocmp_ref, pnew_ref, vcn_ref, o_ref,
                          m_sc, l_sc, acc_sc, *, past, n_sel, ns_past):
    b = pl.program_id(0)
    n = pl.program_id(1)
    page = k0_ref.shape[-1]
    spp = page // SLC_BLOCK
    q8 = q_ref[0] * ATTN_SCALE
    q8_bf = q8.astype(BF16)
    slope = _alibi_slope_col(N_HEADS, 1, 0, N_HEADS)

    @pl.when(n == 0)
    def _():
        m_sc[...] = jnp.full(m_sc.shape, NEG, F32)
        l_sc[...] = jnp.zeros(l_sc.shape, F32)
        acc_sc[...] = jnp.zeros(acc_sc.shape, F32)

    lane = lax.broadcasted_iota(I32, (N_HEADS, page), 1)
    qk_g, blk_g, blk_c = [], [], []
    for g, k_ref in enumerate((k0_ref, k1_ref)):
        blk = sel_ref[(b * N_KV + g) * n_sel + n]
        qk_g.append(_dot(q8_bf, k_ref[0, 0].astype(BF16)))
        blk_g.append(jnp.full((N_HEADS, page), blk, I32))
        blk_c.append(jnp.full((N_HEADS, 1), blk, I32))
    blk_rows = _merge_groups(blk_g)
    blk_col = _merge_groups(blk_c)
    page_pos = blk_rows // spp
    dist = (past - (page_pos * page + lane)).astype(F32)
    in_block = (lane // SLC_BLOCK) == (blk_rows - page_pos * spp)
    cached = jnp.logical_and(blk_rows >= 0, blk_rows < ns_past)
    mask = jnp.logical_and(jnp.logical_and(in_block, cached), dist >= 0.0)
    s = jnp.where(mask, _merge_groups(qk_g) - slope * dist, NEG)
    is_new = blk_col >= ns_past
    s_n, v_n = _new_token_terms(q8, ksn_ref[0], vsn_ref[0])
    s_n = jnp.where(is_new, s_n, NEG)
    m_old = m_sc[...]
    m_new = jnp.maximum(m_old, jnp.maximum(jnp.max(s, axis=1, keepdims=True), s_n))
    alpha = jnp.exp(m_old - m_new)
    p = jnp.where(mask, jnp.exp(s - m_new), 0.0)
    p_n = jnp.where(is_new, jnp.exp(s_n - m_new), 0.0)
    p_bf = p.astype(BF16)
    pv = _merge_groups([_dot_nt(p_bf, v_ref[0, 0].astype(BF16)) for v_ref in (v0_ref, v1_ref)])
    l_sc[...] = alpha * l_sc[...] + jnp.sum(p, axis=1, keepdims=True) + p_n
    acc_sc[...] = alpha * acc_sc[...] + pv + p_n * v_n
    m_sc[...] = m_new

    @pl.when(n == n_sel - 1)
    def _():
        o_slc = acc_sc[...] / jnp.maximum(l_sc[...], 1e-30)
        w_buf = kw_ref.shape[-1]
        wl = lax.broadcasted_iota(I32, (N_HEADS, w_buf), 1)
        win_pos = past - w_buf + wl
        dist_w = (past - win_pos).astype(F32)
        mask_w = jnp.logical_and(jnp.logical_and(dist_w >= 0.0, dist_w <= float(WINDOW)), win_pos >= 0)
        qk_w = _merge_groups([_dot(q8_bf, kw_ref[0, g].astype(BF16)) for g in range(N_KV)])
        s_w = jnp.where(mask_w, qk_w - slope * dist_w, NEG)
        s_t, v_t = _new_token_terms(q8, kwn_ref[0], vwn_ref[0])
        m_w = jnp.maximum(jnp.max(s_w, axis=1, keepdims=True), s_t)
        p_w = jnp.where(mask_w, jnp.exp(s_w - m_w), 0.0)
        p_t = jnp.exp(s_t - m_w)
        den = jnp.maximum(jnp.sum(p_w, axis=1, keepdims=True) + p_t, 1e-30)
        pw_bf = p_w.astype(BF16)
        o_w = _merge_groups([_dot_nt(pw_bf, vw_ref[0, g].astype(BF16)) for g in range(N_KV)])
        o_win = (o_w + p_t * v_t) / den
        n_new = vcn_ref.shape[1]
        vcn = jnp.concatenate([vcn_ref[0], jnp.zeros((LANES - n_new, KV_COLS), F32)], axis=0).astype(BF16)
        pn_bf = pnew_ref[0].astype(BF16)
        o_cmp = ocmp_ref[0] + _merge_groups([_dot(pn_bf, _group_slice(vcn, g)) for g in range(N_KV)])
        gt = gate_ref[0]
        o_ref[0] = o_cmp * gt[:, 0:1] + o_slc * gt[:, 1:2] + o_win * gt[:, 2:3]


def _sample_attend(sel, page_table, q8, k_pages, v_pages, ks_new, vs_new, kw_state, vw_state,
                   kw_new, vw_new, gates8, o_cmp, p_new, vc_new, past):
    b, n_pages = page_table.shape
    n_sel = sel.shape[-1]
    page = k_pages.shape[-1]
    ns_past = past // SLC_BLOCK
    spp = page // SLC_BLOCK
    w_buf = kw_state.shape[-1]
    n_new = vc_new.shape[1]

    def cache_map(g):
        def index(i, n, sel_ref, pt_ref):
            blk = jnp.clip(sel_ref[(i * N_KV + g) * n_sel + n], 0, ns_past - 1)
            return (pt_ref[i * n_pages + blk // spp], g, 0, 0)
        return pl.BlockSpec((1, 1, HEAD_DIM, page), index)

    per_b = lambda *s: pl.BlockSpec((1,) + s, lambda i, n, sl, pt: (i,) + (0,) * len(s))
    return pl.pallas_call(
        functools.partial(_sample_attend_kernel, past=past, n_sel=n_sel, ns_past=ns_past),
        grid_spec=pltpu.PrefetchScalarGridSpec(
            num_scalar_prefetch=2,
            grid=(b, n_sel),
            in_specs=[per_b(N_HEADS, HEAD_DIM), cache_map(0), cache_map(1), cache_map(0), cache_map(1),
                      per_b(1, KV_COLS), per_b(1, KV_COLS),
                      per_b(N_KV, HEAD_DIM, w_buf), per_b(N_KV, HEAD_DIM, w_buf),
                      per_b(1, KV_COLS), per_b(1, KV_COLS),
                      per_b(N_HEADS, 3), per_b(N_HEADS, HEAD_DIM), per_b(N_HEADS, LANES),
                      per_b(n_new, KV_COLS)],
            out_specs=per_b(N_HEADS, HEAD_DIM),
            scratch_shapes=[pltpu.VMEM((N_HEADS, 1), F32), pltpu.VMEM((N_HEADS, 1), F32),
                            pltpu.VMEM((N_HEADS, HEAD_DIM), F32)],
        ),
        out_shape=jax.ShapeDtypeStruct((b, N_HEADS, HEAD_DIM), F32),
        compiler_params=_cparams("arbitrary", "arbitrary"),
        name="sample_attend",
    )(sel.reshape(-1), page_table.reshape(-1), q8, k_pages, k_pages, v_pages, v_pages,
      ks_new, vs_new, kw_state, vw_state, kw_new, vw_new, gates8, o_cmp, p_new, vc_new)


def _layernorm_silu(y, g, b):
    mu = jnp.mean(y, axis=-1, keepdims=True)
    var = jnp.mean(jnp.square(y - mu), axis=-1, keepdims=True)
    return _silu((y - mu) * lax.rsqrt(var + EPS) * g + b)


def _conv_prompt_kernel(u_ref, w_ref, b_ref, g_ref, beta_ref, o_ref, buf):
    j = pl.program_id(1)
    tt = u_ref.shape[1]
    kw = w_ref.shape[0]

    @pl.when(j == 0)
    def _():
        buf[0:CONV_HALO, :] = jnp.zeros((CONV_HALO, buf.shape[1]), F32)

    buf[CONV_HALO:CONV_HALO + tt, :] = u_ref[0]
    w = w_ref[...]
    acc = jnp.zeros((tt, buf.shape[1]), F32)
    for k in range(kw):
        acc = acc + w[k:k + 1, :] * buf[pl.ds(CONV_HALO - (kw - 1) + k, tt), :]
    o_ref[0] = _layernorm_silu(acc + b_ref[...], g_ref[...], beta_ref[...])
    buf[0:CONV_HALO, :] = buf[tt:tt + CONV_HALO, :]


def _conv_prompt(u, w_dw, b_dw, ln_g, ln_b, tt):
    b, t, c = u.shape
    vec = pl.BlockSpec((1, c), lambda i, j: (0, 0))
    return pl.pallas_call(
        _conv_prompt_kernel,
        grid=(b, t // tt),
        in_specs=[pl.BlockSpec((1, tt, c), lambda i, j: (i, j, 0)),
                  pl.BlockSpec(w_dw.shape, lambda i, j: (0, 0)), vec, vec, vec],
        out_specs=pl.BlockSpec((1, tt, c), lambda i, j: (i, j, 0)),
        out_shape=jax.ShapeDtypeStruct((b, t, c), F32),
        scratch_shapes=[pltpu.VMEM((CONV_HALO + tt, c), F32)],
        compiler_params=_cparams("arbitrary", "arbitrary"),
        name="conv_prompt",
    )(u, w_dw, b_dw, ln_g, ln_b)


def _conv_sample_kernel(up_ref, w_ref, b_ref, g_ref, beta_ref, o_ref):
    y = jnp.sum(up_ref[...] * w_ref[...][None, :, :], axis=1)
    o_ref[...] = _layernorm_silu(y + b_ref[...], g_ref[...], beta_ref[...])


def _conv_sample(up, w_dw, b_dw, ln_g, ln_b):
    b, kw, c = up.shape
    return pl.pallas_call(
        _conv_sample_kernel,
        out_shape=jax.ShapeDtypeStruct((b, c), F32),
        name="conv_sample",
    )(up, w_dw, b_dw, ln_g, ln_b)


def _merge_router_kernel(oa_ref, oc_ref, x_ref, gate_ref, shift_ref, scale_ref, goa_ref, goc_ref,
                         wout_ref, g2_ref, rwh_ref, rwl_ref, rb_ref, cnt_in_ref,
                         x1_ref, h2_ref, eidx_ref, wts_ref, rank_ref, cnt_out_ref, run):
    first = jnp.logical_and(pl.program_id(0) == 0, pl.program_id(1) == 0)

    @pl.when(first)
    def _():
        run[...] = cnt_in_ref[...]

    a = _rms(oa_ref[0], goa_ref[...])
    c = _rms(oc_ref[0], goc_ref[...])
    cat = jnp.concatenate([a, c], axis=1).astype(BF16)
    x1 = x_ref[0] + gate_ref[0] * _dot(cat, wout_ref[...])
    x1_ref[0] = x1
    h2 = _rms(x1, g2_ref[...]) * (1.0 + scale_ref[0]) + shift_ref[0]
    h2_ref[0] = h2

    hh, hl = _split2(h2)
    logits = _dot(hh, rwh_ref[...]) + (_dot(hh, rwl_ref[...]) + _dot(hl, rwh_ref[...]))
    aff = _sigmoid(logits)
    tm, n_exp = aff.shape
    lane_f = lax.broadcasted_iota(I32, (tm, n_exp), 1).astype(F32)
    s = aff + rb_ref[...]
    hot = jnp.zeros((tm, n_exp), jnp.bool_)
    experts, weights = [], []
    for _ in range(TOP_K):
        m = jnp.max(s, axis=1, keepdims=True)
        e = jnp.min(jnp.where(s == m, lane_f, 1e9), axis=1, keepdims=True)
        pick = lane_f == e
        experts.append(e)
        weights.append(jnp.sum(jnp.where(pick, aff, 0.0), axis=1, keepdims=True))
        hot = jnp.logical_or(hot, pick)
        s = jnp.where(pick, NEG, s)
    total = weights[0]
    for w in weights[1:]:
        total = total + w

    hot_f = jnp.where(hot, 1.0, 0.0)
    r_i = lax.broadcasted_iota(I32, (tm, tm), 0)
    c_i = lax.broadcasted_iota(I32, (tm, tm), 1)
    lower = jnp.where(c_i < r_i, 1.0, 0.0).astype(BF16)
    before = _dot(lower, hot_f.astype(BF16)) + run[...]
    out_lane = lax.broadcasted_iota(I32, (tm, LANES), 1)
    e_out = jnp.zeros((tm, LANES), I32)
    w_out = jnp.zeros((tm, LANES), F32)
    r_out = jnp.zeros((tm, LANES), I32)
    for k in range(TOP_K):
        rank = jnp.sum(jnp.where(lane_f == experts[k], before, 0.0), axis=1, keepdims=True)
        e_out = jnp.where(out_lane == k, experts[k].astype(I32), e_out)
        w_out = jnp.where(out_lane == k, ROUTE_SCALE * weights[k] / total, w_out)
        r_out = jnp.where(out_lane == k, rank.astype(I32), r_out)
    eidx_ref[0] = e_out
    wts_ref[0] = w_out
    rank_ref[0] = r_out
    run[...] = run[...] + jnp.sum(hot_f, axis=0, keepdims=True)
    cnt_out_ref[...] = run[...]


def _merge_router(o_attn, o_conv, x, gate, shift, scale, goa, goc, wout_bf, g2, rw_hi, rw_lo, rb,
                  cnt_in, tm):
    b, t, d = x.shape
    n_exp = rw_hi.shape[1]
    row = lambda n: pl.BlockSpec((1, tm, n), lambda i, j: (i, j, 0))
    const = lambda shape: pl.BlockSpec(shape, lambda i, j: (0,) * len(shape))
    sds = lambda n, dt: jax.ShapeDtypeStruct((b, t, n), dt)
    return pl.pallas_call(
        _merge_router_kernel,
        grid=(b, t // tm),
        in_specs=[row(o_attn.shape[-1]), row(o_conv.shape[-1]), row(d),
                  _mod_spec(gate, tm, d), _mod_spec(shift, tm, d), _mod_spec(scale, tm, d),
                  const(goa.shape), const(goc.shape), const(wout_bf.shape), const(g2.shape),
                  const(rw_hi.shape), const(rw_lo.shape), const(rb.shape), const(cnt_in.shape)],
        out_specs=[row(d), row(d), row(LANES), row(LANES), row(LANES), const((1, n_exp))],
        out_shape=[sds(d, F32), sds(d, F32), sds(LANES, I32), sds(LANES, F32), sds(LANES, I32),
                   jax.ShapeDtypeStruct((1, n_exp), F32)],
        scratch_shapes=[pltpu.VMEM((1, n_exp), F32)],
        compiler_params=_cparams("arbitrary", "arbitrary"),
        name="merge_router",
    )(o_attn, o_conv, x, gate, shift, scale, goa, goc, wout_bf, g2, rw_hi, rw_lo, rb, cnt_in)


def _row_copy(src_hbm, dst, src_row, dst_row, sem, chunks):
    return pltpu.make_async_copy(src_hbm.at[pl.ds(src_row * chunks, chunks)],
                                 dst.at[pl.ds(dst_row * chunks, chunks)], sem)


def _slots_kernel(e_ref, r_ref, start_ref, o_ref):
    e = e_ref[0]
    r = r_ref[0]
    start = start_ref[...]
    tm = e.shape[0]
    lane_e = lax.broadcasted_iota(I32, (tm, start.shape[1]), 1)
    out_lane = lax.broadcasted_iota(I32, (tm, LANES), 1)
    out = jnp.zeros((tm, LANES), I32)
    for k in range(TOP_K):
        base = jnp.sum(jnp.where(lane_e == e[:, k:k + 1], start, 0.0), axis=1, keepdims=True)
        out = jnp.where(out_lane == k, base.astype(I32) + r[:, k:k + 1], out)
    o_ref[0] = out


def _slots(e_idx, rank, start, tm):
    b, t, _ = e_idx.shape
    row = pl.BlockSpec((1, tm, LANES), lambda i, j: (i, j, 0))
    return pl.pallas_call(
        _slots_kernel,
        grid=(b, t // tm),
        in_specs=[row, row, pl.BlockSpec(start.shape, lambda i, j: (0, 0))],
        out_specs=row,
        out_shape=jax.ShapeDtypeStruct((b, t, LANES), I32),
        compiler_params=_cparams("arbitrary", "arbitrary"),
        name="moe_slots",
    )(e_idx, rank, start)


def _dispatch_kernel(cnt_ref, end_ref, dest_ref, h_ref, xs_hbm, zbuf, zsem, sem,
                     *, tokens, rows, chunks, n_blocks):
    j = pl.program_id(0)
    n_exp = cnt_ref.shape[0]
    blk_rows = rows * chunks

    def zero_block(blk):
        return pltpu.make_async_copy(zbuf, xs_hbm.at[pl.ds(blk * blk_rows, blk_rows)], zsem)

    @pl.when(j == 0)
    def _():
        zbuf[...] = jnp.zeros(zbuf.shape, F32)
        n_active = end_ref[n_exp - 1] // rows

        def zero_tail(e, issued):
            partial = cnt_ref[e] % rows != 0

            @pl.when(partial)
            def _():
                zero_block(end_ref[e] // rows - 1).start()

            return issued + partial.astype(I32)

        def zero_unused(blk, _):
            zero_block(blk).start()
            return 0

        def drain_zero(_, c):
            zero_block(0).wait()
            return c

        issued = lax.fori_loop(0, n_exp, zero_tail, 0)
        lax.fori_loop(n_active, n_blocks, zero_unused, 0)
        lax.fori_loop(0, issued + (n_blocks - n_active), drain_zero, 0)

    def issue(r, _):
        for k in range(TOP_K):
            _row_copy(h_ref, xs_hbm, r, dest_ref[0, 0, r * TOP_K + k], sem, chunks).start()
        return 0

    def drain(r, _):
        for k in range(TOP_K):
            _row_copy(h_ref, xs_hbm, 0, 0, sem, chunks).wait()
        return 0

    lax.fori_loop(0, tokens, issue, 0)
    lax.fori_loop(0, tokens, drain, 0)


def _dispatch(counts, pad_end, dest, h_rows, tokens, rows, chunks, n_blocks):
    n_tiles = dest.shape[0]
    return pl.pallas_call(
        functools.partial(_dispatch_kernel, tokens=tokens, rows=rows, chunks=chunks, n_blocks=n_blocks),
        grid_spec=pltpu.PrefetchScalarGridSpec(
            num_scalar_prefetch=2,
            grid=(n_tiles,),
            in_specs=[pl.BlockSpec((1, 1, tokens * TOP_K), lambda j, c, e: (j, 0, 0),
                                   memory_space=pltpu.SMEM),
                      pl.BlockSpec((tokens * chunks, LANES), lambda j, c, e: (j, 0))],
            out_specs=pl.BlockSpec(memory_space=pl.ANY),
            scratch_shapes=[pltpu.VMEM((rows * chunks, LANES), F32),
                            pltpu.SemaphoreType.DMA(()), pltpu.SemaphoreType.DMA(())],
        ),
        out_shape=jax.ShapeDtypeStruct((n_blocks * rows * chunks, LANES), F32),
        compiler_params=_cparams("arbitrary"),
        name="moe_dispatch",
    )(counts, pad_end, dest, h_rows)


def _expert_kernel(be_ref, nact_ref, x_ref, wg_ref, wu_ref, wd_ref, y_ref, *, rows, chunks):
    j = pl.program_id(0)

    @pl.when(j < nact_ref[0])
    def _():
        f = wg_ref.shape[2]
        gate = jnp.zeros((rows, f), F32)
        up = jnp.zeros((rows, f), F32)
        for c in range(chunks):
            xc = x_ref[pl.ds(c, rows, stride=chunks), :].astype(BF16)
            cs = slice(c * LANES, (c + 1) * LANES)
            gate = gate + _dot(xc, wg_ref[0, cs, :].astype(BF16))
            up = up + _dot(xc, wu_ref[0, cs, :].astype(BF16))
        h = (_silu(gate) * up).astype(BF16)
        y = _dot(h, wd_ref[0].astype(BF16))
        for c in range(chunks):
            y_ref[pl.ds(c, rows, stride=chunks), :] = y[:, c * LANES:(c + 1) * LANES]

    @pl.when(j >= nact_ref[0])
    def _():
        y_ref[...] = jnp.zeros(y_ref.shape, F32)


def _experts(blk_expert, n_active, xs, wg, wu, wd, rows, chunks):
    n_blocks = blk_expert.shape[0]
    d, f = wg.shape[1], wg.shape[2]
    last = lambda j, na: jnp.minimum(j, na[0] - 1)
    xspec = pl.BlockSpec((rows * chunks, LANES), lambda j, be, na: (last(j, na), 0))
    wspec = lambda s: pl.BlockSpec((1,) + s, lambda j, be, na: (be[last(j, na)], 0, 0))
    return pl.pallas_call(
        functools.partial(_expert_kernel, rows=rows, chunks=chunks),
        grid_spec=pltpu.PrefetchScalarGridSpec(
            num_scalar_prefetch=2,
            grid=(n_blocks,),
            in_specs=[xspec, wspec((d, f)), wspec((d, f)), wspec((f, d))],
            out_specs=pl.BlockSpec((rows * chunks, LANES), lambda j, be, na: (j, 0)),
        ),
        out_shape=jax.ShapeDtypeStruct(xs.shape, F32),
        compiler_params=_cparams("arbitrary"),
        name="moe_experts",
    )(blk_expert, n_active, xs, wg, wu, wd)


def _combine_kernel(dest_ref, w_ref, x1_ref, h2_ref, gate_ref, wsg_ref, wsu_ref, wsd_ref, gf_ref,
                    ys_hbm, o_ref, buf, sem, *, chunks):
    tm = x1_ref.shape[1]

    def issue(r, _):
        for k in range(TOP_K):
            _row_copy(ys_hbm, buf.at[k], dest_ref[0, 0, r * TOP_K + k], r, sem, chunks).start()
        return 0

    def drain(r, _):
        for k in range(TOP_K):
            _row_copy(ys_hbm, buf.at[k], 0, r, sem, chunks).wait()
        return 0

    lax.fori_loop(0, tm, issue, 0)
    h_bf = h2_ref[0].astype(BF16)
    hid = (_silu(_dot(h_bf, wsg_ref[...])) * _dot(h_bf, wsu_ref[...])).astype(BF16)
    shared = _dot(hid, wsd_ref[...])
    lax.fori_loop(0, tm, drain, 0)

    w = w_ref[0]
    cols = []
    for c in range(chunks):
        tot = jnp.zeros((tm, LANES), F32)
        for k in range(TOP_K):
            tot = tot + buf[k, pl.ds(c, tm, stride=chunks), :] * w[:, k:k + 1]
        cols.append(tot)
    routed = jnp.concatenate(cols, axis=1)
    x2 = x1_ref[0] + gate_ref[0] * (routed + shared)
    o_ref[0] = _rms(x2, gf_ref[...])


def _combine(dest, wts, x1, h2, gate, wsg_bf, wsu_bf, wsd_bf, gf, ys, tm, chunks):
    b, t, d = x1.shape
    nt = t // tm
    row = lambda n: pl.BlockSpec((1, tm, n), lambda i, j: (i, j, 0))
    const = lambda shape: pl.BlockSpec(shape, lambda i, j: (0,) * len(shape))
    return pl.pallas_call(
        functools.partial(_combine_kernel, chunks=chunks),
        grid=(b, nt),
        in_specs=[pl.BlockSpec((1, 1, tm * TOP_K), lambda i, j: (i * nt + j, 0, 0),
                               memory_space=pltpu.SMEM),
                  row(LANES), row(d), row(d), _mod_spec(gate, tm, d),
                  const(wsg_bf.shape), const(wsu_bf.shape), const(wsd_bf.shape), const(gf.shape),
                  pl.BlockSpec(memory_space=pl.ANY)],
        out_specs=row(d),
        out_shape=jax.ShapeDtypeStruct((b, t, d), F32),
        scratch_shapes=[pltpu.VMEM((TOP_K, tm * chunks, LANES), F32), pltpu.SemaphoreType.DMA(())],
        compiler_params=_cparams("arbitrary", "arbitrary"),
        name="moe_combine",
    )(dest, wts, x1, h2, gate, wsg_bf, wsu_bf, wsd_bf, gf, ys)


def _split_mod(mod, per_token):
    parts = jnp.split(mod, 6, axis=-1)
    if per_token:
        return [p[None] for p in parts]
    return [p[:, None, :] for p in parts]


def _padded_in_weight(w_in, conv_width):
    n_gate = 3 * N_HEADS
    o = ATTN_WIDTH + 6 * KV_COLS
    main = w_in[:, :o]
    gates = jnp.pad(w_in[:, o:o + n_gate], ((0, 0), (0, LANES - n_gate)))
    glu = w_in[:, o + n_gate:o + n_gate + 2 * conv_width]
    return jnp.concatenate([main, gates, glu], axis=1).astype(BF16)


def _cmp_rows(x):
    return x.reshape(x.shape[:-2] + (x.shape[-2] // CMP_BLOCK, CMP_BLOCK * KV_COLS))


def _largest_tile(n, cap):
    best = [k for k in range(SUBLANES, cap + 1, SUBLANES) if n % k == 0]
    assert best, (n, cap)
    return best[-1]


def _kv5(x):
    return x.reshape(x.shape[:-1] + (N_KV, HEAD_DIM))[None]


def kernel(x_prompt, x_sample, cache_k_cmp, cache_v_cmp, cache_k_slc, cache_v_slc, state_k_win, state_v_win, state_conv, page_table, c_prompt, c_sample, norm1_g, norm2_g, w_ada, b_ada, w_in, w_cmp_k, w_cmp_v, w_dw, b_dw, ln_conv_g, ln_conv_b, g_out_attn, g_out_conv, w_out, router_w, router_b, w_exp_gate, w_exp_up, w_exp_down, w_sh_gate, w_sh_up, w_sh_down, norm_f_g):
    assert w_ada.shape[0] == 1, "single layer"
    bp, t, d = x_prompt.shape
    bs, s_new, _ = x_sample.shape
    assert s_new == 1
    n_pool, page = cache_k_cmp.shape[1], cache_k_cmp.shape[2]
    n_pages = page_table.shape[1]
    past = n_pages * page
    conv_width = state_conv.shape[-1]
    n_exp = router_w.shape[-1]
    chunks = d // LANES
    tm = min(ROW_TILE, t)

    w_in_bf = _padded_in_weight(w_in[0], conv_width)
    wck = _compress_weight(w_cmp_k[0])
    wcv = _compress_weight(w_cmp_v[0])
    wout_bf = w_out[0].astype(BF16)
    rw_hi = router_w[0].astype(BF16)
    rw_lo = (router_w[0] - rw_hi.astype(F32)).astype(BF16)
    wsg_bf, wsu_bf, wsd_bf = (w[0].astype(BF16) for w in (w_sh_gate, w_sh_up, w_sh_down))
    gf = norm_f_g[None, :]

    n_c = bp + bs
    c_all = jnp.concatenate([c_prompt, c_sample], axis=0)
    c_all = jnp.pad(c_all, ((0, (-n_c) % SUBLANES), (0, 0)))
    mod = _modulation(c_all, w_ada[0], b_ada)
    mp = _split_mod(mod[:bp], per_token=False)
    ms = _split_mod(mod[bp:n_c], per_token=True)

    (q_p, kc_p, vc_p, ks_p, vs_p, kw_p, vw_p, gate_p, u_p) = _in_proj(
        x_prompt, mp[0], mp[1], norm1_g, w_in_bf, tm)
    nc_p = t // CMP_BLOCK
    kcc, vcc = _compress(_cmp_rows(kc_p).reshape(bp * nc_p, -1), _cmp_rows(vc_p).reshape(bp * nc_p, -1),
                         wck, wcv)
    o_attn_p = _prompt_attention(q_p, gate_p, kcc.reshape(bp, nc_p, KV_COLS), vcc.reshape(bp, nc_p, KV_COLS),
                                 ks_p, vs_p, kw_p, vw_p)
    o_conv_p = _conv_prompt(u_p, w_dw[0], b_dw, ln_conv_g, ln_conv_b, tm)

    xs_row = x_sample.reshape(1, bs, d)
    (q_s, kc_s, vc_s, ks_s, vs_s, kw_s, vw_s, gate_s, u_s) = _in_proj(
        xs_row, ms[0], ms[1], norm1_g, w_in_bf, bs)
    q8 = q_s.reshape(bs, N_HEADS, HEAD_DIM)
    gates8 = gate_s[0, :, :3 * N_HEADS].reshape(bs, N_HEADS, 3)
    pages_t = lambda c: jnp.transpose(c[0], (0, 2, 3, 1))
    tail = (-(past + s_new)) % SLC_BLOCK
    n_new = (s_new + tail) // CMP_BLOCK
    tail_rows = lambda x: _cmp_rows(jnp.pad(x[0][:, None, :], ((0, 0), (0, tail), (0, 0)))).reshape(bs * n_new, -1)
    pad_rows = (-(bs * n_new)) % SUBLANES
    kc_new, vc_new = _compress(jnp.pad(tail_rows(kc_s), ((0, pad_rows), (0, 0))),
                               jnp.pad(tail_rows(vc_s), ((0, pad_rows), (0, 0))), wck, wcv)
    kc_new = kc_new[:bs * n_new].reshape(bs, n_new, KV_COLS)
    vc_new = vc_new[:bs * n_new].reshape(bs, n_new, KV_COLS)
    reps = page // CMP_BLOCK
    wk_fold = jnp.transpose(w_cmp_k[0], (2, 1, 0)).reshape(HEAD_DIM, HEAD_DIM * CMP_BLOCK)
    ut = _matmul3(q8.reshape(bs * N_HEADS, HEAD_DIM) * ATTN_SCALE, wk_fold)
    ut = jnp.tile(ut.reshape(bs, N_HEADS, HEAD_DIM, CMP_BLOCK), (1, 1, 1, reps))
    s_raw = _sample_scores(page_table, ut, pages_t(cache_k_cmp))
    p_exp, p_new, sel = _sample_select(s_raw, q8, kc_new, past)
    y_acc = _sample_values(page_table, p_exp, pages_t(cache_v_cmp))
    wv_fold = jnp.tile(jnp.transpose(w_cmp_v[0], (1, 0, 2)), (1, reps, 1)).reshape(HEAD_DIM * page, HEAD_DIM)
    o_cmp_s = _matmul3(y_acc.reshape(bs * N_HEADS, HEAD_DIM * page), wv_fold).reshape(bs, N_HEADS, HEAD_DIM)
    n_sel = min(N_SEL, (past // CMP_BLOCK + n_new) // CMP_PER_SLC)
    sel = sel[:, :N_KV, :n_sel]
    row3 = lambda x: x[0][:, None, :]
    o_attn_s = _sample_attend(
        sel, page_table, q8, pages_t(cache_k_slc), pages_t(cache_v_slc), row3(ks_s), row3(vs_s),
        pages_t(state_k_win), pages_t(state_v_win), row3(kw_s), row3(vw_s), gates8, o_cmp_s, p_new, vc_new,
        past)
    o_attn_s = o_attn_s.reshape(1, bs, ATTN_WIDTH)
    up_s = jnp.concatenate([state_conv[0], u_s[0][:, None, :]], axis=1)
    o_conv_s = _conv_sample(up_s, w_dw[0], b_dw, ln_conv_g, ln_conv_b)[None]

    router = functools.partial(_merge_router, goa=g_out_attn, goc=g_out_conv, wout_bf=wout_bf, g2=norm2_g,
                               rw_hi=rw_hi, rw_lo=rw_lo, rb=router_b)
    x1_p, h2_p, e_p, w_p, r_p, cnt = router(o_attn_p, o_conv_p, x_prompt, mp[2], mp[3], mp[4],
                                            cnt_in=jnp.zeros((1, n_exp), F32), tm=tm)
    x1_s, h2_s, e_s, w_s, r_s, cnt = router(o_attn_s, o_conv_s, xs_row, ms[2], ms[3], ms[4],
                                            cnt_in=cnt, tm=bs)

    n_tok = bp * t + bs
    counts = cnt[0].astype(I32)
    padded = (counts + MOE_ROWS - 1) // MOE_ROWS * MOE_ROWS
    pad_end = jnp.cumsum(padded)
    pad_start = (pad_end - padded).astype(F32)[None, :]
    n_blocks = -(-(n_tok * TOP_K) // MOE_ROWS) + n_exp
    blk_expert = jnp.minimum(jnp.searchsorted(pad_end, jnp.arange(n_blocks, dtype=I32) * MOE_ROWS,
                                              side='right'), n_exp - 1).astype(I32)
    n_active = (pad_end[-1:] // MOE_ROWS).astype(I32)
    dest_p = _slots(e_p, r_p, pad_start, tm)[:, :, :TOP_K]
    dest_s = _slots(e_s, r_s, pad_start, bs)[:, :, :TOP_K]
    dest = jnp.concatenate([dest_p.reshape(-1, TOP_K), dest_s.reshape(-1, TOP_K)], axis=0)

    tile = _largest_tile(n_tok, 512)
    h_rows = jnp.concatenate([h2_p.reshape(-1, d), h2_s.reshape(-1, d)], axis=0).reshape(n_tok * chunks, LANES)
    xs = _dispatch(counts, pad_end.astype(I32), dest.reshape(n_tok // tile, 1, tile * TOP_K), h_rows,
                   tile, MOE_ROWS, chunks, n_blocks)
    ys = _experts(blk_expert, n_active, xs, w_exp_gate[0], w_exp_up[0], w_exp_down[0], MOE_ROWS, chunks)

    comb = functools.partial(_combine, wsg_bf=wsg_bf, wsu_bf=wsu_bf, wsd_bf=wsd_bf, gf=gf, ys=ys, chunks=chunks)
    y_prompt = comb(dest_p.reshape(bp * (t // tm), 1, tm * TOP_K), w_p, x1_p, h2_p, mp[5], tm=tm)
    y_sample = comb(dest_s.reshape(1, 1, bs * TOP_K), w_s, x1_s, h2_s, ms[5], tm=bs).reshape(bs, 1, d)

    win = min(WINDOW, t)
    hist = state_conv.shape[2]
    out_p = [_kv5(a) for a in (kc_p, vc_p, ks_p, vs_p, kw_p[:, t - win:], vw_p[:, t - win:])]
    conv_p = u_p[:, t - hist:][None]
    out_s = [_kv5(a[0][:, None, :]) for a in (kc_s, vc_s, ks_s, vs_s)]
    w_buf = state_k_win.shape[2]
    kw_buf = jnp.concatenate([state_k_win, _kv5(kw_s[0][:, None, :])], axis=2)[:, :, -w_buf:]
    vw_buf = jnp.concatenate([state_v_win, _kv5(vw_s[0][:, None, :])], axis=2)[:, :, -w_buf:]
    conv_s = up_s[:, -hist:][None]
    return (y_prompt, y_sample, *out_p, conv_p, *out_s, kw_buf, vw_buf, conv_s)
```

```python
import functools

import jax
import jax.numpy as jnp
from jax import lax
from jax.experimental import pallas as pl
from jax.experimental.pallas import tpu as pltpu

F32 = jnp.float32
BF16 = jnp.bfloat16
I32 = jnp.int32

N_HEADS = 8
HEAD_DIM = 64
N_KV = 2
Q_PER_KV = N_HEADS // N_KV
ATTN_WIDTH = N_HEADS * HEAD_DIM
KV_COLS = N_KV * HEAD_DIM
CMP_BLOCK = 32
SLC_BLOCK = 64
CMP_PER_SLC = SLC_BLOCK // CMP_BLOCK
N_SEL = 16
WINDOW = 512
TOP_K = 8
ROUTE_SCALE = 2.5
EPS = 1e-6
FORCED = 1e4
NEG = -1e30
ATTN_SCALE = HEAD_DIM ** -0.5

LANES = 128
SUBLANES = 8
VMEM_LIMIT = 56 * 1024 * 1024

ROW_TILE = 256
Q_TILE = 128
KEY_TILE = 1024
CMP_ROW_TILE = 512
MOE_ROWS = 256
CONV_HALO = 32
PAGES_PER_STEP = 32


def _cparams(*sem):
    return pltpu.CompilerParams(dimension_semantics=sem, vmem_limit_bytes=VMEM_LIMIT)


def _dot(a, b):
    return jnp.dot(a, b, preferred_element_type=F32)


def _dot_nt(a, b):
    return lax.dot_general(a, b, (((1,), (1,)), ((), ())), preferred_element_type=F32)


def _dot_tn(a, b):
    return lax.dot_general(a, b, (((0,), (0,)), ((), ())), preferred_element_type=F32)


def _split2(x):
    hi = x.astype(BF16)
    lo = (x - hi.astype(F32)).astype(BF16)
    return hi, lo


def _dot3(a, b):
    ah, al = _split2(a)
    bh, bl = _split2(b)
    return _dot(ah, bh) + (_dot(ah, bl) + _dot(al, bh))


def _dot3_nt(a, b):
    ah, al = _split2(a)
    bh, bl = _split2(b)
    return _dot_nt(ah, bh) + (_dot_nt(ah, bl) + _dot_nt(al, bh))


def _sigmoid(x):
    return 1.0 / (1.0 + jnp.exp(-x))


def _silu(x):
    return x * _sigmoid(x)


def _rms(x, g):
    return x * lax.rsqrt(jnp.mean(x * x, axis=-1, keepdims=True) + EPS) * g


def _alibi_slope_col(rows, rows_per_head, first_head, n_heads):
    r = lax.broadcasted_iota(I32, (rows, 1), 0) // rows_per_head
    out = jnp.zeros((rows, 1), F32)
    for k in range(n_heads):
        out = jnp.where(r == k, 2.0 ** (-8.0 * (first_head + k + 1) / N_HEADS), out)
    return out


def _modulation_kernel(c_ref, w_ref, b_ref, o_ref):
    o_ref[...] = _dot3(c_ref[...], w_ref[...]) + b_ref[...]


def _modulation(c, w, b):
    m, d = c.shape
    n = w.shape[1]
    tn = 768
    return pl.pallas_call(
        _modulation_kernel,
        grid=(n // tn,),
        in_specs=[pl.BlockSpec((m, d), lambda j: (0, 0)),
                  pl.BlockSpec((d, tn), lambda j: (0, j)),
                  pl.BlockSpec((1, tn), lambda j: (0, j))],
        out_specs=pl.BlockSpec((m, tn), lambda j: (0, j)),
        out_shape=jax.ShapeDtypeStruct((m, n), F32),
        compiler_params=_cparams("arbitrary"),
        name="modulation",
    )(c, w, b)


def _mod_spec(mod, tm, d):
    if mod.shape[1] == 1:
        return pl.BlockSpec((1, 1, d), lambda i, j, *_: (i, 0, 0))
    return pl.BlockSpec((1, tm, d), lambda i, j, *_: (i, j, 0))


def _in_proj_kernel(x_ref, shift_ref, scale_ref, g_ref, w_ref,
                    q_ref, kc_ref, vc_ref, ks_ref, vs_ref, kw_ref, vw_ref, gate_ref, u_ref):
    x = x_ref[0]
    h = _rms(x, g_ref[...]) * (1.0 + scale_ref[0]) + shift_ref[0]
    z = _dot(h.astype(BF16), w_ref[...])
    q_ref[0] = z[:, :ATTN_WIDTH]
    o = ATTN_WIDTH
    for ref in (kc_ref, vc_ref, ks_ref, vs_ref, kw_ref, vw_ref):
        ref[0] = z[:, o:o + KV_COLS]
        o += KV_COLS
    gate_ref[0] = _sigmoid(z[:, o:o + LANES])
    o += LANES
    cw = u_ref.shape[-1]
    u_ref[0] = z[:, o:o + cw] * _sigmoid(z[:, o + cw:o + 2 * cw])


def _in_proj(x, shift, scale, g, w_bf, tm):
    b, t, d = x.shape
    cw = (w_bf.shape[1] - ATTN_WIDTH - 6 * KV_COLS - LANES) // 2
    row = lambda n: pl.BlockSpec((1, tm, n), lambda i, j: (i, j, 0))
    sds = lambda n: jax.ShapeDtypeStruct((b, t, n), F32)
    return pl.pallas_call(
        _in_proj_kernel,
        grid=(b, t // tm),
        in_specs=[row(d), _mod_spec(shift, tm, d), _mod_spec(scale, tm, d),
                  pl.BlockSpec((1, d), lambda i, j: (0, 0)),
                  pl.BlockSpec(w_bf.shape, lambda i, j: (0, 0))],
        out_specs=[row(ATTN_WIDTH)] + [row(KV_COLS)] * 6 + [row(LANES), row(cw)],
        out_shape=[sds(ATTN_WIDTH)] + [sds(KV_COLS)] * 6 + [sds(LANES), sds(cw)],
        compiler_params=_cparams("arbitrary", "arbitrary"),
        name="in_proj",
    )(x, shift, scale, g, w_bf)


def _compress_kernel(k_ref, v_ref, wk_ref, wv_ref, ko_ref, vo_ref):
    ko_ref[...] = _dot3(k_ref[...], wk_ref[...])
    vo_ref[...] = _dot3(v_ref[...], wv_ref[...])


def _compress(k_rows, v_rows, wk, wv):
    r, kdim = k_rows.shape
    tr = min(CMP_ROW_TILE, r)
    assert r % tr == 0
    rows = pl.BlockSpec((tr, kdim), lambda i: (i, 0))
    wspec = pl.BlockSpec((kdim, KV_COLS), lambda i: (0, 0))
    ospec = pl.BlockSpec((tr, KV_COLS), lambda i: (i, 0))
    return pl.pallas_call(
        _compress_kernel,
        grid=(r // tr,),
        in_specs=[rows, rows, wspec, wspec],
        out_specs=[ospec, ospec],
        out_shape=[jax.ShapeDtypeStruct((r, KV_COLS), F32)] * 2,
        compiler_params=_cparams("arbitrary"),
        name="compress",
    )(k_rows, v_rows, wk, wv)


def _compress_weight(w):
    eye = jnp.eye(N_KV, dtype=w.dtype)
    big = jnp.einsum('lde,gh->lgdhe', w, eye)
    return big.reshape(CMP_BLOCK * KV_COLS, KV_COLS)


def _pair_sum(x, axis):
    n = x.shape[axis]
    idx = lax.broadcasted_iota(I32, x.shape, axis)
    nxt = pltpu.roll(x, n - 1, axis)
    prv = pltpu.roll(x, 1, axis)
    return x + jnp.where((idx & 1) == 0, nxt, prv)


def _block_scores(imp, blk, q_pos, n_blocks_total):
    cur = q_pos // SLC_BLOCK
    valid = jnp.logical_and(blk * SLC_BLOCK <= q_pos, blk < n_blocks_total)
    forced = jnp.logical_or(blk == 0, jnp.logical_or(blk == cur, blk == cur - 1))
    return jnp.where(valid, jnp.where(forced, FORCED, imp), -1.0)


def _select_blocks(score, blk, n_sel):
    blk_f = blk.astype(F32)
    s = score
    for _ in range(n_sel):
        m = jnp.max(s, axis=0, keepdims=True)
        first = jnp.min(jnp.where(s == m, blk_f, 1e9), axis=0, keepdims=True)
        s = jnp.where(blk_f == first, -2.0, s)
    return jnp.where(jnp.logical_and(s == -2.0, score >= 0.0), 1.0, 0.0)


def _prompt_attn_kernel(q_ref, gate_ref, kc_ref, vc_ref, ks_ref, vs_ref, kw_ref, vw_ref, o_ref,
                        *, seq, n_sel):
    i = pl.program_id(1)
    tq = Q_TILE
    nc = kc_ref.shape[1]
    q_blk = q_ref[0] * ATTN_SCALE
    gates = gate_ref[0]
    slopes = [2.0 ** (-8.0 * (h + 1) / N_HEADS) for h in range(N_HEADS)]
    group_heads = [list(range(g * Q_PER_KV, (g + 1) * Q_PER_KV)) for g in range(N_KV)]
    gsl = [slice(g * HEAD_DIM, (g + 1) * HEAD_DIM) for g in range(N_KV)]
    rsl = [slice(r * tq, (r + 1) * tq) for r in range(Q_PER_KV)]
    q_pos_col = i * tq + lax.broadcasted_iota(I32, (tq, 1), 0)
    q_pos_row = i * tq + lax.broadcasted_iota(I32, (1, tq), 1)
    qg = [jnp.concatenate([q_blk[:, h * HEAD_DIM:(h + 1) * HEAD_DIM] for h in hs], axis=0)
          for hs in group_heads]
    qg_bf = [x.astype(BF16) for x in qg]

    cmp_row = lax.broadcasted_iota(I32, (nc, tq), 0)
    dist_c = (q_pos_row - (cmp_row * CMP_BLOCK + (CMP_BLOCK - 1))).astype(F32)
    mask_c = dist_c >= 0.0
    blk = cmp_row >> 1
    o_cmp = [None] * N_HEADS
    sel_bf = []
    for g, hs in enumerate(group_heads):
        vc_bf = vc_ref[0][:, gsl[g]].astype(BF16)
        qk = _dot3_nt(kc_ref[0][:, gsl[g]], qg[g])
        imp = jnp.zeros((nc, tq), F32)
        for r, h in enumerate(hs):
            s = jnp.where(mask_c, qk[:, rsl[r]] - slopes[h] * dist_c, NEG)
            m = jnp.max(s, axis=0, keepdims=True)
            p = jnp.where(mask_c, jnp.exp(s - m), 0.0)
            p = p / jnp.maximum(jnp.sum(p, axis=0, keepdims=True), 1e-30)
            o_cmp[h] = _dot_tn(p.astype(BF16), vc_bf)
            imp = imp + p
        score = _block_scores(_pair_sum(imp, 0), blk, q_pos_row, seq // SLC_BLOCK)
        sel_bf.append(_select_blocks(score, blk, n_sel).astype(BF16))

    span = WINDOW + tq
    w_start = pl.multiple_of(jnp.maximum(i * tq - WINDOW, 0), tq)
    dist_w = (q_pos_col - (w_start + lax.broadcasted_iota(I32, (tq, span), 1))).astype(F32)
    bias_w = jnp.where(jnp.logical_and(dist_w >= 0.0, dist_w <= float(WINDOW)), 0.0, NEG)
    o_win = [None] * N_HEADS
    for g, hs in enumerate(group_heads):
        kw_bf = kw_ref[0, pl.ds(w_start, span), :][:, gsl[g]].astype(BF16)
        vw_bf = vw_ref[0, pl.ds(w_start, span), :][:, gsl[g]].astype(BF16)
        s_all = _dot_nt(qg_bf[g], kw_bf)
        probs, sums = [], []
        for r, h in enumerate(hs):
            s = s_all[rsl[r]] + (bias_w - slopes[h] * dist_w)
            p = jnp.exp(s - jnp.max(s, axis=1, keepdims=True))
            sums.append(jnp.sum(p, axis=1, keepdims=True))
            probs.append(p.astype(BF16))
        o_all = _dot(jnp.concatenate(probs, axis=0), vw_bf)
        for r, h in enumerate(hs):
            o_win[h] = o_all[rsl[r]] / sums[r]

    n_tiles = ((i + 1) * tq + KEY_TILE - 1) // KEY_TILE

    def slc_step(t, carry):
        ms, ls, accs = (list(c) for c in carry)
        k0 = pl.multiple_of(t * KEY_TILE, KEY_TILE)
        dist = (q_pos_col - (k0 + lax.broadcasted_iota(I32, (tq, KEY_TILE), 1))).astype(F32)
        causal = dist >= 0.0
        key_cmp = (k0 + lax.broadcasted_iota(I32, (nc, KEY_TILE), 1)) // CMP_BLOCK
        expand = jnp.where(key_cmp == lax.broadcasted_iota(I32, (nc, KEY_TILE), 0), 1.0, 0.0).astype(BF16)
        for g, hs in enumerate(group_heads):
            kt_bf = ks_ref[0, pl.ds(k0, KEY_TILE), :][:, gsl[g]].astype(BF16)
            vt_bf = vs_ref[0, pl.ds(k0, KEY_TILE), :][:, gsl[g]].astype(BF16)
            chosen = _dot_tn(sel_bf[g], expand)
            bias = jnp.where(jnp.logical_and(causal, chosen > 0.5), 0.0, NEG)
            s_all = _dot_nt(qg_bf[g], kt_bf)
            probs, alphas = [], []
            for r, h in enumerate(hs):
                s = s_all[rsl[r]] + (bias - slopes[h] * dist)
                m_new = jnp.maximum(ms[h], jnp.max(s, axis=1, keepdims=True))
                alpha = jnp.exp(ms[h] - m_new)
                p = jnp.exp(s - m_new)
                ls[h] = alpha * ls[h] + jnp.sum(p, axis=1, keepdims=True)
                ms[h] = m_new
                alphas.append(alpha)
                probs.append(p.astype(BF16))
            pv = _dot(jnp.concatenate(probs, axis=0), vt_bf)
            for r, h in enumerate(hs):
                accs[h] = alphas[r] * accs[h] + pv[rsl[r]]
        return tuple(ms), tuple(ls), tuple(accs)

    init = (tuple(jnp.full((tq, 1), NEG, F32) for _ in range(N_HEADS)),
            tuple(jnp.zeros((tq, 1), F32) for _ in range(N_HEADS)),
            tuple(jnp.zeros((tq, HEAD_DIM), F32) for _ in range(N_HEADS)))
    _, l_s, acc_s = lax.fori_loop(0, n_tiles, slc_step, init)

    pieces = []
    for h in range(N_HEADS):
        o_slc = acc_s[h] / jnp.maximum(l_s[h], 1e-30)
        pieces.append(o_cmp[h] * gates[:, 3 * h + 0:3 * h + 1] + o_slc * gates[:, 3 * h + 1:3 * h + 2]
                      + o_win[h] * gates[:, 3 * h + 2:3 * h + 3])
    o_ref[0] = jnp.concatenate(pieces, axis=1)


def _prompt_attention(q, gates, kc, vc, ks, vs, kw, vw):
    b, t, _ = q.shape
    nc = kc.shape[1]
    assert t % KEY_TILE == 0 and t >= WINDOW + Q_TILE
    n_sel = min(N_SEL, t // SLC_BLOCK)
    qspec = lambda n: pl.BlockSpec((1, Q_TILE, n), lambda bi, i: (bi, i, 0))
    full = lambda r: pl.BlockSpec((1, r, KV_COLS), lambda bi, i: (bi, 0, 0))
    return pl.pallas_call(
        functools.partial(_prompt_attn_kernel, seq=t, n_sel=n_sel),
        grid=(b, t // Q_TILE),
        in_specs=[qspec(ATTN_WIDTH), qspec(LANES), full(nc), full(nc),
                  full(t), full(t), full(t), full(t)],
        out_specs=qspec(ATTN_WIDTH),
        out_shape=jax.ShapeDtypeStruct((b, t, ATTN_WIDTH), F32),
        compiler_params=_cparams("arbitrary", "arbitrary"),
        name="prompt_attention",
    )(q, gates, kc, vc, ks, vs, kw, vw)


def _merge_groups(per_group):
    row = lax.broadcasted_iota(I32, per_group[0].shape, 0) // Q_PER_KV
    out = per_group[0]
    for g in range(1, N_KV):
        out = jnp.where(row == g, per_group[g], out)
    return out


def _group_slice(x, g):
    return x[:, g * HEAD_DIM:(g + 1) * HEAD_DIM]


def _matmul3_kernel(a_ref, b_ref, o_ref):
    o_ref[...] = _dot3(a_ref[...], b_ref[...])


def _matmul3(a, b):
    return pl.pallas_call(
        _matmul3_kernel,
        out_shape=jax.ShapeDtypeStruct((a.shape[0], b.shape[1]), F32),
        compiler_params=pltpu.CompilerParams(vmem_limit_bytes=VMEM_LIMIT),
        name="matmul3",
    )(a, b)


def _page_specs(n_pages, page):
    def spec(o):
        return pl.BlockSpec((1, N_KV, HEAD_DIM, page),
                            lambda i, j, pt: (pt[i * n_pages + j * PAGES_PER_STEP + o], 0, 0, 0))
    return [spec(o) for o in range(PAGES_PER_STEP)]


def _sample_scores_kernel(pt_ref, ut_ref, *refs):
    k_refs, o_ref = refs[:-1], refs[-1]
    for h in range(N_HEADS):
        g = h // Q_PER_KV
        u = ut_ref[0, h]
        rows = [jnp.sum(k_ref[0, g] * u, axis=0, keepdims=True) for k_ref in k_refs]
        o_ref[0, h] = jnp.concatenate(rows, axis=0)


def _sample_scores(page_table, ut, k_pages):
    b, n_pages = page_table.shape
    page = k_pages.shape[-1]
    assert n_pages % PAGES_PER_STEP == 0
    return pl.pallas_call(
        _sample_scores_kernel,
        grid_spec=pltpu.PrefetchScalarGridSpec(
            num_scalar_prefetch=1,
            grid=(b, n_pages // PAGES_PER_STEP),
            in_specs=[pl.BlockSpec((1, N_HEADS, HEAD_DIM, page), lambda i, j, pt: (i, 0, 0, 0))]
                     + _page_specs(n_pages, page),
            out_specs=pl.BlockSpec((1, N_HEADS, PAGES_PER_STEP, page), lambda i, j, pt: (i, 0, j, 0)),
        ),
        out_shape=jax.ShapeDtypeStruct((b, N_HEADS, n_pages, page), F32),
        compiler_params=_cparams("arbitrary", "arbitrary"),
        name="sample_scores",
    )(page_table.reshape(-1), ut, *([k_pages] * PAGES_PER_STEP))


def _max_all(x):
    return jnp.max(jnp.max(x, axis=0, keepdims=True), axis=1, keepdims=True)


def _min_all(x):
    return jnp.min(jnp.min(x, axis=0, keepdims=True), axis=1, keepdims=True)


def _sum_all(x):
    return jnp.sum(jnp.sum(x, axis=0, keepdims=True), axis=1, keepdims=True)


def _sample_select_kernel(s_ref, q_ref, kcn_ref, pexp_ref, pnew_ref, sel_ref, *, past, n_new, n_sel):
    n_pages, page = s_ref.shape[2], s_ref.shape[3]
    cpp = page // CMP_BLOCK
    n_past = n_pages * cpp
    n_blocks_total = (n_past + n_new) // CMP_PER_SLC
    lane = lax.broadcasted_iota(I32, (n_pages, page), 1)
    prow = lax.broadcasted_iota(I32, (n_pages, page), 0)
    dist = (past - ((prow * cpp + lane // CMP_BLOCK) * CMP_BLOCK + (CMP_BLOCK - 1))).astype(F32)
    mask = jnp.logical_and(lane % CMP_BLOCK == 0, dist >= 0.0)

    q8 = q_ref[0] * ATTN_SCALE
    slope = _alibi_slope_col(N_HEADS, 1, 0, N_HEADS)
    kcn = jnp.concatenate([kcn_ref[0], jnp.zeros((LANES - n_new, KV_COLS), F32)], axis=0)
    new_lane = lax.broadcasted_iota(I32, (N_HEADS, LANES), 1)
    dist_n = (past - ((n_past + new_lane) * CMP_BLOCK + (CMP_BLOCK - 1))).astype(F32)
    mask_n = jnp.logical_and(dist_n >= 0.0, new_lane < n_new)
    qk_n = _merge_groups([_dot3_nt(q8, _group_slice(kcn, g)) for g in range(N_KV)])
    s_new = jnp.where(mask_n, qk_n - slope * dist_n, NEG)

    probs, probs_new = [], []
    for h in range(N_HEADS):
        x = s_ref[0, h]
        for sh in (16, 8, 4, 2, 1):
            x = x + pltpu.roll(x, page - sh, 1)
        s = jnp.where(mask, x - 2.0 ** (-8.0 * (h + 1) / N_HEADS) * dist, NEG)
        sn = s_new[h:h + 1, :]
        mn = jnp.logical_and(dist_n[h:h + 1, :] >= 0.0, new_lane[h:h + 1, :] < n_new)
        m = jnp.maximum(_max_all(s), jnp.max(sn, axis=1, keepdims=True))
        p = jnp.where(mask, jnp.exp(s - m), 0.0)
        pn = jnp.where(mn, jnp.exp(sn - m), 0.0)
        den = jnp.maximum(_sum_all(p) + jnp.sum(pn, axis=1, keepdims=True), 1e-30)
        p = p / den
        probs.append(p)
        probs_new.append(pn / den)
        z = p
        for sh in (1, 2, 4, 8, 16):
            z = z + pltpu.roll(z, sh, 1)
        pexp_ref[0, h] = z
    pnew_ref[0] = jnp.concatenate(probs_new, axis=0)

    row1 = lax.broadcasted_iota(I32, (1, LANES), 1)
    blk = jnp.where(lane % SLC_BLOCK == 0, prow * (page // SLC_BLOCK) + lane // SLC_BLOCK, -1)
    blk_n = jnp.where(row1 < n_new, n_past // CMP_PER_SLC + (row1 >> 1), -1)
    blk_f = blk.astype(F32)
    blk_nf = blk_n.astype(F32)
    out_lane = lax.broadcasted_iota(I32, (N_HEADS, LANES), 1)
    out_row = lax.broadcasted_iota(I32, (N_HEADS, LANES), 0)
    out = jnp.full((N_HEADS, LANES), -1, I32)
    for g in range(N_KV):
        imp = probs[g * Q_PER_KV]
        imp_n = probs_new[g * Q_PER_KV]
        for r in range(1, Q_PER_KV):
            imp = imp + probs[g * Q_PER_KV + r]
            imp_n = imp_n + probs_new[g * Q_PER_KV + r]
        imp = imp + pltpu.roll(imp, page - CMP_BLOCK, 1)
        s_m = jnp.where(blk >= 0, _block_scores(imp, blk, past, n_blocks_total), -4.0)
        s_n = jnp.where(blk_n >= 0, _block_scores(_pair_sum(imp_n, 1), blk_n, past, n_blocks_total), -4.0)
        for j in range(n_sel):
            top = jnp.maximum(_max_all(s_m), jnp.max(s_n, axis=1, keepdims=True))
            first = jnp.minimum(_min_all(jnp.where(s_m == top, blk_f, 1e9)),
                                jnp.min(jnp.where(s_n == top, blk_nf, 1e9), axis=1, keepdims=True))
            s_m = jnp.where(blk_f == first, -2.0, s_m)
            s_n = jnp.where(blk_nf == first, -2.0, s_n)
            pick = jnp.where(top >= 0.0, first.astype(I32), -1)
            out = jnp.where(jnp.logical_and(out_row == g, out_lane == j), pick, out)
    sel_ref[0] = out


def _sample_select(s_raw, q8, kc_new, past):
    b, _, n_pages, page = s_raw.shape
    n_new = kc_new.shape[1]
    assert CMP_PER_SLC == 2 and CMP_BLOCK == 32
    n_sel = min(N_SEL, (past // CMP_BLOCK + n_new) // CMP_PER_SLC)
    per_b = lambda *s: pl.BlockSpec((1,) + s, lambda i: (i,) + (0,) * len(s))
    return pl.pallas_call(
        functools.partial(_sample_select_kernel, past=past, n_new=n_new, n_sel=n_sel),
        grid=(b,),
        in_specs=[per_b(N_HEADS, n_pages, page), per_b(N_HEADS, HEAD_DIM), per_b(n_new, KV_COLS)],
        out_specs=[per_b(N_HEADS, n_pages, page), per_b(N_HEADS, LANES), per_b(N_HEADS, LANES)],
        out_shape=[jax.ShapeDtypeStruct((b, N_HEADS, n_pages, page), F32),
                   jax.ShapeDtypeStruct((b, N_HEADS, LANES), F32),
                   jax.ShapeDtypeStruct((b, N_HEADS, LANES), I32)],
        compiler_params=_cparams("arbitrary"),
        name="sample_select",
    )(s_raw, q8, kc_new)


def _sample_values_kernel(pt_ref, pe_ref, *refs):
    v_refs, y_ref = refs[:-1], refs[-1]

    @pl.when(pl.program_id(1) == 0)
    def _():
        y_ref[...] = jnp.zeros(y_ref.shape, F32)

    for g in range(N_KV):
        heads = range(g * Q_PER_KV, (g + 1) * Q_PER_KV)
        pe = [pe_ref[0, h] for h in heads]
        acc = [jnp.zeros(y_ref.shape[2:], F32) for _ in heads]
        for o, v_ref in enumerate(v_refs):
            v = v_ref[0, g]
            for r in range(Q_PER_KV):
                acc[r] = acc[r] + v * pe[r][o:o + 1, :]
        for r, h in enumerate(heads):
            y_ref[0, h] = y_ref[0, h] + acc[r]


def _sample_values(page_table, pexp, v_pages):
    b, n_pages = page_table.shape
    page = v_pages.shape[-1]
    return pl.pallas_call(
        _sample_values_kernel,
        grid_spec=pltpu.PrefetchScalarGridSpec(
            num_scalar_prefetch=1,
            grid=(b, n_pages // PAGES_PER_STEP),
            in_specs=[pl.BlockSpec((1, N_HEADS, PAGES_PER_STEP, page), lambda i, j, pt: (i, 0, j, 0))]
                     + _page_specs(n_pages, page),
            out_specs=pl.BlockSpec((1, N_HEADS, HEAD_DIM, page), lambda i, j, pt: (i, 0, 0, 0)),
        ),
        out_shape=jax.ShapeDtypeStruct((b, N_HEADS, HEAD_DIM, page), F32),
        compiler_params=_cparams("arbitrary", "arbitrary"),
        name="sample_values",
    )(page_table.reshape(-1), pexp, *([v_pages] * PAGES_PER_STEP))


def _new_token_terms(q8, k_row, v_row):
    s = _merge_groups([jnp.sum(q8 * _group_slice(k_row, g), axis=1, keepdims=True) for g in range(N_KV)])
    v = _merge_groups([jnp.broadcast_to(_group_slice(v_row, g), (N_HEADS, HEAD_DIM)) for g in range(N_KV)])
    return s, v


def _sample_attend_kernel(sel_ref, pt_ref, q_ref, *refs, past, n_sel, ns_past):
    page_refs = refs[:4 * n_sel]
    (ksn_ref, vsn_ref, kw_ref, vw_ref, kwn_ref, vwn_ref, gate_ref, ocmp_ref, pnew_ref, vcn_ref,
     o_ref) = refs[4 * n_sel:]
    b = pl.program_id(0)
    page = page_refs[0].shape[-1]
    spp = page // SLC_BLOCK
    q8 = q_ref[0] * ATTN_SCALE
    q8_bf = q8.astype(BF16)
    slope = _alibi_slope_col(N_HEADS, 1, 0, N_HEADS)
    lane = lax.broadcasted_iota(I32, (N_HEADS, page), 1)
    s_t, v_t = _new_token_terms(q8, ksn_ref[0], vsn_ref[0])

    scores, masks, new_scores = [], [], []
    for n in range(n_sel):
        k_refs = page_refs[4 * n:4 * n + N_KV]
        blks = [sel_ref[(b * N_KV + g) * n_sel + n] for g in range(N_KV)]
        blk_rows = _merge_groups([jnp.full((N_HEADS, page), blk, I32) for blk in blks])
        blk_col = _merge_groups([jnp.full((N_HEADS, 1), blk, I32) for blk in blks])
        qk = _merge_groups([_dot(q8_bf, k_ref[0, 0].astype(BF16)) for k_ref in k_refs])
        page_pos = blk_rows // spp
        dist = (past - (page_pos * page + lane)).astype(F32)
        in_block = (lane // SLC_BLOCK) == (blk_rows - page_pos * spp)
        cached = jnp.logical_and(blk_rows >= 0, blk_rows < ns_past)
        mask = jnp.logical_and(jnp.logical_and(in_block, cached), dist >= 0.0)
        scores.append(jnp.where(mask, qk - slope * dist, NEG))
        masks.append(mask)
        new_scores.append(jnp.where(blk_col >= ns_past, s_t, NEG))
    m = new_scores[0]
    for s, sn in zip(scores, new_scores):
        m = jnp.maximum(m, jnp.maximum(jnp.max(s, axis=1, keepdims=True), sn))
    l_tot = jnp.zeros((N_HEADS, 1), F32)
    p_new = jnp.zeros((N_HEADS, 1), F32)
    acc = jnp.zeros((N_HEADS, HEAD_DIM), F32)
    for n in range(n_sel):
        v_refs = page_refs[4 * n + N_KV:4 * n + 2 * N_KV]
        p = jnp.where(masks[n], jnp.exp(scores[n] - m), 0.0)
        p_bf = p.astype(BF16)
        l_tot = l_tot + jnp.sum(p, axis=1, keepdims=True)
        p_new = p_new + jnp.where(new_scores[n] > 0.5 * NEG, jnp.exp(new_scores[n] - m), 0.0)
        acc = acc + _merge_groups([_dot_nt(p_bf, v_ref[0, 0].astype(BF16)) for v_ref in v_refs])
    o_slc = (acc + p_new * v_t) / jnp.maximum(l_tot + p_new, 1e-30)

    w_buf = kw_ref.shape[-1]
    wl = lax.broadcasted_iota(I32, (N_HEADS, w_buf), 1)
    win_pos = past - w_buf + wl
    dist_w = (past - win_pos).astype(F32)
    mask_w = jnp.logical_and(jnp.logical_and(dist_w >= 0.0, dist_w <= float(WINDOW)), win_pos >= 0)
    qk_w = _merge_groups([_dot(q8_bf, kw_ref[0, g].astype(BF16)) for g in range(N_KV)])
    s_w = jnp.where(mask_w, qk_w - slope * dist_w, NEG)
    s_t, v_t = _new_token_terms(q8, kwn_ref[0], vwn_ref[0])
    m_w = jnp.maximum(jnp.max(s_w, axis=1, keepdims=True), s_t)
    p_w = jnp.where(mask_w, jnp.exp(s_w - m_w), 0.0)
    p_t = jnp.exp(s_t - m_w)
    den = jnp.maximum(jnp.sum(p_w, axis=1, keepdims=True) + p_t, 1e-30)
    pw_bf = p_w.astype(BF16)
    o_w = _merge_groups([_dot_nt(pw_bf, vw_ref[0, g].astype(BF16)) for g in range(N_KV)])
    o_win = (o_w + p_t * v_t) / den
    n_new = vcn_ref.shape[1]
    vcn = jnp.concatenate([vcn_ref[0], jnp.zeros((LANES - n_new, KV_COLS), F32)], axis=0).astype(BF16)
    pn_bf = pnew_ref[0].astype(BF16)
    o_cmp = ocmp_ref[0] + _merge_groups([_dot(pn_bf, _group_slice(vcn, g)) for g in range(N_KV)])
    gt = gate_ref[0]
    o_ref[0] = o_cmp * gt[:, 0:1] + o_slc * gt[:, 1:2] + o_win * gt[:, 2:3]


def _sample_attend(sel, page_table, q8, k_pages, v_pages, ks_new, vs_new, kw_state, vw_state,
                   kw_new, vw_new, gates8, o_cmp, p_new, vc_new, past):
    b, n_pages = page_table.shape
    n_sel = sel.shape[-1]
    page = k_pages.shape[-1]
    ns_past = past // SLC_BLOCK
    spp = page // SLC_BLOCK
    w_buf = kw_state.shape[-1]
    n_new = vc_new.shape[1]

    def cache_map(n, g):
        def index(i, sel_ref, pt_ref):
            blk = jnp.clip(sel_ref[(i * N_KV + g) * n_sel + n], 0, ns_past - 1)
            return (pt_ref[i * n_pages + blk // spp], g, 0, 0)
        return pl.BlockSpec((1, 1, HEAD_DIM, page), index)

    page_specs, page_args = [], []
    for n in range(n_sel):
        for arr in (k_pages, v_pages):
            for g in range(N_KV):
                page_specs.append(cache_map(n, g))
                page_args.append(arr)
    per_b = lambda *s: pl.BlockSpec((1,) + s, lambda i, sl, pt: (i,) + (0,) * len(s))
    return pl.pallas_call(
        functools.partial(_sample_attend_kernel, past=past, n_sel=n_sel, ns_past=ns_past),
        grid_spec=pltpu.PrefetchScalarGridSpec(
            num_scalar_prefetch=2,
            grid=(b,),
            in_specs=[per_b(N_HEADS, HEAD_DIM)] + page_specs
                     + [per_b(1, KV_COLS), per_b(1, KV_COLS),
                        per_b(N_KV, HEAD_DIM, w_buf), per_b(N_KV, HEAD_DIM, w_buf),
                        per_b(1, KV_COLS), per_b(1, KV_COLS),
                        per_b(N_HEADS, 3), per_b(N_HEADS, HEAD_DIM), per_b(N_HEADS, LANES),
                        per_b(n_new, KV_COLS)],
            out_specs=per_b(N_HEADS, HEAD_DIM),
        ),
        out_shape=jax.ShapeDtypeStruct((b, N_HEADS, HEAD_DIM), F32),
        compiler_params=_cparams("arbitrary"),
        name="sample_attend",
    )(sel.reshape(-1), page_table.reshape(-1), q8, *page_args,
      ks_new, vs_new, kw_state, vw_state, kw_new, vw_new, gates8, o_cmp, p_new, vc_new)


def _layernorm_silu(y, g, b):
    mu = jnp.mean(y, axis=-1, keepdims=True)
    var = jnp.mean(jnp.square(y - mu), axis=-1, keepdims=True)
    return _silu((y - mu) * lax.rsqrt(var + EPS) * g + b)


def _conv_prompt_kernel(u_ref, w_ref, b_ref, g_ref, beta_ref, o_ref, buf):
    j = pl.program_id(1)
    tt = u_ref.shape[1]
    kw = w_ref.shape[0]

    @pl.when(j == 0)
    def _():
        buf[0:CONV_HALO, :] = jnp.zeros((CONV_HALO, buf.shape[1]), F32)

    buf[CONV_HALO:CONV_HALO + tt, :] = u_ref[0]
    w = w_ref[...]
    acc = jnp.zeros((tt, buf.shape[1]), F32)
    for k in range(kw):
        acc = acc + w[k:k + 1, :] * buf[pl.ds(CONV_HALO - (kw - 1) + k, tt), :]
    o_ref[0] = _layernorm_silu(acc + b_ref[...], g_ref[...], beta_ref[...])
    buf[0:CONV_HALO, :] = buf[tt:tt + CONV_HALO, :]


def _conv_prompt(u, w_dw, b_dw, ln_g, ln_b, tt):
    b, t, c = u.shape
    vec = pl.BlockSpec((1, c), lambda i, j: (0, 0))
    return pl.pallas_call(
        _conv_prompt_kernel,
        grid=(b, t // tt),
        in_specs=[pl.BlockSpec((1, tt, c), lambda i, j: (i, j, 0)),
                  pl.BlockSpec(w_dw.shape, lambda i, j: (0, 0)), vec, vec, vec],
        out_specs=pl.BlockSpec((1, tt, c), lambda i, j: (i, j, 0)),
        out_shape=jax.ShapeDtypeStruct((b, t, c), F32),
        scratch_shapes=[pltpu.VMEM((CONV_HALO + tt, c), F32)],
        compiler_params=_cparams("arbitrary", "arbitrary"),
        name="conv_prompt",
    )(u, w_dw, b_dw, ln_g, ln_b)


def _conv_sample_kernel(up_ref, w_ref, b_ref, g_ref, beta_ref, o_ref):
    y = jnp.sum(up_ref[...] * w_ref[...][None, :, :], axis=1)
    o_ref[...] = _layernorm_silu(y + b_ref[...], g_ref[...], beta_ref[...])


def _conv_sample(up, w_dw, b_dw, ln_g, ln_b):
    b, kw, c = up.shape
    return pl.pallas_call(
        _conv_sample_kernel,
        out_shape=jax.ShapeDtypeStruct((b, c), F32),
        name="conv_sample",
    )(up, w_dw, b_dw, ln_g, ln_b)


def _merge_router_kernel(oa_ref, oc_ref, x_ref, gate_ref, shift_ref, scale_ref, goa_ref, goc_ref,
                         wout_ref, g2_ref, rwh_ref, rwl_ref, rb_ref, cnt_in_ref,
                         x1_ref, h2_ref, eidx_ref, wts_ref, rank_ref, cnt_out_ref, run):
    first = jnp.logical_and(pl.program_id(0) == 0, pl.program_id(1) == 0)

    @pl.when(first)
    def _():
        run[...] = cnt_in_ref[...]

    a = _rms(oa_ref[0], goa_ref[...])
    c = _rms(oc_ref[0], goc_ref[...])
    cat = jnp.concatenate([a, c], axis=1).astype(BF16)
    x1 = x_ref[0] + gate_ref[0] * _dot(cat, wout_ref[...])
    x1_ref[0] = x1
    h2 = _rms(x1, g2_ref[...]) * (1.0 + scale_ref[0]) + shift_ref[0]
    h2_ref[0] = h2

    hh, hl = _split2(h2)
    logits = _dot(hh, rwh_ref[...]) + (_dot(hh, rwl_ref[...]) + _dot(hl, rwh_ref[...]))
    aff = _sigmoid(logits)
    tm, n_exp = aff.shape
    lane_f = lax.broadcasted_iota(I32, (tm, n_exp), 1).astype(F32)
    s = aff + rb_ref[...]
    hot = jnp.zeros((tm, n_exp), jnp.bool_)
    experts, weights = [], []
    for _ in range(TOP_K):
        m = jnp.max(s, axis=1, keepdims=True)
        e = jnp.min(jnp.where(s == m, lane_f, 1e9), axis=1, keepdims=True)
        pick = lane_f == e
        experts.append(e)
        weights.append(jnp.sum(jnp.where(pick, aff, 0.0), axis=1, keepdims=True))
        hot = jnp.logical_or(hot, pick)
        s = jnp.where(pick, NEG, s)
    total = weights[0]
    for w in weights[1:]:
        total = total + w

    hot_f = jnp.where(hot, 1.0, 0.0)
    r_i = lax.broadcasted_iota(I32, (tm, tm), 0)
    c_i = lax.broadcasted_iota(I32, (tm, tm), 1)
    lower = jnp.where(c_i < r_i, 1.0, 0.0).astype(BF16)
    before = _dot(lower, hot_f.astype(BF16)) + run[...]
    out_lane = lax.broadcasted_iota(I32, (tm, LANES), 1)
    e_out = jnp.zeros((tm, LANES), I32)
    w_out = jnp.zeros((tm, LANES), F32)
    r_out = jnp.zeros((tm, LANES), I32)
    for k in range(TOP_K):
        rank = jnp.sum(jnp.where(lane_f == experts[k], before, 0.0), axis=1, keepdims=True)
        e_out = jnp.where(out_lane == k, experts[k].astype(I32), e_out)
        w_out = jnp.where(out_lane == k, ROUTE_SCALE * weights[k] / total, w_out)
        r_out = jnp.where(out_lane == k, rank.astype(I32), r_out)
    eidx_ref[0] = e_out
    wts_ref[0] = w_out
    rank_ref[0] = r_out
    run[...] = run[...] + jnp.sum(hot_f, axis=0, keepdims=True)
    cnt_out_ref[...] = run[...]


def _merge_router(o_attn, o_conv, x, gate, shift, scale, goa, goc, wout_bf, g2, rw_hi, rw_lo, rb,
                  cnt_in, tm):
    b, t, d = x.shape
    n_exp = rw_hi.shape[1]
    row = lambda n: pl.BlockSpec((1, tm, n), lambda i, j: (i, j, 0))
    const = lambda shape: pl.BlockSpec(shape, lambda i, j: (0,) * len(shape))
    sds = lambda n, dt: jax.ShapeDtypeStruct((b, t, n), dt)
    return pl.pallas_call(
        _merge_router_kernel,
        grid=(b, t // tm),
        in_specs=[row(o_attn.shape[-1]), row(o_conv.shape[-1]), row(d),
                  _mod_spec(gate, tm, d), _mod_spec(shift, tm, d), _mod_spec(scale, tm, d),
                  const(goa.shape), const(goc.shape), const(wout_bf.shape), const(g2.shape),
                  const(rw_hi.shape), const(rw_lo.shape), const(rb.shape), const(cnt_in.shape)],
        out_specs=[row(d), row(d), row(LANES), row(LANES), row(LANES), const((1, n_exp))],
        out_shape=[sds(d, F32), sds(d, F32), sds(LANES, I32), sds(LANES, F32), sds(LANES, I32),
                   jax.ShapeDtypeStruct((1, n_exp), F32)],
        scratch_shapes=[pltpu.VMEM((1, n_exp), F32)],
        compiler_params=_cparams("arbitrary", "arbitrary"),
        name="merge_router",
    )(o_attn, o_conv, x, gate, shift, scale, goa, goc, wout_bf, g2, rw_hi, rw_lo, rb, cnt_in)


def _row_copy(src_hbm, dst, src_row, dst_row, sem, chunks):
    return pltpu.make_async_copy(src_hbm.at[pl.ds(src_row * chunks, chunks)],
                                 dst.at[pl.ds(dst_row * chunks, chunks)], sem)


def _slot(start_ref, e_ref, r_ref, idx):
    return start_ref[e_ref[0, 0, idx]] + r_ref[0, 0, idx]


def _dispatch_kernel(cnt_ref, end_ref, start_ref, e_ref, r_ref, h_ref, xs_hbm, zbuf, zsem, sem,
                     *, tokens, rows, chunks, n_blocks):
    j = pl.program_id(0)
    n_exp = cnt_ref.shape[0]
    blk_rows = rows * chunks

    def zero_block(blk):
        return pltpu.make_async_copy(zbuf, xs_hbm.at[pl.ds(blk * blk_rows, blk_rows)], zsem)

    @pl.when(j == 0)
    def _():
        zbuf[...] = jnp.zeros(zbuf.shape, F32)
        n_active = end_ref[n_exp - 1] // rows

        def zero_tail(e, issued):
            partial = cnt_ref[e] % rows != 0

            @pl.when(partial)
            def _():
                zero_block(end_ref[e] // rows - 1).start()

            return issued + partial.astype(I32)

        def zero_unused(blk, _):
            zero_block(blk).start()
            return 0

        def drain_zero(_, c):
            zero_block(0).wait()
            return c

        issued = lax.fori_loop(0, n_exp, zero_tail, 0)
        lax.fori_loop(n_active, n_blocks, zero_unused, 0)
        lax.fori_loop(0, issued + (n_blocks - n_active), drain_zero, 0)

    def issue(r, _):
        for k in range(TOP_K):
            _row_copy(h_ref, xs_hbm, r, _slot(start_ref, e_ref, r_ref, r * TOP_K + k), sem, chunks).start()
        return 0

    def drain(r, _):
        for k in range(TOP_K):
            _row_copy(h_ref, xs_hbm, 0, 0, sem, chunks).wait()
        return 0

    lax.fori_loop(0, tokens, issue, 0)
    lax.fori_loop(0, tokens, drain, 0)


def _dispatch(counts, pad_end, pad_start, e_idx, rank, h_rows, tokens, rows, chunks, n_blocks):
    n_tiles = e_idx.shape[0]
    picks = pl.BlockSpec((1, 1, tokens * TOP_K), lambda j, c, e, s: (j, 0, 0), memory_space=pltpu.SMEM)
    return pl.pallas_call(
        functools.partial(_dispatch_kernel, tokens=tokens, rows=rows, chunks=chunks, n_blocks=n_blocks),
        grid_spec=pltpu.PrefetchScalarGridSpec(
            num_scalar_prefetch=3,
            grid=(n_tiles,),
            in_specs=[picks, picks,
                      pl.BlockSpec((tokens * chunks, LANES), lambda j, c, e, s: (j, 0))],
            out_specs=pl.BlockSpec(memory_space=pl.ANY),
            scratch_shapes=[pltpu.VMEM((rows * chunks, LANES), F32),
                            pltpu.SemaphoreType.DMA(()), pltpu.SemaphoreType.DMA(())],
        ),
        out_shape=jax.ShapeDtypeStruct((n_blocks * rows * chunks, LANES), F32),
        compiler_params=_cparams("arbitrary"),
        name="moe_dispatch",
    )(counts, pad_end, pad_start, e_idx, rank, h_rows)


def _expert_kernel(be_ref, nact_ref, x_ref, wg_ref, wu_ref, wd_ref, y_ref, *, rows, chunks):
    j = pl.program_id(0)

    @pl.when(j < nact_ref[0])
    def _():
        f = wg_ref.shape[2]
        gate = jnp.zeros((rows, f), F32)
        up = jnp.zeros((rows, f), F32)
        for c in range(chunks):
            xc = x_ref[pl.ds(c, rows, stride=chunks), :].astype(BF16)
            cs = slice(c * LANES, (c + 1) * LANES)
            gate = gate + _dot(xc, wg_ref[0, cs, :].astype(BF16))
            up = up + _dot(xc, wu_ref[0, cs, :].astype(BF16))
        h = (_silu(gate) * up).astype(BF16)
        y = _dot(h, wd_ref[0].astype(BF16))
        for c in range(chunks):
            y_ref[pl.ds(c, rows, stride=chunks), :] = y[:, c * LANES:(c + 1) * LANES]

    @pl.when(j >= nact_ref[0])
    def _():
        y_ref[...] = jnp.zeros(y_ref.shape, F32)


def _experts(blk_expert, n_active, xs, wg, wu, wd, rows, chunks):
    n_blocks = blk_expert.shape[0]
    d, f = wg.shape[1], wg.shape[2]
    last = lambda j, na: jnp.minimum(j, na[0] - 1)
    xspec = pl.BlockSpec((rows * chunks, LANES), lambda j, be, na: (last(j, na), 0))
    wspec = lambda s: pl.BlockSpec((1,) + s, lambda j, be, na: (be[last(j, na)], 0, 0))
    return pl.pallas_call(
        functools.partial(_expert_kernel, rows=rows, chunks=chunks),
        grid_spec=pltpu.PrefetchScalarGridSpec(
            num_scalar_prefetch=2,
            grid=(n_blocks,),
            in_specs=[xspec, wspec((d, f)), wspec((d, f)), wspec((f, d))],
            out_specs=pl.BlockSpec((rows * chunks, LANES), lambda j, be, na: (j, 0)),
        ),
        out_shape=jax.ShapeDtypeStruct(xs.shape, F32),
        compiler_params=_cparams("arbitrary"),
        name="moe_experts",
    )(blk_expert, n_active, xs, wg, wu, wd)


def _combine_kernel(start_ref, e_ref, r_ref, w_ref, x1_ref, h2_ref, gate_ref, wsg_ref, wsu_ref, wsd_ref,
                    gf_ref, ys_hbm, o_ref, buf, sem, *, chunks):
    tm = x1_ref.shape[1]

    def issue(r, _):
        for k in range(TOP_K):
            _row_copy(ys_hbm, buf.at[k], _slot(start_ref, e_ref, r_ref, r * TOP_K + k), r, sem, chunks).start()
        return 0

    def drain(r, _):
        for k in range(TOP_K):
            _row_copy(ys_hbm, buf.at[k], 0, r, sem, chunks).wait()
        return 0

    lax.fori_loop(0, tm, issue, 0)
    h_bf = h2_ref[0].astype(BF16)
    hid = (_silu(_dot(h_bf, wsg_ref[...])) * _dot(h_bf, wsu_ref[...])).astype(BF16)
    shared = _dot(hid, wsd_ref[...])
    lax.fori_loop(0, tm, drain, 0)

    w = w_ref[0]
    cols = []
    for c in range(chunks):
        tot = jnp.zeros((tm, LANES), F32)
        for k in range(TOP_K):
            tot = tot + buf[k, pl.ds(c, tm, stride=chunks), :] * w[:, k:k + 1]
        cols.append(tot)
    routed = jnp.concatenate(cols, axis=1)
    x2 = x1_ref[0] + gate_ref[0] * (routed + shared)
    o_ref[0] = _rms(x2, gf_ref[...])


def _combine(pad_start, e_idx, rank, wts, x1, h2, gate, wsg_bf, wsu_bf, wsd_bf, gf, ys, tm, chunks):
    b, t, d = x1.shape
    nt = t // tm
    row = lambda n: pl.BlockSpec((1, tm, n), lambda i, j, *_: (i, j, 0))
    const = lambda shape: pl.BlockSpec(shape, lambda i, j, *_: (0,) * len(shape))
    picks = pl.BlockSpec((1, 1, tm * TOP_K), lambda i, j, *_: (i * nt + j, 0, 0), memory_space=pltpu.SMEM)
    return pl.pallas_call(
        functools.partial(_combine_kernel, chunks=chunks),
        grid_spec=pltpu.PrefetchScalarGridSpec(
            num_scalar_prefetch=1,
            grid=(b, nt),
            in_specs=[picks, picks, row(LANES), row(d), row(d), _mod_spec(gate, tm, d),
                      const(wsg_bf.shape), const(wsu_bf.shape), const(wsd_bf.shape), const(gf.shape),
                      pl.BlockSpec(memory_space=pl.ANY)],
            out_specs=row(d),
            scratch_shapes=[pltpu.VMEM((TOP_K, tm * chunks, LANES), F32), pltpu.SemaphoreType.DMA(())],
        ),
        out_shape=jax.ShapeDtypeStruct((b, t, d), F32),
        compiler_params=_cparams("arbitrary", "arbitrary"),
        name="moe_combine",
    )(pad_start, e_idx, rank, wts, x1, h2, gate, wsg_bf, wsu_bf, wsd_bf, gf, ys)


def _split_mod(mod, per_token):
    parts = jnp.split(mod, 6, axis=-1)
    if per_token:
        return [p[None] for p in parts]
    return [p[:, None, :] for p in parts]


def _padded_in_weight(w_in, conv_width):
    n_gate = 3 * N_HEADS
    o = ATTN_WIDTH + 6 * KV_COLS
    main = w_in[:, :o]
    gates = jnp.pad(w_in[:, o:o + n_gate], ((0, 0), (0, LANES - n_gate)))
    glu = w_in[:, o + n_gate:o + n_gate + 2 * conv_width]
    return jnp.concatenate([main, gates, glu], axis=1).astype(BF16)


def _cmp_rows(x):
    return x.reshape(x.shape[:-2] + (x.shape[-2] // CMP_BLOCK, CMP_BLOCK * KV_COLS))


def _largest_tile(n, cap):
    best = [k for k in range(SUBLANES, cap + 1, SUBLANES) if n % k == 0]
    assert best, (n, cap)
    return best[-1]


def _kv5(x):
    return x.reshape(x.shape[:-1] + (N_KV, HEAD_DIM))[None]


def kernel(x_prompt, x_sample, cache_k_cmp, cache_v_cmp, cache_k_slc, cache_v_slc, state_k_win, state_v_win, state_conv, page_table, c_prompt, c_sample, norm1_g, norm2_g, w_ada, b_ada, w_in, w_cmp_k, w_cmp_v, w_dw, b_dw, ln_conv_g, ln_conv_b, g_out_attn, g_out_conv, w_out, router_w, router_b, w_exp_gate, w_exp_up, w_exp_down, w_sh_gate, w_sh_up, w_sh_down, norm_f_g):
    assert w_ada.shape[0] == 1, "single layer"
    bp, t, d = x_prompt.shape
    bs, s_new, _ = x_sample.shape
    assert s_new == 1
    n_pool, page = cache_k_cmp.shape[1], cache_k_cmp.shape[2]
    n_pages = page_table.shape[1]
    past = n_pages * page
    conv_width = state_conv.shape[-1]
    n_exp = router_w.shape[-1]
    chunks = d // LANES
    tm = min(ROW_TILE, t)

    w_in_bf = _padded_in_weight(w_in[0], conv_width)
    wck = _compress_weight(w_cmp_k[0])
    wcv = _compress_weight(w_cmp_v[0])
    wout_bf = w_out[0].astype(BF16)
    rw_hi = router_w[0].astype(BF16)
    rw_lo = (router_w[0] - rw_hi.astype(F32)).astype(BF16)
    wsg_bf, wsu_bf, wsd_bf = (w[0].astype(BF16) for w in (w_sh_gate, w_sh_up, w_sh_down))
    gf = norm_f_g[None, :]

    n_c = bp + bs
    c_all = jnp.concatenate([c_prompt, c_sample], axis=0)
    c_all = jnp.pad(c_all, ((0, (-n_c) % SUBLANES), (0, 0)))
    mod = _modulation(c_all, w_ada[0], b_ada)
    mp = _split_mod(mod[:bp], per_token=False)
    ms = _split_mod(mod[bp:n_c], per_token=True)

    (q_p, kc_p, vc_p, ks_p, vs_p, kw_p, vw_p, gate_p, u_p) = _in_proj(
        x_prompt, mp[0], mp[1], norm1_g, w_in_bf, tm)
    nc_p = t // CMP_BLOCK
    kcc, vcc = _compress(_cmp_rows(kc_p).reshape(bp * nc_p, -1), _cmp_rows(vc_p).reshape(bp * nc_p, -1),
                         wck, wcv)
    o_attn_p = _prompt_attention(q_p, gate_p, kcc.reshape(bp, nc_p, KV_COLS), vcc.reshape(bp, nc_p, KV_COLS),
                                 ks_p, vs_p, kw_p, vw_p)
    o_conv_p = _conv_prompt(u_p, w_dw[0], b_dw, ln_conv_g, ln_conv_b, tm)

    xs_row = x_sample.reshape(1, bs, d)
    (q_s, kc_s, vc_s, ks_s, vs_s, kw_s, vw_s, gate_s, u_s) = _in_proj(
        xs_row, ms[0], ms[1], norm1_g, w_in_bf, bs)
    q8 = q_s.reshape(bs, N_HEADS, HEAD_DIM)
    gates8 = gate_s[0, :, :3 * N_HEADS].reshape(bs, N_HEADS, 3)
    pages_t = lambda c: jnp.transpose(c[0], (0, 2, 3, 1))
    tail = (-(past + s_new)) % SLC_BLOCK
    n_new = (s_new + tail) // CMP_BLOCK
    tail_rows = lambda x: _cmp_rows(jnp.pad(x[0][:, None, :], ((0, 0), (0, tail), (0, 0)))).reshape(bs * n_new, -1)
    pad_rows = (-(bs * n_new)) % SUBLANES
    kc_new, vc_new = _compress(jnp.pad(tail_rows(kc_s), ((0, pad_rows), (0, 0))),
                               jnp.pad(tail_rows(vc_s), ((0, pad_rows), (0, 0))), wck, wcv)
    kc_new = kc_new[:bs * n_new].reshape(bs, n_new, KV_COLS)
    vc_new = vc_new[:bs * n_new].reshape(bs, n_new, KV_COLS)
    reps = page // CMP_BLOCK
    wk_fold = jnp.transpose(w_cmp_k[0], (2, 1, 0)).reshape(HEAD_DIM, HEAD_DIM * CMP_BLOCK)
    ut = _matmul3(q8.reshape(bs * N_HEADS, HEAD_DIM) * ATTN_SCALE, wk_fold)
    ut = jnp.tile(ut.reshape(bs, N_HEADS, HEAD_DIM, CMP_BLOCK), (1, 1, 1, reps))
    s_raw = _sample_scores(page_table, ut, pages_t(cache_k_cmp))
    p_exp, p_new, sel = _sample_select(s_raw, q8, kc_new, past)
    y_acc = _sample_values(page_table, p_exp, pages_t(cache_v_cmp))
    wv_fold = jnp.tile(jnp.transpose(w_cmp_v[0], (1, 0, 2)), (1, reps, 1)).reshape(HEAD_DIM * page, HEAD_DIM)
    o_cmp_s = _matmul3(y_acc.reshape(bs * N_HEADS, HEAD_DIM * page), wv_fold).reshape(bs, N_HEADS, HEAD_DIM)
    n_sel = min(N_SEL, (past // CMP_BLOCK + n_new) // CMP_PER_SLC)
    sel = sel[:, :N_KV, :n_sel]
    row3 = lambda x: x[0][:, None, :]
    o_attn_s = _sample_attend(
        sel, page_table, q8, pages_t(cache_k_slc), pages_t(cache_v_slc), row3(ks_s), row3(vs_s),
        pages_t(state_k_win), pages_t(state_v_win), row3(kw_s), row3(vw_s), gates8, o_cmp_s, p_new, vc_new,
        past)
    o_attn_s = o_attn_s.reshape(1, bs, ATTN_WIDTH)
    up_s = jnp.concatenate([state_conv[0], u_s[0][:, None, :]], axis=1)
    o_conv_s = _conv_sample(up_s, w_dw[0], b_dw, ln_conv_g, ln_conv_b)[None]

    router = functools.partial(_merge_router, goa=g_out_attn, goc=g_out_conv, wout_bf=wout_bf, g2=norm2_g,
                               rw_hi=rw_hi, rw_lo=rw_lo, rb=router_b)
    x1_p, h2_p, e_p, w_p, r_p, cnt = router(o_attn_p, o_conv_p, x_prompt, mp[2], mp[3], mp[4],
                                            cnt_in=jnp.zeros((1, n_exp), F32), tm=tm)
    x1_s, h2_s, e_s, w_s, r_s, cnt = router(o_attn_s, o_conv_s, xs_row, ms[2], ms[3], ms[4],
                                            cnt_in=cnt, tm=bs)

    n_tok = bp * t + bs
    counts = cnt[0].astype(I32)
    padded = (counts + MOE_ROWS - 1) // MOE_ROWS * MOE_ROWS
    pad_end = jnp.cumsum(padded)
    pad_end = pad_end.astype(I32)
    pad_start = pad_end - padded
    n_blocks = -(-(n_tok * TOP_K) // MOE_ROWS) + n_exp
    blk_first = jnp.arange(n_blocks, dtype=I32) * MOE_ROWS
    blk_expert = jnp.minimum(jnp.sum(pad_end[None, :] <= blk_first[:, None], axis=1), n_exp - 1).astype(I32)
    n_active = pad_end[-1:] // MOE_ROWS
    picks = lambda a: a[..., :TOP_K]
    e_all = jnp.concatenate([picks(e_p).reshape(-1, TOP_K), picks(e_s).reshape(-1, TOP_K)], axis=0)
    r_all = jnp.concatenate([picks(r_p).reshape(-1, TOP_K), picks(r_s).reshape(-1, TOP_K)], axis=0)

    tile = _largest_tile(n_tok, 512)
    h_rows = jnp.concatenate([h2_p.reshape(-1, d), h2_s.reshape(-1, d)], axis=0).reshape(n_tok * chunks, LANES)
    xs = _dispatch(counts, pad_end, pad_start, e_all.reshape(n_tok // tile, 1, tile * TOP_K),
                   r_all.reshape(n_tok // tile, 1, tile * TOP_K), h_rows, tile, MOE_ROWS, chunks, n_blocks)
    ys = _experts(blk_expert, n_active, xs, w_exp_gate[0], w_exp_up[0], w_exp_down[0], MOE_ROWS, chunks)

    comb = functools.partial(_combine, pad_start, wsg_bf=wsg_bf, wsu_bf=wsu_bf, wsd_bf=wsd_bf, gf=gf, ys=ys,
                             chunks=chunks)
    tiles_p = (bp * (t // tm), 1, tm * TOP_K)
    y_prompt = comb(picks(e_p).reshape(tiles_p), picks(r_p).reshape(tiles_p), w_p, x1_p, h2_p, mp[5], tm=tm)
    y_sample = comb(picks(e_s).reshape(1, 1, bs * TOP_K), picks(r_s).reshape(1, 1, bs * TOP_K), w_s, x1_s, h2_s,
                    ms[5], tm=bs).reshape(bs, 1, d)

    win = min(WINDOW, t)
    hist = state_conv.shape[2]
    out_p = [_kv5(a) for a in (kc_p, vc_p, ks_p, vs_p, kw_p[:, t - win:], vw_p[:, t - win:])]
    conv_p = u_p[:, t - hist:][None]
    out_s = [_kv5(a[0][:, None, :]) for a in (kc_s, vc_s, ks_s, vs_s)]
    w_buf = state_k_win.shape[2]
    kw_buf = jnp.concatenate([state_k_win, _kv5(kw_s[0][:, None, :])], axis=2)[:, :, -w_buf:]
    vw_buf = jnp.concatenate([state_v_win, _kv5(vw_s[0][:, None, :])], axis=2)[:, :, -w_buf:]
    conv_s = up_s[:, -hist:][None]
    return (y_prompt, y_sample, *out_p, conv_p, *out_s, kw_buf, vw_buf, conv_s)
```

```python
import functools

import jax
import jax.numpy as jnp
from jax import lax
from jax.experimental import pallas as pl
from jax.experimental.pallas import tpu as pltpu

F32 = jnp.float32
BF16 = jnp.bfloat16
I32 = jnp.int32

N_HEADS = 8
HEAD_DIM = 64
N_KV = 2
Q_PER_KV = N_HEADS // N_KV
ATTN_WIDTH = N_HEADS * HEAD_DIM
KV_COLS = N_KV * HEAD_DIM
CMP_BLOCK = 32
SLC_BLOCK = 64
CMP_PER_SLC = SLC_BLOCK // CMP_BLOCK
N_SEL = 16
WINDOW = 512
TOP_K = 8
ROUTE_SCALE = 2.5
EPS = 1e-6
FORCED = 1e4
NEG = -1e30
ATTN_SCALE = HEAD_DIM ** -0.5

LANES = 128
SUBLANES = 8
VMEM_LIMIT = 56 * 1024 * 1024

ROW_TILE = 256
Q_TILE = 128
KEY_TILE = 1024
CMP_ROW_TILE = 512
MOE_ROWS = 256
CONV_HALO = 32
PAGES_PER_STEP = 32


def _cparams(*sem):
    return pltpu.CompilerParams(dimension_semantics=sem, vmem_limit_bytes=VMEM_LIMIT)


def _dot(a, b):
    return jnp.dot(a, b, preferred_element_type=F32)


def _dot_nt(a, b):
    return lax.dot_general(a, b, (((1,), (1,)), ((), ())), preferred_element_type=F32)


def _dot_tn(a, b):
    return lax.dot_general(a, b, (((0,), (0,)), ((), ())), preferred_element_type=F32)


def _split2(x):
    hi = x.astype(BF16)
    lo = (x - hi.astype(F32)).astype(BF16)
    return hi, lo


def _dot3(a, b):
    ah, al = _split2(a)
    bh, bl = _split2(b)
    return _dot(ah, bh) + (_dot(ah, bl) + _dot(al, bh))


def _dot3_nt(a, b):
    ah, al = _split2(a)
    bh, bl = _split2(b)
    return _dot_nt(ah, bh) + (_dot_nt(ah, bl) + _dot_nt(al, bh))


def _sigmoid(x):
    return 1.0 / (1.0 + jnp.exp(-x))


def _silu(x):
    return x * _sigmoid(x)


def _rms(x, g):
    return x * lax.rsqrt(jnp.mean(x * x, axis=-1, keepdims=True) + EPS) * g


def _alibi_slope_col(rows, rows_per_head, first_head, n_heads):
    r = lax.broadcasted_iota(I32, (rows, 1), 0) // rows_per_head
    out = jnp.zeros((rows, 1), F32)
    for k in range(n_heads):
        out = jnp.where(r == k, 2.0 ** (-8.0 * (first_head + k + 1) / N_HEADS), out)
    return out


def _modulation_kernel(c_ref, w_ref, b_ref, o_ref):
    o_ref[...] = _dot3(c_ref[...], w_ref[...]) + b_ref[...]


def _modulation(c, w, b):
    m, d = c.shape
    n = w.shape[1]
    tn = 768
    return pl.pallas_call(
        _modulation_kernel,
        grid=(n // tn,),
        in_specs=[pl.BlockSpec((m, d), lambda j: (0, 0)),
                  pl.BlockSpec((d, tn), lambda j: (0, j)),
                  pl.BlockSpec((1, tn), lambda j: (0, j))],
        out_specs=pl.BlockSpec((m, tn), lambda j: (0, j)),
        out_shape=jax.ShapeDtypeStruct((m, n), F32),
        compiler_params=_cparams("arbitrary"),
        name="modulation",
    )(c, w, b)


def _mod_spec(mod, tm, d):
    if mod.shape[1] == 1:
        return pl.BlockSpec((1, 1, d), lambda i, j, *_: (i, 0, 0))
    return pl.BlockSpec((1, tm, d), lambda i, j, *_: (i, j, 0))


def _in_proj_kernel(x_ref, shift_ref, scale_ref, g_ref, w_ref,
                    q_ref, kc_ref, vc_ref, ks_ref, vs_ref, kw_ref, vw_ref, gate_ref, u_ref):
    x = x_ref[0]
    h = _rms(x, g_ref[...]) * (1.0 + scale_ref[0]) + shift_ref[0]
    z = _dot(h.astype(BF16), w_ref[...])
    q_ref[0] = z[:, :ATTN_WIDTH]
    o = ATTN_WIDTH
    for ref in (kc_ref, vc_ref, ks_ref, vs_ref, kw_ref, vw_ref):
        ref[0] = z[:, o:o + KV_COLS]
        o += KV_COLS
    gate_ref[0] = _sigmoid(z[:, o:o + LANES])
    o += LANES
    cw = u_ref.shape[-1]
    u_ref[0] = z[:, o:o + cw] * _sigmoid(z[:, o + cw:o + 2 * cw])


def _in_proj(x, shift, scale, g, w_bf, tm):
    b, t, d = x.shape
    cw = (w_bf.shape[1] - ATTN_WIDTH - 6 * KV_COLS - LANES) // 2
    row = lambda n: pl.BlockSpec((1, tm, n), lambda i, j: (i, j, 0))
    sds = lambda n: jax.ShapeDtypeStruct((b, t, n), F32)
    return pl.pallas_call(
        _in_proj_kernel,
        grid=(b, t // tm),
        in_specs=[row(d), _mod_spec(shift, tm, d), _mod_spec(scale, tm, d),
                  pl.BlockSpec((1, d), lambda i, j: (0, 0)),
                  pl.BlockSpec(w_bf.shape, lambda i, j: (0, 0))],
        out_specs=[row(ATTN_WIDTH)] + [row(KV_COLS)] * 6 + [row(LANES), row(cw)],
        out_shape=[sds(ATTN_WIDTH)] + [sds(KV_COLS)] * 6 + [sds(LANES), sds(cw)],
        compiler_params=_cparams("arbitrary", "arbitrary"),
        name="in_proj",
    )(x, shift, scale, g, w_bf)


def _compress_kernel(k_ref, v_ref, wk_ref, wv_ref, ko_ref, vo_ref):
    ko_ref[...] = _dot3(k_ref[...], wk_ref[...])
    vo_ref[...] = _dot3(v_ref[...], wv_ref[...])


def _compress(k_rows, v_rows, wk, wv):
    r, kdim = k_rows.shape
    tr = min(CMP_ROW_TILE, r)
    assert r % tr == 0
    rows = pl.BlockSpec((tr, kdim), lambda i: (i, 0))
    wspec = pl.BlockSpec((kdim, KV_COLS), lambda i: (0, 0))
    ospec = pl.BlockSpec((tr, KV_COLS), lambda i: (i, 0))
    return pl.pallas_call(
        _compress_kernel,
        grid=(r // tr,),
        in_specs=[rows, rows, wspec, wspec],
        out_specs=[ospec, ospec],
        out_shape=[jax.ShapeDtypeStruct((r, KV_COLS), F32)] * 2,
        compiler_params=_cparams("arbitrary"),
        name="compress",
    )(k_rows, v_rows, wk, wv)


def _compress_weight(w):
    eye = jnp.eye(N_KV, dtype=w.dtype)
    big = jnp.einsum('lde,gh->lgdhe', w, eye)
    return big.reshape(CMP_BLOCK * KV_COLS, KV_COLS)


def _pair_sum(x, axis):
    n = x.shape[axis]
    idx = lax.broadcasted_iota(I32, x.shape, axis)
    nxt = pltpu.roll(x, n - 1, axis)
    prv = pltpu.roll(x, 1, axis)
    return x + jnp.where((idx & 1) == 0, nxt, prv)


def _block_scores(imp, blk, q_pos, n_blocks_total):
    cur = q_pos // SLC_BLOCK
    valid = jnp.logical_and(blk * SLC_BLOCK <= q_pos, blk < n_blocks_total)
    forced = jnp.logical_or(blk == 0, jnp.logical_or(blk == cur, blk == cur - 1))
    return jnp.where(valid, jnp.where(forced, FORCED, imp), -1.0)


def _select_blocks(score, blk, n_sel):
    blk_f = blk.astype(F32)
    s = score
    for _ in range(n_sel):
        m = jnp.max(s, axis=0, keepdims=True)
        first = jnp.min(jnp.where(s == m, blk_f, 1e9), axis=0, keepdims=True)
        s = jnp.where(blk_f == first, -2.0, s)
    return jnp.where(jnp.logical_and(s == -2.0, score >= 0.0), 1.0, 0.0)


def _prompt_attn_kernel(q_ref, gate_ref, kc_ref, vc_ref, ks_ref, vs_ref, kw_ref, vw_ref, o_ref,
                        *, seq, n_sel):
    i = pl.program_id(1)
    tq = Q_TILE
    nc = kc_ref.shape[1]
    q_blk = q_ref[0] * ATTN_SCALE
    gates = gate_ref[0]
    slopes = [2.0 ** (-8.0 * (h + 1) / N_HEADS) for h in range(N_HEADS)]
    group_heads = [list(range(g * Q_PER_KV, (g + 1) * Q_PER_KV)) for g in range(N_KV)]
    gsl = [slice(g * HEAD_DIM, (g + 1) * HEAD_DIM) for g in range(N_KV)]
    rsl = [slice(r * tq, (r + 1) * tq) for r in range(Q_PER_KV)]
    q_pos_col = i * tq + lax.broadcasted_iota(I32, (tq, 1), 0)
    q_pos_row = i * tq + lax.broadcasted_iota(I32, (1, tq), 1)
    qg = [jnp.concatenate([q_blk[:, h * HEAD_DIM:(h + 1) * HEAD_DIM] for h in hs], axis=0)
          for hs in group_heads]
    qg_bf = [x.astype(BF16) for x in qg]

    cmp_row = lax.broadcasted_iota(I32, (nc, tq), 0)
    dist_c = (q_pos_row - (cmp_row * CMP_BLOCK + (CMP_BLOCK - 1))).astype(F32)
    mask_c = dist_c >= 0.0
    blk = cmp_row >> 1
    o_cmp = [None] * N_HEADS
    sel_bf = []
    for g, hs in enumerate(group_heads):
        vc_bf = vc_ref[0][:, gsl[g]].astype(BF16)
        qk = _dot3_nt(kc_ref[0][:, gsl[g]], qg[g])
        imp = jnp.zeros((nc, tq), F32)
        for r, h in enumerate(hs):
            s = jnp.where(mask_c, qk[:, rsl[r]] - slopes[h] * dist_c, NEG)
            m = jnp.max(s, axis=0, keepdims=True)
            p = jnp.where(mask_c, jnp.exp(s - m), 0.0)
            p = p / jnp.maximum(jnp.sum(p, axis=0, keepdims=True), 1e-30)
            o_cmp[h] = _dot_tn(p.astype(BF16), vc_bf)
            imp = imp + p
        score = _block_scores(_pair_sum(imp, 0), blk, q_pos_row, seq // SLC_BLOCK)
        sel_bf.append(_select_blocks(score, blk, n_sel).astype(BF16))

    span = WINDOW + tq
    w_start = pl.multiple_of(jnp.maximum(i * tq - WINDOW, 0), tq)
    dist_w = (q_pos_col - (w_start + lax.broadcasted_iota(I32, (tq, span), 1))).astype(F32)
    bias_w = jnp.where(jnp.logical_and(dist_w >= 0.0, dist_w <= float(WINDOW)), 0.0, NEG)
    o_win = [None] * N_HEADS
    for g, hs in enumerate(group_heads):
        kw_bf = kw_ref[0, pl.ds(w_start, span), :][:, gsl[g]].astype(BF16)
        vw_bf = vw_ref[0, pl.ds(w_start, span), :][:, gsl[g]].astype(BF16)
        s_all = _dot_nt(qg_bf[g], kw_bf)
        probs, sums = [], []
        for r, h in enumerate(hs):
            s = s_all[rsl[r]] + (bias_w - slopes[h] * dist_w)
            p = jnp.exp(s - jnp.max(s, axis=1, keepdims=True))
            sums.append(jnp.sum(p, axis=1, keepdims=True))
            probs.append(p.astype(BF16))
        o_all = _dot(jnp.concatenate(probs, axis=0), vw_bf)
        for r, h in enumerate(hs):
            o_win[h] = o_all[rsl[r]] / sums[r]

    n_tiles = ((i + 1) * tq + KEY_TILE - 1) // KEY_TILE

    def slc_step(t, carry):
        ms, ls, accs = (list(c) for c in carry)
        k0 = pl.multiple_of(t * KEY_TILE, KEY_TILE)
        dist = (q_pos_col - (k0 + lax.broadcasted_iota(I32, (tq, KEY_TILE), 1))).astype(F32)
        causal = dist >= 0.0
        key_cmp = (k0 + lax.broadcasted_iota(I32, (nc, KEY_TILE), 1)) // CMP_BLOCK
        expand = jnp.where(key_cmp == lax.broadcasted_iota(I32, (nc, KEY_TILE), 0), 1.0, 0.0).astype(BF16)
        for g, hs in enumerate(group_heads):
            kt_bf = ks_ref[0, pl.ds(k0, KEY_TILE), :][:, gsl[g]].astype(BF16)
            vt_bf = vs_ref[0, pl.ds(k0, KEY_TILE), :][:, gsl[g]].astype(BF16)
            chosen = _dot_tn(sel_bf[g], expand)
            bias = jnp.where(jnp.logical_and(causal, chosen > 0.5), 0.0, NEG)
            s_all = _dot_nt(qg_bf[g], kt_bf)
            probs, alphas = [], []
            for r, h in enumerate(hs):
                s = s_all[rsl[r]] + (bias - slopes[h] * dist)
                m_new = jnp.maximum(ms[h], jnp.max(s, axis=1, keepdims=True))
                alpha = jnp.exp(ms[h] - m_new)
                p = jnp.exp(s - m_new)
                ls[h] = alpha * ls[h] + jnp.sum(p, axis=1, keepdims=True)
                ms[h] = m_new
                alphas.append(alpha)
                probs.append(p.astype(BF16))
            pv = _dot(jnp.concatenate(probs, axis=0), vt_bf)
            for r, h in enumerate(hs):
                accs[h] = alphas[r] * accs[h] + pv[rsl[r]]
        return tuple(ms), tuple(ls), tuple(accs)

    init = (tuple(jnp.full((tq, 1), NEG, F32) for _ in range(N_HEADS)),
            tuple(jnp.zeros((tq, 1), F32) for _ in range(N_HEADS)),
            tuple(jnp.zeros((tq, HEAD_DIM), F32) for _ in range(N_HEADS)))
    _, l_s, acc_s = lax.fori_loop(0, n_tiles, slc_step, init)

    pieces = []
    for h in range(N_HEADS):
        o_slc = acc_s[h] / jnp.maximum(l_s[h], 1e-30)
        pieces.append(o_cmp[h] * gates[:, 3 * h + 0:3 * h + 1] + o_slc * gates[:, 3 * h + 1:3 * h + 2]
                      + o_win[h] * gates[:, 3 * h + 2:3 * h + 3])
    o_ref[0] = jnp.concatenate(pieces, axis=1)


def _prompt_attention(q, gates, kc, vc, ks, vs, kw, vw):
    b, t, _ = q.shape
    nc = kc.shape[1]
    assert t % KEY_TILE == 0 and t >= WINDOW + Q_TILE
    n_sel = min(N_SEL, t // SLC_BLOCK)
    qspec = lambda n: pl.BlockSpec((1, Q_TILE, n), lambda bi, i: (bi, i, 0))
    full = lambda r: pl.BlockSpec((1, r, KV_COLS), lambda bi, i: (bi, 0, 0))
    return pl.pallas_call(
        functools.partial(_prompt_attn_kernel, seq=t, n_sel=n_sel),
        grid=(b, t // Q_TILE),
        in_specs=[qspec(ATTN_WIDTH), qspec(LANES), full(nc), full(nc),
                  full(t), full(t), full(t), full(t)],
        out_specs=qspec(ATTN_WIDTH),
        out_shape=jax.ShapeDtypeStruct((b, t, ATTN_WIDTH), F32),
        compiler_params=_cparams("arbitrary", "arbitrary"),
        name="prompt_attention",
    )(q, gates, kc, vc, ks, vs, kw, vw)


def _merge_groups(per_group):
    row = lax.broadcasted_iota(I32, per_group[0].shape, 0) // Q_PER_KV
    out = per_group[0]
    for g in range(1, N_KV):
        out = jnp.where(row == g, per_group[g], out)
    return out


def _group_slice(x, g):
    return x[:, g * HEAD_DIM:(g + 1) * HEAD_DIM]


def _matmul3_kernel(a_ref, b_ref, o_ref):
    o_ref[...] = _dot3(a_ref[...], b_ref[...])


def _matmul3(a, b):
    return pl.pallas_call(
        _matmul3_kernel,
        out_shape=jax.ShapeDtypeStruct((a.shape[0], b.shape[1]), F32),
        compiler_params=pltpu.CompilerParams(vmem_limit_bytes=VMEM_LIMIT),
        name="matmul3",
    )(a, b)


def _page_specs(n_pages, page):
    def spec(o):
        return pl.BlockSpec((1, N_KV, HEAD_DIM, page),
                            lambda i, j, pt: (pt[i * n_pages + j * PAGES_PER_STEP + o], 0, 0, 0))
    return [spec(o) for o in range(PAGES_PER_STEP)]


def _sample_scores_kernel(pt_ref, ut_ref, *refs):
    k_refs, o_ref = refs[:-1], refs[-1]
    for h in range(N_HEADS):
        g = h // Q_PER_KV
        u = ut_ref[0, h]
        rows = [jnp.sum(k_ref[0, g] * u, axis=0, keepdims=True) for k_ref in k_refs]
        o_ref[0, h] = jnp.concatenate(rows, axis=0)


def _sample_scores(page_table, ut, k_pages):
    b, n_pages = page_table.shape
    page = k_pages.shape[-1]
    assert n_pages % PAGES_PER_STEP == 0
    return pl.pallas_call(
        _sample_scores_kernel,
        grid_spec=pltpu.PrefetchScalarGridSpec(
            num_scalar_prefetch=1,
            grid=(b, n_pages // PAGES_PER_STEP),
            in_specs=[pl.BlockSpec((1, N_HEADS, HEAD_DIM, page), lambda i, j, pt: (i, 0, 0, 0))]
                     + _page_specs(n_pages, page),
            out_specs=pl.BlockSpec((1, N_HEADS, PAGES_PER_STEP, page), lambda i, j, pt: (i, 0, j, 0)),
        ),
        out_shape=jax.ShapeDtypeStruct((b, N_HEADS, n_pages, page), F32),
        compiler_params=_cparams("arbitrary", "arbitrary"),
        name="sample_scores",
    )(page_table.reshape(-1), ut, *([k_pages] * PAGES_PER_STEP))


def _max_all(x):
    return jnp.max(jnp.max(x, axis=0, keepdims=True), axis=1, keepdims=True)


def _min_all(x):
    return jnp.min(jnp.min(x, axis=0, keepdims=True), axis=1, keepdims=True)


def _sum_all(x):
    return jnp.sum(jnp.sum(x, axis=0, keepdims=True), axis=1, keepdims=True)


def _sample_select_kernel(s_ref, q_ref, kcn_ref, pexp_ref, pnew_ref, sel_ref, *, past, n_new, n_sel):
    n_pages, page = s_ref.shape[2], s_ref.shape[3]
    cpp = page // CMP_BLOCK
    n_past = n_pages * cpp
    n_blocks_total = (n_past + n_new) // CMP_PER_SLC
    lane = lax.broadcasted_iota(I32, (n_pages, page), 1)
    prow = lax.broadcasted_iota(I32, (n_pages, page), 0)
    dist = (past - ((prow * cpp + lane // CMP_BLOCK) * CMP_BLOCK + (CMP_BLOCK - 1))).astype(F32)
    mask = jnp.logical_and(lane % CMP_BLOCK == 0, dist >= 0.0)

    q8 = q_ref[0] * ATTN_SCALE
    slope = _alibi_slope_col(N_HEADS, 1, 0, N_HEADS)
    kcn = jnp.concatenate([kcn_ref[0], jnp.zeros((LANES - n_new, KV_COLS), F32)], axis=0)
    new_lane = lax.broadcasted_iota(I32, (N_HEADS, LANES), 1)
    dist_n = (past - ((n_past + new_lane) * CMP_BLOCK + (CMP_BLOCK - 1))).astype(F32)
    mask_n = jnp.logical_and(dist_n >= 0.0, new_lane < n_new)
    qk_n = _merge_groups([_dot3_nt(q8, _group_slice(kcn, g)) for g in range(N_KV)])
    s_new = jnp.where(mask_n, qk_n - slope * dist_n, NEG)

    probs, probs_new = [], []
    for h in range(N_HEADS):
        x = s_ref[0, h]
        for sh in (16, 8, 4, 2, 1):
            x = x + pltpu.roll(x, page - sh, 1)
        s = jnp.where(mask, x - 2.0 ** (-8.0 * (h + 1) / N_HEADS) * dist, NEG)
        sn = s_new[h:h + 1, :]
        mn = jnp.logical_and(dist_n[h:h + 1, :] >= 0.0, new_lane[h:h + 1, :] < n_new)
        m = jnp.maximum(_max_all(s), jnp.max(sn, axis=1, keepdims=True))
        p = jnp.where(mask, jnp.exp(s - m), 0.0)
        pn = jnp.where(mn, jnp.exp(sn - m), 0.0)
        den = jnp.maximum(_sum_all(p) + jnp.sum(pn, axis=1, keepdims=True), 1e-30)
        p = p / den
        probs.append(p)
        probs_new.append(pn / den)
        z = p
        for sh in (1, 2, 4, 8, 16):
            z = z + pltpu.roll(z, sh, 1)
        pexp_ref[0, h] = z
    pnew_ref[0] = jnp.concatenate(probs_new, axis=0)

    row1 = lax.broadcasted_iota(I32, (1, LANES), 1)
    blk = jnp.where(lane % SLC_BLOCK == 0, prow * (page // SLC_BLOCK) + lane // SLC_BLOCK, -1)
    blk_n = jnp.where(row1 < n_new, n_past // CMP_PER_SLC + (row1 >> 1), -1)
    blk_f = blk.astype(F32)
    blk_nf = blk_n.astype(F32)
    out_lane = lax.broadcasted_iota(I32, (N_HEADS, LANES), 1)
    out_row = lax.broadcasted_iota(I32, (N_HEADS, LANES), 0)
    out = jnp.full((N_HEADS, LANES), -1, I32)
    for g in range(N_KV):
        imp = probs[g * Q_PER_KV]
        imp_n = probs_new[g * Q_PER_KV]
        for r in range(1, Q_PER_KV):
            imp = imp + probs[g * Q_PER_KV + r]
            imp_n = imp_n + probs_new[g * Q_PER_KV + r]
        imp = imp + pltpu.roll(imp, page - CMP_BLOCK, 1)
        s_m = jnp.where(blk >= 0, _block_scores(imp, blk, past, n_blocks_total), -4.0)
        s_n = jnp.where(blk_n >= 0, _block_scores(_pair_sum(imp_n, 1), blk_n, past, n_blocks_total), -4.0)
        for j in range(n_sel):
            top = jnp.maximum(_max_all(s_m), jnp.max(s_n, axis=1, keepdims=True))
            first = jnp.minimum(_min_all(jnp.where(s_m == top, blk_f, 1e9)),
                                jnp.min(jnp.where(s_n == top, blk_nf, 1e9), axis=1, keepdims=True))
            s_m = jnp.where(blk_f == first, -2.0, s_m)
            s_n = jnp.where(blk_nf == first, -2.0, s_n)
            pick = jnp.where(top >= 0.0, first.astype(I32), -1)
            out = jnp.where(jnp.logical_and(out_row == g, out_lane == j), pick, out)
    sel_ref[0] = out


def _sample_select(s_raw, q8, kc_new, past):
    b, _, n_pages, page = s_raw.shape
    n_new = kc_new.shape[1]
    assert CMP_PER_SLC == 2 and CMP_BLOCK == 32
    n_sel = min(N_SEL, (past // CMP_BLOCK + n_new) // CMP_PER_SLC)
    per_b = lambda *s: pl.BlockSpec((1,) + s, lambda i: (i,) + (0,) * len(s))
    return pl.pallas_call(
        functools.partial(_sample_select_kernel, past=past, n_new=n_new, n_sel=n_sel),
        grid=(b,),
        in_specs=[per_b(N_HEADS, n_pages, page), per_b(N_HEADS, HEAD_DIM), per_b(n_new, KV_COLS)],
        out_specs=[per_b(N_HEADS, n_pages, page), per_b(N_HEADS, LANES), per_b(N_HEADS, LANES)],
        out_shape=[jax.ShapeDtypeStruct((b, N_HEADS, n_pages, page), F32),
                   jax.ShapeDtypeStruct((b, N_HEADS, LANES), F32),
                   jax.ShapeDtypeStruct((b, N_HEADS, LANES), I32)],
        compiler_params=_cparams("arbitrary"),
        name="sample_select",
    )(s_raw, q8, kc_new)


def _sample_values_kernel(pt_ref, pe_ref, *refs):
    v_refs, y_ref = refs[:-1], refs[-1]

    @pl.when(pl.program_id(1) == 0)
    def _():
        y_ref[...] = jnp.zeros(y_ref.shape, F32)

    for g in range(N_KV):
        heads = range(g * Q_PER_KV, (g + 1) * Q_PER_KV)
        pe = [pe_ref[0, h] for h in heads]
        acc = [jnp.zeros(y_ref.shape[2:], F32) for _ in heads]
        for o, v_ref in enumerate(v_refs):
            v = v_ref[0, g]
            for r in range(Q_PER_KV):
                acc[r] = acc[r] + v * pe[r][o:o + 1, :]
        for r, h in enumerate(heads):
            y_ref[0, h] = y_ref[0, h] + acc[r]


def _sample_values(page_table, pexp, v_pages):
    b, n_pages = page_table.shape
    page = v_pages.shape[-1]
    return pl.pallas_call(
        _sample_values_kernel,
        grid_spec=pltpu.PrefetchScalarGridSpec(
            num_scalar_prefetch=1,
            grid=(b, n_pages // PAGES_PER_STEP),
            in_specs=[pl.BlockSpec((1, N_HEADS, PAGES_PER_STEP, page), lambda i, j, pt: (i, 0, j, 0))]
                     + _page_specs(n_pages, page),
            out_specs=pl.BlockSpec((1, N_HEADS, HEAD_DIM, page), lambda i, j, pt: (i, 0, 0, 0)),
        ),
        out_shape=jax.ShapeDtypeStruct((b, N_HEADS, HEAD_DIM, page), F32),
        compiler_params=_cparams("arbitrary", "arbitrary"),
        name="sample_values",
    )(page_table.reshape(-1), pexp, *([v_pages] * PAGES_PER_STEP))


def _new_token_terms(q8, k_row, v_row):
    s = _merge_groups([jnp.sum(q8 * _group_slice(k_row, g), axis=1, keepdims=True) for g in range(N_KV)])
    v = _merge_groups([jnp.broadcast_to(_group_slice(v_row, g), (N_HEADS, HEAD_DIM)) for g in range(N_KV)])
    return s, v


def _sample_attend_kernel(sel_ref, pt_ref, q_ref, *refs, past, n_sel, ns_past):
    page_refs = refs[:4 * n_sel]
    (ksn_ref, vsn_ref, kw_ref, vw_ref, kwn_ref, vwn_ref, gate_ref, ocmp_ref, pnew_ref, vcn_ref,
     o_ref) = refs[4 * n_sel:]
    b = pl.program_id(0)
    page = page_refs[0].shape[-1]
    spp = page // SLC_BLOCK
    q8 = q_ref[0] * ATTN_SCALE
    q8_bf = q8.astype(BF16)
    slope = _alibi_slope_col(N_HEADS, 1, 0, N_HEADS)
    lane = lax.broadcasted_iota(I32, (N_HEADS, page), 1)
    s_t, v_t = _new_token_terms(q8, ksn_ref[0], vsn_ref[0])

    scores, masks, new_scores = [], [], []
    for n in range(n_sel):
        k_refs = page_refs[4 * n:4 * n + N_KV]
        blks = [sel_ref[(b * N_KV + g) * n_sel + n] for g in range(N_KV)]
        blk_rows = _merge_groups([jnp.full((N_HEADS, page), blk, I32) for blk in blks])
        blk_col = _merge_groups([jnp.full((N_HEADS, 1), blk, I32) for blk in blks])
        qk = _merge_groups([_dot(q8_bf, k_ref[0, 0].astype(BF16)) for k_ref in k_refs])
        page_pos = blk_rows // spp
        dist = (past - (page_pos * page + lane)).astype(F32)
        in_block = (lane // SLC_BLOCK) == (blk_rows - page_pos * spp)
        cached = jnp.logical_and(blk_rows >= 0, blk_rows < ns_past)
        mask = jnp.logical_and(jnp.logical_and(in_block, cached), dist >= 0.0)
        scores.append(jnp.where(mask, qk - slope * dist, NEG))
        masks.append(mask)
        new_scores.append(jnp.where(blk_col >= ns_past, s_t, NEG))
    m = new_scores[0]
    for s, sn in zip(scores, new_scores):
        m = jnp.maximum(m, jnp.maximum(jnp.max(s, axis=1, keepdims=True), sn))
    l_tot = jnp.zeros((N_HEADS, 1), F32)
    p_new = jnp.zeros((N_HEADS, 1), F32)
    acc = jnp.zeros((N_HEADS, HEAD_DIM), F32)
    for n in range(n_sel):
        v_refs = page_refs[4 * n + N_KV:4 * n + 2 * N_KV]
        p = jnp.where(masks[n], jnp.exp(scores[n] - m), 0.0)
        p_bf = p.astype(BF16)
        l_tot = l_tot + jnp.sum(p, axis=1, keepdims=True)
        p_new = p_new + jnp.where(new_scores[n] > 0.5 * NEG, jnp.exp(new_scores[n] - m), 0.0)
        acc = acc + _merge_groups([_dot_nt(p_bf, v_ref[0, 0].astype(BF16)) for v_ref in v_refs])
    o_slc = (acc + p_new * v_t) / jnp.maximum(l_tot + p_new, 1e-30)

    w_buf = kw_ref.shape[-1]
    wl = lax.broadcasted_iota(I32, (N_HEADS, w_buf), 1)
    win_pos = past - w_buf + wl
    dist_w = (past - win_pos).astype(F32)
    mask_w = jnp.logical_and(jnp.logical_and(dist_w >= 0.0, dist_w <= float(WINDOW)), win_pos >= 0)
    qk_w = _merge_groups([_dot(q8_bf, kw_ref[0, g].astype(BF16)) for g in range(N_KV)])
    s_w = jnp.where(mask_w, qk_w - slope * dist_w, NEG)
    s_t, v_t = _new_token_terms(q8, kwn_ref[0], vwn_ref[0])
    m_w = jnp.maximum(jnp.max(s_w, axis=1, keepdims=True), s_t)
    p_w = jnp.where(mask_w, jnp.exp(s_w - m_w), 0.0)
    p_t = jnp.exp(s_t - m_w)
    den = jnp.maximum(jnp.sum(p_w, axis=1, keepdims=True) + p_t, 1e-30)
    pw_bf = p_w.astype(BF16)
    o_w = _merge_groups([_dot_nt(pw_bf, vw_ref[0, g].astype(BF16)) for g in range(N_KV)])
    o_win = (o_w + p_t * v_t) / den
    n_new = vcn_ref.shape[1]
    vcn = jnp.concatenate([vcn_ref[0], jnp.zeros((LANES - n_new, KV_COLS), F32)], axis=0).astype(BF16)
    pn_bf = pnew_ref[0].astype(BF16)
    o_cmp = ocmp_ref[0] + _merge_groups([_dot(pn_bf, _group_slice(vcn, g)) for g in range(N_KV)])
    gt = gate_ref[0]
    o_ref[0] = o_cmp * gt[:, 0:1] + o_slc * gt[:, 1:2] + o_win * gt[:, 2:3]


def _sample_attend(sel, page_table, q8, k_pages, v_pages, ks_new, vs_new, kw_state, vw_state,
                   kw_new, vw_new, gates8, o_cmp, p_new, vc_new, past):
    b, n_pages = page_table.shape
    n_sel = sel.shape[-1]
    page = k_pages.shape[-1]
    ns_past = past // SLC_BLOCK
    spp = page // SLC_BLOCK
    w_buf = kw_state.shape[-1]
    n_new = vc_new.shape[1]

    def cache_map(n, g):
        def index(i, sel_ref, pt_ref):
            blk = jnp.clip(sel_ref[(i * N_KV + g) * n_sel + n], 0, ns_past - 1)
            return (pt_ref[i * n_pages + blk // spp], g, 0, 0)
        return pl.BlockSpec((1, 1, HEAD_DIM, page), index)

    page_specs, page_args = [], []
    for n in range(n_sel):
        for arr in (k_pages, v_pages):
            for g in range(N_KV):
                page_specs.append(cache_map(n, g))
                page_args.append(arr)
    per_b = lambda *s: pl.BlockSpec((1,) + s, lambda i, sl, pt: (i,) + (0,) * len(s))
    return pl.pallas_call(
        functools.partial(_sample_attend_kernel, past=past, n_sel=n_sel, ns_past=ns_past),
        grid_spec=pltpu.PrefetchScalarGridSpec(
            num_scalar_prefetch=2,
            grid=(b,),
            in_specs=[per_b(N_HEADS, HEAD_DIM)] + page_specs
                     + [per_b(1, KV_COLS), per_b(1, KV_COLS),
                        per_b(N_KV, HEAD_DIM, w_buf), per_b(N_KV, HEAD_DIM, w_buf),
                        per_b(1, KV_COLS), per_b(1, KV_COLS),
                        per_b(N_HEADS, 3), per_b(N_HEADS, HEAD_DIM), per_b(N_HEADS, LANES),
                        per_b(n_new, KV_COLS)],
            out_specs=per_b(N_HEADS, HEAD_DIM),
        ),
        out_shape=jax.ShapeDtypeStruct((b, N_HEADS, HEAD_DIM), F32),
        compiler_params=_cparams("arbitrary"),
        name="sample_attend",
    )(sel.reshape(-1), page_table.reshape(-1), q8, *page_args,
      ks_new, vs_new, kw_state, vw_state, kw_new, vw_new, gates8, o_cmp, p_new, vc_new)


def _layernorm_silu(y, g, b):
    mu = jnp.mean(y, axis=-1, keepdims=True)
    var = jnp.mean(jnp.square(y - mu), axis=-1, keepdims=True)
    return _silu((y - mu) * lax.rsqrt(var + EPS) * g + b)


def _conv_prompt_kernel(u_ref, w_ref, b_ref, g_ref, beta_ref, o_ref, buf):
    j = pl.program_id(1)
    tt = u_ref.shape[1]
    kw = w_ref.shape[0]

    @pl.when(j == 0)
    def _():
        buf[0:CONV_HALO, :] = jnp.zeros((CONV_HALO, buf.shape[1]), F32)

    buf[CONV_HALO:CONV_HALO + tt, :] = u_ref[0]
    w = w_ref[...]
    acc = jnp.zeros((tt, buf.shape[1]), F32)
    for k in range(kw):
        acc = acc + w[k:k + 1, :] * buf[pl.ds(CONV_HALO - (kw - 1) + k, tt), :]
    o_ref[0] = _layernorm_silu(acc + b_ref[...], g_ref[...], beta_ref[...])
    buf[0:CONV_HALO, :] = buf[tt:tt + CONV_HALO, :]


def _conv_prompt(u, w_dw, b_dw, ln_g, ln_b, tt):
    b, t, c = u.shape
    vec = pl.BlockSpec((1, c), lambda i, j: (0, 0))
    return pl.pallas_call(
        _conv_prompt_kernel,
        grid=(b, t // tt),
        in_specs=[pl.BlockSpec((1, tt, c), lambda i, j: (i, j, 0)),
                  pl.BlockSpec(w_dw.shape, lambda i, j: (0, 0)), vec, vec, vec],
        out_specs=pl.BlockSpec((1, tt, c), lambda i, j: (i, j, 0)),
        out_shape=jax.ShapeDtypeStruct((b, t, c), F32),
        scratch_shapes=[pltpu.VMEM((CONV_HALO + tt, c), F32)],
        compiler_params=_cparams("arbitrary", "arbitrary"),
        name="conv_prompt",
    )(u, w_dw, b_dw, ln_g, ln_b)


def _conv_sample_kernel(up_ref, w_ref, b_ref, g_ref, beta_ref, o_ref):
    y = jnp.sum(up_ref[...] * w_ref[...][None, :, :], axis=1)
    o_ref[...] = _layernorm_silu(y + b_ref[...], g_ref[...], beta_ref[...])


def _conv_sample(up, w_dw, b_dw, ln_g, ln_b):
    b, kw, c = up.shape
    return pl.pallas_call(
        _conv_sample_kernel,
        out_shape=jax.ShapeDtypeStruct((b, c), F32),
        name="conv_sample",
    )(up, w_dw, b_dw, ln_g, ln_b)


def _merge_router_kernel(oa_ref, oc_ref, x_ref, gate_ref, shift_ref, scale_ref, goa_ref, goc_ref,
                         wout_ref, g2_ref, rwh_ref, rwl_ref, rb_ref, cnt_in_ref,
                         x1_ref, h2_ref, eidx_ref, wts_ref, rank_ref, cnt_out_ref, run):
    first = jnp.logical_and(pl.program_id(0) == 0, pl.program_id(1) == 0)

    @pl.when(first)
    def _():
        run[...] = cnt_in_ref[...]

    a = _rms(oa_ref[0], goa_ref[...])
    c = _rms(oc_ref[0], goc_ref[...])
    cat = jnp.concatenate([a, c], axis=1).astype(BF16)
    x1 = x_ref[0] + gate_ref[0] * _dot(cat, wout_ref[...])
    x1_ref[0] = x1
    h2 = _rms(x1, g2_ref[...]) * (1.0 + scale_ref[0]) + shift_ref[0]
    h2_ref[0] = h2

    hh, hl = _split2(h2)
    logits = _dot_nt(rwh_ref[...], hh) + (_dot_nt(rwl_ref[...], hh) + _dot_nt(rwh_ref[...], hl))
    aff = _sigmoid(logits)
    n_exp, tm = aff.shape
    row_f = lax.broadcasted_iota(I32, (n_exp, tm), 0).astype(F32)
    s = aff + rb_ref[...]
    experts, weights = [], []
    for _ in range(TOP_K):
        m = jnp.max(s, axis=0, keepdims=True)
        e = jnp.min(jnp.where(s == m, row_f, 1e9), axis=0, keepdims=True)
        pick = row_f == e
        experts.append(e)
        weights.append(jnp.sum(jnp.where(pick, aff, 0.0), axis=0, keepdims=True))
        s = jnp.where(pick, NEG, s)
    total = weights[0]
    for w in weights[1:]:
        total = total + w

    hot = jnp.where(s == NEG, 1.0, 0.0)
    r_i = lax.broadcasted_iota(I32, (tm, tm), 0)
    c_i = lax.broadcasted_iota(I32, (tm, tm), 1)
    earlier = jnp.where(r_i < c_i, 1.0, 0.0).astype(BF16)
    before = _dot(hot.astype(BF16), earlier) + run[...]
    ranks = [jnp.sum(jnp.where(row_f == e, before, 0.0), axis=0, keepdims=True) for e in experts]
    eidx_ref[0] = jnp.concatenate(experts, axis=0).astype(I32)
    wts_ref[0] = jnp.concatenate([ROUTE_SCALE * w / total for w in weights], axis=0)
    rank_ref[0] = jnp.concatenate(ranks, axis=0).astype(I32)
    run[...] = run[...] + jnp.sum(hot, axis=1, keepdims=True)
    cnt_out_ref[...] = run[...]


def _merge_router(o_attn, o_conv, x, gate, shift, scale, goa, goc, wout_bf, g2, rw_hi, rw_lo, rb,
                  cnt_in, tm):
    b, t, d = x.shape
    n_exp = rw_hi.shape[0]
    row = lambda n: pl.BlockSpec((1, tm, n), lambda i, j: (i, j, 0))
    pick = pl.BlockSpec((1, TOP_K, tm), lambda i, j: (i, 0, j))
    const = lambda shape: pl.BlockSpec(shape, lambda i, j: (0,) * len(shape))
    sds = lambda n, dt: jax.ShapeDtypeStruct((b, t, n), dt)
    picks = lambda dt: jax.ShapeDtypeStruct((b, TOP_K, t), dt)
    return pl.pallas_call(
        _merge_router_kernel,
        grid=(b, t // tm),
        in_specs=[row(o_attn.shape[-1]), row(o_conv.shape[-1]), row(d),
                  _mod_spec(gate, tm, d), _mod_spec(shift, tm, d), _mod_spec(scale, tm, d),
                  const(goa.shape), const(goc.shape), const(wout_bf.shape), const(g2.shape),
                  const(rw_hi.shape), const(rw_lo.shape), const(rb.shape), const(cnt_in.shape)],
        out_specs=[row(d), row(d), pick, pick, pick, const((n_exp, 1))],
        out_shape=[sds(d, F32), sds(d, F32), picks(I32), picks(F32), picks(I32),
                   jax.ShapeDtypeStruct((n_exp, 1), F32)],
        scratch_shapes=[pltpu.VMEM((n_exp, 1), F32)],
        compiler_params=_cparams("arbitrary", "arbitrary"),
        name="merge_router",
    )(o_attn, o_conv, x, gate, shift, scale, goa, goc, wout_bf, g2, rw_hi, rw_lo, rb, cnt_in)


def _row_copy(src_hbm, dst, src_row, dst_row, sem, chunks):
    return pltpu.make_async_copy(src_hbm.at[pl.ds(src_row * chunks, chunks)],
                                 dst.at[pl.ds(dst_row * chunks, chunks)], sem)


def _slot(start_ref, e_ref, r_ref, idx):
    return start_ref[e_ref[0, 0, idx]] + r_ref[0, 0, idx]


def _dispatch_kernel(cnt_ref, end_ref, start_ref, e_ref, r_ref, h_ref, xs_hbm, zbuf, zsem, sem,
                     *, tokens, rows, chunks, n_blocks):
    j = pl.program_id(0)
    n_exp = cnt_ref.shape[0]
    blk_rows = rows * chunks

    def zero_block(blk):
        return pltpu.make_async_copy(zbuf, xs_hbm.at[pl.ds(blk * blk_rows, blk_rows)], zsem)

    @pl.when(j == 0)
    def _():
        zbuf[...] = jnp.zeros(zbuf.shape, F32)
        n_active = end_ref[n_exp - 1] // rows

        def zero_tail(e, issued):
            partial = cnt_ref[e] % rows != 0

            @pl.when(partial)
            def _():
                zero_block(end_ref[e] // rows - 1).start()

            return issued + partial.astype(I32)

        def zero_unused(blk, _):
            zero_block(blk).start()
            return 0

        def drain_zero(_, c):
            zero_block(0).wait()
            return c

        issued = lax.fori_loop(0, n_exp, zero_tail, 0)
        lax.fori_loop(n_active, n_blocks, zero_unused, 0)
        lax.fori_loop(0, issued + (n_blocks - n_active), drain_zero, 0)

    def issue(r, _):
        for k in range(TOP_K):
            _row_copy(h_ref, xs_hbm, r, _slot(start_ref, e_ref, r_ref, r * TOP_K + k), sem, chunks).start()
        return 0

    def drain(r, _):
        for k in range(TOP_K):
            _row_copy(h_ref, xs_hbm, 0, 0, sem, chunks).wait()
        return 0

    lax.fori_loop(0, tokens, issue, 0)
    lax.fori_loop(0, tokens, drain, 0)


def _dispatch(counts, pad_end, pad_start, e_idx, rank, h_rows, tokens, rows, chunks, n_blocks):
    n_tiles = e_idx.shape[0]
    picks = pl.BlockSpec((1, 1, tokens * TOP_K), lambda j, c, e, s: (j, 0, 0), memory_space=pltpu.SMEM)
    return pl.pallas_call(
        functools.partial(_dispatch_kernel, tokens=tokens, rows=rows, chunks=chunks, n_blocks=n_blocks),
        grid_spec=pltpu.PrefetchScalarGridSpec(
            num_scalar_prefetch=3,
            grid=(n_tiles,),
            in_specs=[picks, picks,
                      pl.BlockSpec((tokens * chunks, LANES), lambda j, c, e, s: (j, 0))],
            out_specs=pl.BlockSpec(memory_space=pl.ANY),
            scratch_shapes=[pltpu.VMEM((rows * chunks, LANES), F32),
                            pltpu.SemaphoreType.DMA(()), pltpu.SemaphoreType.DMA(())],
        ),
        out_shape=jax.ShapeDtypeStruct((n_blocks * rows * chunks, LANES), F32),
        compiler_params=_cparams("arbitrary"),
        name="moe_dispatch",
    )(counts, pad_end, pad_start, e_idx, rank, h_rows)


def _expert_kernel(be_ref, nact_ref, x_ref, wg_ref, wu_ref, wd_ref, y_ref, *, rows, chunks):
    j = pl.program_id(0)

    @pl.when(j < nact_ref[0])
    def _():
        f = wg_ref.shape[2]
        gate = jnp.zeros((rows, f), F32)
        up = jnp.zeros((rows, f), F32)
        for c in range(0, chunks, 2):
            xc = jnp.concatenate([x_ref[pl.ds(c, rows, stride=chunks), :],
                                  x_ref[pl.ds(c + 1, rows, stride=chunks), :]], axis=1).astype(BF16)
            cs = slice(c * LANES, (c + 2) * LANES)
            gate = gate + _dot(xc, wg_ref[0, cs, :].astype(BF16))
            up = up + _dot(xc, wu_ref[0, cs, :].astype(BF16))
        h = (_silu(gate) * up).astype(BF16)
        y = _dot(h, wd_ref[0].astype(BF16))
        for c in range(chunks):
            y_ref[pl.ds(c, rows, stride=chunks), :] = y[:, c * LANES:(c + 1) * LANES]

    @pl.when(j >= nact_ref[0])
    def _():
        y_ref[...] = jnp.zeros(y_ref.shape, F32)


def _experts(blk_expert, n_active, xs, wg, wu, wd, rows, chunks):
    n_blocks = blk_expert.shape[0]
    d, f = wg.shape[1], wg.shape[2]
    last = lambda j, na: jnp.minimum(j, na[0] - 1)
    xspec = pl.BlockSpec((rows * chunks, LANES), lambda j, be, na: (last(j, na), 0))
    wspec = lambda s: pl.BlockSpec((1,) + s, lambda j, be, na: (be[last(j, na)], 0, 0))
    return pl.pallas_call(
        functools.partial(_expert_kernel, rows=rows, chunks=chunks),
        grid_spec=pltpu.PrefetchScalarGridSpec(
            num_scalar_prefetch=2,
            grid=(n_blocks,),
            in_specs=[xspec, wspec((d, f)), wspec((d, f)), wspec((f, d))],
            out_specs=pl.BlockSpec((rows * chunks, LANES), lambda j, be, na: (j, 0)),
        ),
        out_shape=jax.ShapeDtypeStruct(xs.shape, F32),
        compiler_params=_cparams("arbitrary"),
        name="moe_experts",
    )(blk_expert, n_active, xs, wg, wu, wd)


def _combine_kernel(start_ref, e_ref, r_ref, w_ref, x1_ref, h2_ref, gate_ref, wsg_ref, wsu_ref, wsd_ref,
                    gf_ref, ys_hbm, o_ref, buf, sem, *, chunks):
    tm = x1_ref.shape[1]

    def issue(r, _):
        for k in range(TOP_K):
            _row_copy(ys_hbm, buf.at[k], _slot(start_ref, e_ref, r_ref, r * TOP_K + k), r, sem, chunks).start()
        return 0

    def drain(r, _):
        for k in range(TOP_K):
            _row_copy(ys_hbm, buf.at[k], 0, r, sem, chunks).wait()
        return 0

    lax.fori_loop(0, tm, issue, 0)
    h_bf = h2_ref[0].astype(BF16)
    hid = (_silu(_dot(h_bf, wsg_ref[...])) * _dot(h_bf, wsu_ref[...])).astype(BF16)
    shared = _dot(hid, wsd_ref[...])
    lax.fori_loop(0, tm, drain, 0)

    w = w_ref[0]
    cols = []
    for c in range(chunks):
        tot = jnp.zeros((tm, LANES), F32)
        for k in range(TOP_K):
            tot = tot + buf[k, pl.ds(c, tm, stride=chunks), :] * w[:, k:k + 1]
        cols.append(tot)
    routed = jnp.concatenate(cols, axis=1)
    x2 = x1_ref[0] + gate_ref[0] * (routed + shared)
    o_ref[0] = _rms(x2, gf_ref[...])


def _combine(pad_start, e_idx, rank, wts, x1, h2, gate, wsg_bf, wsu_bf, wsd_bf, gf, ys, tm, chunks):
    b, t, d = x1.shape
    nt = t // tm
    row = lambda n: pl.BlockSpec((1, tm, n), lambda i, j, *_: (i, j, 0))
    const = lambda shape: pl.BlockSpec(shape, lambda i, j, *_: (0,) * len(shape))
    picks = pl.BlockSpec((1, 1, tm * TOP_K), lambda i, j, *_: (i * nt + j, 0, 0), memory_space=pltpu.SMEM)
    return pl.pallas_call(
        functools.partial(_combine_kernel, chunks=chunks),
        grid_spec=pltpu.PrefetchScalarGridSpec(
            num_scalar_prefetch=1,
            grid=(b, nt),
            in_specs=[picks, picks, row(TOP_K), row(d), row(d), _mod_spec(gate, tm, d),
                      const(wsg_bf.shape), const(wsu_bf.shape), const(wsd_bf.shape), const(gf.shape),
                      pl.BlockSpec(memory_space=pl.ANY)],
            out_specs=row(d),
            scratch_shapes=[pltpu.VMEM((TOP_K, tm * chunks, LANES), F32), pltpu.SemaphoreType.DMA(())],
        ),
        out_shape=jax.ShapeDtypeStruct((b, t, d), F32),
        compiler_params=_cparams("arbitrary", "arbitrary"),
        name="moe_combine",
    )(pad_start, e_idx, rank, wts, x1, h2, gate, wsg_bf, wsu_bf, wsd_bf, gf, ys)


def _split_mod(mod, per_token):
    parts = jnp.split(mod, 6, axis=-1)
    if per_token:
        return [p[None] for p in parts]
    return [p[:, None, :] for p in parts]


def _padded_in_weight(w_in, conv_width):
    n_gate = 3 * N_HEADS
    o = ATTN_WIDTH + 6 * KV_COLS
    main = w_in[:, :o]
    gates = jnp.pad(w_in[:, o:o + n_gate], ((0, 0), (0, LANES - n_gate)))
    glu = w_in[:, o + n_gate:o + n_gate + 2 * conv_width]
    return jnp.concatenate([main, gates, glu], axis=1).astype(BF16)


def _cmp_rows(x):
    return x.reshape(x.shape[:-2] + (x.shape[-2] // CMP_BLOCK, CMP_BLOCK * KV_COLS))


def _largest_tile(n, cap):
    best = [k for k in range(SUBLANES, cap + 1, SUBLANES) if n % k == 0]
    assert best, (n, cap)
    return best[-1]


def _kv5(x):
    return x.reshape(x.shape[:-1] + (N_KV, HEAD_DIM))[None]


def kernel(x_prompt, x_sample, cache_k_cmp, cache_v_cmp, cache_k_slc, cache_v_slc, state_k_win, state_v_win, state_conv, page_table, c_prompt, c_sample, norm1_g, norm2_g, w_ada, b_ada, w_in, w_cmp_k, w_cmp_v, w_dw, b_dw, ln_conv_g, ln_conv_b, g_out_attn, g_out_conv, w_out, router_w, router_b, w_exp_gate, w_exp_up, w_exp_down, w_sh_gate, w_sh_up, w_sh_down, norm_f_g):
    assert w_ada.shape[0] == 1, "single layer"
    bp, t, d = x_prompt.shape
    bs, s_new, _ = x_sample.shape
    assert s_new == 1
    n_pool, page = cache_k_cmp.shape[1], cache_k_cmp.shape[2]
    n_pages = page_table.shape[1]
    past = n_pages * page
    conv_width = state_conv.shape[-1]
    n_exp = router_w.shape[-1]
    chunks = d // LANES
    tm = min(ROW_TILE, t)

    w_in_bf = _padded_in_weight(w_in[0], conv_width)
    wck = _compress_weight(w_cmp_k[0])
    wcv = _compress_weight(w_cmp_v[0])
    wout_bf = w_out[0].astype(BF16)
    rw_t = router_w[0].T
    rw_hi = rw_t.astype(BF16)
    rw_lo = (rw_t - rw_hi.astype(F32)).astype(BF16)
    wsg_bf, wsu_bf, wsd_bf = (w[0].astype(BF16) for w in (w_sh_gate, w_sh_up, w_sh_down))
    gf = norm_f_g[None, :]

    n_c = bp + bs
    c_all = jnp.concatenate([c_prompt, c_sample], axis=0)
    c_all = jnp.pad(c_all, ((0, (-n_c) % SUBLANES), (0, 0)))
    mod = _modulation(c_all, w_ada[0], b_ada)
    mp = _split_mod(mod[:bp], per_token=False)
    ms = _split_mod(mod[bp:n_c], per_token=True)

    (q_p, kc_p, vc_p, ks_p, vs_p, kw_p, vw_p, gate_p, u_p) = _in_proj(
        x_prompt, mp[0], mp[1], norm1_g, w_in_bf, tm)
    nc_p = t // CMP_BLOCK
    kcc, vcc = _compress(_cmp_rows(kc_p).reshape(bp * nc_p, -1), _cmp_rows(vc_p).reshape(bp * nc_p, -1),
                         wck, wcv)
    o_attn_p = _prompt_attention(q_p, gate_p, kcc.reshape(bp, nc_p, KV_COLS), vcc.reshape(bp, nc_p, KV_COLS),
                                 ks_p, vs_p, kw_p, vw_p)
    o_conv_p = _conv_prompt(u_p, w_dw[0], b_dw, ln_conv_g, ln_conv_b, tm)

    xs_row = x_sample.reshape(1, bs, d)
    (q_s, kc_s, vc_s, ks_s, vs_s, kw_s, vw_s, gate_s, u_s) = _in_proj(
        xs_row, ms[0], ms[1], norm1_g, w_in_bf, bs)
    q8 = q_s.reshape(bs, N_HEADS, HEAD_DIM)
    gates8 = gate_s[0, :, :3 * N_HEADS].reshape(bs, N_HEADS, 3)
    pages_t = lambda c: jnp.transpose(c[0], (0, 2, 3, 1))
    tail = (-(past + s_new)) % SLC_BLOCK
    n_new = (s_new + tail) // CMP_BLOCK
    tail_rows = lambda x: _cmp_rows(jnp.pad(x[0][:, None, :], ((0, 0), (0, tail), (0, 0)))).reshape(bs * n_new, -1)
    pad_rows = (-(bs * n_new)) % SUBLANES
    kc_new, vc_new = _compress(jnp.pad(tail_rows(kc_s), ((0, pad_rows), (0, 0))),
                               jnp.pad(tail_rows(vc_s), ((0, pad_rows), (0, 0))), wck, wcv)
    kc_new = kc_new[:bs * n_new].reshape(bs, n_new, KV_COLS)
    vc_new = vc_new[:bs * n_new].reshape(bs, n_new, KV_COLS)
    reps = page // CMP_BLOCK
    wk_fold = jnp.transpose(w_cmp_k[0], (2, 1, 0)).reshape(HEAD_DIM, HEAD_DIM * CMP_BLOCK)
    ut = _matmul3(q8.reshape(bs * N_HEADS, HEAD_DIM) * ATTN_SCALE, wk_fold)
    ut = jnp.tile(ut.reshape(bs, N_HEADS, HEAD_DIM, CMP_BLOCK), (1, 1, 1, reps))
    s_raw = _sample_scores(page_table, ut, pages_t(cache_k_cmp))
    p_exp, p_new, sel = _sample_select(s_raw, q8, kc_new, past)
    y_acc = _sample_values(page_table, p_exp, pages_t(cache_v_cmp))
    wv_fold = jnp.tile(jnp.transpose(w_cmp_v[0], (1, 0, 2)), (1, reps, 1)).reshape(HEAD_DIM * page, HEAD_DIM)
    o_cmp_s = _matmul3(y_acc.reshape(bs * N_HEADS, HEAD_DIM * page), wv_fold).reshape(bs, N_HEADS, HEAD_DIM)
    n_sel = min(N_SEL, (past // CMP_BLOCK + n_new) // CMP_PER_SLC)
    sel = sel[:, :N_KV, :n_sel]
    row3 = lambda x: x[0][:, None, :]
    o_attn_s = _sample_attend(
        sel, page_table, q8, pages_t(cache_k_slc), pages_t(cache_v_slc), row3(ks_s), row3(vs_s),
        pages_t(state_k_win), pages_t(state_v_win), row3(kw_s), row3(vw_s), gates8, o_cmp_s, p_new, vc_new,
        past)
    o_attn_s = o_attn_s.reshape(1, bs, ATTN_WIDTH)
    up_s = jnp.concatenate([state_conv[0], u_s[0][:, None, :]], axis=1)
    o_conv_s = _conv_sample(up_s, w_dw[0], b_dw, ln_conv_g, ln_conv_b)[None]

    router = functools.partial(_merge_router, goa=g_out_attn, goc=g_out_conv, wout_bf=wout_bf, g2=norm2_g,
                               rw_hi=rw_hi, rw_lo=rw_lo, rb=router_b[0][:, None])
    x1_p, h2_p, e_p, w_p, r_p, cnt = router(o_attn_p, o_conv_p, x_prompt, mp[2], mp[3], mp[4],
                                            cnt_in=jnp.zeros((n_exp, 1), F32), tm=tm)
    x1_s, h2_s, e_s, w_s, r_s, cnt = router(o_attn_s, o_conv_s, xs_row, ms[2], ms[3], ms[4],
                                            cnt_in=cnt, tm=bs)

    n_tok = bp * t + bs
    counts = cnt[:, 0].astype(I32)
    padded = (counts + MOE_ROWS - 1) // MOE_ROWS * MOE_ROWS
    pad_end = jnp.cumsum(padded)
    pad_end = pad_end.astype(I32)
    pad_start = pad_end - padded
    n_blocks = -(-(n_tok * TOP_K) // MOE_ROWS) + n_exp
    blk_first = jnp.arange(n_blocks, dtype=I32) * MOE_ROWS
    blk_expert = jnp.minimum(jnp.sum(pad_end[None, :] <= blk_first[:, None], axis=1), n_exp - 1).astype(I32)
    n_active = pad_end[-1:] // MOE_ROWS
    picks = lambda a: jnp.transpose(a, (0, 2, 1))
    w_p, w_s = picks(w_p), picks(w_s)
    e_all = jnp.concatenate([picks(e_p).reshape(-1, TOP_K), picks(e_s).reshape(-1, TOP_K)], axis=0)
    r_all = jnp.concatenate([picks(r_p).reshape(-1, TOP_K), picks(r_s).reshape(-1, TOP_K)], axis=0)

    tile = _largest_tile(n_tok, 512)
    h_rows = jnp.concatenate([h2_p.reshape(-1, d), h2_s.reshape(-1, d)], axis=0).reshape(n_tok * chunks, LANES)
    xs = _dispatch(counts, pad_end, pad_start, e_all.reshape(n_tok // tile, 1, tile * TOP_K),
                   r_all.reshape(n_tok // tile, 1, tile * TOP_K), h_rows, tile, MOE_ROWS, chunks, n_blocks)
    ys = _experts(blk_expert, n_active, xs, w_exp_gate[0], w_exp_up[0], w_exp_down[0], MOE_ROWS, chunks)

    comb = functools.partial(_combine, pad_start, wsg_bf=wsg_bf, wsu_bf=wsu_bf, wsd_bf=wsd_bf, gf=gf, ys=ys,
                             chunks=chunks)
    tiles_p = (bp * (t // tm), 1, tm * TOP_K)
    y_prompt = comb(picks(e_p).reshape(tiles_p), picks(r_p).reshape(tiles_p), w_p, x1_p, h2_p, mp[5], tm=tm)
    y_sample = comb(picks(e_s).reshape(1, 1, bs * TOP_K), picks(r_s).reshape(1, 1, bs * TOP_K), w_s, x1_s, h2_s,
                    ms[5], tm=bs).reshape(bs, 1, d)

    win = min(WINDOW, t)
    hist = state_conv.shape[2]
    out_p = [_kv5(a) for a in (kc_p, vc_p, ks_p, vs_p, kw_p[:, t - win:], vw_p[:, t - win:])]
    conv_p = u_p[:, t - hist:][None]
    out_s = [_kv5(a[0][:, None, :]) for a in (kc_s, vc_s, ks_s, vs_s)]
    w_buf = state_k_win.shape[2]
    kw_buf = jnp.concatenate([state_k_win, _kv5(kw_s[0][:, None, :])], axis=2)[:, :, -w_buf:]
    vw_buf = jnp.concatenate([state_v_win, _kv5(vw_s[0][:, None, :])], axis=2)[:, :, -w_buf:]
    conv_s = up_s[:, -hist:][None]
    return (y_prompt, y_sample, *out_p, conv_p, *out_s, kw_buf, vw_buf, conv_s)
```

```python
import functools

import jax
import jax.numpy as jnp
from jax import lax
from jax.experimental import pallas as pl
from jax.experimental.pallas import tpu as pltpu

F32 = jnp.float32
BF16 = jnp.bfloat16
I32 = jnp.int32

N_HEADS = 8
HEAD_DIM = 64
N_KV = 2
Q_PER_KV = N_HEADS // N_KV
ATTN_WIDTH = N_HEADS * HEAD_DIM
KV_COLS = N_KV * HEAD_DIM
CMP_BLOCK = 32
SLC_BLOCK = 64
CMP_PER_SLC = SLC_BLOCK // CMP_BLOCK
N_SEL = 16
WINDOW = 512
TOP_K = 8
ROUTE_SCALE = 2.5
EPS = 1e-6
FORCED = 1e4
NEG = -1e30
ATTN_SCALE = HEAD_DIM ** -0.5

LANES = 128
SUBLANES = 8
VMEM_LIMIT = 56 * 1024 * 1024

ROW_TILE = 256
Q_TILE = 128
KEY_TILE = 1024
CMP_ROW_TILE = 512
MOE_ROWS = 256
CONV_HALO = 32
CONV_ROWS = 32
PAGES_PER_STEP = 32


def _cparams(*sem):
    return pltpu.CompilerParams(dimension_semantics=sem, vmem_limit_bytes=VMEM_LIMIT)


def _dot(a, b):
    return jnp.dot(a, b, preferred_element_type=F32)


def _dot_nt(a, b):
    return lax.dot_general(a, b, (((1,), (1,)), ((), ())), preferred_element_type=F32)


def _dot_tn(a, b):
    return lax.dot_general(a, b, (((0,), (0,)), ((), ())), preferred_element_type=F32)


def _split2(x):
    hi = x.astype(BF16)
    lo = (x - hi.astype(F32)).astype(BF16)
    return hi, lo


def _dot3(a, b):
    ah, al = _split2(a)
    bh, bl = _split2(b)
    return _dot(ah, bh) + (_dot(ah, bl) + _dot(al, bh))


def _dot3_nt(a, b):
    ah, al = _split2(a)
    bh, bl = _split2(b)
    return _dot_nt(ah, bh) + (_dot_nt(ah, bl) + _dot_nt(al, bh))


def _sigmoid(x):
    return 1.0 / (1.0 + jnp.exp(-x))


def _silu(x):
    return x * _sigmoid(x)


def _rms(x, g):
    return x * lax.rsqrt(jnp.mean(x * x, axis=-1, keepdims=True) + EPS) * g


def _alibi_slope_col(rows, rows_per_head, first_head, n_heads):
    r = lax.broadcasted_iota(I32, (rows, 1), 0) // rows_per_head
    out = jnp.zeros((rows, 1), F32)
    for k in range(n_heads):
        out = jnp.where(r == k, 2.0 ** (-8.0 * (first_head + k + 1) / N_HEADS), out)
    return out


def _modulation_kernel(c_ref, w_ref, b_ref, o_ref):
    o_ref[...] = _dot3(c_ref[...], w_ref[...]) + b_ref[...]


def _modulation(c, w, b):
    m, d = c.shape
    n = w.shape[1]
    tn = 768
    return pl.pallas_call(
        _modulation_kernel,
        grid=(n // tn,),
        in_specs=[pl.BlockSpec((m, d), lambda j: (0, 0)),
                  pl.BlockSpec((d, tn), lambda j: (0, j)),
                  pl.BlockSpec((1, tn), lambda j: (0, j))],
        out_specs=pl.BlockSpec((m, tn), lambda j: (0, j)),
        out_shape=jax.ShapeDtypeStruct((m, n), F32),
        compiler_params=_cparams("arbitrary"),
        name="modulation",
    )(c, w, b)


def _mod_spec(mod, tm, d):
    if mod.shape[1] == 1:
        return pl.BlockSpec((1, 1, d), lambda i, j, *_: (i, 0, 0))
    return pl.BlockSpec((1, tm, d), lambda i, j, *_: (i, j, 0))


def _in_proj_kernel(x_ref, shift_ref, scale_ref, g_ref, w_ref,
                    q_ref, kc_ref, vc_ref, ks_ref, vs_ref, kw_ref, vw_ref, gate_ref, u_ref):
    x = x_ref[0]
    h = _rms(x, g_ref[...]) * (1.0 + scale_ref[0]) + shift_ref[0]
    z = _dot(h.astype(BF16), w_ref[...])
    q_ref[0] = z[:, :ATTN_WIDTH]
    o = ATTN_WIDTH
    for ref in (kc_ref, vc_ref, ks_ref, vs_ref, kw_ref, vw_ref):
        ref[0] = z[:, o:o + KV_COLS]
        o += KV_COLS
    gate_ref[0] = _sigmoid(z[:, o:o + LANES])
    o += LANES
    cw = u_ref.shape[-1]
    u_ref[0] = z[:, o:o + cw] * _sigmoid(z[:, o + cw:o + 2 * cw])


def _in_proj(x, shift, scale, g, w_bf, tm):
    b, t, d = x.shape
    cw = (w_bf.shape[1] - ATTN_WIDTH - 6 * KV_COLS - LANES) // 2
    row = lambda n: pl.BlockSpec((1, tm, n), lambda i, j: (i, j, 0))
    sds = lambda n: jax.ShapeDtypeStruct((b, t, n), F32)
    return pl.pallas_call(
        _in_proj_kernel,
        grid=(b, t // tm),
        in_specs=[row(d), _mod_spec(shift, tm, d), _mod_spec(scale, tm, d),
                  pl.BlockSpec((1, d), lambda i, j: (0, 0)),
                  pl.BlockSpec(w_bf.shape, lambda i, j: (0, 0))],
        out_specs=[row(ATTN_WIDTH)] + [row(KV_COLS)] * 6 + [row(LANES), row(cw)],
        out_shape=[sds(ATTN_WIDTH)] + [sds(KV_COLS)] * 6 + [sds(LANES), sds(cw)],
        compiler_params=_cparams("arbitrary", "arbitrary"),
        name="in_proj",
    )(x, shift, scale, g, w_bf)


def _compress_kernel(k_ref, v_ref, wk_ref, wv_ref, ko_ref, vo_ref):
    ko_ref[...] = _dot3(k_ref[...], wk_ref[...])
    vo_ref[...] = _dot3(v_ref[...], wv_ref[...])


def _compress(k_rows, v_rows, wk, wv):
    r, kdim = k_rows.shape
    tr = min(CMP_ROW_TILE, r)
    assert r % tr == 0
    rows = pl.BlockSpec((tr, kdim), lambda i: (i, 0))
    wspec = pl.BlockSpec((kdim, KV_COLS), lambda i: (0, 0))
    ospec = pl.BlockSpec((tr, KV_COLS), lambda i: (i, 0))
    return pl.pallas_call(
        _compress_kernel,
        grid=(r // tr,),
        in_specs=[rows, rows, wspec, wspec],
        out_specs=[ospec, ospec],
        out_shape=[jax.ShapeDtypeStruct((r, KV_COLS), F32)] * 2,
        compiler_params=_cparams("arbitrary"),
        name="compress",
    )(k_rows, v_rows, wk, wv)


def _compress_weight(w):
    eye = jnp.eye(N_KV, dtype=w.dtype)
    big = jnp.einsum('lde,gh->lgdhe', w, eye)
    return big.reshape(CMP_BLOCK * KV_COLS, KV_COLS)


def _pair_sum(x, axis):
    n = x.shape[axis]
    idx = lax.broadcasted_iota(I32, x.shape, axis)
    nxt = pltpu.roll(x, n - 1, axis)
    prv = pltpu.roll(x, 1, axis)
    return x + jnp.where((idx & 1) == 0, nxt, prv)


def _block_scores(imp, blk, q_pos, n_blocks_total):
    cur = q_pos // SLC_BLOCK
    valid = jnp.logical_and(blk * SLC_BLOCK <= q_pos, blk < n_blocks_total)
    forced = jnp.logical_or(blk == 0, jnp.logical_or(blk == cur, blk == cur - 1))
    return jnp.where(valid, jnp.where(forced, FORCED, imp), -1.0)


def _select_blocks(score, blk, n_sel):
    blk_f = blk.astype(F32)
    s = score
    for _ in range(n_sel):
        m = jnp.max(s, axis=0, keepdims=True)
        first = jnp.min(jnp.where(s == m, blk_f, 1e9), axis=0, keepdims=True)
        s = jnp.where(blk_f == first, -2.0, s)
    return jnp.where(jnp.logical_and(s == -2.0, score >= 0.0), 1.0, 0.0)


def _prompt_attn_kernel(q_ref, gate_ref, kc_ref, vc_ref, ks_ref, vs_ref, kw_ref, vw_ref, o_ref,
                        *, seq, n_sel):
    i = pl.program_id(1)
    tq = Q_TILE
    nc = kc_ref.shape[1]
    q_blk = q_ref[0] * ATTN_SCALE
    gates = gate_ref[0]
    slopes = [2.0 ** (-8.0 * (h + 1) / N_HEADS) for h in range(N_HEADS)]
    group_heads = [list(range(g * Q_PER_KV, (g + 1) * Q_PER_KV)) for g in range(N_KV)]
    gsl = [slice(g * HEAD_DIM, (g + 1) * HEAD_DIM) for g in range(N_KV)]
    rsl = [slice(r * tq, (r + 1) * tq) for r in range(Q_PER_KV)]
    q_pos_col = i * tq + lax.broadcasted_iota(I32, (tq, 1), 0)
    q_pos_row = i * tq + lax.broadcasted_iota(I32, (1, tq), 1)
    qg = [jnp.concatenate([q_blk[:, h * HEAD_DIM:(h + 1) * HEAD_DIM] for h in hs], axis=0)
          for hs in group_heads]
    qg_bf = [x.astype(BF16) for x in qg]

    cmp_row = lax.broadcasted_iota(I32, (nc, tq), 0)
    dist_c = (q_pos_row - (cmp_row * CMP_BLOCK + (CMP_BLOCK - 1))).astype(F32)
    mask_c = dist_c >= 0.0
    blk = cmp_row >> 1
    o_cmp = [None] * N_HEADS
    sel_bf = []
    for g, hs in enumerate(group_heads):
        vc_bf = vc_ref[0][:, gsl[g]].astype(BF16)
        qk = _dot3_nt(kc_ref[0][:, gsl[g]], qg[g])
        imp = jnp.zeros((nc, tq), F32)
        for r, h in enumerate(hs):
            s = jnp.where(mask_c, qk[:, rsl[r]] - slopes[h] * dist_c, NEG)
            m = jnp.max(s, axis=0, keepdims=True)
            p = jnp.where(mask_c, jnp.exp(s - m), 0.0)
            p = p / jnp.maximum(jnp.sum(p, axis=0, keepdims=True), 1e-30)
            o_cmp[h] = _dot_tn(p.astype(BF16), vc_bf)
            imp = imp + p
        score = _block_scores(_pair_sum(imp, 0), blk, q_pos_row, seq // SLC_BLOCK)
        sel_bf.append(_select_blocks(score, blk, n_sel).astype(BF16))

    span = WINDOW + tq
    w_start = pl.multiple_of(jnp.maximum(i * tq - WINDOW, 0), tq)
    dist_w = (q_pos_col - (w_start + lax.broadcasted_iota(I32, (tq, span), 1))).astype(F32)
    bias_w = jnp.where(jnp.logical_and(dist_w >= 0.0, dist_w <= float(WINDOW)), 0.0, NEG)
    o_win = [None] * N_HEADS
    for g, hs in enumerate(group_heads):
        kw_bf = kw_ref[0, pl.ds(w_start, span), :][:, gsl[g]].astype(BF16)
        vw_bf = vw_ref[0, pl.ds(w_start, span), :][:, gsl[g]].astype(BF16)
        s_all = _dot_nt(qg_bf[g], kw_bf)
        probs, sums = [], []
        for r, h in enumerate(hs):
            s = s_all[rsl[r]] + (bias_w - slopes[h] * dist_w)
            p = jnp.exp(s - jnp.max(s, axis=1, keepdims=True))
            sums.append(jnp.sum(p, axis=1, keepdims=True))
            probs.append(p.astype(BF16))
        o_all = _dot(jnp.concatenate(probs, axis=0), vw_bf)
        for r, h in enumerate(hs):
            o_win[h] = o_all[rsl[r]] / sums[r]

    n_tiles = ((i + 1) * tq + KEY_TILE - 1) // KEY_TILE

    def slc_step(t, carry):
        ms, ls, accs = (list(c) for c in carry)
        k0 = pl.multiple_of(t * KEY_TILE, KEY_TILE)
        dist = (q_pos_col - (k0 + lax.broadcasted_iota(I32, (tq, KEY_TILE), 1))).astype(F32)
        causal = dist >= 0.0
        key_cmp = (k0 + lax.broadcasted_iota(I32, (nc, KEY_TILE), 1)) // CMP_BLOCK
        expand = jnp.where(key_cmp == lax.broadcasted_iota(I32, (nc, KEY_TILE), 0), 1.0, 0.0).astype(BF16)
        for g, hs in enumerate(group_heads):
            kt_bf = ks_ref[0, pl.ds(k0, KEY_TILE), :][:, gsl[g]].astype(BF16)
            vt_bf = vs_ref[0, pl.ds(k0, KEY_TILE), :][:, gsl[g]].astype(BF16)
            chosen = _dot_tn(sel_bf[g], expand)
            bias = jnp.where(jnp.logical_and(causal, chosen > 0.5), 0.0, NEG)
            s_all = _dot_nt(qg_bf[g], kt_bf)
            probs, alphas = [], []
            for r, h in enumerate(hs):
                s = s_all[rsl[r]] + (bias - slopes[h] * dist)
                m_new = jnp.maximum(ms[h], jnp.max(s, axis=1, keepdims=True))
                alpha = jnp.exp(ms[h] - m_new)
                p = jnp.exp(s - m_new)
                ls[h] = alpha * ls[h] + jnp.sum(p, axis=1, keepdims=True)
                ms[h] = m_new
                alphas.append(alpha)
                probs.append(p.astype(BF16))
            pv = _dot(jnp.concatenate(probs, axis=0), vt_bf)
            for r, h in enumerate(hs):
                accs[h] = alphas[r] * accs[h] + pv[rsl[r]]
        return tuple(ms), tuple(ls), tuple(accs)

    init = (tuple(jnp.full((tq, 1), NEG, F32) for _ in range(N_HEADS)),
            tuple(jnp.zeros((tq, 1), F32) for _ in range(N_HEADS)),
            tuple(jnp.zeros((tq, HEAD_DIM), F32) for _ in range(N_HEADS)))
    _, l_s, acc_s = lax.fori_loop(0, n_tiles, slc_step, init)

    pieces = []
    for h in range(N_HEADS):
        o_slc = acc_s[h] / jnp.maximum(l_s[h], 1e-30)
        pieces.append(o_cmp[h] * gates[:, 3 * h + 0:3 * h + 1] + o_slc * gates[:, 3 * h + 1:3 * h + 2]
                      + o_win[h] * gates[:, 3 * h + 2:3 * h + 3])
    o_ref[0] = jnp.concatenate(pieces, axis=1)


def _prompt_attention(q, gates, kc, vc, ks, vs, kw, vw):
    b, t, _ = q.shape
    nc = kc.shape[1]
    assert t % KEY_TILE == 0 and t >= WINDOW + Q_TILE
    n_sel = min(N_SEL, t // SLC_BLOCK)
    qspec = lambda n: pl.BlockSpec((1, Q_TILE, n), lambda bi, i: (bi, i, 0))
    full = lambda r: pl.BlockSpec((1, r, KV_COLS), lambda bi, i: (bi, 0, 0))
    return pl.pallas_call(
        functools.partial(_prompt_attn_kernel, seq=t, n_sel=n_sel),
        grid=(b, t // Q_TILE),
        in_specs=[qspec(ATTN_WIDTH), qspec(LANES), full(nc), full(nc),
                  full(t), full(t), full(t), full(t)],
        out_specs=qspec(ATTN_WIDTH),
        out_shape=jax.ShapeDtypeStruct((b, t, ATTN_WIDTH), F32),
        compiler_params=_cparams("arbitrary", "arbitrary"),
        name="prompt_attention",
    )(q, gates, kc, vc, ks, vs, kw, vw)


def _merge_groups(per_group):
    row = lax.broadcasted_iota(I32, per_group[0].shape, 0) // Q_PER_KV
    out = per_group[0]
    for g in range(1, N_KV):
        out = jnp.where(row == g, per_group[g], out)
    return out


def _group_slice(x, g):
    return x[:, g * HEAD_DIM:(g + 1) * HEAD_DIM]


def _matmul3_kernel(a_ref, b_ref, o_ref):
    o_ref[...] = _dot3(a_ref[...], b_ref[...])


def _matmul3(a, b):
    return pl.pallas_call(
        _matmul3_kernel,
        out_shape=jax.ShapeDtypeStruct((a.shape[0], b.shape[1]), F32),
        compiler_params=pltpu.CompilerParams(vmem_limit_bytes=VMEM_LIMIT),
        name="matmul3",
    )(a, b)


def _page_specs(n_pages, page):
    def spec(o):
        return pl.BlockSpec((1, N_KV, HEAD_DIM, page),
                            lambda i, j, pt: (pt[i * n_pages + j * PAGES_PER_STEP + o], 0, 0, 0))
    return [spec(o) for o in range(PAGES_PER_STEP)]


def _sample_scores_kernel(pt_ref, ut_ref, *refs):
    k_refs, o_ref = refs[:-1], refs[-1]
    for h in range(N_HEADS):
        g = h // Q_PER_KV
        u = ut_ref[0, h]
        rows = [jnp.sum(k_ref[0, g] * u, axis=0, keepdims=True) for k_ref in k_refs]
        o_ref[0, h] = jnp.concatenate(rows, axis=0)


def _sample_scores(page_table, ut, k_pages):
    b, n_pages = page_table.shape
    page = k_pages.shape[-1]
    assert n_pages % PAGES_PER_STEP == 0
    return pl.pallas_call(
        _sample_scores_kernel,
        grid_spec=pltpu.PrefetchScalarGridSpec(
            num_scalar_prefetch=1,
            grid=(b, n_pages // PAGES_PER_STEP),
            in_specs=[pl.BlockSpec((1, N_HEADS, HEAD_DIM, page), lambda i, j, pt: (i, 0, 0, 0))]
                     + _page_specs(n_pages, page),
            out_specs=pl.BlockSpec((1, N_HEADS, PAGES_PER_STEP, page), lambda i, j, pt: (i, 0, j, 0)),
        ),
        out_shape=jax.ShapeDtypeStruct((b, N_HEADS, n_pages, page), F32),
        compiler_params=_cparams("arbitrary", "arbitrary"),
        name="sample_scores",
    )(page_table.reshape(-1), ut, *([k_pages] * PAGES_PER_STEP))


def _max_all(x):
    return jnp.max(jnp.max(x, axis=0, keepdims=True), axis=1, keepdims=True)


def _min_all(x):
    return jnp.min(jnp.min(x, axis=0, keepdims=True), axis=1, keepdims=True)


def _sum_all(x):
    return jnp.sum(jnp.sum(x, axis=0, keepdims=True), axis=1, keepdims=True)


def _sample_select_kernel(s_ref, q_ref, kcn_ref, pexp_ref, pnew_ref, sel_ref, *, past, n_new, n_sel):
    n_pages, page = s_ref.shape[2], s_ref.shape[3]
    cpp = page // CMP_BLOCK
    n_past = n_pages * cpp
    n_blocks_total = (n_past + n_new) // CMP_PER_SLC
    lane = lax.broadcasted_iota(I32, (n_pages, page), 1)
    prow = lax.broadcasted_iota(I32, (n_pages, page), 0)
    dist = (past - ((prow * cpp + lane // CMP_BLOCK) * CMP_BLOCK + (CMP_BLOCK - 1))).astype(F32)
    mask = jnp.logical_and(lane % CMP_BLOCK == 0, dist >= 0.0)

    q8 = q_ref[0] * ATTN_SCALE
    slope = _alibi_slope_col(N_HEADS, 1, 0, N_HEADS)
    kcn = jnp.concatenate([kcn_ref[0], jnp.zeros((LANES - n_new, KV_COLS), F32)], axis=0)
    new_lane = lax.broadcasted_iota(I32, (N_HEADS, LANES), 1)
    dist_n = (past - ((n_past + new_lane) * CMP_BLOCK + (CMP_BLOCK - 1))).astype(F32)
    mask_n = jnp.logical_and(dist_n >= 0.0, new_lane < n_new)
    qk_n = _merge_groups([_dot3_nt(q8, _group_slice(kcn, g)) for g in range(N_KV)])
    s_new = jnp.where(mask_n, qk_n - slope * dist_n, NEG)

    probs, probs_new = [], []
    for h in range(N_HEADS):
        x = s_ref[0, h]
        for sh in (16, 8, 4, 2, 1):
            x = x + pltpu.roll(x, page - sh, 1)
        s = jnp.where(mask, x - 2.0 ** (-8.0 * (h + 1) / N_HEADS) * dist, NEG)
        sn = s_new[h:h + 1, :]
        mn = jnp.logical_and(dist_n[h:h + 1, :] >= 0.0, new_lane[h:h + 1, :] < n_new)
        m = jnp.maximum(_max_all(s), jnp.max(sn, axis=1, keepdims=True))
        p = jnp.where(mask, jnp.exp(s - m), 0.0)
        pn = jnp.where(mn, jnp.exp(sn - m), 0.0)
        den = jnp.maximum(_sum_all(p) + jnp.sum(pn, axis=1, keepdims=True), 1e-30)
        p = p / den
        probs.append(p)
        probs_new.append(pn / den)
        z = p
        for sh in (1, 2, 4, 8, 16):
            z = z + pltpu.roll(z, sh, 1)
        pexp_ref[0, h] = z
    pnew_ref[0] = jnp.concatenate(probs_new, axis=0)

    row1 = lax.broadcasted_iota(I32, (1, LANES), 1)
    blk = jnp.where(lane % SLC_BLOCK == 0, prow * (page // SLC_BLOCK) + lane // SLC_BLOCK, -1)
    blk_n = jnp.where(row1 < n_new, n_past // CMP_PER_SLC + (row1 >> 1), -1)
    blk_f = blk.astype(F32)
    blk_nf = blk_n.astype(F32)
    out_lane = lax.broadcasted_iota(I32, (N_HEADS, LANES), 1)
    out_row = lax.broadcasted_iota(I32, (N_HEADS, LANES), 0)
    out = jnp.full((N_HEADS, LANES), -1, I32)
    for g in range(N_KV):
        imp = probs[g * Q_PER_KV]
        imp_n = probs_new[g * Q_PER_KV]
        for r in range(1, Q_PER_KV):
            imp = imp + probs[g * Q_PER_KV + r]
            imp_n = imp_n + probs_new[g * Q_PER_KV + r]
        imp = imp + pltpu.roll(imp, page - CMP_BLOCK, 1)
        s_m = jnp.where(blk >= 0, _block_scores(imp, blk, past, n_blocks_total), -4.0)
        s_n = jnp.where(blk_n >= 0, _block_scores(_pair_sum(imp_n, 1), blk_n, past, n_blocks_total), -4.0)
        for j in range(n_sel):
            top = jnp.maximum(_max_all(s_m), jnp.max(s_n, axis=1, keepdims=True))
            first = jnp.minimum(_min_all(jnp.where(s_m == top, blk_f, 1e9)),
                                jnp.min(jnp.where(s_n == top, blk_nf, 1e9), axis=1, keepdims=True))
            s_m = jnp.where(blk_f == first, -2.0, s_m)
            s_n = jnp.where(blk_nf == first, -2.0, s_n)
            pick = jnp.where(top >= 0.0, first.astype(I32), -1)
            out = jnp.where(jnp.logical_and(out_row == g, out_lane == j), pick, out)
    sel_ref[0] = out


def _sample_select(s_raw, q8, kc_new, past):
    b, _, n_pages, page = s_raw.shape
    n_new = kc_new.shape[1]
    assert CMP_PER_SLC == 2 and CMP_BLOCK == 32
    n_sel = min(N_SEL, (past // CMP_BLOCK + n_new) // CMP_PER_SLC)
    per_b = lambda *s: pl.BlockSpec((1,) + s, lambda i: (i,) + (0,) * len(s))
    return pl.pallas_call(
        functools.partial(_sample_select_kernel, past=past, n_new=n_new, n_sel=n_sel),
        grid=(b,),
        in_specs=[per_b(N_HEADS, n_pages, page), per_b(N_HEADS, HEAD_DIM), per_b(n_new, KV_COLS)],
        out_specs=[per_b(N_HEADS, n_pages, page), per_b(N_HEADS, LANES), per_b(N_HEADS, LANES)],
        out_shape=[jax.ShapeDtypeStruct((b, N_HEADS, n_pages, page), F32),
                   jax.ShapeDtypeStruct((b, N_HEADS, LANES), F32),
                   jax.ShapeDtypeStruct((b, N_HEADS, LANES), I32)],
        compiler_params=_cparams("arbitrary"),
        name="sample_select",
    )(s_raw, q8, kc_new)


def _sample_values_kernel(pt_ref, pe_ref, *refs):
    v_refs, y_ref = refs[:-1], refs[-1]

    @pl.when(pl.program_id(1) == 0)
    def _():
        y_ref[...] = jnp.zeros(y_ref.shape, F32)

    for g in range(N_KV):
        heads = range(g * Q_PER_KV, (g + 1) * Q_PER_KV)
        pe = [pe_ref[0, h] for h in heads]
        acc = [jnp.zeros(y_ref.shape[2:], F32) for _ in heads]
        for o, v_ref in enumerate(v_refs):
            v = v_ref[0, g]
            for r in range(Q_PER_KV):
                acc[r] = acc[r] + v * pe[r][o:o + 1, :]
        for r, h in enumerate(heads):
            y_ref[0, h] = y_ref[0, h] + acc[r]


def _sample_values(page_table, pexp, v_pages):
    b, n_pages = page_table.shape
    page = v_pages.shape[-1]
    return pl.pallas_call(
        _sample_values_kernel,
        grid_spec=pltpu.PrefetchScalarGridSpec(
            num_scalar_prefetch=1,
            grid=(b, n_pages // PAGES_PER_STEP),
            in_specs=[pl.BlockSpec((1, N_HEADS, PAGES_PER_STEP, page), lambda i, j, pt: (i, 0, j, 0))]
                     + _page_specs(n_pages, page),
            out_specs=pl.BlockSpec((1, N_HEADS, HEAD_DIM, page), lambda i, j, pt: (i, 0, 0, 0)),
        ),
        out_shape=jax.ShapeDtypeStruct((b, N_HEADS, HEAD_DIM, page), F32),
        compiler_params=_cparams("arbitrary", "arbitrary"),
        name="sample_values",
    )(page_table.reshape(-1), pexp, *([v_pages] * PAGES_PER_STEP))


def _new_token_terms(q8, k_row, v_row):
    s = _merge_groups([jnp.sum(q8 * _group_slice(k_row, g), axis=1, keepdims=True) for g in range(N_KV)])
    v = _merge_groups([jnp.broadcast_to(_group_slice(v_row, g), (N_HEADS, HEAD_DIM)) for g in range(N_KV)])
    return s, v


def _sample_attend_kernel(sel_ref, pt_ref, q_ref, *refs, past, n_sel, ns_past):
    page_refs = refs[:4 * n_sel]
    (ksn_ref, vsn_ref, kw_ref, vw_ref, kwn_ref, vwn_ref, gate_ref, ocmp_ref, pnew_ref, vcn_ref,
     o_ref) = refs[4 * n_sel:]
    b = pl.program_id(0)
    page = page_refs[0].shape[-1]
    spp = page // SLC_BLOCK
    q8 = q_ref[0] * ATTN_SCALE
    q8_bf = q8.astype(BF16)
    slope = _alibi_slope_col(N_HEADS, 1, 0, N_HEADS)
    lane = lax.broadcasted_iota(I32, (N_HEADS, page), 1)
    s_t, v_t = _new_token_terms(q8, ksn_ref[0], vsn_ref[0])

    scores, masks, new_scores = [], [], []
    for n in range(n_sel):
        k_refs = page_refs[4 * n:4 * n + N_KV]
        blks = [sel_ref[(b * N_KV + g) * n_sel + n] for g in range(N_KV)]
        blk_rows = _merge_groups([jnp.full((N_HEADS, page), blk, I32) for blk in blks])
        blk_col = _merge_groups([jnp.full((N_HEADS, 1), blk, I32) for blk in blks])
        qk = _merge_groups([_dot(q8_bf, k_ref[0, 0].astype(BF16)) for k_ref in k_refs])
        page_pos = blk_rows // spp
        dist = (past - (page_pos * page + lane)).astype(F32)
        in_block = (lane // SLC_BLOCK) == (blk_rows - page_pos * spp)
        cached = jnp.logical_and(blk_rows >= 0, blk_rows < ns_past)
        mask = jnp.logical_and(jnp.logical_and(in_block, cached), dist >= 0.0)
        scores.append(jnp.where(mask, qk - slope * dist, NEG))
        masks.append(mask)
        new_scores.append(jnp.where(blk_col >= ns_past, s_t, NEG))
    m = new_scores[0]
    for s, sn in zip(scores, new_scores):
        m = jnp.maximum(m, jnp.maximum(jnp.max(s, axis=1, keepdims=True), sn))
    l_tot = jnp.zeros((N_HEADS, 1), F32)
    p_new = jnp.zeros((N_HEADS, 1), F32)
    acc = jnp.zeros((N_HEADS, HEAD_DIM), F32)
    for n in range(n_sel):
        v_refs = page_refs[4 * n + N_KV:4 * n + 2 * N_KV]
        p = jnp.where(masks[n], jnp.exp(scores[n] - m), 0.0)
        p_bf = p.astype(BF16)
        l_tot = l_tot + jnp.sum(p, axis=1, keepdims=True)
        p_new = p_new + jnp.where(new_scores[n] > 0.5 * NEG, jnp.exp(new_scores[n] - m), 0.0)
        acc = acc + _merge_groups([_dot_nt(p_bf, v_ref[0, 0].astype(BF16)) for v_ref in v_refs])
    o_slc = (acc + p_new * v_t) / jnp.maximum(l_tot + p_new, 1e-30)

    w_buf = kw_ref.shape[-1]
    wl = lax.broadcasted_iota(I32, (N_HEADS, w_buf), 1)
    win_pos = past - w_buf + wl
    dist_w = (past - win_pos).astype(F32)
    mask_w = jnp.logical_and(jnp.logical_and(dist_w >= 0.0, dist_w <= float(WINDOW)), win_pos >= 0)
    qk_w = _merge_groups([_dot(q8_bf, kw_ref[0, g].astype(BF16)) for g in range(N_KV)])
    s_w = jnp.where(mask_w, qk_w - slope * dist_w, NEG)
    s_t, v_t = _new_token_terms(q8, kwn_ref[0], vwn_ref[0])
    m_w = jnp.maximum(jnp.max(s_w, axis=1, keepdims=True), s_t)
    p_w = jnp.where(mask_w, jnp.exp(s_w - m_w), 0.0)
    p_t = jnp.exp(s_t - m_w)
    den = jnp.maximum(jnp.sum(p_w, axis=1, keepdims=True) + p_t, 1e-30)
    pw_bf = p_w.astype(BF16)
    o_w = _merge_groups([_dot_nt(pw_bf, vw_ref[0, g].astype(BF16)) for g in range(N_KV)])
    o_win = (o_w + p_t * v_t) / den
    n_new = vcn_ref.shape[1]
    vcn = jnp.concatenate([vcn_ref[0], jnp.zeros((LANES - n_new, KV_COLS), F32)], axis=0).astype(BF16)
    pn_bf = pnew_ref[0].astype(BF16)
    o_cmp = ocmp_ref[0] + _merge_groups([_dot(pn_bf, _group_slice(vcn, g)) for g in range(N_KV)])
    gt = gate_ref[0]
    o_ref[0] = o_cmp * gt[:, 0:1] + o_slc * gt[:, 1:2] + o_win * gt[:, 2:3]


def _sample_attend(sel, page_table, q8, k_pages, v_pages, ks_new, vs_new, kw_state, vw_state,
                   kw_new, vw_new, gates8, o_cmp, p_new, vc_new, past):
    b, n_pages = page_table.shape
    n_sel = sel.shape[-1]
    page = k_pages.shape[-1]
    ns_past = past // SLC_BLOCK
    spp = page // SLC_BLOCK
    w_buf = kw_state.shape[-1]
    n_new = vc_new.shape[1]

    def cache_map(n, g):
        def index(i, sel_ref, pt_ref):
            blk = jnp.clip(sel_ref[(i * N_KV + g) * n_sel + n], 0, ns_past - 1)
            return (pt_ref[i * n_pages + blk // spp], g, 0, 0)
        return pl.BlockSpec((1, 1, HEAD_DIM, page), index)

    page_specs, page_args = [], []
    for n in range(n_sel):
        for arr in (k_pages, v_pages):
            for g in range(N_KV):
                page_specs.append(cache_map(n, g))
                page_args.append(arr)
    per_b = lambda *s: pl.BlockSpec((1,) + s, lambda i, sl, pt: (i,) + (0,) * len(s))
    return pl.pallas_call(
        functools.partial(_sample_attend_kernel, past=past, n_sel=n_sel, ns_past=ns_past),
        grid_spec=pltpu.PrefetchScalarGridSpec(
            num_scalar_prefetch=2,
            grid=(b,),
            in_specs=[per_b(N_HEADS, HEAD_DIM)] + page_specs
                     + [per_b(1, KV_COLS), per_b(1, KV_COLS),
                        per_b(N_KV, HEAD_DIM, w_buf), per_b(N_KV, HEAD_DIM, w_buf),
                        per_b(1, KV_COLS), per_b(1, KV_COLS),
                        per_b(N_HEADS, 3), per_b(N_HEADS, HEAD_DIM), per_b(N_HEADS, LANES),
                        per_b(n_new, KV_COLS)],
            out_specs=per_b(N_HEADS, HEAD_DIM),
        ),
        out_shape=jax.ShapeDtypeStruct((b, N_HEADS, HEAD_DIM), F32),
        compiler_params=_cparams("arbitrary"),
        name="sample_attend",
    )(sel.reshape(-1), page_table.reshape(-1), q8, *page_args,
      ks_new, vs_new, kw_state, vw_state, kw_new, vw_new, gates8, o_cmp, p_new, vc_new)


def _layernorm_silu(y, g, b):
    mu = jnp.mean(y, axis=-1, keepdims=True)
    var = jnp.mean(jnp.square(y - mu), axis=-1, keepdims=True)
    return _silu((y - mu) * lax.rsqrt(var + EPS) * g + b)


def _conv_prompt_kernel(u_ref, w_ref, b_ref, g_ref, beta_ref, o_ref, buf):
    j = pl.program_id(1)
    tt = u_ref.shape[1]
    c = buf.shape[1]
    kw = w_ref.shape[0] // SUBLANES

    @pl.when(j == 0)
    def _():
        buf[0:CONV_HALO, :] = jnp.zeros((CONV_HALO, c), F32)

    buf[CONV_HALO:CONV_HALO + tt, :] = u_ref[0]
    first = CONV_HALO - (kw - 1)
    rows = CONV_ROWS
    for r0 in range(0, tt, rows):
        acc = jnp.zeros((rows // SUBLANES, SUBLANES, c), F32)
        for r in range(SUBLANES):
            taps = range(r, kw, SUBLANES)
            win = buf[pl.ds(first + r + r0, rows + SUBLANES * (len(taps) - 1)), :]
            for t, k in enumerate(taps):
                wk = w_ref[SUBLANES * k:SUBLANES * (k + 1), :]
                tap = win[SUBLANES * t:SUBLANES * t + rows].reshape(rows // SUBLANES, SUBLANES, c)
                acc = acc + wk[None] * tap
        y = acc.reshape(rows, c) + b_ref[...]
        o_ref[0, r0:r0 + rows, :] = _layernorm_silu(y, g_ref[...], beta_ref[...])
    buf[0:CONV_HALO, :] = buf[tt:tt + CONV_HALO, :]


def _conv_prompt(u, w_dw, b_dw, ln_g, ln_b, tt):
    b, t, c = u.shape
    vec = pl.BlockSpec((1, c), lambda i, j: (0, 0))
    w_rep = jnp.repeat(w_dw, SUBLANES, axis=0)
    return pl.pallas_call(
        _conv_prompt_kernel,
        grid=(b, t // tt),
        in_specs=[pl.BlockSpec((1, tt, c), lambda i, j: (i, j, 0)),
                  pl.BlockSpec(w_rep.shape, lambda i, j: (0, 0)), vec, vec, vec],
        out_specs=pl.BlockSpec((1, tt, c), lambda i, j: (i, j, 0)),
        out_shape=jax.ShapeDtypeStruct((b, t, c), F32),
        scratch_shapes=[pltpu.VMEM((CONV_HALO + tt, c), F32)],
        compiler_params=_cparams("arbitrary", "arbitrary"),
        name="conv_prompt",
    )(u, w_rep, b_dw, ln_g, ln_b)


def _conv_sample_kernel(up_ref, w_ref, b_ref, g_ref, beta_ref, o_ref):
    y = jnp.sum(up_ref[...] * w_ref[...][None, :, :], axis=1)
    o_ref[...] = _layernorm_silu(y + b_ref[...], g_ref[...], beta_ref[...])


def _conv_sample(up, w_dw, b_dw, ln_g, ln_b):
    b, kw, c = up.shape
    return pl.pallas_call(
        _conv_sample_kernel,
        out_shape=jax.ShapeDtypeStruct((b, c), F32),
        name="conv_sample",
    )(up, w_dw, b_dw, ln_g, ln_b)


def _merge_router_kernel(oa_ref, oc_ref, x_ref, gate_ref, shift_ref, scale_ref, goa_ref, goc_ref,
                         wout_ref, g2_ref, rwh_ref, rwl_ref, rb_ref, cnt_in_ref,
                         x1_ref, h2_ref, eidx_ref, wts_ref, rank_ref, cnt_out_ref, run):
    first = jnp.logical_and(pl.program_id(0) == 0, pl.program_id(1) == 0)

    @pl.when(first)
    def _():
        run[...] = cnt_in_ref[...]

    a = _rms(oa_ref[0], goa_ref[...])
    c = _rms(oc_ref[0], goc_ref[...])
    cat = jnp.concatenate([a, c], axis=1).astype(BF16)
    x1 = x_ref[0] + gate_ref[0] * _dot(cat, wout_ref[...])
    x1_ref[0] = x1
    h2 = _rms(x1, g2_ref[...]) * (1.0 + scale_ref[0]) + shift_ref[0]
    h2_ref[0] = h2

    hh, hl = _split2(h2)
    logits = _dot_nt(rwh_ref[...], hh) + (_dot_nt(rwl_ref[...], hh) + _dot_nt(rwh_ref[...], hl))
    aff = _sigmoid(logits)
    n_exp, tm = aff.shape
    row_f = lax.broadcasted_iota(I32, (n_exp, tm), 0).astype(F32)
    s = aff + rb_ref[...]
    experts, weights = [], []
    for _ in range(TOP_K):
        m = jnp.max(s, axis=0, keepdims=True)
        e = jnp.min(jnp.where(s == m, row_f, 1e9), axis=0, keepdims=True)
        pick = row_f == e
        experts.append(e)
        weights.append(jnp.sum(jnp.where(pick, aff, 0.0), axis=0, keepdims=True))
        s = jnp.where(pick, NEG, s)
    total = weights[0]
    for w in weights[1:]:
        total = total + w

    hot = jnp.where(s == NEG, 1.0, 0.0)
    r_i = lax.broadcasted_iota(I32, (tm, tm), 0)
    c_i = lax.broadcasted_iota(I32, (tm, tm), 1)
    earlier = jnp.where(r_i < c_i, 1.0, 0.0).astype(BF16)
    before = _dot(hot.astype(BF16), earlier) + run[...]
    ranks = [jnp.sum(jnp.where(row_f == e, before, 0.0), axis=0, keepdims=True) for e in experts]
    eidx_ref[0] = jnp.concatenate(experts, axis=0).astype(I32)
    wts_ref[0] = jnp.concatenate([ROUTE_SCALE * w / total for w in weights], axis=0)
    rank_ref[0] = jnp.concatenate(ranks, axis=0).astype(I32)
    run[...] = run[...] + jnp.sum(hot, axis=1, keepdims=True)
    cnt_out_ref[...] = run[...]


def _merge_router(o_attn, o_conv, x, gate, shift, scale, goa, goc, wout_bf, g2, rw_hi, rw_lo, rb,
                  cnt_in, tm):
    b, t, d = x.shape
    n_exp = rw_hi.shape[0]
    row = lambda n: pl.BlockSpec((1, tm, n), lambda i, j: (i, j, 0))
    pick = pl.BlockSpec((1, TOP_K, tm), lambda i, j: (i, 0, j))
    const = lambda shape: pl.BlockSpec(shape, lambda i, j: (0,) * len(shape))
    sds = lambda n, dt: jax.ShapeDtypeStruct((b, t, n), dt)
    picks = lambda dt: jax.ShapeDtypeStruct((b, TOP_K, t), dt)
    return pl.pallas_call(
        _merge_router_kernel,
        grid=(b, t // tm),
        in_specs=[row(o_attn.shape[-1]), row(o_conv.shape[-1]), row(d),
                  _mod_spec(gate, tm, d), _mod_spec(shift, tm, d), _mod_spec(scale, tm, d),
                  const(goa.shape), const(goc.shape), const(wout_bf.shape), const(g2.shape),
                  const(rw_hi.shape), const(rw_lo.shape), const(rb.shape), const(cnt_in.shape)],
        out_specs=[row(d), row(d), pick, pick, pick, const((n_exp, 1))],
        out_shape=[sds(d, F32), sds(d, F32), picks(I32), picks(F32), picks(I32),
                   jax.ShapeDtypeStruct((n_exp, 1), F32)],
        scratch_shapes=[pltpu.VMEM((n_exp, 1), F32)],
        compiler_params=_cparams("arbitrary", "arbitrary"),
        name="merge_router",
    )(o_attn, o_conv, x, gate, shift, scale, goa, goc, wout_bf, g2, rw_hi, rw_lo, rb, cnt_in)


def _row_copy(src_hbm, dst, src_row, dst_row, sem, chunks):
    return pltpu.make_async_copy(src_hbm.at[pl.ds(src_row * chunks, chunks)],
                                 dst.at[pl.ds(dst_row * chunks, chunks)], sem)


def _slot(start_ref, e_ref, r_ref, idx):
    return start_ref[e_ref[0, 0, idx]] + r_ref[0, 0, idx]


def _dispatch_kernel(cnt_ref, end_ref, start_ref, e_ref, r_ref, h_ref, xs_hbm, zbuf, zsem, sem,
                     *, tokens, rows, chunks, n_blocks):
    j = pl.program_id(0)
    n_exp = cnt_ref.shape[0]
    blk_rows = rows * chunks

    def zero_block(blk):
        return pltpu.make_async_copy(zbuf, xs_hbm.at[pl.ds(blk * blk_rows, blk_rows)], zsem)

    @pl.when(j == 0)
    def _():
        zbuf[...] = jnp.zeros(zbuf.shape, F32)
        n_active = end_ref[n_exp - 1] // rows

        def zero_tail(e, issued):
            partial = cnt_ref[e] % rows != 0

            @pl.when(partial)
            def _():
                zero_block(end_ref[e] // rows - 1).start()

            return issued + partial.astype(I32)

        def zero_unused(blk, _):
            zero_block(blk).start()
            return 0

        def drain_zero(_, c):
            zero_block(0).wait()
            return c

        issued = lax.fori_loop(0, n_exp, zero_tail, 0)
        lax.fori_loop(n_active, n_blocks, zero_unused, 0)
        lax.fori_loop(0, issued + (n_blocks - n_active), drain_zero, 0)

    def issue(r, _):
        for k in range(TOP_K):
            _row_copy(h_ref, xs_hbm, r, _slot(start_ref, e_ref, r_ref, r * TOP_K + k), sem, chunks).start()
        return 0

    def drain(r, _):
        for k in range(TOP_K):
            _row_copy(h_ref, xs_hbm, 0, 0, sem, chunks).wait()
        return 0

    lax.fori_loop(0, tokens, issue, 0)
    lax.fori_loop(0, tokens, drain, 0)


def _dispatch(counts, pad_end, pad_start, e_idx, rank, h_rows, tokens, rows, chunks, n_blocks):
    n_tiles = e_idx.shape[0]
    picks = pl.BlockSpec((1, 1, tokens * TOP_K), lambda j, c, e, s: (j, 0, 0), memory_space=pltpu.SMEM)
    return pl.pallas_call(
        functools.partial(_dispatch_kernel, tokens=tokens, rows=rows, chunks=chunks, n_blocks=n_blocks),
        grid_spec=pltpu.PrefetchScalarGridSpec(
            num_scalar_prefetch=3,
            grid=(n_tiles,),
            in_specs=[picks, picks,
                      pl.BlockSpec((tokens * chunks, LANES), lambda j, c, e, s: (j, 0))],
            out_specs=pl.BlockSpec(memory_space=pl.ANY),
            scratch_shapes=[pltpu.VMEM((rows * chunks, LANES), F32),
                            pltpu.SemaphoreType.DMA(()), pltpu.SemaphoreType.DMA(())],
        ),
        out_shape=jax.ShapeDtypeStruct((n_blocks * rows * chunks, LANES), F32),
        compiler_params=_cparams("arbitrary"),
        name="moe_dispatch",
    )(counts, pad_end, pad_start, e_idx, rank, h_rows)


def _expert_kernel(be_ref, nact_ref, first_ref, slot_ref, next_ref, x_ref, wg_hbm, wu_hbm, wd_hbm, y_ref,
                   wg_buf, wu_buf, wd_buf, sems, *, rows, chunks):
    j = pl.program_id(0)

    def weight_copies(expert, slot):
        return [pltpu.make_async_copy(w_hbm.at[expert], buf.at[slot], sems.at[slot, i])
                for i, (w_hbm, buf) in enumerate(((wg_hbm, wg_buf), (wu_hbm, wu_buf), (wd_hbm, wd_buf)))]

    @pl.when(j < nact_ref[0])
    def _():
        slot = slot_ref[j]

        @pl.when(j == 0)
        def _():
            for cp in weight_copies(be_ref[0], 0):
                cp.start()

        @pl.when(first_ref[j] == 1)
        def _():
            for cp in weight_copies(be_ref[j], slot):
                cp.wait()

            @pl.when(next_ref[j] >= 0)
            def _():
                for cp in weight_copies(next_ref[j], 1 - slot):
                    cp.start()

        f = wg_buf.shape[2]
        gate = jnp.zeros((rows, f), F32)
        up = jnp.zeros((rows, f), F32)
        for c in range(0, chunks, 2):
            xc = jnp.concatenate([x_ref[pl.ds(c, rows, stride=chunks), :],
                                  x_ref[pl.ds(c + 1, rows, stride=chunks), :]], axis=1).astype(BF16)
            cs = pl.ds(c * LANES, 2 * LANES)
            gate = gate + _dot(xc, wg_buf[slot, cs, :].astype(BF16))
            up = up + _dot(xc, wu_buf[slot, cs, :].astype(BF16))
        h = (_silu(gate) * up).astype(BF16)
        y = _dot(h, wd_buf[slot].astype(BF16))
        for c in range(chunks):
            y_ref[pl.ds(c, rows, stride=chunks), :] = y[:, c * LANES:(c + 1) * LANES]

    @pl.when(j >= nact_ref[0])
    def _():
        y_ref[...] = jnp.zeros(y_ref.shape, F32)


def _experts(blk_expert, n_active, pad_end, xs, wg, wu, wd, rows, chunks):
    n_blocks = blk_expert.shape[0]
    n_exp, d, f = wg.shape
    first = jnp.concatenate([jnp.ones((1,), I32), (blk_expert[1:] != blk_expert[:-1]).astype(I32)])
    slot = (jnp.cumsum(first) - 1) % 2
    run_end = pad_end[blk_expert] // rows
    nxt = jnp.where(run_end < n_active[0], blk_expert[jnp.minimum(run_end, n_blocks - 1)], -1)
    last = lambda j, na: jnp.minimum(j, na[0] - 1)
    hbm = pl.BlockSpec(memory_space=pl.ANY)
    return pl.pallas_call(
        functools.partial(_expert_kernel, rows=rows, chunks=chunks),
        grid_spec=pltpu.PrefetchScalarGridSpec(
            num_scalar_prefetch=5,
            grid=(n_blocks,),
            in_specs=[pl.BlockSpec((rows * chunks, LANES), lambda j, be, na, *_: (last(j, na), 0)),
                      hbm, hbm, hbm],
            out_specs=pl.BlockSpec((rows * chunks, LANES), lambda j, *_: (j, 0)),
            scratch_shapes=[pltpu.VMEM((2, d, f), F32), pltpu.VMEM((2, d, f), F32),
                            pltpu.VMEM((2, f, d), F32), pltpu.SemaphoreType.DMA((2, 3))],
        ),
        out_shape=jax.ShapeDtypeStruct(xs.shape, F32),
        compiler_params=_cparams("arbitrary"),
        name="moe_experts",
    )(blk_expert, n_active, first, slot.astype(I32), nxt.astype(I32), xs, wg, wu, wd)


def _combine_kernel(start_ref, e_ref, r_ref, w_ref, x1_ref, h2_ref, gate_ref, wsg_ref, wsu_ref, wsd_ref,
                    gf_ref, ys_hbm, o_ref, buf, sem, *, chunks):
    tm = x1_ref.shape[1]

    def issue(r, _):
        for k in range(TOP_K):
            _row_copy(ys_hbm, buf.at[k], _slot(start_ref, e_ref, r_ref, r * TOP_K + k), r, sem, chunks).start()
        return 0

    def drain(r, _):
        for k in range(TOP_K):
            _row_copy(ys_hbm, buf.at[k], 0, r, sem, chunks).wait()
        return 0

    lax.fori_loop(0, tm, issue, 0)
    h_bf = h2_ref[0].astype(BF16)
    hid = (_silu(_dot(h_bf, wsg_ref[...])) * _dot(h_bf, wsu_ref[...])).astype(BF16)
    shared = _dot(hid, wsd_ref[...])
    lax.fori_loop(0, tm, drain, 0)

    w = w_ref[0]
    cols = []
    for c in range(chunks):
        tot = jnp.zeros((tm, LANES), F32)
        for k in range(TOP_K):
            tot = tot + buf[k, pl.ds(c, tm, stride=chunks), :] * w[:, k:k + 1]
        cols.append(tot)
    routed = jnp.concatenate(cols, axis=1)
    x2 = x1_ref[0] + gate_ref[0] * (routed + shared)
    o_ref[0] = _rms(x2, gf_ref[...])


def _combine(pad_start, e_idx, rank, wts, x1, h2, gate, wsg_bf, wsu_bf, wsd_bf, gf, ys, tm, chunks):
    b, t, d = x1.shape
    nt = t // tm
    row = lambda n: pl.BlockSpec((1, tm, n), lambda i, j, *_: (i, j, 0))
    const = lambda shape: pl.BlockSpec(shape, lambda i, j, *_: (0,) * len(shape))
    picks = pl.BlockSpec((1, 1, tm * TOP_K), lambda i, j, *_: (i * nt + j, 0, 0), memory_space=pltpu.SMEM)
    return pl.pallas_call(
        functools.partial(_combine_kernel, chunks=chunks),
        grid_spec=pltpu.PrefetchScalarGridSpec(
            num_scalar_prefetch=1,
            grid=(b, nt),
            in_specs=[picks, picks, row(TOP_K), row(d), row(d), _mod_spec(gate, tm, d),
                      const(wsg_bf.shape), const(wsu_bf.shape), const(wsd_bf.shape), const(gf.shape),
                      pl.BlockSpec(memory_space=pl.ANY)],
            out_specs=row(d),
            scratch_shapes=[pltpu.VMEM((TOP_K, tm * chunks, LANES), F32), pltpu.SemaphoreType.DMA(())],
        ),
        out_shape=jax.ShapeDtypeStruct((b, t, d), F32),
        compiler_params=_cparams("arbitrary", "arbitrary"),
        name="moe_combine",
    )(pad_start, e_idx, rank, wts, x1, h2, gate, wsg_bf, wsu_bf, wsd_bf, gf, ys)


def _split_mod(mod, per_token):
    parts = jnp.split(mod, 6, axis=-1)
    if per_token:
        return [p[None] for p in parts]
    return [p[:, None, :] for p in parts]


def _padded_in_weight(w_in, conv_width):
    n_gate = 3 * N_HEADS
    o = ATTN_WIDTH + 6 * KV_COLS
    main = w_in[:, :o]
    gates = jnp.pad(w_in[:, o:o + n_gate], ((0, 0), (0, LANES - n_gate)))
    glu = w_in[:, o + n_gate:o + n_gate + 2 * conv_width]
    return jnp.concatenate([main, gates, glu], axis=1).astype(BF16)


def _cmp_rows(x):
    return x.reshape(x.shape[:-2] + (x.shape[-2] // CMP_BLOCK, CMP_BLOCK * KV_COLS))


def _largest_tile(n, cap):
    best = [k for k in range(SUBLANES, cap + 1, SUBLANES) if n % k == 0]
    assert best, (n, cap)
    return best[-1]


def _kv5(x):
    return x.reshape(x.shape[:-1] + (N_KV, HEAD_DIM))[None]


def kernel(x_prompt, x_sample, cache_k_cmp, cache_v_cmp, cache_k_slc, cache_v_slc, state_k_win, state_v_win, state_conv, page_table, c_prompt, c_sample, norm1_g, norm2_g, w_ada, b_ada, w_in, w_cmp_k, w_cmp_v, w_dw, b_dw, ln_conv_g, ln_conv_b, g_out_attn, g_out_conv, w_out, router_w, router_b, w_exp_gate, w_exp_up, w_exp_down, w_sh_gate, w_sh_up, w_sh_down, norm_f_g):
    assert w_ada.shape[0] == 1, "single layer"
    bp, t, d = x_prompt.shape
    bs, s_new, _ = x_sample.shape
    assert s_new == 1
    n_pool, page = cache_k_cmp.shape[1], cache_k_cmp.shape[2]
    n_pages = page_table.shape[1]
    past = n_pages * page
    conv_width = state_conv.shape[-1]
    n_exp = router_w.shape[-1]
    chunks = d // LANES
    tm = min(ROW_TILE, t)

    w_in_bf = _padded_in_weight(w_in[0], conv_width)
    wck = _compress_weight(w_cmp_k[0])
    wcv = _compress_weight(w_cmp_v[0])
    wout_bf = w_out[0].astype(BF16)
    rw_t = router_w[0].T
    rw_hi = rw_t.astype(BF16)
    rw_lo = (rw_t - rw_hi.astype(F32)).astype(BF16)
    wsg_bf, wsu_bf, wsd_bf = (w[0].astype(BF16) for w in (w_sh_gate, w_sh_up, w_sh_down))
    gf = norm_f_g[None, :]

    n_c = bp + bs
    c_all = jnp.concatenate([c_prompt, c_sample], axis=0)
    c_all = jnp.pad(c_all, ((0, (-n_c) % SUBLANES), (0, 0)))
    mod = _modulation(c_all, w_ada[0], b_ada)
    mp = _split_mod(mod[:bp], per_token=False)
    ms = _split_mod(mod[bp:n_c], per_token=True)

    (q_p, kc_p, vc_p, ks_p, vs_p, kw_p, vw_p, gate_p, u_p) = _in_proj(
        x_prompt, mp[0], mp[1], norm1_g, w_in_bf, tm)
    nc_p = t // CMP_BLOCK
    kcc, vcc = _compress(_cmp_rows(kc_p).reshape(bp * nc_p, -1), _cmp_rows(vc_p).reshape(bp * nc_p, -1),
                         wck, wcv)
    o_attn_p = _prompt_attention(q_p, gate_p, kcc.reshape(bp, nc_p, KV_COLS), vcc.reshape(bp, nc_p, KV_COLS),
                                 ks_p, vs_p, kw_p, vw_p)
    o_conv_p = _conv_prompt(u_p, w_dw[0], b_dw, ln_conv_g, ln_conv_b, tm)

    xs_row = x_sample.reshape(1, bs, d)
    (q_s, kc_s, vc_s, ks_s, vs_s, kw_s, vw_s, gate_s, u_s) = _in_proj(
        xs_row, ms[0], ms[1], norm1_g, w_in_bf, bs)
    q8 = q_s.reshape(bs, N_HEADS, HEAD_DIM)
    gates8 = gate_s[0, :, :3 * N_HEADS].reshape(bs, N_HEADS, 3)
    pages_t = lambda c: jnp.transpose(c[0], (0, 2, 3, 1))
    tail = (-(past + s_new)) % SLC_BLOCK
    n_new = (s_new + tail) // CMP_BLOCK
    tail_rows = lambda x: _cmp_rows(jnp.pad(x[0][:, None, :], ((0, 0), (0, tail), (0, 0)))).reshape(bs * n_new, -1)
    pad_rows = (-(bs * n_new)) % SUBLANES
    kc_new, vc_new = _compress(jnp.pad(tail_rows(kc_s), ((0, pad_rows), (0, 0))),
                               jnp.pad(tail_rows(vc_s), ((0, pad_rows), (0, 0))), wck, wcv)
    kc_new = kc_new[:bs * n_new].reshape(bs, n_new, KV_COLS)
    vc_new = vc_new[:bs * n_new].reshape(bs, n_new, KV_COLS)
    reps = page // CMP_BLOCK
    wk_fold = jnp.transpose(w_cmp_k[0], (2, 1, 0)).reshape(HEAD_DIM, HEAD_DIM * CMP_BLOCK)
    ut = _matmul3(q8.reshape(bs * N_HEADS, HEAD_DIM) * ATTN_SCALE, wk_fold)
    ut = jnp.tile(ut.reshape(bs, N_HEADS, HEAD_DIM, CMP_BLOCK), (1, 1, 1, reps))
    s_raw = _sample_scores(page_table, ut, pages_t(cache_k_cmp))
    p_exp, p_new, sel = _sample_select(s_raw, q8, kc_new, past)
    y_acc = _sample_values(page_table, p_exp, pages_t(cache_v_cmp))
    wv_fold = jnp.tile(jnp.transpose(w_cmp_v[0], (1, 0, 2)), (1, reps, 1)).reshape(HEAD_DIM * page, HEAD_DIM)
    o_cmp_s = _matmul3(y_acc.reshape(bs * N_HEADS, HEAD_DIM * page), wv_fold).reshape(bs, N_HEADS, HEAD_DIM)
    n_sel = min(N_SEL, (past // CMP_BLOCK + n_new) // CMP_PER_SLC)
    sel = sel[:, :N_KV, :n_sel]
    row3 = lambda x: x[0][:, None, :]
    o_attn_s = _sample_attend(
        sel, page_table, q8, pages_t(cache_k_slc), pages_t(cache_v_slc), row3(ks_s), row3(vs_s),
        pages_t(state_k_win), pages_t(state_v_win), row3(kw_s), row3(vw_s), gates8, o_cmp_s, p_new, vc_new,
        past)
    o_attn_s = o_attn_s.reshape(1, bs, ATTN_WIDTH)
    up_s = jnp.concatenate([state_conv[0], u_s[0][:, None, :]], axis=1)
    o_conv_s = _conv_sample(up_s, w_dw[0], b_dw, ln_conv_g, ln_conv_b)[None]

    router = functools.partial(_merge_router, goa=g_out_attn, goc=g_out_conv, wout_bf=wout_bf, g2=norm2_g,
                               rw_hi=rw_hi, rw_lo=rw_lo, rb=router_b[0][:, None])
    x1_p, h2_p, e_p, w_p, r_p, cnt = router(o_attn_p, o_conv_p, x_prompt, mp[2], mp[3], mp[4],
                                            cnt_in=jnp.zeros((n_exp, 1), F32), tm=tm)
    x1_s, h2_s, e_s, w_s, r_s, cnt = router(o_attn_s, o_conv_s, xs_row, ms[2], ms[3], ms[4],
                                            cnt_in=cnt, tm=bs)

    n_tok = bp * t + bs
    counts = cnt[:, 0].astype(I32)
    padded = (counts + MOE_ROWS - 1) // MOE_ROWS * MOE_ROWS
    pad_end = jnp.cumsum(padded)
    pad_end = pad_end.astype(I32)
    pad_start = pad_end - padded
    n_blocks = -(-(n_tok * TOP_K) // MOE_ROWS) + n_exp
    blk_first = jnp.arange(n_blocks, dtype=I32) * MOE_ROWS
    blk_expert = jnp.minimum(jnp.sum(pad_end[None, :] <= blk_first[:, None], axis=1), n_exp - 1).astype(I32)
    n_active = pad_end[-1:] // MOE_ROWS
    picks = lambda a: jnp.transpose(a, (0, 2, 1))
    w_p, w_s = picks(w_p), picks(w_s)
    e_all = jnp.concatenate([picks(e_p).reshape(-1, TOP_K), picks(e_s).reshape(-1, TOP_K)], axis=0)
    r_all = jnp.concatenate([picks(r_p).reshape(-1, TOP_K), picks(r_s).reshape(-1, TOP_K)], axis=0)

    tile = _largest_tile(n_tok, 512)
    h_rows = jnp.concatenate([h2_p.reshape(-1, d), h2_s.reshape(-1, d)], axis=0).reshape(n_tok * chunks, LANES)
    xs = _dispatch(counts, pad_end, pad_start, e_all.reshape(n_tok // tile, 1, tile * TOP_K),
                   r_all.reshape(n_tok // tile, 1, tile * TOP_K), h_rows, tile, MOE_ROWS, chunks, n_blocks)
    ys = _experts(blk_expert, n_active, pad_end, xs, w_exp_gate[0], w_exp_up[0], w_exp_down[0], MOE_ROWS, chunks)

    comb = functools.partial(_combine, pad_start, wsg_bf=wsg_bf, wsu_bf=wsu_bf, wsd_bf=wsd_bf, gf=gf, ys=ys,
                             chunks=chunks)
    tiles_p = (bp * (t // tm), 1, tm * TOP_K)
    y_prompt = comb(picks(e_p).reshape(tiles_p), picks(r_p).reshape(tiles_p), w_p, x1_p, h2_p, mp[5], tm=tm)
    y_sample = comb(picks(e_s).reshape(1, 1, bs * TOP_K), picks(r_s).reshape(1, 1, bs * TOP_K), w_s, x1_s, h2_s,
                    ms[5], tm=bs).reshape(bs, 1, d)

    win = min(WINDOW, t)
    hist = state_conv.shape[2]
    out_p = [_kv5(a) for a in (kc_p, vc_p, ks_p, vs_p, kw_p[:, t - win:], vw_p[:, t - win:])]
    conv_p = u_p[:, t - hist:][None]
    out_s = [_kv5(a[0][:, None, :]) for a in (kc_s, vc_s, ks_s, vs_s)]
    w_buf = state_k_win.shape[2]
    kw_buf = jnp.concatenate([state_k_win, _kv5(kw_s[0][:, None, :])], axis=2)[:, :, -w_buf:]
    vw_buf = jnp.concatenate([state_v_win, _kv5(vw_s[0][:, None, :])], axis=2)[:, :, -w_buf:]
    conv_s = up_s[:, -hist:][None]
    return (y_prompt, y_sample, *out_p, conv_p, *out_s, kw_buf, vw_buf, conv_s)
```

```python
import functools

import jax
import jax.numpy as jnp
from jax import lax
from jax.experimental import pallas as pl
from jax.experimental.pallas import tpu as pltpu

F32 = jnp.float32
BF16 = jnp.bfloat16
I32 = jnp.int32

N_HEADS = 8
HEAD_DIM = 64
N_KV = 2
Q_PER_KV = N_HEADS // N_KV
ATTN_WIDTH = N_HEADS * HEAD_DIM
KV_COLS = N_KV * HEAD_DIM
CMP_BLOCK = 32
SLC_BLOCK = 64
CMP_PER_SLC = SLC_BLOCK // CMP_BLOCK
N_SEL = 16
WINDOW = 512
TOP_K = 8
ROUTE_SCALE = 2.5
EPS = 1e-6
FORCED = 1e4
NEG = -1e30
ATTN_SCALE = HEAD_DIM ** -0.5
PICKED = -2.0
NOT_A_BLOCK = -4.0
NO_INDEX = 1e9

LANES = 128
SUBLANES = 8
VMEM_LIMIT = 56 * 1024 * 1024

ROW_TILE = 256
Q_TILE = 128
KEY_TILE = 1024
CMP_ROW_TILE = 512
MOE_ROWS = 256
CONV_HALO = 32
POS_RADIX = 256
CONV_ROWS = 32
PAGES_PER_STEP = 32


def _cparams(*sem):
    return pltpu.CompilerParams(dimension_semantics=sem, vmem_limit_bytes=VMEM_LIMIT)


def _dot(a, b):
    return jnp.dot(a, b, preferred_element_type=F32)


def _dot_nt(a, b):
    return lax.dot_general(a, b, (((1,), (1,)), ((), ())), preferred_element_type=F32)


def _dot_tn(a, b):
    return lax.dot_general(a, b, (((0,), (0,)), ((), ())), preferred_element_type=F32)


def _split2(x):
    hi = x.astype(BF16)
    lo = (x - hi.astype(F32)).astype(BF16)
    return hi, lo


def _dot3(a, b):
    ah, al = _split2(a)
    bh, bl = _split2(b)
    return _dot(ah, bh) + (_dot(ah, bl) + _dot(al, bh))


def _dot3_nt(a, b):
    ah, al = _split2(a)
    bh, bl = _split2(b)
    return _dot_nt(ah, bh) + (_dot_nt(ah, bl) + _dot_nt(al, bh))


def _sigmoid(x):
    return 1.0 / (1.0 + jnp.exp(-x))


def _silu(x):
    return x * _sigmoid(x)


def _rms(x, g):
    return x * lax.rsqrt(jnp.mean(x * x, axis=-1, keepdims=True) + EPS) * g


def _alibi_slope_col(rows, rows_per_head, first_head, n_heads):
    r = lax.broadcasted_iota(I32, (rows, 1), 0) // rows_per_head
    out = jnp.zeros((rows, 1), F32)
    for k in range(n_heads):
        out = jnp.where(r == k, 2.0 ** (-8.0 * (first_head + k + 1) / N_HEADS), out)
    return out


def _modulation_kernel(c_ref, w_ref, b_ref, o_ref):
    o_ref[...] = _dot3(c_ref[...], w_ref[...]) + b_ref[...]


def _modulation(c, w, b):
    m, d = c.shape
    n = w.shape[1]
    tn = 768
    return pl.pallas_call(
        _modulation_kernel,
        grid=(n // tn,),
        in_specs=[pl.BlockSpec((m, d), lambda j: (0, 0)),
                  pl.BlockSpec((d, tn), lambda j: (0, j)),
                  pl.BlockSpec((1, tn), lambda j: (0, j))],
        out_specs=pl.BlockSpec((m, tn), lambda j: (0, j)),
        out_shape=jax.ShapeDtypeStruct((m, n), F32),
        compiler_params=_cparams("arbitrary"),
        name="modulation",
    )(c, w, b)


def _mod_spec(mod, tm, d):
    if mod.shape[1] == 1:
        return pl.BlockSpec((1, 1, d), lambda i, j, *_: (i, 0, 0))
    return pl.BlockSpec((1, tm, d), lambda i, j, *_: (i, j, 0))


def _in_proj_kernel(x_ref, shift_ref, scale_ref, g_ref, w_ref,
                    q_ref, kc_ref, vc_ref, ks_ref, vs_ref, kw_ref, vw_ref, gate_ref, u_ref):
    x = x_ref[0]
    h = _rms(x, g_ref[...]) * (1.0 + scale_ref[0]) + shift_ref[0]
    z = _dot(h.astype(BF16), w_ref[...])
    q_ref[0] = z[:, :ATTN_WIDTH]
    o = ATTN_WIDTH
    for ref in (kc_ref, vc_ref, ks_ref, vs_ref, kw_ref, vw_ref):
        ref[0] = z[:, o:o + KV_COLS]
        o += KV_COLS
    gate_ref[0] = _sigmoid(z[:, o:o + LANES])
    o += LANES
    cw = u_ref.shape[-1]
    u_ref[0] = z[:, o:o + cw] * _sigmoid(z[:, o + cw:o + 2 * cw])


def _in_proj(x, shift, scale, g, w_bf, tm):
    b, t, d = x.shape
    cw = (w_bf.shape[1] - ATTN_WIDTH - 6 * KV_COLS - LANES) // 2
    row = lambda n: pl.BlockSpec((1, tm, n), lambda i, j: (i, j, 0))
    sds = lambda n: jax.ShapeDtypeStruct((b, t, n), F32)
    return pl.pallas_call(
        _in_proj_kernel,
        grid=(b, t // tm),
        in_specs=[row(d), _mod_spec(shift, tm, d), _mod_spec(scale, tm, d),
                  pl.BlockSpec((1, d), lambda i, j: (0, 0)),
                  pl.BlockSpec(w_bf.shape, lambda i, j: (0, 0))],
        out_specs=[row(ATTN_WIDTH)] + [row(KV_COLS)] * 6 + [row(LANES), row(cw)],
        out_shape=[sds(ATTN_WIDTH)] + [sds(KV_COLS)] * 6 + [sds(LANES), sds(cw)],
        compiler_params=_cparams("arbitrary", "arbitrary"),
        name="in_proj",
    )(x, shift, scale, g, w_bf)


def _compress_kernel(k_ref, v_ref, wk_ref, wv_ref, ko_ref, vo_ref):
    ko_ref[...] = _dot3(k_ref[...], wk_ref[...])
    vo_ref[...] = _dot3(v_ref[...], wv_ref[...])


def _compress(k_rows, v_rows, wk, wv):
    r, kdim = k_rows.shape
    tr = min(CMP_ROW_TILE, r)
    assert r % tr == 0
    rows = pl.BlockSpec((tr, kdim), lambda i: (i, 0))
    wspec = pl.BlockSpec((kdim, KV_COLS), lambda i: (0, 0))
    ospec = pl.BlockSpec((tr, KV_COLS), lambda i: (i, 0))
    return pl.pallas_call(
        _compress_kernel,
        grid=(r // tr,),
        in_specs=[rows, rows, wspec, wspec],
        out_specs=[ospec, ospec],
        out_shape=[jax.ShapeDtypeStruct((r, KV_COLS), F32)] * 2,
        compiler_params=_cparams("arbitrary"),
        name="compress",
    )(k_rows, v_rows, wk, wv)


def _compress_weight(w):
    eye = jnp.eye(N_KV, dtype=w.dtype)
    big = jnp.einsum('lde,gh->lgdhe', w, eye)
    return big.reshape(CMP_BLOCK * KV_COLS, KV_COLS)


def _pair_sum(x, axis):
    n = x.shape[axis]
    idx = lax.broadcasted_iota(I32, x.shape, axis)
    nxt = pltpu.roll(x, n - 1, axis)
    prv = pltpu.roll(x, 1, axis)
    return x + jnp.where((idx & 1) == 0, nxt, prv)


def _block_scores(imp, blk, q_pos, n_blocks_total):
    cur = q_pos // SLC_BLOCK
    valid = jnp.logical_and(blk * SLC_BLOCK <= q_pos, blk < n_blocks_total)
    forced = jnp.logical_or(blk == 0, jnp.logical_or(blk == cur, blk == cur - 1))
    return jnp.where(valid, jnp.where(forced, FORCED, imp), -1.0)


def _select_blocks(score, blk, n_sel):
    blk_f = blk.astype(F32)
    s = score
    for _ in range(n_sel):
        m = jnp.max(s, axis=0, keepdims=True)
        first = jnp.min(jnp.where(s == m, blk_f, NO_INDEX), axis=0, keepdims=True)
        s = jnp.where(blk_f == first, PICKED, s)
    return jnp.where(jnp.logical_and(s == PICKED, score >= 0.0), 1.0, 0.0)


def _position_features(n):
    pos = jnp.arange(n, dtype=I32)[:, None]
    lane = jnp.arange(HEAD_DIM, dtype=I32)[None, :]
    feat = jnp.where(lane < 2, 1, jnp.where(lane == 2, pos // POS_RADIX, jnp.where(lane == 3, pos % POS_RADIX, 0)))
    return feat.astype(BF16)


def _query_position_features(q_pos, slope):
    lane = lax.broadcasted_iota(I32, (q_pos.shape[0], HEAD_DIM), 1)
    hi = (q_pos // POS_RADIX).astype(F32) * (-slope * POS_RADIX)
    lo = (q_pos % POS_RADIX).astype(F32) * (-slope)
    return jnp.where(lane == 0, hi, jnp.where(lane == 1, lo, jnp.where(
        lane == 2, slope * POS_RADIX, jnp.where(lane == 3, slope, 0.0))))


def _prompt_attn_kernel(q_ref, gate_ref, kc_ref, vc_ref, ks_ref, vs_ref, kw_ref, vw_ref, kx_ref, o_ref,
                        *, seq, n_sel):
    i = pl.program_id(1)
    tq = Q_TILE
    nc = kc_ref.shape[1]
    q_blk = q_ref[0] * ATTN_SCALE
    gates = gate_ref[0]
    slopes = [2.0 ** (-8.0 * (h + 1) / N_HEADS) for h in range(N_HEADS)]
    group_heads = [list(range(g * Q_PER_KV, (g + 1) * Q_PER_KV)) for g in range(N_KV)]
    gsl = [slice(g * HEAD_DIM, (g + 1) * HEAD_DIM) for g in range(N_KV)]
    rsl = [slice(r * tq, (r + 1) * tq) for r in range(Q_PER_KV)]
    q_pos_col = i * tq + lax.broadcasted_iota(I32, (tq, 1), 0)
    q_pos_row = i * tq + lax.broadcasted_iota(I32, (1, tq), 1)
    q_heads = [q_blk[:, h * HEAD_DIM:(h + 1) * HEAD_DIM] for h in range(N_HEADS)]
    qg = [jnp.concatenate([q_heads[h] for h in hs], axis=0) for hs in group_heads]
    qx_bf = [jnp.concatenate([jnp.concatenate([q_heads[h], _query_position_features(q_pos_col, slopes[h])],
                                              axis=1) for h in hs], axis=0).astype(BF16)
             for hs in group_heads]

    cmp_row = lax.broadcasted_iota(I32, (nc, tq), 0)
    dist_c = (q_pos_row - (cmp_row * CMP_BLOCK + (CMP_BLOCK - 1))).astype(F32)
    mask_c = dist_c >= 0.0
    blk = cmp_row >> 1
    o_cmp = [None] * N_HEADS
    sel_bf = []
    for g, hs in enumerate(group_heads):
        vc_bf = vc_ref[0][:, gsl[g]].astype(BF16)
        qk = _dot3_nt(kc_ref[0][:, gsl[g]], qg[g])
        imp = jnp.zeros((nc, tq), F32)
        for r, h in enumerate(hs):
            s = jnp.where(mask_c, qk[:, rsl[r]] - slopes[h] * dist_c, NEG)
            m = jnp.max(s, axis=0, keepdims=True)
            p = jnp.where(mask_c, jnp.exp(s - m), 0.0)
            p = p / jnp.maximum(jnp.sum(p, axis=0, keepdims=True), 1e-30)
            o_cmp[h] = _dot_tn(p.astype(BF16), vc_bf)
            imp = imp + p
        score = _block_scores(_pair_sum(imp, 0), blk, q_pos_row, seq // SLC_BLOCK)
        sel_bf.append(_select_blocks(score, blk, n_sel).astype(BF16))

    span = WINDOW + tq
    w_start = pl.multiple_of(jnp.maximum(i * tq - WINDOW, 0), tq)
    dist_w = (q_pos_col - (w_start + lax.broadcasted_iota(I32, (tq, span), 1))).astype(F32)
    bias_w = jnp.where(jnp.logical_and(dist_w >= 0.0, dist_w <= float(WINDOW)), 0.0, NEG)
    o_win = [None] * N_HEADS
    kx_w = kx_ref[pl.ds(w_start, span), :]
    for g, hs in enumerate(group_heads):
        kw_bf = kw_ref[0, pl.ds(w_start, span), :][:, gsl[g]].astype(BF16)
        vw_bf = vw_ref[0, pl.ds(w_start, span), :][:, gsl[g]].astype(BF16)
        s_all = _dot_nt(qx_bf[g], jnp.concatenate([kw_bf, kx_w], axis=1))
        probs, sums = [], []
        for r, h in enumerate(hs):
            s = s_all[rsl[r]] + bias_w
            p = jnp.exp(s - jnp.max(s, axis=1, keepdims=True))
            sums.append(jnp.sum(p, axis=1, keepdims=True))
            probs.append(p.astype(BF16))
        o_all = _dot(jnp.concatenate(probs, axis=0), vw_bf)
        for r, h in enumerate(hs):
            o_win[h] = o_all[rsl[r]] / sums[r]

    n_tiles = ((i + 1) * tq + KEY_TILE - 1) // KEY_TILE

    def slc_step(t, carry):
        ms, ls, accs = (list(c) for c in carry)
        k0 = pl.multiple_of(t * KEY_TILE, KEY_TILE)
        dist = (q_pos_col - (k0 + lax.broadcasted_iota(I32, (tq, KEY_TILE), 1))).astype(F32)
        causal = dist >= 0.0
        key_cmp = (k0 + lax.broadcasted_iota(I32, (nc, KEY_TILE), 1)) // CMP_BLOCK
        expand = jnp.where(key_cmp == lax.broadcasted_iota(I32, (nc, KEY_TILE), 0), 1.0, 0.0).astype(BF16)
        kx_t = kx_ref[pl.ds(k0, KEY_TILE), :]
        for g, hs in enumerate(group_heads):
            kt_bf = ks_ref[0, pl.ds(k0, KEY_TILE), :][:, gsl[g]].astype(BF16)
            vt_bf = vs_ref[0, pl.ds(k0, KEY_TILE), :][:, gsl[g]].astype(BF16)
            chosen = _dot_tn(sel_bf[g], expand)
            bias = jnp.where(jnp.logical_and(causal, chosen > 0.5), 0.0, NEG)
            s_all = _dot_nt(qx_bf[g], jnp.concatenate([kt_bf, kx_t], axis=1))
            probs, alphas = [], []
            for r, h in enumerate(hs):
                s = s_all[rsl[r]] + bias
                m_new = jnp.maximum(ms[h], jnp.max(s, axis=1, keepdims=True))
                alpha = jnp.exp(ms[h] - m_new)
                p = jnp.exp(s - m_new)
                ls[h] = alpha * ls[h] + jnp.sum(p, axis=1, keepdims=True)
                ms[h] = m_new
                alphas.append(alpha)
                probs.append(p.astype(BF16))
            pv = _dot(jnp.concatenate(probs, axis=0), vt_bf)
            for r, h in enumerate(hs):
                accs[h] = alphas[r] * accs[h] + pv[rsl[r]]
        return tuple(ms), tuple(ls), tuple(accs)

    init = (tuple(jnp.full((tq, 1), NEG, F32) for _ in range(N_HEADS)),
            tuple(jnp.zeros((tq, 1), F32) for _ in range(N_HEADS)),
            tuple(jnp.zeros((tq, HEAD_DIM), F32) for _ in range(N_HEADS)))
    _, l_s, acc_s = lax.fori_loop(0, n_tiles, slc_step, init)

    pieces = []
    for h in range(N_HEADS):
        o_slc = acc_s[h] / jnp.maximum(l_s[h], 1e-30)
        pieces.append(o_cmp[h] * gates[:, 3 * h + 0:3 * h + 1] + o_slc * gates[:, 3 * h + 1:3 * h + 2]
                      + o_win[h] * gates[:, 3 * h + 2:3 * h + 3])
    o_ref[0] = jnp.concatenate(pieces, axis=1)


def _prompt_attention(q, gates, kc, vc, ks, vs, kw, vw):
    b, t, _ = q.shape
    nc = kc.shape[1]
    assert t % KEY_TILE == 0 and t >= WINDOW + Q_TILE
    n_sel = min(N_SEL, t // SLC_BLOCK)
    qspec = lambda n: pl.BlockSpec((1, Q_TILE, n), lambda bi, i: (bi, i, 0))
    full = lambda r: pl.BlockSpec((1, r, KV_COLS), lambda bi, i: (bi, 0, 0))
    assert t <= POS_RADIX * POS_RADIX
    kx = _position_features(t)
    return pl.pallas_call(
        functools.partial(_prompt_attn_kernel, seq=t, n_sel=n_sel),
        grid=(b, t // Q_TILE),
        in_specs=[qspec(ATTN_WIDTH), qspec(LANES), full(nc), full(nc),
                  full(t), full(t), full(t), full(t), pl.BlockSpec(kx.shape, lambda bi, i: (0, 0))],
        out_specs=qspec(ATTN_WIDTH),
        out_shape=jax.ShapeDtypeStruct((b, t, ATTN_WIDTH), F32),
        compiler_params=_cparams("arbitrary", "arbitrary"),
        name="prompt_attention",
    )(q, gates, kc, vc, ks, vs, kw, vw, kx)


def _merge_groups(per_group):
    row = lax.broadcasted_iota(I32, per_group[0].shape, 0) // Q_PER_KV
    out = per_group[0]
    for g in range(1, N_KV):
        out = jnp.where(row == g, per_group[g], out)
    return out


def _group_slice(x, g):
    return x[:, g * HEAD_DIM:(g + 1) * HEAD_DIM]


def _matmul3_kernel(a_ref, b_ref, o_ref):
    o_ref[...] = _dot3(a_ref[...], b_ref[...])


def _matmul3(a, b):
    return pl.pallas_call(
        _matmul3_kernel,
        out_shape=jax.ShapeDtypeStruct((a.shape[0], b.shape[1]), F32),
        compiler_params=pltpu.CompilerParams(vmem_limit_bytes=VMEM_LIMIT),
        name="matmul3",
    )(a, b)


def _page_specs(n_pages, page):
    def spec(o):
        return pl.BlockSpec((1, N_KV, HEAD_DIM, page),
                            lambda i, j, pt: (pt[i * n_pages + j * PAGES_PER_STEP + o], 0, 0, 0))
    return [spec(o) for o in range(PAGES_PER_STEP)]


def _sample_scores_kernel(pt_ref, ut_ref, *refs):
    k_refs, o_ref = refs[:-1], refs[-1]
    for h in range(N_HEADS):
        g = h // Q_PER_KV
        u = ut_ref[0, h]
        rows = [jnp.sum(k_ref[0, g] * u, axis=0, keepdims=True) for k_ref in k_refs]
        o_ref[0, h] = jnp.concatenate(rows, axis=0)


def _sample_scores(page_table, ut, k_pages):
    b, n_pages = page_table.shape
    page = k_pages.shape[-1]
    assert n_pages % PAGES_PER_STEP == 0
    return pl.pallas_call(
        _sample_scores_kernel,
        grid_spec=pltpu.PrefetchScalarGridSpec(
            num_scalar_prefetch=1,
            grid=(b, n_pages // PAGES_PER_STEP),
            in_specs=[pl.BlockSpec((1, N_HEADS, HEAD_DIM, page), lambda i, j, pt: (i, 0, 0, 0))]
                     + _page_specs(n_pages, page),
            out_specs=pl.BlockSpec((1, N_HEADS, PAGES_PER_STEP, page), lambda i, j, pt: (i, 0, j, 0)),
        ),
        out_shape=jax.ShapeDtypeStruct((b, N_HEADS, n_pages, page), F32),
        compiler_params=_cparams("arbitrary", "arbitrary"),
        name="sample_scores",
    )(page_table.reshape(-1), ut, *([k_pages] * PAGES_PER_STEP))


def _max_all(x):
    return jnp.max(jnp.max(x, axis=0, keepdims=True), axis=1, keepdims=True)


def _min_all(x):
    return jnp.min(jnp.min(x, axis=0, keepdims=True), axis=1, keepdims=True)


def _sum_all(x):
    return jnp.sum(jnp.sum(x, axis=0, keepdims=True), axis=1, keepdims=True)


def _sample_select_kernel(s_ref, q_ref, kcn_ref, pexp_ref, pnew_ref, sel_ref, *, past, n_new, n_sel):
    n_pages, page = s_ref.shape[2], s_ref.shape[3]
    cpp = page // CMP_BLOCK
    n_past = n_pages * cpp
    n_blocks_total = (n_past + n_new) // CMP_PER_SLC
    lane = lax.broadcasted_iota(I32, (n_pages, page), 1)
    prow = lax.broadcasted_iota(I32, (n_pages, page), 0)
    dist = (past - ((prow * cpp + lane // CMP_BLOCK) * CMP_BLOCK + (CMP_BLOCK - 1))).astype(F32)
    mask = jnp.logical_and(lane % CMP_BLOCK == 0, dist >= 0.0)

    q8 = q_ref[0] * ATTN_SCALE
    slope = _alibi_slope_col(N_HEADS, 1, 0, N_HEADS)
    kcn = jnp.concatenate([kcn_ref[0], jnp.zeros((LANES - n_new, KV_COLS), F32)], axis=0)
    new_lane = lax.broadcasted_iota(I32, (N_HEADS, LANES), 1)
    dist_n = (past - ((n_past + new_lane) * CMP_BLOCK + (CMP_BLOCK - 1))).astype(F32)
    mask_n = jnp.logical_and(dist_n >= 0.0, new_lane < n_new)
    qk_n = _merge_groups([_dot3_nt(q8, _group_slice(kcn, g)) for g in range(N_KV)])
    s_new = jnp.where(mask_n, qk_n - slope * dist_n, NEG)

    halvings = [CMP_BLOCK >> k for k in range(1, CMP_BLOCK.bit_length())]
    probs, probs_new = [], []
    for h in range(N_HEADS):
        x = s_ref[0, h]
        for sh in halvings:
            x = x + pltpu.roll(x, page - sh, 1)
        s = jnp.where(mask, x - 2.0 ** (-8.0 * (h + 1) / N_HEADS) * dist, NEG)
        sn = s_new[h:h + 1, :]
        mn = jnp.logical_and(dist_n[h:h + 1, :] >= 0.0, new_lane[h:h + 1, :] < n_new)
        m = jnp.maximum(_max_all(s), jnp.max(sn, axis=1, keepdims=True))
        p = jnp.where(mask, jnp.exp(s - m), 0.0)
        pn = jnp.where(mn, jnp.exp(sn - m), 0.0)
        den = jnp.maximum(_sum_all(p) + jnp.sum(pn, axis=1, keepdims=True), 1e-30)
        p = p / den
        probs.append(p)
        probs_new.append(pn / den)
        z = p
        for sh in reversed(halvings):
            z = z + pltpu.roll(z, sh, 1)
        pexp_ref[0, h] = z
    pnew_ref[0] = jnp.concatenate(probs_new, axis=0)

    row1 = lax.broadcasted_iota(I32, (1, LANES), 1)
    blk = jnp.where(lane % SLC_BLOCK == 0, prow * (page // SLC_BLOCK) + lane // SLC_BLOCK, -1)
    blk_n = jnp.where(row1 < n_new, n_past // CMP_PER_SLC + (row1 >> 1), -1)
    blk_f = blk.astype(F32)
    blk_nf = blk_n.astype(F32)
    out_lane = lax.broadcasted_iota(I32, (N_HEADS, LANES), 1)
    out_row = lax.broadcasted_iota(I32, (N_HEADS, LANES), 0)
    out = jnp.full((N_HEADS, LANES), -1, I32)
    for g in range(N_KV):
        imp = probs[g * Q_PER_KV]
        imp_n = probs_new[g * Q_PER_KV]
        for r in range(1, Q_PER_KV):
            imp = imp + probs[g * Q_PER_KV + r]
            imp_n = imp_n + probs_new[g * Q_PER_KV + r]
        imp = imp + pltpu.roll(imp, page - CMP_BLOCK, 1)
        s_m = jnp.where(blk >= 0, _block_scores(imp, blk, past, n_blocks_total), NOT_A_BLOCK)
        s_n = jnp.where(blk_n >= 0, _block_scores(_pair_sum(imp_n, 1), blk_n, past, n_blocks_total), NOT_A_BLOCK)
        for j in range(n_sel):
            top = jnp.maximum(_max_all(s_m), jnp.max(s_n, axis=1, keepdims=True))
            first = jnp.minimum(_min_all(jnp.where(s_m == top, blk_f, NO_INDEX)),
                                jnp.min(jnp.where(s_n == top, blk_nf, NO_INDEX), axis=1, keepdims=True))
            s_m = jnp.where(blk_f == first, PICKED, s_m)
            s_n = jnp.where(blk_nf == first, PICKED, s_n)
            pick = jnp.where(top >= 0.0, first.astype(I32), -1)
            out = jnp.where(jnp.logical_and(out_row == g, out_lane == j), pick, out)
    sel_ref[0] = out


def _sample_select(s_raw, q8, kc_new, past):
    b, _, n_pages, page = s_raw.shape
    n_new = kc_new.shape[1]
    assert CMP_PER_SLC == 2 and CMP_BLOCK == 32
    n_sel = min(N_SEL, (past // CMP_BLOCK + n_new) // CMP_PER_SLC)
    per_b = lambda *s: pl.BlockSpec((1,) + s, lambda i: (i,) + (0,) * len(s))
    return pl.pallas_call(
        functools.partial(_sample_select_kernel, past=past, n_new=n_new, n_sel=n_sel),
        grid=(b,),
        in_specs=[per_b(N_HEADS, n_pages, page), per_b(N_HEADS, HEAD_DIM), per_b(n_new, KV_COLS)],
        out_specs=[per_b(N_HEADS, n_pages, page), per_b(N_HEADS, LANES), per_b(N_HEADS, LANES)],
        out_shape=[jax.ShapeDtypeStruct((b, N_HEADS, n_pages, page), F32),
                   jax.ShapeDtypeStruct((b, N_HEADS, LANES), F32),
                   jax.ShapeDtypeStruct((b, N_HEADS, LANES), I32)],
        compiler_params=_cparams("arbitrary"),
        name="sample_select",
    )(s_raw, q8, kc_new)


def _sample_values_kernel(pt_ref, pe_ref, *refs):
    v_refs, y_ref = refs[:-1], refs[-1]

    @pl.when(pl.program_id(1) == 0)
    def _():
        y_ref[...] = jnp.zeros(y_ref.shape, F32)

    for g in range(N_KV):
        heads = range(g * Q_PER_KV, (g + 1) * Q_PER_KV)
        pe = [pe_ref[0, h] for h in heads]
        acc = [jnp.zeros(y_ref.shape[2:], F32) for _ in heads]
        for o, v_ref in enumerate(v_refs):
            v = v_ref[0, g]
            for r in range(Q_PER_KV):
                acc[r] = acc[r] + v * pe[r][o:o + 1, :]
        for r, h in enumerate(heads):
            y_ref[0, h] = y_ref[0, h] + acc[r]


def _sample_values(page_table, pexp, v_pages):
    b, n_pages = page_table.shape
    page = v_pages.shape[-1]
    return pl.pallas_call(
        _sample_values_kernel,
        grid_spec=pltpu.PrefetchScalarGridSpec(
            num_scalar_prefetch=1,
            grid=(b, n_pages // PAGES_PER_STEP),
            in_specs=[pl.BlockSpec((1, N_HEADS, PAGES_PER_STEP, page), lambda i, j, pt: (i, 0, j, 0))]
                     + _page_specs(n_pages, page),
            out_specs=pl.BlockSpec((1, N_HEADS, HEAD_DIM, page), lambda i, j, pt: (i, 0, 0, 0)),
        ),
        out_shape=jax.ShapeDtypeStruct((b, N_HEADS, HEAD_DIM, page), F32),
        compiler_params=_cparams("arbitrary", "arbitrary"),
        name="sample_values",
    )(page_table.reshape(-1), pexp, *([v_pages] * PAGES_PER_STEP))


def _new_token_terms(q8, k_row, v_row):
    s = _merge_groups([jnp.sum(q8 * _group_slice(k_row, g), axis=1, keepdims=True) for g in range(N_KV)])
    v = _merge_groups([jnp.broadcast_to(_group_slice(v_row, g), (N_HEADS, HEAD_DIM)) for g in range(N_KV)])
    return s, v


def _sample_attend_kernel(sel_ref, pt_ref, q_ref, *refs, past, n_sel, ns_past):
    page_refs = refs[:4 * n_sel]
    (ksn_ref, vsn_ref, kw_ref, vw_ref, kwn_ref, vwn_ref, gate_ref, ocmp_ref, pnew_ref, vcn_ref,
     o_ref) = refs[4 * n_sel:]
    b = pl.program_id(0)
    page = page_refs[0].shape[-1]
    spp = page // SLC_BLOCK
    q8 = q_ref[0] * ATTN_SCALE
    q8_bf = q8.astype(BF16)
    slope = _alibi_slope_col(N_HEADS, 1, 0, N_HEADS)
    lane = lax.broadcasted_iota(I32, (N_HEADS, page), 1)
    s_t, v_t = _new_token_terms(q8, ksn_ref[0], vsn_ref[0])

    scores, masks, new_scores = [], [], []
    for n in range(n_sel):
        k_refs = page_refs[4 * n:4 * n + N_KV]
        blks = [sel_ref[(b * N_KV + g) * n_sel + n] for g in range(N_KV)]
        blk_rows = _merge_groups([jnp.full((N_HEADS, page), blk, I32) for blk in blks])
        blk_col = _merge_groups([jnp.full((N_HEADS, 1), blk, I32) for blk in blks])
        qk = _merge_groups([_dot(q8_bf, k_ref[0, 0].astype(BF16)) for k_ref in k_refs])
        page_pos = blk_rows // spp
        dist = (past - (page_pos * page + lane)).astype(F32)
        in_block = (lane // SLC_BLOCK) == (blk_rows - page_pos * spp)
        cached = jnp.logical_and(blk_rows >= 0, blk_rows < ns_past)
        mask = jnp.logical_and(jnp.logical_and(in_block, cached), dist >= 0.0)
        scores.append(jnp.where(mask, qk - slope * dist, NEG))
        masks.append(mask)
        new_scores.append(jnp.where(blk_col >= ns_past, s_t, NEG))
    m = new_scores[0]
    for s, sn in zip(scores, new_scores):
        m = jnp.maximum(m, jnp.maximum(jnp.max(s, axis=1, keepdims=True), sn))
    l_tot = jnp.zeros((N_HEADS, 1), F32)
    p_new = jnp.zeros((N_HEADS, 1), F32)
    acc = jnp.zeros((N_HEADS, HEAD_DIM), F32)
    for n in range(n_sel):
        v_refs = page_refs[4 * n + N_KV:4 * n + 2 * N_KV]
        p = jnp.where(masks[n], jnp.exp(scores[n] - m), 0.0)
        p_bf = p.astype(BF16)
        l_tot = l_tot + jnp.sum(p, axis=1, keepdims=True)
        p_new = p_new + jnp.where(new_scores[n] > 0.5 * NEG, jnp.exp(new_scores[n] - m), 0.0)
        acc = acc + _merge_groups([_dot_nt(p_bf, v_ref[0, 0].astype(BF16)) for v_ref in v_refs])
    o_slc = (acc + p_new * v_t) / jnp.maximum(l_tot + p_new, 1e-30)

    w_buf = kw_ref.shape[-1]
    wl = lax.broadcasted_iota(I32, (N_HEADS, w_buf), 1)
    win_pos = past - w_buf + wl
    dist_w = (past - win_pos).astype(F32)
    mask_w = jnp.logical_and(jnp.logical_and(dist_w >= 0.0, dist_w <= float(WINDOW)), win_pos >= 0)
    qk_w = _merge_groups([_dot(q8_bf, kw_ref[0, g].astype(BF16)) for g in range(N_KV)])
    s_w = jnp.where(mask_w, qk_w - slope * dist_w, NEG)
    s_t, v_t = _new_token_terms(q8, kwn_ref[0], vwn_ref[0])
    m_w = jnp.maximum(jnp.max(s_w, axis=1, keepdims=True), s_t)
    p_w = jnp.where(mask_w, jnp.exp(s_w - m_w), 0.0)
    p_t = jnp.exp(s_t - m_w)
    den = jnp.maximum(jnp.sum(p_w, axis=1, keepdims=True) + p_t, 1e-30)
    pw_bf = p_w.astype(BF16)
    o_w = _merge_groups([_dot_nt(pw_bf, vw_ref[0, g].astype(BF16)) for g in range(N_KV)])
    o_win = (o_w + p_t * v_t) / den
    n_new = vcn_ref.shape[1]
    vcn = jnp.concatenate([vcn_ref[0], jnp.zeros((LANES - n_new, KV_COLS), F32)], axis=0).astype(BF16)
    pn_bf = pnew_ref[0].astype(BF16)
    o_cmp = ocmp_ref[0] + _merge_groups([_dot(pn_bf, _group_slice(vcn, g)) for g in range(N_KV)])
    gt = gate_ref[0]
    o_ref[0] = o_cmp * gt[:, 0:1] + o_slc * gt[:, 1:2] + o_win * gt[:, 2:3]


def _sample_attend(sel, page_table, q8, k_pages, v_pages, ks_new, vs_new, kw_state, vw_state,
                   kw_new, vw_new, gates8, o_cmp, p_new, vc_new, past):
    b, n_pages = page_table.shape
    n_sel = sel.shape[-1]
    page = k_pages.shape[-1]
    ns_past = past // SLC_BLOCK
    spp = page // SLC_BLOCK
    w_buf = kw_state.shape[-1]
    n_new = vc_new.shape[1]

    def cache_map(n, g):
        def index(i, sel_ref, pt_ref):
            blk = jnp.clip(sel_ref[(i * N_KV + g) * n_sel + n], 0, ns_past - 1)
            return (pt_ref[i * n_pages + blk // spp], g, 0, 0)
        return pl.BlockSpec((1, 1, HEAD_DIM, page), index)

    page_specs, page_args = [], []
    for n in range(n_sel):
        for arr in (k_pages, v_pages):
            for g in range(N_KV):
                page_specs.append(cache_map(n, g))
                page_args.append(arr)
    per_b = lambda *s: pl.BlockSpec((1,) + s, lambda i, sl, pt: (i,) + (0,) * len(s))
    return pl.pallas_call(
        functools.partial(_sample_attend_kernel, past=past, n_sel=n_sel, ns_past=ns_past),
        grid_spec=pltpu.PrefetchScalarGridSpec(
            num_scalar_prefetch=2,
            grid=(b,),
            in_specs=[per_b(N_HEADS, HEAD_DIM)] + page_specs
                     + [per_b(1, KV_COLS), per_b(1, KV_COLS),
                        per_b(N_KV, HEAD_DIM, w_buf), per_b(N_KV, HEAD_DIM, w_buf),
                        per_b(1, KV_COLS), per_b(1, KV_COLS),
                        per_b(N_HEADS, 3), per_b(N_HEADS, HEAD_DIM), per_b(N_HEADS, LANES),
                        per_b(n_new, KV_COLS)],
            out_specs=per_b(N_HEADS, HEAD_DIM),
        ),
        out_shape=jax.ShapeDtypeStruct((b, N_HEADS, HEAD_DIM), F32),
        compiler_params=_cparams("arbitrary"),
        name="sample_attend",
    )(sel.reshape(-1), page_table.reshape(-1), q8, *page_args,
      ks_new, vs_new, kw_state, vw_state, kw_new, vw_new, gates8, o_cmp, p_new, vc_new)


def _layernorm_silu(y, g, b):
    mu = jnp.mean(y, axis=-1, keepdims=True)
    var = jnp.mean(jnp.square(y - mu), axis=-1, keepdims=True)
    return _silu((y - mu) * lax.rsqrt(var + EPS) * g + b)


def _conv_prompt_kernel(u_ref, w_ref, b_ref, g_ref, beta_ref, o_ref, buf):
    j = pl.program_id(1)
    tt = u_ref.shape[1]
    c = buf.shape[1]
    kw = w_ref.shape[0] // SUBLANES

    @pl.when(j == 0)
    def _():
        buf[0:CONV_HALO, :] = jnp.zeros((CONV_HALO, c), F32)

    buf[CONV_HALO:CONV_HALO + tt, :] = u_ref[0]
    first = CONV_HALO - (kw - 1)
    rows = CONV_ROWS
    for r0 in range(0, tt, rows):
        acc = jnp.zeros((rows // SUBLANES, SUBLANES, c), F32)
        for r in range(SUBLANES):
            taps = range(r, kw, SUBLANES)
            win = buf[pl.ds(first + r + r0, rows + SUBLANES * (len(taps) - 1)), :]
            for t, k in enumerate(taps):
                wk = w_ref[SUBLANES * k:SUBLANES * (k + 1), :]
                tap = win[SUBLANES * t:SUBLANES * t + rows].reshape(rows // SUBLANES, SUBLANES, c)
                acc = acc + wk[None] * tap
        y = acc.reshape(rows, c) + b_ref[...]
        o_ref[0, r0:r0 + rows, :] = _layernorm_silu(y, g_ref[...], beta_ref[...])
    buf[0:CONV_HALO, :] = buf[tt:tt + CONV_HALO, :]


def _conv_prompt(u, w_dw, b_dw, ln_g, ln_b, tt):
    b, t, c = u.shape
    vec = pl.BlockSpec((1, c), lambda i, j: (0, 0))
    w_rep = jnp.repeat(w_dw, SUBLANES, axis=0)
    return pl.pallas_call(
        _conv_prompt_kernel,
        grid=(b, t // tt),
        in_specs=[pl.BlockSpec((1, tt, c), lambda i, j: (i, j, 0)),
                  pl.BlockSpec(w_rep.shape, lambda i, j: (0, 0)), vec, vec, vec],
        out_specs=pl.BlockSpec((1, tt, c), lambda i, j: (i, j, 0)),
        out_shape=jax.ShapeDtypeStruct((b, t, c), F32),
        scratch_shapes=[pltpu.VMEM((CONV_HALO + tt, c), F32)],
        compiler_params=_cparams("arbitrary", "arbitrary"),
        name="conv_prompt",
    )(u, w_rep, b_dw, ln_g, ln_b)


def _conv_sample_kernel(up_ref, w_ref, b_ref, g_ref, beta_ref, o_ref):
    y = jnp.sum(up_ref[...] * w_ref[...][None, :, :], axis=1)
    o_ref[...] = _layernorm_silu(y + b_ref[...], g_ref[...], beta_ref[...])


def _conv_sample(up, w_dw, b_dw, ln_g, ln_b):
    b, kw, c = up.shape
    return pl.pallas_call(
        _conv_sample_kernel,
        out_shape=jax.ShapeDtypeStruct((b, c), F32),
        name="conv_sample",
    )(up, w_dw, b_dw, ln_g, ln_b)


def _merge_router_kernel(oa_ref, oc_ref, x_ref, gate_ref, shift_ref, scale_ref, goa_ref, goc_ref,
                         wout_ref, g2_ref, rwh_ref, rwl_ref, rb_ref, cnt_in_ref,
                         x1_ref, h2_ref, eidx_ref, wts_ref, rank_ref, cnt_out_ref, run):
    first = jnp.logical_and(pl.program_id(0) == 0, pl.program_id(1) == 0)

    @pl.when(first)
    def _():
        run[...] = cnt_in_ref[...]

    a = _rms(oa_ref[0], goa_ref[...])
    c = _rms(oc_ref[0], goc_ref[...])
    cat = jnp.concatenate([a, c], axis=1).astype(BF16)
    x1 = x_ref[0] + gate_ref[0] * _dot(cat, wout_ref[...])
    x1_ref[0] = x1
    h2 = _rms(x1, g2_ref[...]) * (1.0 + scale_ref[0]) + shift_ref[0]
    h2_ref[0] = h2

    hh, hl = _split2(h2)
    logits = _dot_nt(rwh_ref[...], hh) + (_dot_nt(rwl_ref[...], hh) + _dot_nt(rwh_ref[...], hl))
    aff = _sigmoid(logits)
    n_exp, tm = aff.shape
    row_f = lax.broadcasted_iota(I32, (n_exp, tm), 0).astype(F32)
    s = aff + rb_ref[...]
    experts, weights = [], []
    for _ in range(TOP_K):
        m = jnp.max(s, axis=0, keepdims=True)
        e = jnp.min(jnp.where(s == m, row_f, NO_INDEX), axis=0, keepdims=True)
        pick = row_f == e
        experts.append(e)
        weights.append(jnp.sum(jnp.where(pick, aff, 0.0), axis=0, keepdims=True))
        s = jnp.where(pick, NEG, s)
    total = weights[0]
    for w in weights[1:]:
        total = total + w

    hot = jnp.where(s == NEG, 1.0, 0.0)
    r_i = lax.broadcasted_iota(I32, (tm, tm), 0)
    c_i = lax.broadcasted_iota(I32, (tm, tm), 1)
    earlier = jnp.where(r_i < c_i, 1.0, 0.0).astype(BF16)
    before = _dot(hot.astype(BF16), earlier) + run[...]
    ranks = [jnp.sum(jnp.where(row_f == e, before, 0.0), axis=0, keepdims=True) for e in experts]
    eidx_ref[0] = jnp.concatenate(experts, axis=0).astype(I32)
    wts_ref[0] = jnp.concatenate([ROUTE_SCALE * w / total for w in weights], axis=0)
    rank_ref[0] = jnp.concatenate(ranks, axis=0).astype(I32)
    run[...] = run[...] + jnp.sum(hot, axis=1, keepdims=True)
    cnt_out_ref[...] = run[...]


def _merge_router(o_attn, o_conv, x, gate, shift, scale, goa, goc, wout_bf, g2, rw_hi, rw_lo, rb,
                  cnt_in, tm):
    b, t, d = x.shape
    n_exp = rw_hi.shape[0]
    row = lambda n: pl.BlockSpec((1, tm, n), lambda i, j: (i, j, 0))
    pick = pl.BlockSpec((1, TOP_K, tm), lambda i, j: (i, 0, j))
    const = lambda shape: pl.BlockSpec(shape, lambda i, j: (0,) * len(shape))
    sds = lambda n, dt: jax.ShapeDtypeStruct((b, t, n), dt)
    picks = lambda dt: jax.ShapeDtypeStruct((b, TOP_K, t), dt)
    return pl.pallas_call(
        _merge_router_kernel,
        grid=(b, t // tm),
        in_specs=[row(o_attn.shape[-1]), row(o_conv.shape[-1]), row(d),
                  _mod_spec(gate, tm, d), _mod_spec(shift, tm, d), _mod_spec(scale, tm, d),
                  const(goa.shape), const(goc.shape), const(wout_bf.shape), const(g2.shape),
                  const(rw_hi.shape), const(rw_lo.shape), const(rb.shape), const(cnt_in.shape)],
        out_specs=[row(d), row(d), pick, pick, pick, const((n_exp, 1))],
        out_shape=[sds(d, F32), sds(d, F32), picks(I32), picks(F32), picks(I32),
                   jax.ShapeDtypeStruct((n_exp, 1), F32)],
        scratch_shapes=[pltpu.VMEM((n_exp, 1), F32)],
        compiler_params=_cparams("arbitrary", "arbitrary"),
        name="merge_router",
    )(o_attn, o_conv, x, gate, shift, scale, goa, goc, wout_bf, g2, rw_hi, rw_lo, rb, cnt_in)


def _row_copy(src_hbm, dst, src_row, dst_row, sem, chunks):
    return pltpu.make_async_copy(src_hbm.at[pl.ds(src_row * chunks, chunks)],
                                 dst.at[pl.ds(dst_row * chunks, chunks)], sem)


def _slot(start_ref, e_ref, r_ref, idx):
    return start_ref[e_ref[0, 0, idx]] + r_ref[0, 0, idx]


def _dispatch_kernel(cnt_ref, end_ref, start_ref, e_ref, r_ref, h_ref, xs_hbm, zbuf, zsem, sem,
                     *, tokens, rows, chunks, n_blocks):
    j = pl.program_id(0)
    n_exp = cnt_ref.shape[0]
    blk_rows = rows * chunks

    def zero_block(blk):
        return pltpu.make_async_copy(zbuf, xs_hbm.at[pl.ds(blk * blk_rows, blk_rows)], zsem)

    @pl.when(j == 0)
    def _():
        zbuf[...] = jnp.zeros(zbuf.shape, F32)
        n_active = end_ref[n_exp - 1] // rows

        def zero_tail(e, issued):
            partial = cnt_ref[e] % rows != 0

            @pl.when(partial)
            def _():
                zero_block(end_ref[e] // rows - 1).start()

            return issued + partial.astype(I32)

        def zero_unused(blk, _):
            zero_block(blk).start()
            return 0

        def drain_zero(_, c):
            zero_block(0).wait()
            return c

        issued = lax.fori_loop(0, n_exp, zero_tail, 0)
        lax.fori_loop(n_active, n_blocks, zero_unused, 0)
        lax.fori_loop(0, issued + (n_blocks - n_active), drain_zero, 0)

    def issue(r, _):
        for k in range(TOP_K):
            _row_copy(h_ref, xs_hbm, r, _slot(start_ref, e_ref, r_ref, r * TOP_K + k), sem, chunks).start()
        return 0

    def drain(r, _):
        for k in range(TOP_K):
            _row_copy(h_ref, xs_hbm, 0, 0, sem, chunks).wait()
        return 0

    lax.fori_loop(0, tokens, issue, 0)
    lax.fori_loop(0, tokens, drain, 0)


def _dispatch(counts, pad_end, pad_start, e_idx, rank, h_rows, tokens, rows, chunks, n_blocks):
    n_tiles = e_idx.shape[0]
    picks = pl.BlockSpec((1, 1, tokens * TOP_K), lambda j, c, e, s: (j, 0, 0), memory_space=pltpu.SMEM)
    return pl.pallas_call(
        functools.partial(_dispatch_kernel, tokens=tokens, rows=rows, chunks=chunks, n_blocks=n_blocks),
        grid_spec=pltpu.PrefetchScalarGridSpec(
            num_scalar_prefetch=3,
            grid=(n_tiles,),
            in_specs=[picks, picks,
                      pl.BlockSpec((tokens * chunks, LANES), lambda j, c, e, s: (j, 0))],
            out_specs=pl.BlockSpec(memory_space=pl.ANY),
            scratch_shapes=[pltpu.VMEM((rows * chunks, LANES), F32),
                            pltpu.SemaphoreType.DMA(()), pltpu.SemaphoreType.DMA(())],
        ),
        out_shape=jax.ShapeDtypeStruct((n_blocks * rows * chunks, LANES), F32),
        compiler_params=_cparams("arbitrary"),
        name="moe_dispatch",
    )(counts, pad_end, pad_start, e_idx, rank, h_rows)


def _expert_kernel(be_ref, nact_ref, first_ref, slot_ref, next_ref, x_ref, wg_hbm, wu_hbm, wd_hbm, y_ref,
                   wg_buf, wu_buf, wd_buf, sems, *, rows, chunks):
    j = pl.program_id(0)

    def weight_copies(expert, slot):
        return [pltpu.make_async_copy(w_hbm.at[expert], buf.at[slot], sems.at[slot, i])
                for i, (w_hbm, buf) in enumerate(((wg_hbm, wg_buf), (wu_hbm, wu_buf), (wd_hbm, wd_buf)))]

    @pl.when(j < nact_ref[0])
    def _():
        slot = slot_ref[j]

        @pl.when(j == 0)
        def _():
            for cp in weight_copies(be_ref[0], 0):
                cp.start()

        @pl.when(first_ref[j] == 1)
        def _():
            for cp in weight_copies(be_ref[j], slot):
                cp.wait()

            @pl.when(next_ref[j] >= 0)
            def _():
                for cp in weight_copies(next_ref[j], 1 - slot):
                    cp.start()

        f = wg_buf.shape[2]
        gate = jnp.zeros((rows, f), F32)
        up = jnp.zeros((rows, f), F32)
        for c in range(0, chunks, 2):
            xc = jnp.concatenate([x_ref[pl.ds(c, rows, stride=chunks), :],
                                  x_ref[pl.ds(c + 1, rows, stride=chunks), :]], axis=1).astype(BF16)
            cs = pl.ds(c * LANES, 2 * LANES)
            gate = gate + _dot(xc, wg_buf[slot, cs, :].astype(BF16))
            up = up + _dot(xc, wu_buf[slot, cs, :].astype(BF16))
        h = (_silu(gate) * up).astype(BF16)
        y = _dot(h, wd_buf[slot].astype(BF16))
        for c in range(chunks):
            y_ref[pl.ds(c, rows, stride=chunks), :] = y[:, c * LANES:(c + 1) * LANES]

    @pl.when(j >= nact_ref[0])
    def _():
        y_ref[...] = jnp.zeros(y_ref.shape, F32)


def _experts(blk_expert, n_active, pad_end, xs, wg, wu, wd, rows, chunks):
    n_blocks = blk_expert.shape[0]
    n_exp, d, f = wg.shape
    first = jnp.concatenate([jnp.ones((1,), I32), (blk_expert[1:] != blk_expert[:-1]).astype(I32)])
    slot = (jnp.cumsum(first) - 1) % 2
    run_end = pad_end[blk_expert] // rows
    nxt = jnp.where(run_end < n_active[0], blk_expert[jnp.minimum(run_end, n_blocks - 1)], -1)
    last = lambda j, na: jnp.minimum(j, na[0] - 1)
    hbm = pl.BlockSpec(memory_space=pl.ANY)
    return pl.pallas_call(
        functools.partial(_expert_kernel, rows=rows, chunks=chunks),
        grid_spec=pltpu.PrefetchScalarGridSpec(
            num_scalar_prefetch=5,
            grid=(n_blocks,),
            in_specs=[pl.BlockSpec((rows * chunks, LANES), lambda j, be, na, *_: (last(j, na), 0)),
                      hbm, hbm, hbm],
            out_specs=pl.BlockSpec((rows * chunks, LANES), lambda j, *_: (j, 0)),
            scratch_shapes=[pltpu.VMEM((2, d, f), F32), pltpu.VMEM((2, d, f), F32),
                            pltpu.VMEM((2, f, d), F32), pltpu.SemaphoreType.DMA((2, 3))],
        ),
        out_shape=jax.ShapeDtypeStruct(xs.shape, F32),
        compiler_params=_cparams("arbitrary"),
        name="moe_experts",
    )(blk_expert, n_active, first, slot.astype(I32), nxt.astype(I32), xs, wg, wu, wd)


def _combine_kernel(start_ref, e_ref, r_ref, w_ref, x1_ref, h2_ref, gate_ref, wsg_ref, wsu_ref, wsd_ref,
                    gf_ref, ys_hbm, o_ref, buf, sem, *, chunks):
    tm = x1_ref.shape[1]

    def issue(r, _):
        for k in range(TOP_K):
            _row_copy(ys_hbm, buf.at[k], _slot(start_ref, e_ref, r_ref, r * TOP_K + k), r, sem, chunks).start()
        return 0

    def drain(r, _):
        for k in range(TOP_K):
            _row_copy(ys_hbm, buf.at[k], 0, r, sem, chunks).wait()
        return 0

    lax.fori_loop(0, tm, issue, 0)
    h_bf = h2_ref[0].astype(BF16)
    hid = (_silu(_dot(h_bf, wsg_ref[...])) * _dot(h_bf, wsu_ref[...])).astype(BF16)
    shared = _dot(hid, wsd_ref[...])
    lax.fori_loop(0, tm, drain, 0)

    w = w_ref[0]
    cols = []
    for c in range(chunks):
        tot = jnp.zeros((tm, LANES), F32)
        for k in range(TOP_K):
            tot = tot + buf[k, pl.ds(c, tm, stride=chunks), :] * w[:, k:k + 1]
        cols.append(tot)
    routed = jnp.concatenate(cols, axis=1)
    x2 = x1_ref[0] + gate_ref[0] * (routed + shared)
    o_ref[0] = _rms(x2, gf_ref[...])


def _combine(pad_start, e_idx, rank, wts, x1, h2, gate, wsg_bf, wsu_bf, wsd_bf, gf, ys, tm, chunks):
    b, t, d = x1.shape
    nt = t // tm
    row = lambda n: pl.BlockSpec((1, tm, n), lambda i, j, *_: (i, j, 0))
    const = lambda shape: pl.BlockSpec(shape, lambda i, j, *_: (0,) * len(shape))
    picks = pl.BlockSpec((1, 1, tm * TOP_K), lambda i, j, *_: (i * nt + j, 0, 0), memory_space=pltpu.SMEM)
    return pl.pallas_call(
        functools.partial(_combine_kernel, chunks=chunks),
        grid_spec=pltpu.PrefetchScalarGridSpec(
            num_scalar_prefetch=1,
            grid=(b, nt),
            in_specs=[picks, picks, row(TOP_K), row(d), row(d), _mod_spec(gate, tm, d),
                      const(wsg_bf.shape), const(wsu_bf.shape), const(wsd_bf.shape), const(gf.shape),
                      pl.BlockSpec(memory_space=pl.ANY)],
            out_specs=row(d),
            scratch_shapes=[pltpu.VMEM((TOP_K, tm * chunks, LANES), F32), pltpu.SemaphoreType.DMA(())],
        ),
        out_shape=jax.ShapeDtypeStruct((b, t, d), F32),
        compiler_params=_cparams("arbitrary", "arbitrary"),
        name="moe_combine",
    )(pad_start, e_idx, rank, wts, x1, h2, gate, wsg_bf, wsu_bf, wsd_bf, gf, ys)


def _split_mod(mod, per_token):
    parts = jnp.split(mod, 6, axis=-1)
    if per_token:
        return [p[None] for p in parts]
    return [p[:, None, :] for p in parts]


def _padded_in_weight(w_in, conv_width):
    n_gate = 3 * N_HEADS
    o = ATTN_WIDTH + 6 * KV_COLS
    main = w_in[:, :o]
    gates = jnp.pad(w_in[:, o:o + n_gate], ((0, 0), (0, LANES - n_gate)))
    glu = w_in[:, o + n_gate:o + n_gate + 2 * conv_width]
    return jnp.concatenate([main, gates, glu], axis=1).astype(BF16)


def _cmp_rows(x):
    return x.reshape(x.shape[:-2] + (x.shape[-2] // CMP_BLOCK, CMP_BLOCK * KV_COLS))


def _largest_tile(n, cap):
    best = [k for k in range(SUBLANES, cap + 1, SUBLANES) if n % k == 0]
    assert best, (n, cap)
    return best[-1]


def _kv5(x):
    return x.reshape(x.shape[:-1] + (N_KV, HEAD_DIM))[None]


def kernel(x_prompt, x_sample, cache_k_cmp, cache_v_cmp, cache_k_slc, cache_v_slc, state_k_win, state_v_win, state_conv, page_table, c_prompt, c_sample, norm1_g, norm2_g, w_ada, b_ada, w_in, w_cmp_k, w_cmp_v, w_dw, b_dw, ln_conv_g, ln_conv_b, g_out_attn, g_out_conv, w_out, router_w, router_b, w_exp_gate, w_exp_up, w_exp_down, w_sh_gate, w_sh_up, w_sh_down, norm_f_g):
    assert w_ada.shape[0] == 1, "single layer"
    bp, t, d = x_prompt.shape
    bs, s_new, _ = x_sample.shape
    assert s_new == 1
    n_pool, page = cache_k_cmp.shape[1], cache_k_cmp.shape[2]
    n_pages = page_table.shape[1]
    past = n_pages * page
    conv_width = state_conv.shape[-1]
    n_exp = router_w.shape[-1]
    chunks = d // LANES
    tm = min(ROW_TILE, t)

    w_in_bf = _padded_in_weight(w_in[0], conv_width)
    wck = _compress_weight(w_cmp_k[0])
    wcv = _compress_weight(w_cmp_v[0])
    wout_bf = w_out[0].astype(BF16)
    rw_t = router_w[0].T
    rw_hi = rw_t.astype(BF16)
    rw_lo = (rw_t - rw_hi.astype(F32)).astype(BF16)
    wsg_bf, wsu_bf, wsd_bf = (w[0].astype(BF16) for w in (w_sh_gate, w_sh_up, w_sh_down))
    gf = norm_f_g[None, :]

    n_c = bp + bs
    c_all = jnp.concatenate([c_prompt, c_sample], axis=0)
    c_all = jnp.pad(c_all, ((0, (-n_c) % SUBLANES), (0, 0)))
    mod = _modulation(c_all, w_ada[0], b_ada)
    mp = _split_mod(mod[:bp], per_token=False)
    ms = _split_mod(mod[bp:n_c], per_token=True)

    (q_p, kc_p, vc_p, ks_p, vs_p, kw_p, vw_p, gate_p, u_p) = _in_proj(
        x_prompt, mp[0], mp[1], norm1_g, w_in_bf, tm)
    nc_p = t // CMP_BLOCK
    kcc, vcc = _compress(_cmp_rows(kc_p).reshape(bp * nc_p, -1), _cmp_rows(vc_p).reshape(bp * nc_p, -1),
                         wck, wcv)
    o_attn_p = _prompt_attention(q_p, gate_p, kcc.reshape(bp, nc_p, KV_COLS), vcc.reshape(bp, nc_p, KV_COLS),
                                 ks_p, vs_p, kw_p, vw_p)
    o_conv_p = _conv_prompt(u_p, w_dw[0], b_dw, ln_conv_g, ln_conv_b, tm)

    xs_row = x_sample.reshape(1, bs, d)
    (q_s, kc_s, vc_s, ks_s, vs_s, kw_s, vw_s, gate_s, u_s) = _in_proj(
        xs_row, ms[0], ms[1], norm1_g, w_in_bf, bs)
    q8 = q_s.reshape(bs, N_HEADS, HEAD_DIM)
    gates8 = gate_s[0, :, :3 * N_HEADS].reshape(bs, N_HEADS, 3)
    pages_t = lambda c: jnp.transpose(c[0], (0, 2, 3, 1))
    tail = (-(past + s_new)) % SLC_BLOCK
    n_new = (s_new + tail) // CMP_BLOCK
    tail_rows = lambda x: _cmp_rows(jnp.pad(x[0][:, None, :], ((0, 0), (0, tail), (0, 0)))).reshape(bs * n_new, -1)
    pad_rows = (-(bs * n_new)) % SUBLANES
    kc_new, vc_new = _compress(jnp.pad(tail_rows(kc_s), ((0, pad_rows), (0, 0))),
                               jnp.pad(tail_rows(vc_s), ((0, pad_rows), (0, 0))), wck, wcv)
    kc_new = kc_new[:bs * n_new].reshape(bs, n_new, KV_COLS)
    vc_new = vc_new[:bs * n_new].reshape(bs, n_new, KV_COLS)
    reps = page // CMP_BLOCK
    wk_fold = jnp.transpose(w_cmp_k[0], (2, 1, 0)).reshape(HEAD_DIM, HEAD_DIM * CMP_BLOCK)
    ut = _matmul3(q8.reshape(bs * N_HEADS, HEAD_DIM) * ATTN_SCALE, wk_fold)
    ut = jnp.tile(ut.reshape(bs, N_HEADS, HEAD_DIM, CMP_BLOCK), (1, 1, 1, reps))
    s_raw = _sample_scores(page_table, ut, pages_t(cache_k_cmp))
    p_exp, p_new, sel = _sample_select(s_raw, q8, kc_new, past)
    y_acc = _sample_values(page_table, p_exp, pages_t(cache_v_cmp))
    wv_fold = jnp.tile(jnp.transpose(w_cmp_v[0], (1, 0, 2)), (1, reps, 1)).reshape(HEAD_DIM * page, HEAD_DIM)
    o_cmp_s = _matmul3(y_acc.reshape(bs * N_HEADS, HEAD_DIM * page), wv_fold).reshape(bs, N_HEADS, HEAD_DIM)
    n_sel = min(N_SEL, (past // CMP_BLOCK + n_new) // CMP_PER_SLC)
    sel = sel[:, :N_KV, :n_sel]
    row3 = lambda x: x[0][:, None, :]
    o_attn_s = _sample_attend(
        sel, page_table, q8, pages_t(cache_k_slc), pages_t(cache_v_slc), row3(ks_s), row3(vs_s),
        pages_t(state_k_win), pages_t(state_v_win), row3(kw_s), row3(vw_s), gates8, o_cmp_s, p_new, vc_new,
        past)
    o_attn_s = o_attn_s.reshape(1, bs, ATTN_WIDTH)
    up_s = jnp.concatenate([state_conv[0], u_s[0][:, None, :]], axis=1)
    o_conv_s = _conv_sample(up_s, w_dw[0], b_dw, ln_conv_g, ln_conv_b)[None]

    router = functools.partial(_merge_router, goa=g_out_attn, goc=g_out_conv, wout_bf=wout_bf, g2=norm2_g,
                               rw_hi=rw_hi, rw_lo=rw_lo, rb=router_b[0][:, None])
    x1_p, h2_p, e_p, w_p, r_p, cnt = router(o_attn_p, o_conv_p, x_prompt, mp[2], mp[3], mp[4],
                                            cnt_in=jnp.zeros((n_exp, 1), F32), tm=tm)
    x1_s, h2_s, e_s, w_s, r_s, cnt = router(o_attn_s, o_conv_s, xs_row, ms[2], ms[3], ms[4],
                                            cnt_in=cnt, tm=bs)

    n_tok = bp * t + bs
    counts = cnt[:, 0].astype(I32)
    padded = (counts + MOE_ROWS - 1) // MOE_ROWS * MOE_ROWS
    pad_end = jnp.cumsum(padded)
    pad_end = pad_end.astype(I32)
    pad_start = pad_end - padded
    n_blocks = -(-(n_tok * TOP_K) // MOE_ROWS) + n_exp
    blk_first = jnp.arange(n_blocks, dtype=I32) * MOE_ROWS
    blk_expert = jnp.minimum(jnp.sum(pad_end[None, :] <= blk_first[:, None], axis=1), n_exp - 1).astype(I32)
    n_active = pad_end[-1:] // MOE_ROWS
    picks = lambda a: jnp.transpose(a, (0, 2, 1))
    w_p, w_s = picks(w_p), picks(w_s)
    e_all = jnp.concatenate([picks(e_p).reshape(-1, TOP_K), picks(e_s).reshape(-1, TOP_K)], axis=0)
    r_all = jnp.concatenate([picks(r_p).reshape(-1, TOP_K), picks(r_s).reshape(-1, TOP_K)], axis=0)

    tile = _largest_tile(n_tok, 512)
    h_rows = jnp.concatenate([h2_p.reshape(-1, d), h2_s.reshape(-1, d)], axis=0).reshape(n_tok * chunks, LANES)
    xs = _dispatch(counts, pad_end, pad_start, e_all.reshape(n_tok // tile, 1, tile * TOP_K),
                   r_all.reshape(n_tok // tile, 1, tile * TOP_K), h_rows, tile, MOE_ROWS, chunks, n_blocks)
    ys = _experts(blk_expert, n_active, pad_end, xs, w_exp_gate[0], w_exp_up[0], w_exp_down[0], MOE_ROWS, chunks)

    comb = functools.partial(_combine, pad_start, wsg_bf=wsg_bf, wsu_bf=wsu_bf, wsd_bf=wsd_bf, gf=gf, ys=ys,
                             chunks=chunks)
    tiles_p = (bp * (t // tm), 1, tm * TOP_K)
    y_prompt = comb(picks(e_p).reshape(tiles_p), picks(r_p).reshape(tiles_p), w_p, x1_p, h2_p, mp[5], tm=tm)
    y_sample = comb(picks(e_s).reshape(1, 1, bs * TOP_K), picks(r_s).reshape(1, 1, bs * TOP_K), w_s, x1_s, h2_s,
                    ms[5], tm=bs).reshape(bs, 1, d)

    win = min(WINDOW, t)
    hist = state_conv.shape[2]
    out_p = [_kv5(a) for a in (kc_p, vc_p, ks_p, vs_p, kw_p[:, t - win:], vw_p[:, t - win:])]
    conv_p = u_p[:, t - hist:][None]
    out_s = [_kv5(a[0][:, None, :]) for a in (kc_s, vc_s, ks_s, vs_s)]
    w_buf = state_k_win.shape[2]
    kw_buf = jnp.concatenate([state_k_win, _kv5(kw_s[0][:, None, :])], axis=2)[:, :, -w_buf:]
    vw_buf = jnp.concatenate([state_v_win, _kv5(vw_s[0][:, None, :])], axis=2)[:, :, -w_buf:]
    conv_s = up_s[:, -hist:][None]
    return (y_prompt, y_sample, *out_p, conv_p, *out_s, kw_buf, vw_buf, conv_s)
```

```python
import functools

import jax
import jax.numpy as jnp
from jax import lax
from jax.experimental import pallas as pl
from jax.experimental.pallas import tpu as pltpu

F32 = jnp.float32
BF16 = jnp.bfloat16
I32 = jnp.int32

N_HEADS = 8
HEAD_DIM = 64
N_KV = 2
Q_PER_KV = N_HEADS // N_KV
ATTN_WIDTH = N_HEADS * HEAD_DIM
KV_COLS = N_KV * HEAD_DIM
CMP_BLOCK = 32
SLC_BLOCK = 64
CMP_PER_SLC = SLC_BLOCK // CMP_BLOCK
N_SEL = 16
WINDOW = 512
TOP_K = 8
ROUTE_SCALE = 2.5
EPS = 1e-6
FORCED = 1e4
NEG = -1e30
ATTN_SCALE = HEAD_DIM ** -0.5
PICKED = -2.0
NOT_A_BLOCK = -4.0
NO_INDEX = 1e9

LANES = 128
SUBLANES = 8
VMEM_LIMIT = 56 * 1024 * 1024
DMA_PRIORITIES = 2

ROW_TILE = 256
Q_TILE = 128
KEY_TILE = 1024
CMP_ROW_TILE = 512
MOE_ROWS = 256
CONV_HALO = 32
POS_RADIX = 256
CONV_ROWS = 32
PAGES_PER_STEP = 32


def _cparams(*sem):
    return pltpu.CompilerParams(dimension_semantics=sem, vmem_limit_bytes=VMEM_LIMIT)


def _dot(a, b):
    return jnp.dot(a, b, preferred_element_type=F32)


def _dot_nt(a, b):
    return lax.dot_general(a, b, (((1,), (1,)), ((), ())), preferred_element_type=F32)


def _dot_tn(a, b):
    return lax.dot_general(a, b, (((0,), (0,)), ((), ())), preferred_element_type=F32)


def _split2(x):
    hi = x.astype(BF16)
    lo = (x - hi.astype(F32)).astype(BF16)
    return hi, lo


def _dot3(a, b):
    ah, al = _split2(a)
    bh, bl = _split2(b)
    return _dot(ah, bh) + (_dot(ah, bl) + _dot(al, bh))


def _dot3_nt(a, b):
    ah, al = _split2(a)
    bh, bl = _split2(b)
    return _dot_nt(ah, bh) + (_dot_nt(ah, bl) + _dot_nt(al, bh))


def _sigmoid(x):
    return 1.0 / (1.0 + jnp.exp(-x))


def _silu(x):
    return x * _sigmoid(x)


def _rms(x, g):
    return x * lax.rsqrt(jnp.mean(x * x, axis=-1, keepdims=True) + EPS) * g


def _alibi_slope_col(rows, rows_per_head, first_head, n_heads):
    r = lax.broadcasted_iota(I32, (rows, 1), 0) // rows_per_head
    out = jnp.zeros((rows, 1), F32)
    for k in range(n_heads):
        out = jnp.where(r == k, 2.0 ** (-8.0 * (first_head + k + 1) / N_HEADS), out)
    return out


def _modulation_kernel(c_ref, w_ref, b_ref, o_ref):
    o_ref[...] = _dot3(c_ref[...], w_ref[...]) + b_ref[...]


def _modulation(c, w, b):
    m, d = c.shape
    n = w.shape[1]
    tn = 768
    return pl.pallas_call(
        _modulation_kernel,
        grid=(n // tn,),
        in_specs=[pl.BlockSpec((m, d), lambda j: (0, 0)),
                  pl.BlockSpec((d, tn), lambda j: (0, j)),
                  pl.BlockSpec((1, tn), lambda j: (0, j))],
        out_specs=pl.BlockSpec((m, tn), lambda j: (0, j)),
        out_shape=jax.ShapeDtypeStruct((m, n), F32),
        compiler_params=_cparams("arbitrary"),
        name="modulation",
    )(c, w, b)


def _mod_spec(mod, tm, d):
    if mod.shape[1] == 1:
        return pl.BlockSpec((1, 1, d), lambda i, j, *_: (i, 0, 0))
    return pl.BlockSpec((1, tm, d), lambda i, j, *_: (i, j, 0))


def _in_proj_kernel(x_ref, shift_ref, scale_ref, g_ref, w_ref,
                    q_ref, kc_ref, vc_ref, ks_ref, vs_ref, kw_ref, vw_ref, gate_ref, u_ref):
    x = x_ref[0]
    h = _rms(x, g_ref[...]) * (1.0 + scale_ref[0]) + shift_ref[0]
    z = _dot(h.astype(BF16), w_ref[...])
    q_ref[0] = z[:, :ATTN_WIDTH]
    o = ATTN_WIDTH
    for ref in (kc_ref, vc_ref, ks_ref, vs_ref, kw_ref, vw_ref):
        ref[0] = z[:, o:o + KV_COLS]
        o += KV_COLS
    gate_ref[0] = _sigmoid(z[:, o:o + LANES])
    o += LANES
    cw = u_ref.shape[-1]
    u_ref[0] = z[:, o:o + cw] * _sigmoid(z[:, o + cw:o + 2 * cw])


def _in_proj(x, shift, scale, g, w_bf, tm):
    b, t, d = x.shape
    cw = (w_bf.shape[1] - ATTN_WIDTH - 6 * KV_COLS - LANES) // 2
    row = lambda n: pl.BlockSpec((1, tm, n), lambda i, j: (i, j, 0))
    sds = lambda n: jax.ShapeDtypeStruct((b, t, n), F32)
    return pl.pallas_call(
        _in_proj_kernel,
        grid=(b, t // tm),
        in_specs=[row(d), _mod_spec(shift, tm, d), _mod_spec(scale, tm, d),
                  pl.BlockSpec((1, d), lambda i, j: (0, 0)),
                  pl.BlockSpec(w_bf.shape, lambda i, j: (0, 0))],
        out_specs=[row(ATTN_WIDTH)] + [row(KV_COLS)] * 6 + [row(LANES), row(cw)],
        out_shape=[sds(ATTN_WIDTH)] + [sds(KV_COLS)] * 6 + [sds(LANES), sds(cw)],
        compiler_params=_cparams("arbitrary", "arbitrary"),
        name="in_proj",
    )(x, shift, scale, g, w_bf)


def _compress_kernel(k_ref, v_ref, wk_ref, wv_ref, ko_ref, vo_ref):
    ko_ref[...] = _dot3(k_ref[...], wk_ref[...])
    vo_ref[...] = _dot3(v_ref[...], wv_ref[...])


def _compress(k_rows, v_rows, wk, wv):
    r, kdim = k_rows.shape
    tr = min(CMP_ROW_TILE, r)
    assert r % tr == 0
    rows = pl.BlockSpec((tr, kdim), lambda i: (i, 0))
    wspec = pl.BlockSpec((kdim, KV_COLS), lambda i: (0, 0))
    ospec = pl.BlockSpec((tr, KV_COLS), lambda i: (i, 0))
    return pl.pallas_call(
        _compress_kernel,
        grid=(r // tr,),
        in_specs=[rows, rows, wspec, wspec],
        out_specs=[ospec, ospec],
        out_shape=[jax.ShapeDtypeStruct((r, KV_COLS), F32)] * 2,
        compiler_params=_cparams("arbitrary"),
        name="compress",
    )(k_rows, v_rows, wk, wv)


def _compress_weight(w):
    eye = jnp.eye(N_KV, dtype=w.dtype)
    big = jnp.einsum('lde,gh->lgdhe', w, eye)
    return big.reshape(CMP_BLOCK * KV_COLS, KV_COLS)


def _pair_sum(x, axis):
    n = x.shape[axis]
    idx = lax.broadcasted_iota(I32, x.shape, axis)
    nxt = pltpu.roll(x, n - 1, axis)
    prv = pltpu.roll(x, 1, axis)
    return x + jnp.where((idx & 1) == 0, nxt, prv)


def _block_scores(imp, blk, q_pos, n_blocks_total):
    cur = q_pos // SLC_BLOCK
    valid = jnp.logical_and(blk * SLC_BLOCK <= q_pos, blk < n_blocks_total)
    forced = jnp.logical_or(blk == 0, jnp.logical_or(blk == cur, blk == cur - 1))
    return jnp.where(valid, jnp.where(forced, FORCED, imp), -1.0)


def _select_blocks(score, blk, n_sel):
    blk_f = blk.astype(F32)
    s = score
    for _ in range(n_sel):
        m = jnp.max(s, axis=0, keepdims=True)
        first = jnp.min(jnp.where(s == m, blk_f, NO_INDEX), axis=0, keepdims=True)
        s = jnp.where(blk_f == first, PICKED, s)
    return jnp.where(jnp.logical_and(s == PICKED, score >= 0.0), 1.0, 0.0)


def _position_features(n):
    pos = jnp.arange(n, dtype=I32)[:, None]
    lane = jnp.arange(HEAD_DIM, dtype=I32)[None, :]
    feat = jnp.where(lane < 2, 1, jnp.where(lane == 2, pos // POS_RADIX, jnp.where(lane == 3, pos % POS_RADIX, 0)))
    return feat.astype(BF16)


def _query_position_features(q_pos, slope):
    lane = lax.broadcasted_iota(I32, (q_pos.shape[0], HEAD_DIM), 1)
    hi = (q_pos // POS_RADIX).astype(F32) * (-slope * POS_RADIX)
    lo = (q_pos % POS_RADIX).astype(F32) * (-slope)
    return jnp.where(lane == 0, hi, jnp.where(lane == 1, lo, jnp.where(
        lane == 2, slope * POS_RADIX, jnp.where(lane == 3, slope, 0.0))))


def _prompt_attn_kernel(q_ref, gate_ref, kc_ref, vc_ref, ks_ref, vs_ref, kw_ref, vw_ref, kx_ref, o_ref,
                        *, seq, n_sel):
    i = pl.program_id(1)
    tq = Q_TILE
    nc = kc_ref.shape[1]
    q_blk = q_ref[0] * ATTN_SCALE
    gates = gate_ref[0]
    slopes = [2.0 ** (-8.0 * (h + 1) / N_HEADS) for h in range(N_HEADS)]
    group_heads = [list(range(g * Q_PER_KV, (g + 1) * Q_PER_KV)) for g in range(N_KV)]
    gsl = [slice(g * HEAD_DIM, (g + 1) * HEAD_DIM) for g in range(N_KV)]
    rsl = [slice(r * tq, (r + 1) * tq) for r in range(Q_PER_KV)]
    q_pos_col = i * tq + lax.broadcasted_iota(I32, (tq, 1), 0)
    q_pos_row = i * tq + lax.broadcasted_iota(I32, (1, tq), 1)
    q_heads = [q_blk[:, h * HEAD_DIM:(h + 1) * HEAD_DIM] for h in range(N_HEADS)]
    qg = [jnp.concatenate([q_heads[h] for h in hs], axis=0) for hs in group_heads]
    qx_bf = [jnp.concatenate([jnp.concatenate([q_heads[h], _query_position_features(q_pos_col, slopes[h])],
                                              axis=1) for h in hs], axis=0).astype(BF16)
             for hs in group_heads]

    cmp_row = lax.broadcasted_iota(I32, (nc, tq), 0)
    dist_c = (q_pos_row - (cmp_row * CMP_BLOCK + (CMP_BLOCK - 1))).astype(F32)
    mask_c = dist_c >= 0.0
    blk = cmp_row >> 1
    o_cmp = [None] * N_HEADS
    sel_bf = []
    for g, hs in enumerate(group_heads):
        vc_bf = vc_ref[0][:, gsl[g]].astype(BF16)
        qk = _dot3_nt(kc_ref[0][:, gsl[g]], qg[g])
        imp = jnp.zeros((nc, tq), F32)
        for r, h in enumerate(hs):
            s = jnp.where(mask_c, qk[:, rsl[r]] - slopes[h] * dist_c, NEG)
            m = jnp.max(s, axis=0, keepdims=True)
            p = jnp.where(mask_c, jnp.exp(s - m), 0.0)
            p = p / jnp.maximum(jnp.sum(p, axis=0, keepdims=True), 1e-30)
            o_cmp[h] = _dot_tn(p.astype(BF16), vc_bf)
            imp = imp + p
        score = _block_scores(_pair_sum(imp, 0), blk, q_pos_row, seq // SLC_BLOCK)
        sel_bf.append(_select_blocks(score, blk, n_sel).astype(BF16))

    span = WINDOW + tq
    w_start = pl.multiple_of(jnp.maximum(i * tq - WINDOW, 0), tq)
    dist_w = (q_pos_col - (w_start + lax.broadcasted_iota(I32, (tq, span), 1))).astype(F32)
    bias_w = jnp.where(jnp.logical_and(dist_w >= 0.0, dist_w <= float(WINDOW)), 0.0, NEG)
    o_win = [None] * N_HEADS
    kx_w = kx_ref[pl.ds(w_start, span), :]
    for g, hs in enumerate(group_heads):
        kw_bf = kw_ref[0, pl.ds(w_start, span), :][:, gsl[g]].astype(BF16)
        vw_bf = vw_ref[0, pl.ds(w_start, span), :][:, gsl[g]].astype(BF16)
        s_all = _dot_nt(qx_bf[g], jnp.concatenate([kw_bf, kx_w], axis=1))
        probs, sums = [], []
        for r, h in enumerate(hs):
            s = s_all[rsl[r]] + bias_w
            p = jnp.exp(s - jnp.max(s, axis=1, keepdims=True))
            sums.append(jnp.sum(p, axis=1, keepdims=True))
            probs.append(p.astype(BF16))
        o_all = _dot(jnp.concatenate(probs, axis=0), vw_bf)
        for r, h in enumerate(hs):
            o_win[h] = o_all[rsl[r]] / sums[r]

    n_tiles = ((i + 1) * tq + KEY_TILE - 1) // KEY_TILE

    def slc_step(t, carry):
        ms, ls, accs = (list(c) for c in carry)
        k0 = pl.multiple_of(t * KEY_TILE, KEY_TILE)
        dist = (q_pos_col - (k0 + lax.broadcasted_iota(I32, (tq, KEY_TILE), 1))).astype(F32)
        causal = dist >= 0.0
        key_cmp = (k0 + lax.broadcasted_iota(I32, (nc, KEY_TILE), 1)) // CMP_BLOCK
        expand = jnp.where(key_cmp == lax.broadcasted_iota(I32, (nc, KEY_TILE), 0), 1.0, 0.0).astype(BF16)
        kx_t = kx_ref[pl.ds(k0, KEY_TILE), :]
        for g, hs in enumerate(group_heads):
            kt_bf = ks_ref[0, pl.ds(k0, KEY_TILE), :][:, gsl[g]].astype(BF16)
            vt_bf = vs_ref[0, pl.ds(k0, KEY_TILE), :][:, gsl[g]].astype(BF16)
            chosen = _dot_tn(sel_bf[g], expand)
            bias = jnp.where(jnp.logical_and(causal, chosen > 0.5), 0.0, NEG)
            s_all = _dot_nt(qx_bf[g], jnp.concatenate([kt_bf, kx_t], axis=1))
            probs, alphas = [], []
            for r, h in enumerate(hs):
                s = s_all[rsl[r]] + bias
                m_new = jnp.maximum(ms[h], jnp.max(s, axis=1, keepdims=True))
                alpha = jnp.exp(ms[h] - m_new)
                p = jnp.exp(s - m_new)
                ls[h] = alpha * ls[h] + jnp.sum(p, axis=1, keepdims=True)
                ms[h] = m_new
                alphas.append(alpha)
                probs.append(p.astype(BF16))
            pv = _dot(jnp.concatenate(probs, axis=0), vt_bf)
            for r, h in enumerate(hs):
                accs[h] = alphas[r] * accs[h] + pv[rsl[r]]
        return tuple(ms), tuple(ls), tuple(accs)

    init = (tuple(jnp.full((tq, 1), NEG, F32) for _ in range(N_HEADS)),
            tuple(jnp.zeros((tq, 1), F32) for _ in range(N_HEADS)),
            tuple(jnp.zeros((tq, HEAD_DIM), F32) for _ in range(N_HEADS)))
    _, l_s, acc_s = lax.fori_loop(0, n_tiles, slc_step, init)

    pieces = []
    for h in range(N_HEADS):
        o_slc = acc_s[h] / jnp.maximum(l_s[h], 1e-30)
        pieces.append(o_cmp[h] * gates[:, 3 * h + 0:3 * h + 1] + o_slc * gates[:, 3 * h + 1:3 * h + 2]
                      + o_win[h] * gates[:, 3 * h + 2:3 * h + 3])
    o_ref[0] = jnp.concatenate(pieces, axis=1)


def _prompt_attention(q, gates, kc, vc, ks, vs, kw, vw):
    b, t, _ = q.shape
    nc = kc.shape[1]
    assert t % KEY_TILE == 0 and t >= WINDOW + Q_TILE
    n_sel = min(N_SEL, t // SLC_BLOCK)
    qspec = lambda n: pl.BlockSpec((1, Q_TILE, n), lambda bi, i: (bi, i, 0))
    full = lambda r: pl.BlockSpec((1, r, KV_COLS), lambda bi, i: (bi, 0, 0))
    assert t <= POS_RADIX * POS_RADIX
    kx = _position_features(t)
    return pl.pallas_call(
        functools.partial(_prompt_attn_kernel, seq=t, n_sel=n_sel),
        grid=(b, t // Q_TILE),
        in_specs=[qspec(ATTN_WIDTH), qspec(LANES), full(nc), full(nc),
                  full(t), full(t), full(t), full(t), pl.BlockSpec(kx.shape, lambda bi, i: (0, 0))],
        out_specs=qspec(ATTN_WIDTH),
        out_shape=jax.ShapeDtypeStruct((b, t, ATTN_WIDTH), F32),
        compiler_params=_cparams("arbitrary", "arbitrary"),
        name="prompt_attention",
    )(q, gates, kc, vc, ks, vs, kw, vw, kx)


def _merge_groups(per_group):
    row = lax.broadcasted_iota(I32, per_group[0].shape, 0) // Q_PER_KV
    out = per_group[0]
    for g in range(1, N_KV):
        out = jnp.where(row == g, per_group[g], out)
    return out


def _group_slice(x, g):
    return x[:, g * HEAD_DIM:(g + 1) * HEAD_DIM]


def _matmul3_kernel(a_ref, b_ref, o_ref):
    o_ref[...] = _dot3(a_ref[...], b_ref[...])


def _matmul3(a, b):
    return pl.pallas_call(
        _matmul3_kernel,
        out_shape=jax.ShapeDtypeStruct((a.shape[0], b.shape[1]), F32),
        compiler_params=pltpu.CompilerParams(vmem_limit_bytes=VMEM_LIMIT),
        name="matmul3",
    )(a, b)


def _page_specs(n_pages, page):
    def spec(o):
        return pl.BlockSpec((1, N_KV, HEAD_DIM, page),
                            lambda i, j, pt: (pt[i * n_pages + j * PAGES_PER_STEP + o], 0, 0, 0))
    return [spec(o) for o in range(PAGES_PER_STEP)]


def _sample_scores_kernel(pt_ref, ut_ref, *refs):
    k_refs, o_ref = refs[:-1], refs[-1]
    for h in range(N_HEADS):
        g = h // Q_PER_KV
        u = ut_ref[0, h]
        rows = [jnp.sum(k_ref[0, g] * u, axis=0, keepdims=True) for k_ref in k_refs]
        o_ref[0, h] = jnp.concatenate(rows, axis=0)


def _sample_scores(page_table, ut, k_pages):
    b, n_pages = page_table.shape
    page = k_pages.shape[-1]
    assert n_pages % PAGES_PER_STEP == 0
    return pl.pallas_call(
        _sample_scores_kernel,
        grid_spec=pltpu.PrefetchScalarGridSpec(
            num_scalar_prefetch=1,
            grid=(b, n_pages // PAGES_PER_STEP),
            in_specs=[pl.BlockSpec((1, N_HEADS, HEAD_DIM, page), lambda i, j, pt: (i, 0, 0, 0))]
                     + _page_specs(n_pages, page),
            out_specs=pl.BlockSpec((1, N_HEADS, PAGES_PER_STEP, page), lambda i, j, pt: (i, 0, j, 0)),
        ),
        out_shape=jax.ShapeDtypeStruct((b, N_HEADS, n_pages, page), F32),
        compiler_params=_cparams("arbitrary", "arbitrary"),
        name="sample_scores",
    )(page_table.reshape(-1), ut, *([k_pages] * PAGES_PER_STEP))


def _max_all(x):
    return jnp.max(jnp.max(x, axis=0, keepdims=True), axis=1, keepdims=True)


def _min_all(x):
    return jnp.min(jnp.min(x, axis=0, keepdims=True), axis=1, keepdims=True)


def _sum_all(x):
    return jnp.sum(jnp.sum(x, axis=0, keepdims=True), axis=1, keepdims=True)


def _sample_select_kernel(s_ref, q_ref, kcn_ref, pexp_ref, pnew_ref, sel_ref, *, past, n_new, n_sel):
    n_pages, page = s_ref.shape[2], s_ref.shape[3]
    cpp = page // CMP_BLOCK
    n_past = n_pages * cpp
    n_blocks_total = (n_past + n_new) // CMP_PER_SLC
    lane = lax.broadcasted_iota(I32, (n_pages, page), 1)
    prow = lax.broadcasted_iota(I32, (n_pages, page), 0)
    dist = (past - ((prow * cpp + lane // CMP_BLOCK) * CMP_BLOCK + (CMP_BLOCK - 1))).astype(F32)
    mask = jnp.logical_and(lane % CMP_BLOCK == 0, dist >= 0.0)

    q8 = q_ref[0] * ATTN_SCALE
    slope = _alibi_slope_col(N_HEADS, 1, 0, N_HEADS)
    kcn = jnp.concatenate([kcn_ref[0], jnp.zeros((LANES - n_new, KV_COLS), F32)], axis=0)
    new_lane = lax.broadcasted_iota(I32, (N_HEADS, LANES), 1)
    dist_n = (past - ((n_past + new_lane) * CMP_BLOCK + (CMP_BLOCK - 1))).astype(F32)
    mask_n = jnp.logical_and(dist_n >= 0.0, new_lane < n_new)
    qk_n = _merge_groups([_dot3_nt(q8, _group_slice(kcn, g)) for g in range(N_KV)])
    s_new = jnp.where(mask_n, qk_n - slope * dist_n, NEG)

    halvings = [CMP_BLOCK >> k for k in range(1, CMP_BLOCK.bit_length())]
    probs, probs_new = [], []
    for h in range(N_HEADS):
        x = s_ref[0, h]
        for sh in halvings:
            x = x + pltpu.roll(x, page - sh, 1)
        s = jnp.where(mask, x - 2.0 ** (-8.0 * (h + 1) / N_HEADS) * dist, NEG)
        sn = s_new[h:h + 1, :]
        mn = jnp.logical_and(dist_n[h:h + 1, :] >= 0.0, new_lane[h:h + 1, :] < n_new)
        m = jnp.maximum(_max_all(s), jnp.max(sn, axis=1, keepdims=True))
        p = jnp.where(mask, jnp.exp(s - m), 0.0)
        pn = jnp.where(mn, jnp.exp(sn - m), 0.0)
        den = jnp.maximum(_sum_all(p) + jnp.sum(pn, axis=1, keepdims=True), 1e-30)
        p = p / den
        probs.append(p)
        probs_new.append(pn / den)
        z = p
        for sh in reversed(halvings):
            z = z + pltpu.roll(z, sh, 1)
        pexp_ref[0, h] = z
    pnew_ref[0] = jnp.concatenate(probs_new, axis=0)

    row1 = lax.broadcasted_iota(I32, (1, LANES), 1)
    blk = jnp.where(lane % SLC_BLOCK == 0, prow * (page // SLC_BLOCK) + lane // SLC_BLOCK, -1)
    blk_n = jnp.where(row1 < n_new, n_past // CMP_PER_SLC + (row1 >> 1), -1)
    blk_f = blk.astype(F32)
    blk_nf = blk_n.astype(F32)
    out_lane = lax.broadcasted_iota(I32, (N_HEADS, LANES), 1)
    out_row = lax.broadcasted_iota(I32, (N_HEADS, LANES), 0)
    out = jnp.full((N_HEADS, LANES), -1, I32)
    for g in range(N_KV):
        imp = probs[g * Q_PER_KV]
        imp_n = probs_new[g * Q_PER_KV]
        for r in range(1, Q_PER_KV):
            imp = imp + probs[g * Q_PER_KV + r]
            imp_n = imp_n + probs_new[g * Q_PER_KV + r]
        imp = imp + pltpu.roll(imp, page - CMP_BLOCK, 1)
        s_m = jnp.where(blk >= 0, _block_scores(imp, blk, past, n_blocks_total), NOT_A_BLOCK)
        s_n = jnp.where(blk_n >= 0, _block_scores(_pair_sum(imp_n, 1), blk_n, past, n_blocks_total), NOT_A_BLOCK)
        for j in range(n_sel):
            top = jnp.maximum(_max_all(s_m), jnp.max(s_n, axis=1, keepdims=True))
            first = jnp.minimum(_min_all(jnp.where(s_m == top, blk_f, NO_INDEX)),
                                jnp.min(jnp.where(s_n == top, blk_nf, NO_INDEX), axis=1, keepdims=True))
            s_m = jnp.where(blk_f == first, PICKED, s_m)
            s_n = jnp.where(blk_nf == first, PICKED, s_n)
            pick = jnp.where(top >= 0.0, first.astype(I32), -1)
            out = jnp.where(jnp.logical_and(out_row == g, out_lane == j), pick, out)
    sel_ref[0] = out


def _sample_select(s_raw, q8, kc_new, past):
    b, _, n_pages, page = s_raw.shape
    n_new = kc_new.shape[1]
    assert CMP_PER_SLC == 2 and CMP_BLOCK == 32
    n_sel = min(N_SEL, (past // CMP_BLOCK + n_new) // CMP_PER_SLC)
    per_b = lambda *s: pl.BlockSpec((1,) + s, lambda i: (i,) + (0,) * len(s))
    return pl.pallas_call(
        functools.partial(_sample_select_kernel, past=past, n_new=n_new, n_sel=n_sel),
        grid=(b,),
        in_specs=[per_b(N_HEADS, n_pages, page), per_b(N_HEADS, HEAD_DIM), per_b(n_new, KV_COLS)],
        out_specs=[per_b(N_HEADS, n_pages, page), per_b(N_HEADS, LANES), per_b(N_HEADS, LANES)],
        out_shape=[jax.ShapeDtypeStruct((b, N_HEADS, n_pages, page), F32),
                   jax.ShapeDtypeStruct((b, N_HEADS, LANES), F32),
                   jax.ShapeDtypeStruct((b, N_HEADS, LANES), I32)],
        compiler_params=_cparams("arbitrary"),
        name="sample_select",
    )(s_raw, q8, kc_new)


def _sample_values_kernel(pt_ref, pe_ref, *refs):
    v_refs, y_ref = refs[:-1], refs[-1]

    @pl.when(pl.program_id(1) == 0)
    def _():
        y_ref[...] = jnp.zeros(y_ref.shape, F32)

    for g in range(N_KV):
        heads = range(g * Q_PER_KV, (g + 1) * Q_PER_KV)
        pe = [pe_ref[0, h] for h in heads]
        acc = [jnp.zeros(y_ref.shape[2:], F32) for _ in heads]
        for o, v_ref in enumerate(v_refs):
            v = v_ref[0, g]
            for r in range(Q_PER_KV):
                acc[r] = acc[r] + v * pe[r][o:o + 1, :]
        for r, h in enumerate(heads):
            y_ref[0, h] = y_ref[0, h] + acc[r]


def _sample_values(page_table, pexp, v_pages):
    b, n_pages = page_table.shape
    page = v_pages.shape[-1]
    return pl.pallas_call(
        _sample_values_kernel,
        grid_spec=pltpu.PrefetchScalarGridSpec(
            num_scalar_prefetch=1,
            grid=(b, n_pages // PAGES_PER_STEP),
            in_specs=[pl.BlockSpec((1, N_HEADS, PAGES_PER_STEP, page), lambda i, j, pt: (i, 0, j, 0))]
                     + _page_specs(n_pages, page),
            out_specs=pl.BlockSpec((1, N_HEADS, HEAD_DIM, page), lambda i, j, pt: (i, 0, 0, 0)),
        ),
        out_shape=jax.ShapeDtypeStruct((b, N_HEADS, HEAD_DIM, page), F32),
        compiler_params=_cparams("arbitrary", "arbitrary"),
        name="sample_values",
    )(page_table.reshape(-1), pexp, *([v_pages] * PAGES_PER_STEP))


def _new_token_terms(q8, k_row, v_row):
    s = _merge_groups([jnp.sum(q8 * _group_slice(k_row, g), axis=1, keepdims=True) for g in range(N_KV)])
    v = _merge_groups([jnp.broadcast_to(_group_slice(v_row, g), (N_HEADS, HEAD_DIM)) for g in range(N_KV)])
    return s, v


def _sample_attend_kernel(sel_ref, pt_ref, q_ref, *refs, past, n_sel, ns_past):
    page_refs = refs[:4 * n_sel]
    (ksn_ref, vsn_ref, kw_ref, vw_ref, kwn_ref, vwn_ref, gate_ref, ocmp_ref, pnew_ref, vcn_ref,
     o_ref) = refs[4 * n_sel:]
    b = pl.program_id(0)
    page = page_refs[0].shape[-1]
    spp = page // SLC_BLOCK
    q8 = q_ref[0] * ATTN_SCALE
    q8_bf = q8.astype(BF16)
    slope = _alibi_slope_col(N_HEADS, 1, 0, N_HEADS)
    lane = lax.broadcasted_iota(I32, (N_HEADS, page), 1)
    s_t, v_t = _new_token_terms(q8, ksn_ref[0], vsn_ref[0])

    scores, masks, new_scores = [], [], []
    for n in range(n_sel):
        k_refs = page_refs[4 * n:4 * n + N_KV]
        blks = [sel_ref[(b * N_KV + g) * n_sel + n] for g in range(N_KV)]
        blk_rows = _merge_groups([jnp.full((N_HEADS, page), blk, I32) for blk in blks])
        blk_col = _merge_groups([jnp.full((N_HEADS, 1), blk, I32) for blk in blks])
        qk = _merge_groups([_dot(q8_bf, k_ref[0, 0].astype(BF16)) for k_ref in k_refs])
        page_pos = blk_rows // spp
        dist = (past - (page_pos * page + lane)).astype(F32)
        in_block = (lane // SLC_BLOCK) == (blk_rows - page_pos * spp)
        cached = jnp.logical_and(blk_rows >= 0, blk_rows < ns_past)
        mask = jnp.logical_and(jnp.logical_and(in_block, cached), dist >= 0.0)
        scores.append(jnp.where(mask, qk - slope * dist, NEG))
        masks.append(mask)
        new_scores.append(jnp.where(blk_col >= ns_past, s_t, NEG))
    m = new_scores[0]
    for s, sn in zip(scores, new_scores):
        m = jnp.maximum(m, jnp.maximum(jnp.max(s, axis=1, keepdims=True), sn))
    l_tot = jnp.zeros((N_HEADS, 1), F32)
    p_new = jnp.zeros((N_HEADS, 1), F32)
    acc = jnp.zeros((N_HEADS, HEAD_DIM), F32)
    for n in range(n_sel):
        v_refs = page_refs[4 * n + N_KV:4 * n + 2 * N_KV]
        p = jnp.where(masks[n], jnp.exp(scores[n] - m), 0.0)
        p_bf = p.astype(BF16)
        l_tot = l_tot + jnp.sum(p, axis=1, keepdims=True)
        p_new = p_new + jnp.where(new_scores[n] > 0.5 * NEG, jnp.exp(new_scores[n] - m), 0.0)
        acc = acc + _merge_groups([_dot_nt(p_bf, v_ref[0, 0].astype(BF16)) for v_ref in v_refs])
    o_slc = (acc + p_new * v_t) / jnp.maximum(l_tot + p_new, 1e-30)

    w_buf = kw_ref.shape[-1]
    wl = lax.broadcasted_iota(I32, (N_HEADS, w_buf), 1)
    win_pos = past - w_buf + wl
    dist_w = (past - win_pos).astype(F32)
    mask_w = jnp.logical_and(jnp.logical_and(dist_w >= 0.0, dist_w <= float(WINDOW)), win_pos >= 0)
    qk_w = _merge_groups([_dot(q8_bf, kw_ref[0, g].astype(BF16)) for g in range(N_KV)])
    s_w = jnp.where(mask_w, qk_w - slope * dist_w, NEG)
    s_t, v_t = _new_token_terms(q8, kwn_ref[0], vwn_ref[0])
    m_w = jnp.maximum(jnp.max(s_w, axis=1, keepdims=True), s_t)
    p_w = jnp.where(mask_w, jnp.exp(s_w - m_w), 0.0)
    p_t = jnp.exp(s_t - m_w)
    den = jnp.maximum(jnp.sum(p_w, axis=1, keepdims=True) + p_t, 1e-30)
    pw_bf = p_w.astype(BF16)
    o_w = _merge_groups([_dot_nt(pw_bf, vw_ref[0, g].astype(BF16)) for g in range(N_KV)])
    o_win = (o_w + p_t * v_t) / den
    n_new = vcn_ref.shape[1]
    vcn = jnp.concatenate([vcn_ref[0], jnp.zeros((LANES - n_new, KV_COLS), F32)], axis=0).astype(BF16)
    pn_bf = pnew_ref[0].astype(BF16)
    o_cmp = ocmp_ref[0] + _merge_groups([_dot(pn_bf, _group_slice(vcn, g)) for g in range(N_KV)])
    gt = gate_ref[0]
    o_ref[0] = o_cmp * gt[:, 0:1] + o_slc * gt[:, 1:2] + o_win * gt[:, 2:3]


def _sample_attend(sel, page_table, q8, k_pages, v_pages, ks_new, vs_new, kw_state, vw_state,
                   kw_new, vw_new, gates8, o_cmp, p_new, vc_new, past):
    b, n_pages = page_table.shape
    n_sel = sel.shape[-1]
    page = k_pages.shape[-1]
    ns_past = past // SLC_BLOCK
    spp = page // SLC_BLOCK
    w_buf = kw_state.shape[-1]
    n_new = vc_new.shape[1]

    def cache_map(n, g):
        def index(i, sel_ref, pt_ref):
            blk = jnp.clip(sel_ref[(i * N_KV + g) * n_sel + n], 0, ns_past - 1)
            return (pt_ref[i * n_pages + blk // spp], g, 0, 0)
        return pl.BlockSpec((1, 1, HEAD_DIM, page), index)

    page_specs, page_args = [], []
    for n in range(n_sel):
        for arr in (k_pages, v_pages):
            for g in range(N_KV):
                page_specs.append(cache_map(n, g))
                page_args.append(arr)
    per_b = lambda *s: pl.BlockSpec((1,) + s, lambda i, sl, pt: (i,) + (0,) * len(s))
    return pl.pallas_call(
        functools.partial(_sample_attend_kernel, past=past, n_sel=n_sel, ns_past=ns_past),
        grid_spec=pltpu.PrefetchScalarGridSpec(
            num_scalar_prefetch=2,
            grid=(b,),
            in_specs=[per_b(N_HEADS, HEAD_DIM)] + page_specs
                     + [per_b(1, KV_COLS), per_b(1, KV_COLS),
                        per_b(N_KV, HEAD_DIM, w_buf), per_b(N_KV, HEAD_DIM, w_buf),
                        per_b(1, KV_COLS), per_b(1, KV_COLS),
                        per_b(N_HEADS, 3), per_b(N_HEADS, HEAD_DIM), per_b(N_HEADS, LANES),
                        per_b(n_new, KV_COLS)],
            out_specs=per_b(N_HEADS, HEAD_DIM),
        ),
        out_shape=jax.ShapeDtypeStruct((b, N_HEADS, HEAD_DIM), F32),
        compiler_params=_cparams("arbitrary"),
        name="sample_attend",
    )(sel.reshape(-1), page_table.reshape(-1), q8, *page_args,
      ks_new, vs_new, kw_state, vw_state, kw_new, vw_new, gates8, o_cmp, p_new, vc_new)


def _layernorm_silu(y, g, b):
    mu = jnp.mean(y, axis=-1, keepdims=True)
    var = jnp.mean(jnp.square(y - mu), axis=-1, keepdims=True)
    return _silu((y - mu) * lax.rsqrt(var + EPS) * g + b)


def _conv_prompt_kernel(u_ref, w_ref, b_ref, g_ref, beta_ref, o_ref, buf):
    j = pl.program_id(1)
    tt = u_ref.shape[1]
    c = buf.shape[1]
    kw = w_ref.shape[0] // SUBLANES

    @pl.when(j == 0)
    def _():
        buf[0:CONV_HALO, :] = jnp.zeros((CONV_HALO, c), F32)

    buf[CONV_HALO:CONV_HALO + tt, :] = u_ref[0]
    first = CONV_HALO - (kw - 1)
    rows = CONV_ROWS
    for r0 in range(0, tt, rows):
        acc = jnp.zeros((rows // SUBLANES, SUBLANES, c), F32)
        for r in range(SUBLANES):
            taps = range(r, kw, SUBLANES)
            win = buf[pl.ds(first + r + r0, rows + SUBLANES * (len(taps) - 1)), :]
            for t, k in enumerate(taps):
                wk = w_ref[SUBLANES * k:SUBLANES * (k + 1), :]
                tap = win[SUBLANES * t:SUBLANES * t + rows].reshape(rows // SUBLANES, SUBLANES, c)
                acc = acc + wk[None] * tap
        y = acc.reshape(rows, c) + b_ref[...]
        o_ref[0, r0:r0 + rows, :] = _layernorm_silu(y, g_ref[...], beta_ref[...])
    buf[0:CONV_HALO, :] = buf[tt:tt + CONV_HALO, :]


def _conv_prompt(u, w_dw, b_dw, ln_g, ln_b, tt):
    b, t, c = u.shape
    vec = pl.BlockSpec((1, c), lambda i, j: (0, 0))
    w_rep = jnp.repeat(w_dw, SUBLANES, axis=0)
    return pl.pallas_call(
        _conv_prompt_kernel,
        grid=(b, t // tt),
        in_specs=[pl.BlockSpec((1, tt, c), lambda i, j: (i, j, 0)),
                  pl.BlockSpec(w_rep.shape, lambda i, j: (0, 0)), vec, vec, vec],
        out_specs=pl.BlockSpec((1, tt, c), lambda i, j: (i, j, 0)),
        out_shape=jax.ShapeDtypeStruct((b, t, c), F32),
        scratch_shapes=[pltpu.VMEM((CONV_HALO + tt, c), F32)],
        compiler_params=_cparams("arbitrary", "arbitrary"),
        name="conv_prompt",
    )(u, w_rep, b_dw, ln_g, ln_b)


def _conv_sample_kernel(up_ref, w_ref, b_ref, g_ref, beta_ref, o_ref):
    y = jnp.sum(up_ref[...] * w_ref[...][None, :, :], axis=1)
    o_ref[...] = _layernorm_silu(y + b_ref[...], g_ref[...], beta_ref[...])


def _conv_sample(up, w_dw, b_dw, ln_g, ln_b):
    b, kw, c = up.shape
    return pl.pallas_call(
        _conv_sample_kernel,
        out_shape=jax.ShapeDtypeStruct((b, c), F32),
        name="conv_sample",
    )(up, w_dw, b_dw, ln_g, ln_b)


def _merge_router_kernel(oa_ref, oc_ref, x_ref, gate_ref, shift_ref, scale_ref, goa_ref, goc_ref,
                         wout_ref, g2_ref, rwh_ref, rwl_ref, rb_ref, cnt_in_ref,
                         x1_ref, h2_ref, eidx_ref, wts_ref, rank_ref, cnt_out_ref, run):
    first = jnp.logical_and(pl.program_id(0) == 0, pl.program_id(1) == 0)

    @pl.when(first)
    def _():
        run[...] = cnt_in_ref[...]

    a = _rms(oa_ref[0], goa_ref[...])
    c = _rms(oc_ref[0], goc_ref[...])
    cat = jnp.concatenate([a, c], axis=1).astype(BF16)
    x1 = x_ref[0] + gate_ref[0] * _dot(cat, wout_ref[...])
    x1_ref[0] = x1
    h2 = _rms(x1, g2_ref[...]) * (1.0 + scale_ref[0]) + shift_ref[0]
    h2_ref[0] = h2

    hh, hl = _split2(h2)
    logits = _dot_nt(rwh_ref[...], hh) + (_dot_nt(rwl_ref[...], hh) + _dot_nt(rwh_ref[...], hl))
    aff = _sigmoid(logits)
    n_exp, tm = aff.shape
    row_f = lax.broadcasted_iota(I32, (n_exp, tm), 0).astype(F32)
    s = aff + rb_ref[...]
    experts, weights = [], []
    for _ in range(TOP_K):
        m = jnp.max(s, axis=0, keepdims=True)
        e = jnp.min(jnp.where(s == m, row_f, NO_INDEX), axis=0, keepdims=True)
        pick = row_f == e
        experts.append(e)
        weights.append(jnp.sum(jnp.where(pick, aff, 0.0), axis=0, keepdims=True))
        s = jnp.where(pick, NEG, s)
    total = weights[0]
    for w in weights[1:]:
        total = total + w

    hot = jnp.where(s == NEG, 1.0, 0.0)
    r_i = lax.broadcasted_iota(I32, (tm, tm), 0)
    c_i = lax.broadcasted_iota(I32, (tm, tm), 1)
    earlier = jnp.where(r_i < c_i, 1.0, 0.0).astype(BF16)
    before = _dot(hot.astype(BF16), earlier) + run[...]
    ranks = [jnp.sum(jnp.where(row_f == e, before, 0.0), axis=0, keepdims=True) for e in experts]
    eidx_ref[0] = jnp.concatenate(experts, axis=0).astype(I32)
    wts_ref[0] = jnp.concatenate([ROUTE_SCALE * w / total for w in weights], axis=0)
    rank_ref[0] = jnp.concatenate(ranks, axis=0).astype(I32)
    run[...] = run[...] + jnp.sum(hot, axis=1, keepdims=True)
    cnt_out_ref[...] = run[...]


def _merge_router(o_attn, o_conv, x, gate, shift, scale, goa, goc, wout_bf, g2, rw_hi, rw_lo, rb,
                  cnt_in, tm):
    b, t, d = x.shape
    n_exp = rw_hi.shape[0]
    row = lambda n: pl.BlockSpec((1, tm, n), lambda i, j: (i, j, 0))
    pick = pl.BlockSpec((1, TOP_K, tm), lambda i, j: (i, 0, j))
    const = lambda shape: pl.BlockSpec(shape, lambda i, j: (0,) * len(shape))
    sds = lambda n, dt: jax.ShapeDtypeStruct((b, t, n), dt)
    picks = lambda dt: jax.ShapeDtypeStruct((b, TOP_K, t), dt)
    return pl.pallas_call(
        _merge_router_kernel,
        grid=(b, t // tm),
        in_specs=[row(o_attn.shape[-1]), row(o_conv.shape[-1]), row(d),
                  _mod_spec(gate, tm, d), _mod_spec(shift, tm, d), _mod_spec(scale, tm, d),
                  const(goa.shape), const(goc.shape), const(wout_bf.shape), const(g2.shape),
                  const(rw_hi.shape), const(rw_lo.shape), const(rb.shape), const(cnt_in.shape)],
        out_specs=[row(d), row(d), pick, pick, pick, const((n_exp, 1))],
        out_shape=[sds(d, F32), sds(d, F32), picks(I32), picks(F32), picks(I32),
                   jax.ShapeDtypeStruct((n_exp, 1), F32)],
        scratch_shapes=[pltpu.VMEM((n_exp, 1), F32)],
        compiler_params=_cparams("arbitrary", "arbitrary"),
        name="merge_router",
    )(o_attn, o_conv, x, gate, shift, scale, goa, goc, wout_bf, g2, rw_hi, rw_lo, rb, cnt_in)


def _row_copy(src_hbm, dst, src_row, dst_row, sem, chunks):
    return pltpu.make_async_copy(src_hbm.at[pl.ds(src_row * chunks, chunks)],
                                 dst.at[pl.ds(dst_row * chunks, chunks)], sem)


def _slot(start_ref, e_ref, r_ref, idx):
    return start_ref[e_ref[0, 0, idx]] + r_ref[0, 0, idx]


def _dispatch_kernel(cnt_ref, end_ref, start_ref, e_ref, r_ref, h_ref, xs_hbm, zbuf, zsem, sem,
                     *, tokens, rows, chunks, n_blocks):
    j = pl.program_id(0)
    n_exp = cnt_ref.shape[0]
    blk_rows = rows * chunks

    def zero_block(blk):
        return pltpu.make_async_copy(zbuf, xs_hbm.at[pl.ds(blk * blk_rows, blk_rows)], zsem)

    @pl.when(j == 0)
    def _():
        zbuf[...] = jnp.zeros(zbuf.shape, F32)
        n_active = end_ref[n_exp - 1] // rows

        def zero_tail(e, issued):
            partial = cnt_ref[e] % rows != 0

            @pl.when(partial)
            def _():
                zero_block(end_ref[e] // rows - 1).start()

            return issued + partial.astype(I32)

        def zero_unused(blk, _):
            zero_block(blk).start()
            return 0

        def drain_zero(_, c):
            zero_block(0).wait()
            return c

        issued = lax.fori_loop(0, n_exp, zero_tail, 0)
        lax.fori_loop(n_active, n_blocks, zero_unused, 0)
        lax.fori_loop(0, issued + (n_blocks - n_active), drain_zero, 0)

    def issue(r, _):
        for k in range(TOP_K):
            _row_copy(h_ref, xs_hbm, r, _slot(start_ref, e_ref, r_ref, r * TOP_K + k), sem, chunks).start(
                priority=k % DMA_PRIORITIES)
        return 0

    def drain(r, _):
        for k in range(TOP_K):
            _row_copy(h_ref, xs_hbm, 0, 0, sem, chunks).wait()
        return 0

    lax.fori_loop(0, tokens, issue, 0)
    lax.fori_loop(0, tokens, drain, 0)


def _dispatch(counts, pad_end, pad_start, e_idx, rank, h_rows, tokens, rows, chunks, n_blocks):
    n_tiles = e_idx.shape[0]
    picks = pl.BlockSpec((1, 1, tokens * TOP_K), lambda j, c, e, s: (j, 0, 0), memory_space=pltpu.SMEM)
    return pl.pallas_call(
        functools.partial(_dispatch_kernel, tokens=tokens, rows=rows, chunks=chunks, n_blocks=n_blocks),
        grid_spec=pltpu.PrefetchScalarGridSpec(
            num_scalar_prefetch=3,
            grid=(n_tiles,),
            in_specs=[picks, picks,
                      pl.BlockSpec((tokens * chunks, LANES), lambda j, c, e, s: (j, 0))],
            out_specs=pl.BlockSpec(memory_space=pl.ANY),
            scratch_shapes=[pltpu.VMEM((rows * chunks, LANES), F32),
                            pltpu.SemaphoreType.DMA(()), pltpu.SemaphoreType.DMA(())],
        ),
        out_shape=jax.ShapeDtypeStruct((n_blocks * rows * chunks, LANES), F32),
        compiler_params=_cparams("arbitrary"),
        name="moe_dispatch",
    )(counts, pad_end, pad_start, e_idx, rank, h_rows)


def _expert_kernel(be_ref, nact_ref, first_ref, slot_ref, next_ref, x_ref, wg_hbm, wu_hbm, wd_hbm, y_ref,
                   wg_buf, wu_buf, wd_buf, sems, *, rows, chunks):
    j = pl.program_id(0)

    def weight_copies(expert, slot):
        return [pltpu.make_async_copy(w_hbm.at[expert], buf.at[slot], sems.at[slot, i])
                for i, (w_hbm, buf) in enumerate(((wg_hbm, wg_buf), (wu_hbm, wu_buf), (wd_hbm, wd_buf)))]

    @pl.when(j < nact_ref[0])
    def _():
        slot = slot_ref[j]

        @pl.when(j == 0)
        def _():
            for cp in weight_copies(be_ref[0], 0):
                cp.start()

        @pl.when(first_ref[j] == 1)
        def _():
            for cp in weight_copies(be_ref[j], slot):
                cp.wait()

            @pl.when(next_ref[j] >= 0)
            def _():
                for cp in weight_copies(next_ref[j], 1 - slot):
                    cp.start()

        f = wg_buf.shape[2]
        gate = jnp.zeros((rows, f), F32)
        up = jnp.zeros((rows, f), F32)
        for c in range(0, chunks, 2):
            xc = jnp.concatenate([x_ref[pl.ds(c, rows, stride=chunks), :],
                                  x_ref[pl.ds(c + 1, rows, stride=chunks), :]], axis=1).astype(BF16)
            cs = pl.ds(c * LANES, 2 * LANES)
            gate = gate + _dot(xc, wg_buf[slot, cs, :].astype(BF16))
            up = up + _dot(xc, wu_buf[slot, cs, :].astype(BF16))
        h = (_silu(gate) * up).astype(BF16)
        y = _dot(h, wd_buf[slot].astype(BF16))
        for c in range(chunks):
            y_ref[pl.ds(c, rows, stride=chunks), :] = y[:, c * LANES:(c + 1) * LANES]

    @pl.when(j >= nact_ref[0])
    def _():
        y_ref[...] = jnp.zeros(y_ref.shape, F32)


def _experts(blk_expert, n_active, pad_end, xs, wg, wu, wd, rows, chunks):
    n_blocks = blk_expert.shape[0]
    n_exp, d, f = wg.shape
    first = jnp.concatenate([jnp.ones((1,), I32), (blk_expert[1:] != blk_expert[:-1]).astype(I32)])
    slot = (jnp.cumsum(first) - 1) % 2
    run_end = pad_end[blk_expert] // rows
    nxt = jnp.where(run_end < n_active[0], blk_expert[jnp.minimum(run_end, n_blocks - 1)], -1)
    last = lambda j, na: jnp.minimum(j, na[0] - 1)
    hbm = pl.BlockSpec(memory_space=pl.ANY)
    return pl.pallas_call(
        functools.partial(_expert_kernel, rows=rows, chunks=chunks),
        grid_spec=pltpu.PrefetchScalarGridSpec(
            num_scalar_prefetch=5,
            grid=(n_blocks,),
            in_specs=[pl.BlockSpec((rows * chunks, LANES), lambda j, be, na, *_: (last(j, na), 0)),
                      hbm, hbm, hbm],
            out_specs=pl.BlockSpec((rows * chunks, LANES), lambda j, *_: (j, 0)),
            scratch_shapes=[pltpu.VMEM((2, d, f), F32), pltpu.VMEM((2, d, f), F32),
                            pltpu.VMEM((2, f, d), F32), pltpu.SemaphoreType.DMA((2, 3))],
        ),
        out_shape=jax.ShapeDtypeStruct(xs.shape, F32),
        compiler_params=_cparams("arbitrary"),
        name="moe_experts",
    )(blk_expert, n_active, first, slot.astype(I32), nxt.astype(I32), xs, wg, wu, wd)


def _combine_kernel(start_ref, e_ref, r_ref, w_ref, x1_ref, h2_ref, gate_ref, wsg_ref, wsu_ref, wsd_ref,
                    gf_ref, ys_hbm, o_ref, buf, sem, *, chunks):
    tm = x1_ref.shape[1]

    def issue(r, _):
        for k in range(TOP_K):
            _row_copy(ys_hbm, buf.at[k], _slot(start_ref, e_ref, r_ref, r * TOP_K + k), r, sem, chunks).start(
                priority=k % DMA_PRIORITIES)
        return 0

    def drain(r, _):
        for k in range(TOP_K):
            _row_copy(ys_hbm, buf.at[k], 0, r, sem, chunks).wait()
        return 0

    lax.fori_loop(0, tm, issue, 0)
    h_bf = h2_ref[0].astype(BF16)
    hid = (_silu(_dot(h_bf, wsg_ref[...])) * _dot(h_bf, wsu_ref[...])).astype(BF16)
    shared = _dot(hid, wsd_ref[...])
    lax.fori_loop(0, tm, drain, 0)

    w = w_ref[0]
    cols = []
    for c in range(chunks):
        tot = jnp.zeros((tm, LANES), F32)
        for k in range(TOP_K):
            tot = tot + buf[k, pl.ds(c, tm, stride=chunks), :] * w[:, k:k + 1]
        cols.append(tot)
    routed = jnp.concatenate(cols, axis=1)
    x2 = x1_ref[0] + gate_ref[0] * (routed + shared)
    o_ref[0] = _rms(x2, gf_ref[...])


def _combine(pad_start, e_idx, rank, wts, x1, h2, gate, wsg_bf, wsu_bf, wsd_bf, gf, ys, tm, chunks):
    b, t, d = x1.shape
    nt = t // tm
    row = lambda n: pl.BlockSpec((1, tm, n), lambda i, j, *_: (i, j, 0))
    const = lambda shape: pl.BlockSpec(shape, lambda i, j, *_: (0,) * len(shape))
    picks = pl.BlockSpec((1, 1, tm * TOP_K), lambda i, j, *_: (i * nt + j, 0, 0), memory_space=pltpu.SMEM)
    return pl.pallas_call(
        functools.partial(_combine_kernel, chunks=chunks),
        grid_spec=pltpu.PrefetchScalarGridSpec(
            num_scalar_prefetch=1,
            grid=(b, nt),
            in_specs=[picks, picks, row(TOP_K), row(d), row(d), _mod_spec(gate, tm, d),
                      const(wsg_bf.shape), const(wsu_bf.shape), const(wsd_bf.shape), const(gf.shape),
                      pl.BlockSpec(memory_space=pl.ANY)],
            out_specs=row(d),
            scratch_shapes=[pltpu.VMEM((TOP_K, tm * chunks, LANES), F32), pltpu.SemaphoreType.DMA(())],
        ),
        out_shape=jax.ShapeDtypeStruct((b, t, d), F32),
        compiler_params=_cparams("arbitrary", "arbitrary"),
        name="moe_combine",
    )(pad_start, e_idx, rank, wts, x1, h2, gate, wsg_bf, wsu_bf, wsd_bf, gf, ys)


def _split_mod(mod, per_token):
    parts = jnp.split(mod, 6, axis=-1)
    if per_token:
        return [p[None] for p in parts]
    return [p[:, None, :] for p in parts]


def _padded_in_weight(w_in, conv_width):
    n_gate = 3 * N_HEADS
    o = ATTN_WIDTH + 6 * KV_COLS
    main = w_in[:, :o]
    gates = jnp.pad(w_in[:, o:o + n_gate], ((0, 0), (0, LANES - n_gate)))
    glu = w_in[:, o + n_gate:o + n_gate + 2 * conv_width]
    return jnp.concatenate([main, gates, glu], axis=1).astype(BF16)


def _cmp_rows(x):
    return x.reshape(x.shape[:-2] + (x.shape[-2] // CMP_BLOCK, CMP_BLOCK * KV_COLS))


def _largest_tile(n, cap):
    best = [k for k in range(SUBLANES, cap + 1, SUBLANES) if n % k == 0]
    assert best, (n, cap)
    return best[-1]


def _kv5(x):
    return x.reshape(x.shape[:-1] + (N_KV, HEAD_DIM))[None]


def kernel(x_prompt, x_sample, cache_k_cmp, cache_v_cmp, cache_k_slc, cache_v_slc, state_k_win, state_v_win, state_conv, page_table, c_prompt, c_sample, norm1_g, norm2_g, w_ada, b_ada, w_in, w_cmp_k, w_cmp_v, w_dw, b_dw, ln_conv_g, ln_conv_b, g_out_attn, g_out_conv, w_out, router_w, router_b, w_exp_gate, w_exp_up, w_exp_down, w_sh_gate, w_sh_up, w_sh_down, norm_f_g):
    assert w_ada.shape[0] == 1, "single layer"
    bp, t, d = x_prompt.shape
    bs, s_new, _ = x_sample.shape
    assert s_new == 1
    n_pool, page = cache_k_cmp.shape[1], cache_k_cmp.shape[2]
    n_pages = page_table.shape[1]
    past = n_pages * page
    conv_width = state_conv.shape[-1]
    n_exp = router_w.shape[-1]
    chunks = d // LANES
    tm = min(ROW_TILE, t)

    w_in_bf = _padded_in_weight(w_in[0], conv_width)
    wck = _compress_weight(w_cmp_k[0])
    wcv = _compress_weight(w_cmp_v[0])
    wout_bf = w_out[0].astype(BF16)
    rw_t = router_w[0].T
    rw_hi = rw_t.astype(BF16)
    rw_lo = (rw_t - rw_hi.astype(F32)).astype(BF16)
    wsg_bf, wsu_bf, wsd_bf = (w[0].astype(BF16) for w in (w_sh_gate, w_sh_up, w_sh_down))
    gf = norm_f_g[None, :]

    n_c = bp + bs
    c_all = jnp.concatenate([c_prompt, c_sample], axis=0)
    c_all = jnp.pad(c_all, ((0, (-n_c) % SUBLANES), (0, 0)))
    mod = _modulation(c_all, w_ada[0], b_ada)
    mp = _split_mod(mod[:bp], per_token=False)
    ms = _split_mod(mod[bp:n_c], per_token=True)

    (q_p, kc_p, vc_p, ks_p, vs_p, kw_p, vw_p, gate_p, u_p) = _in_proj(
        x_prompt, mp[0], mp[1], norm1_g, w_in_bf, tm)
    nc_p = t // CMP_BLOCK
    kcc, vcc = _compress(_cmp_rows(kc_p).reshape(bp * nc_p, -1), _cmp_rows(vc_p).reshape(bp * nc_p, -1),
                         wck, wcv)
    o_attn_p = _prompt_attention(q_p, gate_p, kcc.reshape(bp, nc_p, KV_COLS), vcc.reshape(bp, nc_p, KV_COLS),
                                 ks_p, vs_p, kw_p, vw_p)
    o_conv_p = _conv_prompt(u_p, w_dw[0], b_dw, ln_conv_g, ln_conv_b, tm)

    xs_row = x_sample.reshape(1, bs, d)
    (q_s, kc_s, vc_s, ks_s, vs_s, kw_s, vw_s, gate_s, u_s) = _in_proj(
        xs_row, ms[0], ms[1], norm1_g, w_in_bf, bs)
    q8 = q_s.reshape(bs, N_HEADS, HEAD_DIM)
    gates8 = gate_s[0, :, :3 * N_HEADS].reshape(bs, N_HEADS, 3)
    pages_t = lambda c: jnp.transpose(c[0], (0, 2, 3, 1))
    tail = (-(past + s_new)) % SLC_BLOCK
    n_new = (s_new + tail) // CMP_BLOCK
    tail_rows = lambda x: _cmp_rows(jnp.pad(x[0][:, None, :], ((0, 0), (0, tail), (0, 0)))).reshape(bs * n_new, -1)
    pad_rows = (-(bs * n_new)) % SUBLANES
    kc_new, vc_new = _compress(jnp.pad(tail_rows(kc_s), ((0, pad_rows), (0, 0))),
                               jnp.pad(tail_rows(vc_s), ((0, pad_rows), (0, 0))), wck, wcv)
    kc_new = kc_new[:bs * n_new].reshape(bs, n_new, KV_COLS)
    vc_new = vc_new[:bs * n_new].reshape(bs, n_new, KV_COLS)
    reps = page // CMP_BLOCK
    wk_fold = jnp.transpose(w_cmp_k[0], (2, 1, 0)).reshape(HEAD_DIM, HEAD_DIM * CMP_BLOCK)
    ut = _matmul3(q8.reshape(bs * N_HEADS, HEAD_DIM) * ATTN_SCALE, wk_fold)
    ut = jnp.tile(ut.reshape(bs, N_HEADS, HEAD_DIM, CMP_BLOCK), (1, 1, 1, reps))
    s_raw = _sample_scores(page_table, ut, pages_t(cache_k_cmp))
    p_exp, p_new, sel = _sample_select(s_raw, q8, kc_new, past)
    y_acc = _sample_values(page_table, p_exp, pages_t(cache_v_cmp))
    wv_fold = jnp.tile(jnp.transpose(w_cmp_v[0], (1, 0, 2)), (1, reps, 1)).reshape(HEAD_DIM * page, HEAD_DIM)
    o_cmp_s = _matmul3(y_acc.reshape(bs * N_HEADS, HEAD_DIM * page), wv_fold).reshape(bs, N_HEADS, HEAD_DIM)
    n_sel = min(N_SEL, (past // CMP_BLOCK + n_new) // CMP_PER_SLC)
    sel = sel[:, :N_KV, :n_sel]
    row3 = lambda x: x[0][:, None, :]
    o_attn_s = _sample_attend(
        sel, page_table, q8, pages_t(cache_k_slc), pages_t(cache_v_slc), row3(ks_s), row3(vs_s),
        pages_t(state_k_win), pages_t(state_v_win), row3(kw_s), row3(vw_s), gates8, o_cmp_s, p_new, vc_new,
        past)
    o_attn_s = o_attn_s.reshape(1, bs, ATTN_WIDTH)
    up_s = jnp.concatenate([state_conv[0], u_s[0][:, None, :]], axis=1)
    o_conv_s = _conv_sample(up_s, w_dw[0], b_dw, ln_conv_g, ln_conv_b)[None]

    router = functools.partial(_merge_router, goa=g_out_attn, goc=g_out_conv, wout_bf=wout_bf, g2=norm2_g,
                               rw_hi=rw_hi, rw_lo=rw_lo, rb=router_b[0][:, None])
    x1_p, h2_p, e_p, w_p, r_p, cnt = router(o_attn_p, o_conv_p, x_prompt, mp[2], mp[3], mp[4],
                                            cnt_in=jnp.zeros((n_exp, 1), F32), tm=tm)
    x1_s, h2_s, e_s, w_s, r_s, cnt = router(o_attn_s, o_conv_s, xs_row, ms[2], ms[3], ms[4],
                                            cnt_in=cnt, tm=bs)

    n_tok = bp * t + bs
    counts = cnt[:, 0].astype(I32)
    padded = (counts + MOE_ROWS - 1) // MOE_ROWS * MOE_ROWS
    pad_end = jnp.cumsum(padded)
    pad_end = pad_end.astype(I32)
    pad_start = pad_end - padded
    n_blocks = -(-(n_tok * TOP_K) // MOE_ROWS) + n_exp
    blk_first = jnp.arange(n_blocks, dtype=I32) * MOE_ROWS
    blk_expert = jnp.minimum(jnp.sum(pad_end[None, :] <= blk_first[:, None], axis=1), n_exp - 1).astype(I32)
    n_active = pad_end[-1:] // MOE_ROWS
    picks = lambda a: jnp.transpose(a, (0, 2, 1))
    w_p, w_s = picks(w_p), picks(w_s)
    e_all = jnp.concatenate([picks(e_p).reshape(-1, TOP_K), picks(e_s).reshape(-1, TOP_K)], axis=0)
    r_all = jnp.concatenate([picks(r_p).reshape(-1, TOP_K), picks(r_s).reshape(-1, TOP_K)], axis=0)

    tile = _largest_tile(n_tok, 512)
    h_rows = jnp.concatenate([h2_p.reshape(-1, d), h2_s.reshape(-1, d)], axis=0).reshape(n_tok * chunks, LANES)
    xs = _dispatch(counts, pad_end, pad_start, e_all.reshape(n_tok // tile, 1, tile * TOP_K),
                   r_all.reshape(n_tok // tile, 1, tile * TOP_K), h_rows, tile, MOE_ROWS, chunks, n_blocks)
    ys = _experts(blk_expert, n_active, pad_end, xs, w_exp_gate[0], w_exp_up[0], w_exp_down[0], MOE_ROWS, chunks)

    comb = functools.partial(_combine, pad_start, wsg_bf=wsg_bf, wsu_bf=wsu_bf, wsd_bf=wsd_bf, gf=gf, ys=ys,
                             chunks=chunks)
    tiles_p = (bp * (t // tm), 1, tm * TOP_K)
    y_prompt = comb(picks(e_p).reshape(tiles_p), picks(r_p).reshape(tiles_p), w_p, x1_p, h2_p, mp[5], tm=tm)
    y_sample = comb(picks(e_s).reshape(1, 1, bs * TOP_K), picks(r_s).reshape(1, 1, bs * TOP_K), w_s, x1_s, h2_s,
                    ms[5], tm=bs).reshape(bs, 1, d)

    win = min(WINDOW, t)
    hist = state_conv.shape[2]
    out_p = [_kv5(a) for a in (kc_p, vc_p, ks_p, vs_p, kw_p[:, t - win:], vw_p[:, t - win:])]
    conv_p = u_p[:, t - hist:][None]
    out_s = [_kv5(a[0][:, None, :]) for a in (kc_s, vc_s, ks_s, vs_s)]
    w_buf = state_k_win.shape[2]
    kw_buf = jnp.concatenate([state_k_win, _kv5(kw_s[0][:, None, :])], axis=2)[:, :, -w_buf:]
    vw_buf = jnp.concatenate([state_v_win, _kv5(vw_s[0][:, None, :])], axis=2)[:, :, -w_buf:]
    conv_s = up_s[:, -hist:][None]
    return (y_prompt, y_sample, *out_p, conv_p, *out_s, kw_buf, vw_buf, conv_s)
```

```python
import functools

import jax
import jax.numpy as jnp
from jax import lax
from jax.experimental import pallas as pl
from jax.experimental.pallas import tpu as pltpu

F32 = jnp.float32
BF16 = jnp.bfloat16
I32 = jnp.int32

N_HEADS = 8
HEAD_DIM = 64
N_KV = 2
Q_PER_KV = N_HEADS // N_KV
ATTN_WIDTH = N_HEADS * HEAD_DIM
KV_COLS = N_KV * HEAD_DIM
CMP_BLOCK = 32
SLC_BLOCK = 64
CMP_PER_SLC = SLC_BLOCK // CMP_BLOCK
N_SEL = 16
WINDOW = 512
TOP_K = 8
ROUTE_SCALE = 2.5
EPS = 1e-6
FORCED = 1e4
NEG = -1e30
ATTN_SCALE = HEAD_DIM ** -0.5
PICKED = -2.0
NOT_A_BLOCK = -4.0
NO_INDEX = 1e9

LANES = 128
SUBLANES = 8
VMEM_LIMIT = 56 * 1024 * 1024
DMA_PRIORITIES = 2

ROW_TILE = 256
Q_TILE = 128
KEY_TILE = 1024
CMP_ROW_TILE = 512
MOE_ROWS = 256
CONV_HALO = 32
POS_RADIX = 256
CONV_ROWS = 32
PAGES_PER_STEP = 32


def _cparams(*sem):
    return pltpu.CompilerParams(dimension_semantics=sem, vmem_limit_bytes=VMEM_LIMIT)


def _dot(a, b):
    return jnp.dot(a, b, preferred_element_type=F32)


def _dot_nt(a, b):
    return lax.dot_general(a, b, (((1,), (1,)), ((), ())), preferred_element_type=F32)


def _dot_tn(a, b):
    return lax.dot_general(a, b, (((0,), (0,)), ((), ())), preferred_element_type=F32)


def _split2(x):
    hi = x.astype(BF16)
    lo = (x - hi.astype(F32)).astype(BF16)
    return hi, lo


def _dot3(a, b):
    ah, al = _split2(a)
    bh, bl = _split2(b)
    return _dot(ah, bh) + (_dot(ah, bl) + _dot(al, bh))


def _dot3_nt(a, b):
    ah, al = _split2(a)
    bh, bl = _split2(b)
    return _dot_nt(ah, bh) + (_dot_nt(ah, bl) + _dot_nt(al, bh))


def _sigmoid(x):
    return 1.0 / (1.0 + jnp.exp(-x))


def _silu(x):
    return x * _sigmoid(x)


def _rms(x, g):
    return x * lax.rsqrt(jnp.mean(x * x, axis=-1, keepdims=True) + EPS) * g


def _alibi_slope_col(rows, rows_per_head, first_head, n_heads):
    r = lax.broadcasted_iota(I32, (rows, 1), 0) // rows_per_head
    out = jnp.zeros((rows, 1), F32)
    for k in range(n_heads):
        out = jnp.where(r == k, 2.0 ** (-8.0 * (first_head + k + 1) / N_HEADS), out)
    return out


def _modulation_kernel(c_ref, w_ref, b_ref, o_ref):
    o_ref[...] = _dot3(c_ref[...], w_ref[...]) + b_ref[...]


def _modulation(c, w, b):
    m, d = c.shape
    n = w.shape[1]
    tn = 768
    return pl.pallas_call(
        _modulation_kernel,
        grid=(n // tn,),
        in_specs=[pl.BlockSpec((m, d), lambda j: (0, 0)),
                  pl.BlockSpec((d, tn), lambda j: (0, j)),
                  pl.BlockSpec((1, tn), lambda j: (0, j))],
        out_specs=pl.BlockSpec((m, tn), lambda j: (0, j)),
        out_shape=jax.ShapeDtypeStruct((m, n), F32),
        compiler_params=_cparams("arbitrary"),
        name="modulation",
    )(c, w, b)


def _mod_spec(mod, tm, d):
    if mod.shape[1] == 1:
        return pl.BlockSpec((1, 1, d), lambda i, j, *_: (i, 0, 0))
    return pl.BlockSpec((1, tm, d), lambda i, j, *_: (i, j, 0))


def _in_proj_kernel(x_ref, shift_ref, scale_ref, g_ref, w_ref,
                    q_ref, kc_ref, vc_ref, ks_ref, vs_ref, kw_ref, vw_ref, gate_ref, u_ref):
    x = x_ref[0]
    h = _rms(x, g_ref[...]) * (1.0 + scale_ref[0]) + shift_ref[0]
    z = _dot(h.astype(BF16), w_ref[...])
    q_ref[0] = z[:, :ATTN_WIDTH]
    o = ATTN_WIDTH
    for ref in (kc_ref, vc_ref, ks_ref, vs_ref, kw_ref, vw_ref):
        ref[0] = z[:, o:o + KV_COLS]
        o += KV_COLS
    gate_ref[0] = _sigmoid(z[:, o:o + LANES])
    o += LANES
    cw = u_ref.shape[-1]
    u_ref[0] = z[:, o:o + cw] * _sigmoid(z[:, o + cw:o + 2 * cw])


def _in_proj(x, shift, scale, g, w_bf, tm):
    b, t, d = x.shape
    cw = (w_bf.shape[1] - ATTN_WIDTH - 6 * KV_COLS - LANES) // 2
    row = lambda n: pl.BlockSpec((1, tm, n), lambda i, j: (i, j, 0))
    sds = lambda n: jax.ShapeDtypeStruct((b, t, n), F32)
    return pl.pallas_call(
        _in_proj_kernel,
        grid=(b, t // tm),
        in_specs=[row(d), _mod_spec(shift, tm, d), _mod_spec(scale, tm, d),
                  pl.BlockSpec((1, d), lambda i, j: (0, 0)),
                  pl.BlockSpec(w_bf.shape, lambda i, j: (0, 0))],
        out_specs=[row(ATTN_WIDTH)] + [row(KV_COLS)] * 6 + [row(LANES), row(cw)],
        out_shape=[sds(ATTN_WIDTH)] + [sds(KV_COLS)] * 6 + [sds(LANES), sds(cw)],
        compiler_params=_cparams("arbitrary", "arbitrary"),
        name="in_proj",
    )(x, shift, scale, g, w_bf)


def _compress_kernel(k_ref, v_ref, wk_ref, wv_ref, ko_ref, vo_ref):
    ko_ref[...] = _dot3(k_ref[...], wk_ref[...])
    vo_ref[...] = _dot3(v_ref[...], wv_ref[...])


def _compress(k_rows, v_rows, wk, wv):
    r, kdim = k_rows.shape
    tr = min(CMP_ROW_TILE, r)
    assert r % tr == 0
    rows = pl.BlockSpec((tr, kdim), lambda i: (i, 0))
    wspec = pl.BlockSpec((kdim, KV_COLS), lambda i: (0, 0))
    ospec = pl.BlockSpec((tr, KV_COLS), lambda i: (i, 0))
    return pl.pallas_call(
        _compress_kernel,
        grid=(r // tr,),
        in_specs=[rows, rows, wspec, wspec],
        out_specs=[ospec, ospec],
        out_shape=[jax.ShapeDtypeStruct((r, KV_COLS), F32)] * 2,
        compiler_params=_cparams("arbitrary"),
        name="compress",
    )(k_rows, v_rows, wk, wv)


def _compress_weight(w):
    eye = jnp.eye(N_KV, dtype=w.dtype)
    big = jnp.einsum('lde,gh->lgdhe', w, eye)
    return big.reshape(CMP_BLOCK * KV_COLS, KV_COLS)


def _pair_sum(x, axis):
    n = x.shape[axis]
    idx = lax.broadcasted_iota(I32, x.shape, axis)
    nxt = pltpu.roll(x, n - 1, axis)
    prv = pltpu.roll(x, 1, axis)
    return x + jnp.where((idx & 1) == 0, nxt, prv)


def _block_scores(imp, blk, q_pos, n_blocks_total):
    cur = q_pos // SLC_BLOCK
    valid = jnp.logical_and(blk * SLC_BLOCK <= q_pos, blk < n_blocks_total)
    forced = jnp.logical_or(blk == 0, jnp.logical_or(blk == cur, blk == cur - 1))
    return jnp.where(valid, jnp.where(forced, FORCED, imp), -1.0)


def _select_blocks(score, blk, n_sel):
    blk_f = blk.astype(F32)
    s = score
    for _ in range(n_sel):
        m = jnp.max(s, axis=0, keepdims=True)
        first = jnp.min(jnp.where(s == m, blk_f, NO_INDEX), axis=0, keepdims=True)
        s = jnp.where(blk_f == first, PICKED, s)
    return jnp.where(jnp.logical_and(s == PICKED, score >= 0.0), 1.0, 0.0)


def _position_features(n):
    pos = jnp.arange(n, dtype=I32)[:, None]
    lane = jnp.arange(HEAD_DIM, dtype=I32)[None, :]
    feat = jnp.where(lane < 2, 1, jnp.where(lane == 2, pos // POS_RADIX, jnp.where(lane == 3, pos % POS_RADIX, 0)))
    return feat.astype(BF16)


def _query_position_features(q_pos, slope):
    lane = lax.broadcasted_iota(I32, (q_pos.shape[0], HEAD_DIM), 1)
    hi = (q_pos // POS_RADIX).astype(F32) * (-slope * POS_RADIX)
    lo = (q_pos % POS_RADIX).astype(F32) * (-slope)
    return jnp.where(lane == 0, hi, jnp.where(lane == 1, lo, jnp.where(
        lane == 2, slope * POS_RADIX, jnp.where(lane == 3, slope, 0.0))))


def _prompt_attn_kernel(q_ref, gate_ref, kc_ref, vc_ref, ks_ref, vs_ref, kw_ref, vw_ref, kx_ref, o_ref,
                        *, seq, n_sel):
    i = pl.program_id(1)
    tq = Q_TILE
    nc = kc_ref.shape[1]
    q_blk = q_ref[0] * ATTN_SCALE
    gates = gate_ref[0]
    slopes = [2.0 ** (-8.0 * (h + 1) / N_HEADS) for h in range(N_HEADS)]
    group_heads = [list(range(g * Q_PER_KV, (g + 1) * Q_PER_KV)) for g in range(N_KV)]
    gsl = [slice(g * HEAD_DIM, (g + 1) * HEAD_DIM) for g in range(N_KV)]
    rsl = [slice(r * tq, (r + 1) * tq) for r in range(Q_PER_KV)]
    q_pos_col = i * tq + lax.broadcasted_iota(I32, (tq, 1), 0)
    q_pos_row = i * tq + lax.broadcasted_iota(I32, (1, tq), 1)
    q_heads = [q_blk[:, h * HEAD_DIM:(h + 1) * HEAD_DIM] for h in range(N_HEADS)]
    qg = [jnp.concatenate([q_heads[h] for h in hs], axis=0) for hs in group_heads]
    qx_bf = [jnp.concatenate([jnp.concatenate([q_heads[h], _query_position_features(q_pos_col, slopes[h])],
                                              axis=1) for h in hs], axis=0).astype(BF16)
             for hs in group_heads]

    cmp_row = lax.broadcasted_iota(I32, (nc, tq), 0)
    dist_c = (q_pos_row - (cmp_row * CMP_BLOCK + (CMP_BLOCK - 1))).astype(F32)
    mask_c = dist_c >= 0.0
    blk = cmp_row >> 1
    o_cmp = [None] * N_HEADS
    sel_bf = []
    for g, hs in enumerate(group_heads):
        vc_bf = vc_ref[0][:, gsl[g]].astype(BF16)
        qk = _dot3_nt(kc_ref[0][:, gsl[g]], qg[g])
        imp = jnp.zeros((nc, tq), F32)
        for r, h in enumerate(hs):
            s = jnp.where(mask_c, qk[:, rsl[r]] - slopes[h] * dist_c, NEG)
            m = jnp.max(s, axis=0, keepdims=True)
            p = jnp.where(mask_c, jnp.exp(s - m), 0.0)
            p = p / jnp.maximum(jnp.sum(p, axis=0, keepdims=True), 1e-30)
            o_cmp[h] = _dot_tn(p.astype(BF16), vc_bf)
            imp = imp + p
        score = _block_scores(_pair_sum(imp, 0), blk, q_pos_row, seq // SLC_BLOCK)
        sel_bf.append(_select_blocks(score, blk, n_sel).astype(BF16))

    span = WINDOW + tq
    w_start = pl.multiple_of(jnp.maximum(i * tq - WINDOW, 0), tq)
    dist_w = (q_pos_col - (w_start + lax.broadcasted_iota(I32, (tq, span), 1))).astype(F32)
    bias_w = jnp.where(jnp.logical_and(dist_w >= 0.0, dist_w <= float(WINDOW)), 0.0, NEG)
    o_win = [None] * N_HEADS
    kx_w = kx_ref[pl.ds(w_start, span), :]
    for g, hs in enumerate(group_heads):
        kw_bf = kw_ref[0, pl.ds(w_start, span), :][:, gsl[g]].astype(BF16)
        vw_bf = vw_ref[0, pl.ds(w_start, span), :][:, gsl[g]].astype(BF16)
        s_all = _dot_nt(qx_bf[g], jnp.concatenate([kw_bf, kx_w], axis=1))
        probs, sums = [], []
        for r, h in enumerate(hs):
            s = s_all[rsl[r]] + bias_w
            p = jnp.exp(s - jnp.max(s, axis=1, keepdims=True))
            sums.append(jnp.sum(p, axis=1, keepdims=True))
            probs.append(p.astype(BF16))
        o_all = _dot(jnp.concatenate(probs, axis=0), vw_bf)
        for r, h in enumerate(hs):
            o_win[h] = o_all[rsl[r]] / sums[r]

    n_tiles = ((i + 1) * tq + KEY_TILE - 1) // KEY_TILE

    def slc_step(t, carry):
        ms, ls, accs = (list(c) for c in carry)
        k0 = pl.multiple_of(t * KEY_TILE, KEY_TILE)
        dist = (q_pos_col - (k0 + lax.broadcasted_iota(I32, (tq, KEY_TILE), 1))).astype(F32)
        causal = dist >= 0.0
        key_cmp = (k0 + lax.broadcasted_iota(I32, (nc, KEY_TILE), 1)) // CMP_BLOCK
        expand = jnp.where(key_cmp == lax.broadcasted_iota(I32, (nc, KEY_TILE), 0), 1.0, 0.0).astype(BF16)
        kx_t = kx_ref[pl.ds(k0, KEY_TILE), :]
        for g, hs in enumerate(group_heads):
            kt_bf = ks_ref[0, pl.ds(k0, KEY_TILE), :][:, gsl[g]].astype(BF16)
            vt_bf = vs_ref[0, pl.ds(k0, KEY_TILE), :][:, gsl[g]].astype(BF16)
            chosen = _dot_tn(sel_bf[g], expand)
            bias = jnp.where(jnp.logical_and(causal, chosen > 0.5), 0.0, NEG)
            s_all = _dot_nt(qx_bf[g], jnp.concatenate([kt_bf, kx_t], axis=1))
            probs, alphas = [], []
            for r, h in enumerate(hs):
                s = s_all[rsl[r]] + bias
                m_new = jnp.maximum(ms[h], jnp.max(s, axis=1, keepdims=True))
                alpha = jnp.exp(ms[h] - m_new)
                p = jnp.exp(s - m_new)
                ls[h] = alpha * ls[h] + jnp.sum(p, axis=1, keepdims=True)
                ms[h] = m_new
                alphas.append(alpha)
                probs.append(p.astype(BF16))
            pv = _dot(jnp.concatenate(probs, axis=0), vt_bf)
            for r, h in enumerate(hs):
                accs[h] = alphas[r] * accs[h] + pv[rsl[r]]
        return tuple(ms), tuple(ls), tuple(accs)

    init = (tuple(jnp.full((tq, 1), NEG, F32) for _ in range(N_HEADS)),
            tuple(jnp.zeros((tq, 1), F32) for _ in range(N_HEADS)),
            tuple(jnp.zeros((tq, HEAD_DIM), F32) for _ in range(N_HEADS)))
    _, l_s, acc_s = lax.fori_loop(0, n_tiles, slc_step, init)

    pieces = []
    for h in range(N_HEADS):
        o_slc = acc_s[h] / jnp.maximum(l_s[h], 1e-30)
        pieces.append(o_cmp[h] * gates[:, 3 * h + 0:3 * h + 1] + o_slc * gates[:, 3 * h + 1:3 * h + 2]
                      + o_win[h] * gates[:, 3 * h + 2:3 * h + 3])
    o_ref[0] = jnp.concatenate(pieces, axis=1)


def _prompt_attention(q, gates, kc, vc, ks, vs, kw, vw):
    b, t, _ = q.shape
    nc = kc.shape[1]
    assert t % KEY_TILE == 0 and t >= WINDOW + Q_TILE
    n_sel = min(N_SEL, t // SLC_BLOCK)
    qspec = lambda n: pl.BlockSpec((1, Q_TILE, n), lambda bi, i: (bi, i, 0))
    full = lambda r: pl.BlockSpec((1, r, KV_COLS), lambda bi, i: (bi, 0, 0))
    assert t <= POS_RADIX * POS_RADIX
    kx = _position_features(t)
    return pl.pallas_call(
        functools.partial(_prompt_attn_kernel, seq=t, n_sel=n_sel),
        grid=(b, t // Q_TILE),
        in_specs=[qspec(ATTN_WIDTH), qspec(LANES), full(nc), full(nc),
                  full(t), full(t), full(t), full(t), pl.BlockSpec(kx.shape, lambda bi, i: (0, 0))],
        out_specs=qspec(ATTN_WIDTH),
        out_shape=jax.ShapeDtypeStruct((b, t, ATTN_WIDTH), F32),
        compiler_params=_cparams("arbitrary", "arbitrary"),
        name="prompt_attention",
    )(q, gates, kc, vc, ks, vs, kw, vw, kx)


def _merge_groups(per_group):
    row = lax.broadcasted_iota(I32, per_group[0].shape, 0) // Q_PER_KV
    out = per_group[0]
    for g in range(1, N_KV):
        out = jnp.where(row == g, per_group[g], out)
    return out


def _group_slice(x, g):
    return x[:, g * HEAD_DIM:(g + 1) * HEAD_DIM]


def _matmul3_kernel(a_ref, b_ref, o_ref):
    o_ref[...] = _dot3(a_ref[...], b_ref[...])


def _matmul3(a, b):
    return pl.pallas_call(
        _matmul3_kernel,
        out_shape=jax.ShapeDtypeStruct((a.shape[0], b.shape[1]), F32),
        compiler_params=pltpu.CompilerParams(vmem_limit_bytes=VMEM_LIMIT),
        name="matmul3",
    )(a, b)


def _page_specs(n_pages, page):
    def spec(o):
        return pl.BlockSpec((1, N_KV, HEAD_DIM, page),
                            lambda i, j, pt: (pt[i * n_pages + j * PAGES_PER_STEP + o], 0, 0, 0))
    return [spec(o) for o in range(PAGES_PER_STEP)]


def _sample_scores_kernel(pt_ref, ut_ref, *refs):
    k_refs, o_ref = refs[:-1], refs[-1]
    for h in range(N_HEADS):
        g = h // Q_PER_KV
        u = ut_ref[0, h]
        rows = [jnp.sum(k_ref[0, g] * u, axis=0, keepdims=True) for k_ref in k_refs]
        o_ref[0, h] = jnp.concatenate(rows, axis=0)


def _sample_scores(page_table, ut, k_pages):
    b, n_pages = page_table.shape
    page = k_pages.shape[-1]
    assert n_pages % PAGES_PER_STEP == 0
    return pl.pallas_call(
        _sample_scores_kernel,
        grid_spec=pltpu.PrefetchScalarGridSpec(
            num_scalar_prefetch=1,
            grid=(b, n_pages // PAGES_PER_STEP),
            in_specs=[pl.BlockSpec((1, N_HEADS, HEAD_DIM, page), lambda i, j, pt: (i, 0, 0, 0))]
                     + _page_specs(n_pages, page),
            out_specs=pl.BlockSpec((1, N_HEADS, PAGES_PER_STEP, page), lambda i, j, pt: (i, 0, j, 0)),
        ),
        out_shape=jax.ShapeDtypeStruct((b, N_HEADS, n_pages, page), F32),
        compiler_params=_cparams("arbitrary", "arbitrary"),
        name="sample_scores",
    )(page_table.reshape(-1), ut, *([k_pages] * PAGES_PER_STEP))


def _max_all(x):
    return jnp.max(jnp.max(x, axis=0, keepdims=True), axis=1, keepdims=True)


def _min_all(x):
    return jnp.min(jnp.min(x, axis=0, keepdims=True), axis=1, keepdims=True)


def _sum_all(x):
    return jnp.sum(jnp.sum(x, axis=0, keepdims=True), axis=1, keepdims=True)


def _sample_select_kernel(s_ref, q_ref, kcn_ref, pexp_ref, pnew_ref, sel_ref, *, past, n_new, n_sel):
    n_pages, page = s_ref.shape[2], s_ref.shape[3]
    cpp = page // CMP_BLOCK
    n_past = n_pages * cpp
    n_blocks_total = (n_past + n_new) // CMP_PER_SLC
    lane = lax.broadcasted_iota(I32, (n_pages, page), 1)
    prow = lax.broadcasted_iota(I32, (n_pages, page), 0)
    dist = (past - ((prow * cpp + lane // CMP_BLOCK) * CMP_BLOCK + (CMP_BLOCK - 1))).astype(F32)
    mask = jnp.logical_and(lane % CMP_BLOCK == 0, dist >= 0.0)

    q8 = q_ref[0] * ATTN_SCALE
    slope = _alibi_slope_col(N_HEADS, 1, 0, N_HEADS)
    kcn = jnp.concatenate([kcn_ref[0], jnp.zeros((LANES - n_new, KV_COLS), F32)], axis=0)
    new_lane = lax.broadcasted_iota(I32, (N_HEADS, LANES), 1)
    dist_n = (past - ((n_past + new_lane) * CMP_BLOCK + (CMP_BLOCK - 1))).astype(F32)
    mask_n = jnp.logical_and(dist_n >= 0.0, new_lane < n_new)
    qk_n = _merge_groups([_dot3_nt(q8, _group_slice(kcn, g)) for g in range(N_KV)])
    s_new = jnp.where(mask_n, qk_n - slope * dist_n, NEG)

    halvings = [CMP_BLOCK >> k for k in range(1, CMP_BLOCK.bit_length())]
    probs, probs_new = [], []
    for h in range(N_HEADS):
        x = s_ref[0, h]
        for sh in halvings:
            x = x + pltpu.roll(x, page - sh, 1)
        s = jnp.where(mask, x - 2.0 ** (-8.0 * (h + 1) / N_HEADS) * dist, NEG)
        sn = s_new[h:h + 1, :]
        mn = jnp.logical_and(dist_n[h:h + 1, :] >= 0.0, new_lane[h:h + 1, :] < n_new)
        m = jnp.maximum(_max_all(s), jnp.max(sn, axis=1, keepdims=True))
        p = jnp.where(mask, jnp.exp(s - m), 0.0)
        pn = jnp.where(mn, jnp.exp(sn - m), 0.0)
        den = jnp.maximum(_sum_all(p) + jnp.sum(pn, axis=1, keepdims=True), 1e-30)
        p = p / den
        probs.append(p)
        probs_new.append(pn / den)
        z = p
        for sh in reversed(halvings):
            z = z + pltpu.roll(z, sh, 1)
        pexp_ref[0, h] = z
    pnew_ref[0] = jnp.concatenate(probs_new, axis=0)

    row1 = lax.broadcasted_iota(I32, (1, LANES), 1)
    blk = jnp.where(lane % SLC_BLOCK == 0, prow * (page // SLC_BLOCK) + lane // SLC_BLOCK, -1)
    blk_n = jnp.where(row1 < n_new, n_past // CMP_PER_SLC + (row1 >> 1), -1)
    blk_f = blk.astype(F32)
    blk_nf = blk_n.astype(F32)
    out_lane = lax.broadcasted_iota(I32, (N_HEADS, LANES), 1)
    out_row = lax.broadcasted_iota(I32, (N_HEADS, LANES), 0)
    out = jnp.full((N_HEADS, LANES), -1, I32)
    for g in range(N_KV):
        imp = probs[g * Q_PER_KV]
        imp_n = probs_new[g * Q_PER_KV]
        for r in range(1, Q_PER_KV):
            imp = imp + probs[g * Q_PER_KV + r]
            imp_n = imp_n + probs_new[g * Q_PER_KV + r]
        imp = imp + pltpu.roll(imp, page - CMP_BLOCK, 1)
        s_m = jnp.where(blk >= 0, _block_scores(imp, blk, past, n_blocks_total), NOT_A_BLOCK)
        s_n = jnp.where(blk_n >= 0, _block_scores(_pair_sum(imp_n, 1), blk_n, past, n_blocks_total), NOT_A_BLOCK)
        for j in range(n_sel):
            top = jnp.maximum(_max_all(s_m), jnp.max(s_n, axis=1, keepdims=True))
            first = jnp.minimum(_min_all(jnp.where(s_m == top, blk_f, NO_INDEX)),
                                jnp.min(jnp.where(s_n == top, blk_nf, NO_INDEX), axis=1, keepdims=True))
            s_m = jnp.where(blk_f == first, PICKED, s_m)
            s_n = jnp.where(blk_nf == first, PICKED, s_n)
            pick = jnp.where(top >= 0.0, first.astype(I32), -1)
            out = jnp.where(jnp.logical_and(out_row == g, out_lane == j), pick, out)
    sel_ref[0] = out


def _sample_select(s_raw, q8, kc_new, past):
    b, _, n_pages, page = s_raw.shape
    n_new = kc_new.shape[1]
    assert CMP_PER_SLC == 2 and CMP_BLOCK == 32
    n_sel = min(N_SEL, (past // CMP_BLOCK + n_new) // CMP_PER_SLC)
    per_b = lambda *s: pl.BlockSpec((1,) + s, lambda i: (i,) + (0,) * len(s))
    return pl.pallas_call(
        functools.partial(_sample_select_kernel, past=past, n_new=n_new, n_sel=n_sel),
        grid=(b,),
        in_specs=[per_b(N_HEADS, n_pages, page), per_b(N_HEADS, HEAD_DIM), per_b(n_new, KV_COLS)],
        out_specs=[per_b(N_HEADS, n_pages, page), per_b(N_HEADS, LANES), per_b(N_HEADS, LANES)],
        out_shape=[jax.ShapeDtypeStruct((b, N_HEADS, n_pages, page), F32),
                   jax.ShapeDtypeStruct((b, N_HEADS, LANES), F32),
                   jax.ShapeDtypeStruct((b, N_HEADS, LANES), I32)],
        compiler_params=_cparams("arbitrary"),
        name="sample_select",
    )(s_raw, q8, kc_new)


def _sample_values_kernel(pt_ref, pe_ref, *refs):
    v_refs, y_ref = refs[:-1], refs[-1]

    @pl.when(pl.program_id(1) == 0)
    def _():
        y_ref[...] = jnp.zeros(y_ref.shape, F32)

    for g in range(N_KV):
        heads = range(g * Q_PER_KV, (g + 1) * Q_PER_KV)
        pe = [pe_ref[0, h] for h in heads]
        acc = [jnp.zeros(y_ref.shape[2:], F32) for _ in heads]
        for o, v_ref in enumerate(v_refs):
            v = v_ref[0, g]
            for r in range(Q_PER_KV):
                acc[r] = acc[r] + v * pe[r][o:o + 1, :]
        for r, h in enumerate(heads):
            y_ref[0, h] = y_ref[0, h] + acc[r]


def _sample_values(page_table, pexp, v_pages):
    b, n_pages = page_table.shape
    page = v_pages.shape[-1]
    return pl.pallas_call(
        _sample_values_kernel,
        grid_spec=pltpu.PrefetchScalarGridSpec(
            num_scalar_prefetch=1,
            grid=(b, n_pages // PAGES_PER_STEP),
            in_specs=[pl.BlockSpec((1, N_HEADS, PAGES_PER_STEP, page), lambda i, j, pt: (i, 0, j, 0))]
                     + _page_specs(n_pages, page),
            out_specs=pl.BlockSpec((1, N_HEADS, HEAD_DIM, page), lambda i, j, pt: (i, 0, 0, 0)),
        ),
        out_shape=jax.ShapeDtypeStruct((b, N_HEADS, HEAD_DIM, page), F32),
        compiler_params=_cparams("arbitrary", "arbitrary"),
        name="sample_values",
    )(page_table.reshape(-1), pexp, *([v_pages] * PAGES_PER_STEP))


def _new_token_terms(q8, k_row, v_row):
    s = _merge_groups([jnp.sum(q8 * _group_slice(k_row, g), axis=1, keepdims=True) for g in range(N_KV)])
    v = _merge_groups([jnp.broadcast_to(_group_slice(v_row, g), (N_HEADS, HEAD_DIM)) for g in range(N_KV)])
    return s, v


def _sample_attend_kernel(sel_ref, pt_ref, q_ref, *refs, past, n_sel, ns_past):
    page_refs = refs[:4 * n_sel]
    (ksn_ref, vsn_ref, kw_ref, vw_ref, kwn_ref, vwn_ref, gate_ref, ocmp_ref, pnew_ref, vcn_ref,
     o_ref) = refs[4 * n_sel:]
    b = pl.program_id(0)
    page = page_refs[0].shape[-1]
    spp = page // SLC_BLOCK
    q8 = q_ref[0] * ATTN_SCALE
    q8_bf = q8.astype(BF16)
    slope = _alibi_slope_col(N_HEADS, 1, 0, N_HEADS)
    lane = lax.broadcasted_iota(I32, (N_HEADS, page), 1)
    s_t, v_t = _new_token_terms(q8, ksn_ref[0], vsn_ref[0])

    scores, masks, new_scores = [], [], []
    for n in range(n_sel):
        k_refs = page_refs[4 * n:4 * n + N_KV]
        blks = [sel_ref[(b * N_KV + g) * n_sel + n] for g in range(N_KV)]
        blk_rows = _merge_groups([jnp.full((N_HEADS, page), blk, I32) for blk in blks])
        blk_col = _merge_groups([jnp.full((N_HEADS, 1), blk, I32) for blk in blks])
        qk = _merge_groups([_dot(q8_bf, k_ref[0, 0].astype(BF16)) for k_ref in k_refs])
        page_pos = blk_rows // spp
        dist = (past - (page_pos * page + lane)).astype(F32)
        in_block = (lane // SLC_BLOCK) == (blk_rows - page_pos * spp)
        cached = jnp.logical_and(blk_rows >= 0, blk_rows < ns_past)
        mask = jnp.logical_and(jnp.logical_and(in_block, cached), dist >= 0.0)
        scores.append(jnp.where(mask, qk - slope * dist, NEG))
        masks.append(mask)
        new_scores.append(jnp.where(blk_col >= ns_past, s_t, NEG))
    m = new_scores[0]
    for s, sn in zip(scores, new_scores):
        m = jnp.maximum(m, jnp.maximum(jnp.max(s, axis=1, keepdims=True), sn))
    l_tot = jnp.zeros((N_HEADS, 1), F32)
    p_new = jnp.zeros((N_HEADS, 1), F32)
    acc = jnp.zeros((N_HEADS, HEAD_DIM), F32)
    for n in range(n_sel):
        v_refs = page_refs[4 * n + N_KV:4 * n + 2 * N_KV]
        p = jnp.where(masks[n], jnp.exp(scores[n] - m), 0.0)
        p_bf = p.astype(BF16)
        l_tot = l_tot + jnp.sum(p, axis=1, keepdims=True)
        p_new = p_new + jnp.where(new_scores[n] > 0.5 * NEG, jnp.exp(new_scores[n] - m), 0.0)
        acc = acc + _merge_groups([_dot_nt(p_bf, v_ref[0, 0].astype(BF16)) for v_ref in v_refs])
    o_slc = (acc + p_new * v_t) / jnp.maximum(l_tot + p_new, 1e-30)

    w_buf = kw_ref.shape[-1]
    wl = lax.broadcasted_iota(I32, (N_HEADS, w_buf), 1)
    win_pos = past - w_buf + wl
    dist_w = (past - win_pos).astype(F32)
    mask_w = jnp.logical_and(jnp.logical_and(dist_w >= 0.0, dist_w <= float(WINDOW)), win_pos >= 0)
    qk_w = _merge_groups([_dot(q8_bf, kw_ref[0, g].astype(BF16)) for g in range(N_KV)])
    s_w = jnp.where(mask_w, qk_w - slope * dist_w, NEG)
    s_t, v_t = _new_token_terms(q8, kwn_ref[0], vwn_ref[0])
    m_w = jnp.maximum(jnp.max(s_w, axis=1, keepdims=True), s_t)
    p_w = jnp.where(mask_w, jnp.exp(s_w - m_w), 0.0)
    p_t = jnp.exp(s_t - m_w)
    den = jnp.maximum(jnp.sum(p_w, axis=1, keepdims=True) + p_t, 1e-30)
    pw_bf = p_w.astype(BF16)
    o_w = _merge_groups([_dot_nt(pw_bf, vw_ref[0, g].astype(BF16)) for g in range(N_KV)])
    o_win = (o_w + p_t * v_t) / den
    n_new = vcn_ref.shape[1]
    vcn = jnp.concatenate([vcn_ref[0], jnp.zeros((LANES - n_new, KV_COLS), F32)], axis=0).astype(BF16)
    pn_bf = pnew_ref[0].astype(BF16)
    o_cmp = ocmp_ref[0] + _merge_groups([_dot(pn_bf, _group_slice(vcn, g)) for g in range(N_KV)])
    gt = gate_ref[0]
    o_ref[0] = o_cmp * gt[:, 0:1] + o_slc * gt[:, 1:2] + o_win * gt[:, 2:3]


def _sample_attend(sel, page_table, q8, k_pages, v_pages, ks_new, vs_new, kw_state, vw_state,
                   kw_new, vw_new, gates8, o_cmp, p_new, vc_new, past):
    b, n_pages = page_table.shape
    n_sel = sel.shape[-1]
    page = k_pages.shape[-1]
    ns_past = past // SLC_BLOCK
    spp = page // SLC_BLOCK
    w_buf = kw_state.shape[-1]
    n_new = vc_new.shape[1]

    def cache_map(n, g):
        def index(i, sel_ref, pt_ref):
            blk = jnp.clip(sel_ref[(i * N_KV + g) * n_sel + n], 0, ns_past - 1)
            return (pt_ref[i * n_pages + blk // spp], g, 0, 0)
        return pl.BlockSpec((1, 1, HEAD_DIM, page), index)

    page_specs, page_args = [], []
    for n in range(n_sel):
        for arr in (k_pages, v_pages):
            for g in range(N_KV):
                page_specs.append(cache_map(n, g))
                page_args.append(arr)
    per_b = lambda *s: pl.BlockSpec((1,) + s, lambda i, sl, pt: (i,) + (0,) * len(s))
    return pl.pallas_call(
        functools.partial(_sample_attend_kernel, past=past, n_sel=n_sel, ns_past=ns_past),
        grid_spec=pltpu.PrefetchScalarGridSpec(
            num_scalar_prefetch=2,
            grid=(b,),
            in_specs=[per_b(N_HEADS, HEAD_DIM)] + page_specs
                     + [per_b(1, KV_COLS), per_b(1, KV_COLS),
                        per_b(N_KV, HEAD_DIM, w_buf), per_b(N_KV, HEAD_DIM, w_buf),
                        per_b(1, KV_COLS), per_b(1, KV_COLS),
                        per_b(N_HEADS, 3), per_b(N_HEADS, HEAD_DIM), per_b(N_HEADS, LANES),
                        per_b(n_new, KV_COLS)],
            out_specs=per_b(N_HEADS, HEAD_DIM),
        ),
        out_shape=jax.ShapeDtypeStruct((b, N_HEADS, HEAD_DIM), F32),
        compiler_params=_cparams("arbitrary"),
        name="sample_attend",
    )(sel.reshape(-1), page_table.reshape(-1), q8, *page_args,
      ks_new, vs_new, kw_state, vw_state, kw_new, vw_new, gates8, o_cmp, p_new, vc_new)


def _layernorm_silu(y, g, b):
    mu = jnp.mean(y, axis=-1, keepdims=True)
    var = jnp.mean(jnp.square(y - mu), axis=-1, keepdims=True)
    return _silu((y - mu) * lax.rsqrt(var + EPS) * g + b)


def _conv_prompt_kernel(u_ref, w_ref, b_ref, g_ref, beta_ref, o_ref, buf):
    j = pl.program_id(1)
    tt = u_ref.shape[1]
    c = buf.shape[1]
    kw = w_ref.shape[0] // SUBLANES

    @pl.when(j == 0)
    def _():
        buf[0:CONV_HALO, :] = jnp.zeros((CONV_HALO, c), F32)

    buf[CONV_HALO:CONV_HALO + tt, :] = u_ref[0]
    first = CONV_HALO - (kw - 1)
    rows = CONV_ROWS
    for r0 in range(0, tt, rows):
        acc = jnp.zeros((rows // SUBLANES, SUBLANES, c), F32)
        for r in range(SUBLANES):
            taps = range(r, kw, SUBLANES)
            win = buf[pl.ds(first + r + r0, rows + SUBLANES * (len(taps) - 1)), :]
            for t, k in enumerate(taps):
                wk = w_ref[SUBLANES * k:SUBLANES * (k + 1), :]
                tap = win[SUBLANES * t:SUBLANES * t + rows].reshape(rows // SUBLANES, SUBLANES, c)
                acc = acc + wk[None] * tap
        y = acc.reshape(rows, c) + b_ref[...]
        o_ref[0, r0:r0 + rows, :] = _layernorm_silu(y, g_ref[...], beta_ref[...])
    buf[0:CONV_HALO, :] = buf[tt:tt + CONV_HALO, :]


def _conv_prompt(u, w_dw, b_dw, ln_g, ln_b, tt):
    b, t, c = u.shape
    vec = pl.BlockSpec((1, c), lambda i, j: (0, 0))
    w_rep = jnp.repeat(w_dw, SUBLANES, axis=0)
    return pl.pallas_call(
        _conv_prompt_kernel,
        grid=(b, t // tt),
        in_specs=[pl.BlockSpec((1, tt, c), lambda i, j: (i, j, 0)),
                  pl.BlockSpec(w_rep.shape, lambda i, j: (0, 0)), vec, vec, vec],
        out_specs=pl.BlockSpec((1, tt, c), lambda i, j: (i, j, 0)),
        out_shape=jax.ShapeDtypeStruct((b, t, c), F32),
        scratch_shapes=[pltpu.VMEM((CONV_HALO + tt, c), F32)],
        compiler_params=_cparams("arbitrary", "arbitrary"),
        name="conv_prompt",
    )(u, w_rep, b_dw, ln_g, ln_b)


def _conv_sample_kernel(up_ref, w_ref, b_ref, g_ref, beta_ref, o_ref):
    y = jnp.sum(up_ref[...] * w_ref[...][None, :, :], axis=1)
    o_ref[...] = _layernorm_silu(y + b_ref[...], g_ref[...], beta_ref[...])


def _conv_sample(up, w_dw, b_dw, ln_g, ln_b):
    b, kw, c = up.shape
    return pl.pallas_call(
        _conv_sample_kernel,
        out_shape=jax.ShapeDtypeStruct((b, c), F32),
        name="conv_sample",
    )(up, w_dw, b_dw, ln_g, ln_b)


def _merge_router_kernel(oa_ref, oc_ref, x_ref, gate_ref, shift_ref, scale_ref, goa_ref, goc_ref,
                         wout_ref, g2_ref, rwh_ref, rwl_ref, rb_ref, cnt_in_ref,
                         x1_ref, h2_ref, eidx_ref, wts_ref, rank_ref, cnt_out_ref, run):
    first = jnp.logical_and(pl.program_id(0) == 0, pl.program_id(1) == 0)

    @pl.when(first)
    def _():
        run[...] = cnt_in_ref[...]

    a = _rms(oa_ref[0], goa_ref[...])
    c = _rms(oc_ref[0], goc_ref[...])
    cat = jnp.concatenate([a, c], axis=1).astype(BF16)
    x1 = x_ref[0] + gate_ref[0] * _dot(cat, wout_ref[...])
    x1_ref[0] = x1
    h2 = _rms(x1, g2_ref[...]) * (1.0 + scale_ref[0]) + shift_ref[0]
    h2_ref[0] = h2

    hh, hl = _split2(h2)
    logits = _dot_nt(rwh_ref[...], hh) + (_dot_nt(rwl_ref[...], hh) + _dot_nt(rwh_ref[...], hl))
    aff = _sigmoid(logits)
    n_exp, tm = aff.shape
    row_f = lax.broadcasted_iota(I32, (n_exp, tm), 0).astype(F32)
    s = aff + rb_ref[...]
    experts, weights = [], []
    for _ in range(TOP_K):
        m = jnp.max(s, axis=0, keepdims=True)
        e = jnp.min(jnp.where(s == m, row_f, NO_INDEX), axis=0, keepdims=True)
        pick = row_f == e
        experts.append(e)
        weights.append(jnp.sum(jnp.where(pick, aff, 0.0), axis=0, keepdims=True))
        s = jnp.where(pick, NEG, s)
    total = weights[0]
    for w in weights[1:]:
        total = total + w

    hot = jnp.where(s == NEG, 1.0, 0.0)
    r_i = lax.broadcasted_iota(I32, (tm, tm), 0)
    c_i = lax.broadcasted_iota(I32, (tm, tm), 1)
    earlier = jnp.where(r_i < c_i, 1.0, 0.0).astype(BF16)
    before = _dot(hot.astype(BF16), earlier) + run[...]
    ranks = [jnp.sum(jnp.where(row_f == e, before, 0.0), axis=0, keepdims=True) for e in experts]
    eidx_ref[0] = jnp.concatenate(experts, axis=0).astype(I32)
    wts_ref[0] = jnp.concatenate([ROUTE_SCALE * w / total for w in weights], axis=0)
    rank_ref[0] = jnp.concatenate(ranks, axis=0).astype(I32)
    run[...] = run[...] + jnp.sum(hot, axis=1, keepdims=True)
    cnt_out_ref[...] = run[...]


def _merge_router(o_attn, o_conv, x, gate, shift, scale, goa, goc, wout_bf, g2, rw_hi, rw_lo, rb,
                  cnt_in, tm):
    b, t, d = x.shape
    n_exp = rw_hi.shape[0]
    row = lambda n: pl.BlockSpec((1, tm, n), lambda i, j: (i, j, 0))
    pick = pl.BlockSpec((1, TOP_K, tm), lambda i, j: (i, 0, j))
    const = lambda shape: pl.BlockSpec(shape, lambda i, j: (0,) * len(shape))
    sds = lambda n, dt: jax.ShapeDtypeStruct((b, t, n), dt)
    picks = lambda dt: jax.ShapeDtypeStruct((b, TOP_K, t), dt)
    return pl.pallas_call(
        _merge_router_kernel,
        grid=(b, t // tm),
        in_specs=[row(o_attn.shape[-1]), row(o_conv.shape[-1]), row(d),
                  _mod_spec(gate, tm, d), _mod_spec(shift, tm, d), _mod_spec(scale, tm, d),
                  const(goa.shape), const(goc.shape), const(wout_bf.shape), const(g2.shape),
                  const(rw_hi.shape), const(rw_lo.shape), const(rb.shape), const(cnt_in.shape)],
        out_specs=[row(d), row(d), pick, pick, pick, const((n_exp, 1))],
        out_shape=[sds(d, F32), sds(d, F32), picks(I32), picks(F32), picks(I32),
                   jax.ShapeDtypeStruct((n_exp, 1), F32)],
        scratch_shapes=[pltpu.VMEM((n_exp, 1), F32)],
        compiler_params=_cparams("arbitrary", "arbitrary"),
        name="merge_router",
    )(o_attn, o_conv, x, gate, shift, scale, goa, goc, wout_bf, g2, rw_hi, rw_lo, rb, cnt_in)


def _row_copy(src_hbm, dst, src_row, dst_row, sem, chunks):
    return pltpu.make_async_copy(src_hbm.at[pl.ds(src_row * chunks, chunks)],
                                 dst.at[pl.ds(dst_row * chunks, chunks)], sem)


def _slot(start_ref, e_ref, r_ref, idx):
    return start_ref[e_ref[0, 0, idx]] + r_ref[0, 0, idx]


def _dispatch_kernel(cnt_ref, end_ref, start_ref, e_ref, r_ref, h_ref, xs_hbm, zbuf, zsem, sem,
                     *, tokens, rows, chunks, n_blocks):
    j = pl.program_id(0)
    n_exp = cnt_ref.shape[0]
    blk_rows = rows * chunks

    def zero_block(blk):
        return pltpu.make_async_copy(zbuf, xs_hbm.at[pl.ds(blk * blk_rows, blk_rows)], zsem)

    @pl.when(j == 0)
    def _():
        zbuf[...] = jnp.zeros(zbuf.shape, F32)
        n_active = end_ref[n_exp - 1] // rows

        def zero_tail(e, issued):
            partial = cnt_ref[e] % rows != 0

            @pl.when(partial)
            def _():
                zero_block(end_ref[e] // rows - 1).start()

            return issued + partial.astype(I32)

        def zero_unused(blk, _):
            zero_block(blk).start()
            return 0

        def drain_zero(_, c):
            zero_block(0).wait()
            return c

        issued = lax.fori_loop(0, n_exp, zero_tail, 0)
        lax.fori_loop(n_active, n_blocks, zero_unused, 0)
        lax.fori_loop(0, issued + (n_blocks - n_active), drain_zero, 0)

    def issue(r, _):
        for k in range(TOP_K):
            _row_copy(h_ref, xs_hbm, r, _slot(start_ref, e_ref, r_ref, r * TOP_K + k), sem, chunks).start(
                priority=k % DMA_PRIORITIES)
        return 0

    def drain(r, _):
        for k in range(TOP_K):
            _row_copy(h_ref, xs_hbm, 0, 0, sem, chunks).wait()
        return 0

    lax.fori_loop(0, tokens, issue, 0)
    lax.fori_loop(0, tokens, drain, 0)


def _dispatch(counts, pad_end, pad_start, e_idx, rank, h_rows, tokens, rows, chunks, n_blocks):
    n_tiles = e_idx.shape[0]
    picks = pl.BlockSpec((1, 1, tokens * TOP_K), lambda j, c, e, s: (j, 0, 0), memory_space=pltpu.SMEM)
    return pl.pallas_call(
        functools.partial(_dispatch_kernel, tokens=tokens, rows=rows, chunks=chunks, n_blocks=n_blocks),
        grid_spec=pltpu.PrefetchScalarGridSpec(
            num_scalar_prefetch=3,
            grid=(n_tiles,),
            in_specs=[picks, picks,
                      pl.BlockSpec((tokens * chunks, LANES), lambda j, c, e, s: (j, 0))],
            out_specs=pl.BlockSpec(memory_space=pl.ANY),
            scratch_shapes=[pltpu.VMEM((rows * chunks, LANES), F32),
                            pltpu.SemaphoreType.DMA(()), pltpu.SemaphoreType.DMA(())],
        ),
        out_shape=jax.ShapeDtypeStruct((n_blocks * rows * chunks, LANES), F32),
        compiler_params=_cparams("arbitrary"),
        name="moe_dispatch",
    )(counts, pad_end, pad_start, e_idx, rank, h_rows)


def _expert_kernel(be_ref, nact_ref, first_ref, slot_ref, next_ref, x_ref, wg_hbm, wu_hbm, wd_hbm, y_ref,
                   wg_buf, wu_buf, wd_buf, sems, *, rows, chunks):
    j = pl.program_id(0)

    def weight_copies(expert, slot):
        return [pltpu.make_async_copy(w_hbm.at[expert], buf.at[slot], sems.at[slot, i])
                for i, (w_hbm, buf) in enumerate(((wg_hbm, wg_buf), (wu_hbm, wu_buf), (wd_hbm, wd_buf)))]

    @pl.when(j < nact_ref[0])
    def _():
        slot = slot_ref[j]

        @pl.when(j == 0)
        def _():
            for cp in weight_copies(be_ref[0], 0):
                cp.start()

        @pl.when(first_ref[j] == 1)
        def _():
            for cp in weight_copies(be_ref[j], slot):
                cp.wait()

            @pl.when(next_ref[j] >= 0)
            def _():
                for cp in weight_copies(next_ref[j], 1 - slot):
                    cp.start()

        f = wg_buf.shape[2]
        gate = jnp.zeros((rows, f), F32)
        up = jnp.zeros((rows, f), F32)
        for c in range(0, chunks, 2):
            xc = jnp.concatenate([x_ref[pl.ds(c, rows, stride=chunks), :],
                                  x_ref[pl.ds(c + 1, rows, stride=chunks), :]], axis=1).astype(BF16)
            cs = pl.ds(c * LANES, 2 * LANES)
            gate = gate + _dot(xc, wg_buf[slot, cs, :].astype(BF16))
            up = up + _dot(xc, wu_buf[slot, cs, :].astype(BF16))
        h = (_silu(gate) * up).astype(BF16)
        y = _dot(h, wd_buf[slot].astype(BF16))
        for c in range(chunks):
            y_ref[pl.ds(c, rows, stride=chunks), :] = y[:, c * LANES:(c + 1) * LANES]


def _experts(blk_expert, n_active, pad_end, xs, wg, wu, wd, rows, chunks):
    n_blocks = blk_expert.shape[0]
    n_exp, d, f = wg.shape
    first = jnp.concatenate([jnp.ones((1,), I32), (blk_expert[1:] != blk_expert[:-1]).astype(I32)])
    slot = (jnp.cumsum(first) - 1) % 2
    run_end = pad_end[blk_expert] // rows
    nxt = jnp.where(run_end < n_active[0], blk_expert[jnp.minimum(run_end, n_blocks - 1)], -1)
    block = pl.BlockSpec((rows * chunks, LANES), lambda j, be, na, *_: (jnp.minimum(j, na[0] - 1), 0))
    hbm = pl.BlockSpec(memory_space=pl.ANY)
    n_prefetch = 5
    return pl.pallas_call(
        functools.partial(_expert_kernel, rows=rows, chunks=chunks),
        grid_spec=pltpu.PrefetchScalarGridSpec(
            num_scalar_prefetch=n_prefetch,
            grid=(n_blocks,),
            in_specs=[block, hbm, hbm, hbm],
            out_specs=block,
            scratch_shapes=[pltpu.VMEM((2, d, f), F32), pltpu.VMEM((2, d, f), F32),
                            pltpu.VMEM((2, f, d), F32), pltpu.SemaphoreType.DMA((2, 3))],
        ),
        out_shape=jax.ShapeDtypeStruct(xs.shape, F32),
        input_output_aliases={n_prefetch: 0},
        compiler_params=_cparams("arbitrary"),
        name="moe_experts",
    )(blk_expert, n_active, first, slot.astype(I32), nxt.astype(I32), xs, wg, wu, wd)


def _combine_kernel(start_ref, e_ref, r_ref, w_ref, x1_ref, h2_ref, gate_ref, wsg_ref, wsu_ref, wsd_ref,
                    gf_ref, ys_hbm, o_ref, buf, sem, *, chunks):
    tm = x1_ref.shape[1]

    def issue(r, _):
        for k in range(TOP_K):
            _row_copy(ys_hbm, buf.at[k], _slot(start_ref, e_ref, r_ref, r * TOP_K + k), r, sem, chunks).start(
                priority=k % DMA_PRIORITIES)
        return 0

    def drain(r, _):
        for k in range(TOP_K):
            _row_copy(ys_hbm, buf.at[k], 0, r, sem, chunks).wait()
        return 0

    lax.fori_loop(0, tm, issue, 0)
    h_bf = h2_ref[0].astype(BF16)
    hid = (_silu(_dot(h_bf, wsg_ref[...])) * _dot(h_bf, wsu_ref[...])).astype(BF16)
    shared = _dot(hid, wsd_ref[...])
    lax.fori_loop(0, tm, drain, 0)

    w = w_ref[0]
    cols = []
    for c in range(chunks):
        tot = jnp.zeros((tm, LANES), F32)
        for k in range(TOP_K):
            tot = tot + buf[k, pl.ds(c, tm, stride=chunks), :] * w[:, k:k + 1]
        cols.append(tot)
    routed = jnp.concatenate(cols, axis=1)
    x2 = x1_ref[0] + gate_ref[0] * (routed + shared)
    o_ref[0] = _rms(x2, gf_ref[...])


def _combine(pad_start, e_idx, rank, wts, x1, h2, gate, wsg_bf, wsu_bf, wsd_bf, gf, ys, tm, chunks):
    b, t, d = x1.shape
    nt = t // tm
    row = lambda n: pl.BlockSpec((1, tm, n), lambda i, j, *_: (i, j, 0))
    const = lambda shape: pl.BlockSpec(shape, lambda i, j, *_: (0,) * len(shape))
    picks = pl.BlockSpec((1, 1, tm * TOP_K), lambda i, j, *_: (i * nt + j, 0, 0), memory_space=pltpu.SMEM)
    return pl.pallas_call(
        functools.partial(_combine_kernel, chunks=chunks),
        grid_spec=pltpu.PrefetchScalarGridSpec(
            num_scalar_prefetch=1,
            grid=(b, nt),
            in_specs=[picks, picks, row(TOP_K), row(d), row(d), _mod_spec(gate, tm, d),
                      const(wsg_bf.shape), const(wsu_bf.shape), const(wsd_bf.shape), const(gf.shape),
                      pl.BlockSpec(memory_space=pl.ANY)],
            out_specs=row(d),
            scratch_shapes=[pltpu.VMEM((TOP_K, tm * chunks, LANES), F32), pltpu.SemaphoreType.DMA(())],
        ),
        out_shape=jax.ShapeDtypeStruct((b, t, d), F32),
        compiler_params=_cparams("arbitrary", "arbitrary"),
        name="moe_combine",
    )(pad_start, e_idx, rank, wts, x1, h2, gate, wsg_bf, wsu_bf, wsd_bf, gf, ys)


def _split_mod(mod, per_token):
    parts = jnp.split(mod, 6, axis=-1)
    if per_token:
        return [p[None] for p in parts]
    return [p[:, None, :] for p in parts]


def _padded_in_weight(w_in, conv_width):
    n_gate = 3 * N_HEADS
    o = ATTN_WIDTH + 6 * KV_COLS
    main = w_in[:, :o]
    gates = jnp.pad(w_in[:, o:o + n_gate], ((0, 0), (0, LANES - n_gate)))
    glu = w_in[:, o + n_gate:o + n_gate + 2 * conv_width]
    return jnp.concatenate([main, gates, glu], axis=1).astype(BF16)


def _cmp_rows(x):
    return x.reshape(x.shape[:-2] + (x.shape[-2] // CMP_BLOCK, CMP_BLOCK * KV_COLS))


def _largest_tile(n, cap):
    best = [k for k in range(SUBLANES, cap + 1, SUBLANES) if n % k == 0]
    assert best, (n, cap)
    return best[-1]


def _kv5(x):
    return x.reshape(x.shape[:-1] + (N_KV, HEAD_DIM))[None]


def kernel(x_prompt, x_sample, cache_k_cmp, cache_v_cmp, cache_k_slc, cache_v_slc, state_k_win, state_v_win, state_conv, page_table, c_prompt, c_sample, norm1_g, norm2_g, w_ada, b_ada, w_in, w_cmp_k, w_cmp_v, w_dw, b_dw, ln_conv_g, ln_conv_b, g_out_attn, g_out_conv, w_out, router_w, router_b, w_exp_gate, w_exp_up, w_exp_down, w_sh_gate, w_sh_up, w_sh_down, norm_f_g):
    assert w_ada.shape[0] == 1, "single layer"
    bp, t, d = x_prompt.shape
    bs, s_new, _ = x_sample.shape
    assert s_new == 1
    n_pool, page = cache_k_cmp.shape[1], cache_k_cmp.shape[2]
    n_pages = page_table.shape[1]
    past = n_pages * page
    conv_width = state_conv.shape[-1]
    n_exp = router_w.shape[-1]
    chunks = d // LANES
    tm = min(ROW_TILE, t)

    w_in_bf = _padded_in_weight(w_in[0], conv_width)
    wck = _compress_weight(w_cmp_k[0])
    wcv = _compress_weight(w_cmp_v[0])
    wout_bf = w_out[0].astype(BF16)
    rw_t = router_w[0].T
    rw_hi = rw_t.astype(BF16)
    rw_lo = (rw_t - rw_hi.astype(F32)).astype(BF16)
    wsg_bf, wsu_bf, wsd_bf = (w[0].astype(BF16) for w in (w_sh_gate, w_sh_up, w_sh_down))
    gf = norm_f_g[None, :]

    n_c = bp + bs
    c_all = jnp.concatenate([c_prompt, c_sample], axis=0)
    c_all = jnp.pad(c_all, ((0, (-n_c) % SUBLANES), (0, 0)))
    mod = _modulation(c_all, w_ada[0], b_ada)
    mp = _split_mod(mod[:bp], per_token=False)
    ms = _split_mod(mod[bp:n_c], per_token=True)

    (q_p, kc_p, vc_p, ks_p, vs_p, kw_p, vw_p, gate_p, u_p) = _in_proj(
        x_prompt, mp[0], mp[1], norm1_g, w_in_bf, tm)
    nc_p = t // CMP_BLOCK
    kcc, vcc = _compress(_cmp_rows(kc_p).reshape(bp * nc_p, -1), _cmp_rows(vc_p).reshape(bp * nc_p, -1),
                         wck, wcv)
    o_attn_p = _prompt_attention(q_p, gate_p, kcc.reshape(bp, nc_p, KV_COLS), vcc.reshape(bp, nc_p, KV_COLS),
                                 ks_p, vs_p, kw_p, vw_p)
    o_conv_p = _conv_prompt(u_p, w_dw[0], b_dw, ln_conv_g, ln_conv_b, tm)

    xs_row = x_sample.reshape(1, bs, d)
    (q_s, kc_s, vc_s, ks_s, vs_s, kw_s, vw_s, gate_s, u_s) = _in_proj(
        xs_row, ms[0], ms[1], norm1_g, w_in_bf, bs)
    q8 = q_s.reshape(bs, N_HEADS, HEAD_DIM)
    gates8 = gate_s[0, :, :3 * N_HEADS].reshape(bs, N_HEADS, 3)
    pages_t = lambda c: jnp.transpose(c[0], (0, 2, 3, 1))
    tail = (-(past + s_new)) % SLC_BLOCK
    n_new = (s_new + tail) // CMP_BLOCK
    tail_rows = lambda x: _cmp_rows(jnp.pad(x[0][:, None, :], ((0, 0), (0, tail), (0, 0)))).reshape(bs * n_new, -1)
    pad_rows = (-(bs * n_new)) % SUBLANES
    kc_new, vc_new = _compress(jnp.pad(tail_rows(kc_s), ((0, pad_rows), (0, 0))),
                               jnp.pad(tail_rows(vc_s), ((0, pad_rows), (0, 0))), wck, wcv)
    kc_new = kc_new[:bs * n_new].reshape(bs, n_new, KV_COLS)
    vc_new = vc_new[:bs * n_new].reshape(bs, n_new, KV_COLS)
    reps = page // CMP_BLOCK
    wk_fold = jnp.transpose(w_cmp_k[0], (2, 1, 0)).reshape(HEAD_DIM, HEAD_DIM * CMP_BLOCK)
    ut = _matmul3(q8.reshape(bs * N_HEADS, HEAD_DIM) * ATTN_SCALE, wk_fold)
    ut = jnp.tile(ut.reshape(bs, N_HEADS, HEAD_DIM, CMP_BLOCK), (1, 1, 1, reps))
    s_raw = _sample_scores(page_table, ut, pages_t(cache_k_cmp))
    p_exp, p_new, sel = _sample_select(s_raw, q8, kc_new, past)
    y_acc = _sample_values(page_table, p_exp, pages_t(cache_v_cmp))
    wv_fold = jnp.tile(jnp.transpose(w_cmp_v[0], (1, 0, 2)), (1, reps, 1)).reshape(HEAD_DIM * page, HEAD_DIM)
    o_cmp_s = _matmul3(y_acc.reshape(bs * N_HEADS, HEAD_DIM * page), wv_fold).reshape(bs, N_HEADS, HEAD_DIM)
    n_sel = min(N_SEL, (past // CMP_BLOCK + n_new) // CMP_PER_SLC)
    sel = sel[:, :N_KV, :n_sel]
    row3 = lambda x: x[0][:, None, :]
    o_attn_s = _sample_attend(
        sel, page_table, q8, pages_t(cache_k_slc), pages_t(cache_v_slc), row3(ks_s), row3(vs_s),
        pages_t(state_k_win), pages_t(state_v_win), row3(kw_s), row3(vw_s), gates8, o_cmp_s, p_new, vc_new,
        past)
    o_attn_s = o_attn_s.reshape(1, bs, ATTN_WIDTH)
    up_s = jnp.concatenate([state_conv[0], u_s[0][:, None, :]], axis=1)
    o_conv_s = _conv_sample(up_s, w_dw[0], b_dw, ln_conv_g, ln_conv_b)[None]

    router = functools.partial(_merge_router, goa=g_out_attn, goc=g_out_conv, wout_bf=wout_bf, g2=norm2_g,
                               rw_hi=rw_hi, rw_lo=rw_lo, rb=router_b[0][:, None])
    x1_p, h2_p, e_p, w_p, r_p, cnt = router(o_attn_p, o_conv_p, x_prompt, mp[2], mp[3], mp[4],
                                            cnt_in=jnp.zeros((n_exp, 1), F32), tm=tm)
    x1_s, h2_s, e_s, w_s, r_s, cnt = router(o_attn_s, o_conv_s, xs_row, ms[2], ms[3], ms[4],
                                            cnt_in=cnt, tm=bs)

    n_tok = bp * t + bs
    counts = cnt[:, 0].astype(I32)
    padded = (counts + MOE_ROWS - 1) // MOE_ROWS * MOE_ROWS
    pad_end = jnp.cumsum(padded)
    pad_end = pad_end.astype(I32)
    pad_start = pad_end - padded
    n_blocks = -(-(n_tok * TOP_K) // MOE_ROWS) + n_exp
    blk_first = jnp.arange(n_blocks, dtype=I32) * MOE_ROWS
    blk_expert = jnp.minimum(jnp.sum(pad_end[None, :] <= blk_first[:, None], axis=1), n_exp - 1).astype(I32)
    n_active = pad_end[-1:] // MOE_ROWS
    picks = lambda a: jnp.transpose(a, (0, 2, 1))
    w_p, w_s = picks(w_p), picks(w_s)
    e_all = jnp.concatenate([picks(e_p).reshape(-1, TOP_K), picks(e_s).reshape(-1, TOP_K)], axis=0)
    r_all = jnp.concatenate([picks(r_p).reshape(-1, TOP_K), picks(r_s).reshape(-1, TOP_K)], axis=0)

    tile = _largest_tile(n_tok, 512)
    h_rows = jnp.concatenate([h2_p.reshape(-1, d), h2_s.reshape(-1, d)], axis=0).reshape(n_tok * chunks, LANES)
    xs = _dispatch(counts, pad_end, pad_start, e_all.reshape(n_tok // tile, 1, tile * TOP_K),
                   r_all.reshape(n_tok // tile, 1, tile * TOP_K), h_rows, tile, MOE_ROWS, chunks, n_blocks)
    ys = _experts(blk_expert, n_active, pad_end, xs, w_exp_gate[0], w_exp_up[0], w_exp_down[0], MOE_ROWS, chunks)

    comb = functools.partial(_combine, pad_start, wsg_bf=wsg_bf, wsu_bf=wsu_bf, wsd_bf=wsd_bf, gf=gf, ys=ys,
                             chunks=chunks)
    tiles_p = (bp * (t // tm), 1, tm * TOP_K)
    y_prompt = comb(picks(e_p).reshape(tiles_p), picks(r_p).reshape(tiles_p), w_p, x1_p, h2_p, mp[5], tm=tm)
    y_sample = comb(picks(e_s).reshape(1, 1, bs * TOP_K), picks(r_s).reshape(1, 1, bs * TOP_K), w_s, x1_s, h2_s,
                    ms[5], tm=bs).reshape(bs, 1, d)

    win = min(WINDOW, t)
    hist = state_conv.shape[2]
    out_p = [_kv5(a) for a in (kc_p, vc_p, ks_p, vs_p, kw_p[:, t - win:], vw_p[:, t - win:])]
    conv_p = u_p[:, t - hist:][None]
    out_s = [_kv5(a[0][:, None, :]) for a in (kc_s, vc_s, ks_s, vs_s)]
    w_buf = state_k_win.shape[2]
    kw_buf = jnp.concatenate([state_k_win, _kv5(kw_s[0][:, None, :])], axis=2)[:, :, -w_buf:]
    vw_buf = jnp.concatenate([state_v_win, _kv5(vw_s[0][:, None, :])], axis=2)[:, :, -w_buf:]
    conv_s = up_s[:, -hist:][None]
    return (y_prompt, y_sample, *out_p, conv_p, *out_s, kw_buf, vw_buf, conv_s)
```

```python
import functools

import jax
import jax.numpy as jnp
from jax import lax
from jax.experimental import pallas as pl
from jax.experimental.pallas import tpu as pltpu

F32 = jnp.float32
BF16 = jnp.bfloat16
I32 = jnp.int32

N_HEADS = 8
HEAD_DIM = 64
N_KV = 2
Q_PER_KV = N_HEADS // N_KV
ATTN_WIDTH = N_HEADS * HEAD_DIM
KV_COLS = N_KV * HEAD_DIM
CMP_BLOCK = 32
SLC_BLOCK = 64
CMP_PER_SLC = SLC_BLOCK // CMP_BLOCK
N_SEL = 16
WINDOW = 512
TOP_K = 8
ROUTE_SCALE = 2.5
EPS = 1e-6
FORCED = 1e4
NEG = -1e30
ATTN_SCALE = HEAD_DIM ** -0.5
PICKED = -2.0
NOT_A_BLOCK = -4.0
NO_INDEX = 1e9

LANES = 128
SUBLANES = 8
VMEM_LIMIT = 56 * 1024 * 1024
DMA_PRIORITIES = 2

ROW_TILE = 256
Q_TILE = 256
KEY_TILE = 1024
CMP_ROW_TILE = 512
MOE_ROWS = 256
CONV_HALO = 32
POS_RADIX = 256
CONV_ROWS = 32
PAGES_PER_STEP = 32


def _cparams(*sem):
    return pltpu.CompilerParams(dimension_semantics=sem, vmem_limit_bytes=VMEM_LIMIT)


def _dot(a, b):
    return jnp.dot(a, b, preferred_element_type=F32)


def _dot_nt(a, b):
    return lax.dot_general(a, b, (((1,), (1,)), ((), ())), preferred_element_type=F32)


def _dot_tn(a, b):
    return lax.dot_general(a, b, (((0,), (0,)), ((), ())), preferred_element_type=F32)


def _split2(x):
    hi = x.astype(BF16)
    lo = (x - hi.astype(F32)).astype(BF16)
    return hi, lo


def _dot3(a, b):
    ah, al = _split2(a)
    bh, bl = _split2(b)
    return _dot(ah, bh) + (_dot(ah, bl) + _dot(al, bh))


def _dot3_nt(a, b):
    ah, al = _split2(a)
    bh, bl = _split2(b)
    return _dot_nt(ah, bh) + (_dot_nt(ah, bl) + _dot_nt(al, bh))


def _sigmoid(x):
    return 1.0 / (1.0 + jnp.exp(-x))


def _silu(x):
    return x * _sigmoid(x)


def _rms(x, g):
    return x * lax.rsqrt(jnp.mean(x * x, axis=-1, keepdims=True) + EPS) * g


def _alibi_slope_col(rows, rows_per_head, first_head, n_heads):
    r = lax.broadcasted_iota(I32, (rows, 1), 0) // rows_per_head
    out = jnp.zeros((rows, 1), F32)
    for k in range(n_heads):
        out = jnp.where(r == k, 2.0 ** (-8.0 * (first_head + k + 1) / N_HEADS), out)
    return out


def _modulation_kernel(c_ref, w_ref, b_ref, o_ref):
    o_ref[...] = _dot3(c_ref[...], w_ref[...]) + b_ref[...]


def _modulation(c, w, b):
    m, d = c.shape
    n = w.shape[1]
    tn = 768
    return pl.pallas_call(
        _modulation_kernel,
        grid=(n // tn,),
        in_specs=[pl.BlockSpec((m, d), lambda j: (0, 0)),
                  pl.BlockSpec((d, tn), lambda j: (0, j)),
                  pl.BlockSpec((1, tn), lambda j: (0, j))],
        out_specs=pl.BlockSpec((m, tn), lambda j: (0, j)),
        out_shape=jax.ShapeDtypeStruct((m, n), F32),
        compiler_params=_cparams("arbitrary"),
        name="modulation",
    )(c, w, b)


def _mod_spec(mod, tm, d):
    if mod.shape[1] == 1:
        return pl.BlockSpec((1, 1, d), lambda i, j, *_: (i, 0, 0))
    return pl.BlockSpec((1, tm, d), lambda i, j, *_: (i, j, 0))


def _in_proj_kernel(x_ref, shift_ref, scale_ref, g_ref, w_ref,
                    q_ref, kc_ref, vc_ref, ks_ref, vs_ref, kw_ref, vw_ref, gate_ref, u_ref):
    x = x_ref[0]
    h = _rms(x, g_ref[...]) * (1.0 + scale_ref[0]) + shift_ref[0]
    z = _dot(h.astype(BF16), w_ref[...])
    q_ref[0] = z[:, :ATTN_WIDTH]
    o = ATTN_WIDTH
    for ref in (kc_ref, vc_ref, ks_ref, vs_ref, kw_ref, vw_ref):
        ref[0] = z[:, o:o + KV_COLS]
        o += KV_COLS
    gate_ref[0] = _sigmoid(z[:, o:o + LANES])
    o += LANES
    cw = u_ref.shape[-1]
    u_ref[0] = z[:, o:o + cw] * _sigmoid(z[:, o + cw:o + 2 * cw])


def _in_proj(x, shift, scale, g, w_bf, tm):
    b, t, d = x.shape
    cw = (w_bf.shape[1] - ATTN_WIDTH - 6 * KV_COLS - LANES) // 2
    row = lambda n: pl.BlockSpec((1, tm, n), lambda i, j: (i, j, 0))
    sds = lambda n: jax.ShapeDtypeStruct((b, t, n), F32)
    return pl.pallas_call(
        _in_proj_kernel,
        grid=(b, t // tm),
        in_specs=[row(d), _mod_spec(shift, tm, d), _mod_spec(scale, tm, d),
                  pl.BlockSpec((1, d), lambda i, j: (0, 0)),
                  pl.BlockSpec(w_bf.shape, lambda i, j: (0, 0))],
        out_specs=[row(ATTN_WIDTH)] + [row(KV_COLS)] * 6 + [row(LANES), row(cw)],
        out_shape=[sds(ATTN_WIDTH)] + [sds(KV_COLS)] * 6 + [sds(LANES), sds(cw)],
        compiler_params=_cparams("arbitrary", "arbitrary"),
        name="in_proj",
    )(x, shift, scale, g, w_bf)


def _compress_kernel(k_ref, v_ref, wk_ref, wv_ref, ko_ref, vo_ref):
    ko_ref[...] = _dot3(k_ref[...], wk_ref[...])
    vo_ref[...] = _dot3(v_ref[...], wv_ref[...])


def _compress(k_rows, v_rows, wk, wv):
    r, kdim = k_rows.shape
    tr = min(CMP_ROW_TILE, r)
    assert r % tr == 0
    rows = pl.BlockSpec((tr, kdim), lambda i: (i, 0))
    wspec = pl.BlockSpec((kdim, KV_COLS), lambda i: (0, 0))
    ospec = pl.BlockSpec((tr, KV_COLS), lambda i: (i, 0))
    return pl.pallas_call(
        _compress_kernel,
        grid=(r // tr,),
        in_specs=[rows, rows, wspec, wspec],
        out_specs=[ospec, ospec],
        out_shape=[jax.ShapeDtypeStruct((r, KV_COLS), F32)] * 2,
        compiler_params=_cparams("arbitrary"),
        name="compress",
    )(k_rows, v_rows, wk, wv)


def _compress_weight(w):
    eye = jnp.eye(N_KV, dtype=w.dtype)
    big = jnp.einsum('lde,gh->lgdhe', w, eye)
    return big.reshape(CMP_BLOCK * KV_COLS, KV_COLS)


def _pair_sum(x, axis):
    n = x.shape[axis]
    idx = lax.broadcasted_iota(I32, x.shape, axis)
    nxt = pltpu.roll(x, n - 1, axis)
    prv = pltpu.roll(x, 1, axis)
    return x + jnp.where((idx & 1) == 0, nxt, prv)


def _block_scores(imp, blk, q_pos, n_blocks_total):
    cur = q_pos // SLC_BLOCK
    valid = jnp.logical_and(blk * SLC_BLOCK <= q_pos, blk < n_blocks_total)
    forced = jnp.logical_or(blk == 0, jnp.logical_or(blk == cur, blk == cur - 1))
    return jnp.where(valid, jnp.where(forced, FORCED, imp), -1.0)


def _select_blocks(score, blk, n_sel):
    blk_f = blk.astype(F32)
    s = score
    for _ in range(n_sel):
        m = jnp.max(s, axis=0, keepdims=True)
        first = jnp.min(jnp.where(s == m, blk_f, NO_INDEX), axis=0, keepdims=True)
        s = jnp.where(blk_f == first, PICKED, s)
    return jnp.where(jnp.logical_and(s == PICKED, score >= 0.0), 1.0, 0.0)


def _position_features(n):
    pos = jnp.arange(n, dtype=I32)[:, None]
    lane = jnp.arange(HEAD_DIM, dtype=I32)[None, :]
    feat = jnp.where(lane < 2, 1, jnp.where(lane == 2, pos // POS_RADIX, jnp.where(lane == 3, pos % POS_RADIX, 0)))
    return feat.astype(BF16)


def _query_position_features(q_pos, slope):
    lane = lax.broadcasted_iota(I32, (q_pos.shape[0], HEAD_DIM), 1)
    hi = (q_pos // POS_RADIX).astype(F32) * (-slope * POS_RADIX)
    lo = (q_pos % POS_RADIX).astype(F32) * (-slope)
    return jnp.where(lane == 0, hi, jnp.where(lane == 1, lo, jnp.where(
        lane == 2, slope * POS_RADIX, jnp.where(lane == 3, slope, 0.0))))


def _prompt_attn_kernel(q_ref, gate_ref, kc_ref, vc_ref, ks_ref, vs_ref, kw_ref, vw_ref, kx_ref, o_ref,
                        *, seq, n_sel):
    i = pl.program_id(1)
    tq = Q_TILE
    nc = kc_ref.shape[1]
    q_blk = q_ref[0] * ATTN_SCALE
    gates = gate_ref[0]
    slopes = [2.0 ** (-8.0 * (h + 1) / N_HEADS) for h in range(N_HEADS)]
    group_heads = [list(range(g * Q_PER_KV, (g + 1) * Q_PER_KV)) for g in range(N_KV)]
    gsl = [slice(g * HEAD_DIM, (g + 1) * HEAD_DIM) for g in range(N_KV)]
    rsl = [slice(r * tq, (r + 1) * tq) for r in range(Q_PER_KV)]
    q_pos_col = i * tq + lax.broadcasted_iota(I32, (tq, 1), 0)
    q_pos_row = i * tq + lax.broadcasted_iota(I32, (1, tq), 1)
    q_heads = [q_blk[:, h * HEAD_DIM:(h + 1) * HEAD_DIM] for h in range(N_HEADS)]
    qg = [jnp.concatenate([q_heads[h] for h in hs], axis=0) for hs in group_heads]
    qx_bf = [jnp.concatenate([jnp.concatenate([q_heads[h], _query_position_features(q_pos_col, slopes[h])],
                                              axis=1) for h in hs], axis=0).astype(BF16)
             for hs in group_heads]

    cmp_row = lax.broadcasted_iota(I32, (nc, tq), 0)
    dist_c = (q_pos_row - (cmp_row * CMP_BLOCK + (CMP_BLOCK - 1))).astype(F32)
    mask_c = dist_c >= 0.0
    blk = cmp_row >> 1
    o_cmp = [None] * N_HEADS
    sel_bf = []
    for g, hs in enumerate(group_heads):
        vc_bf = vc_ref[0][:, gsl[g]].astype(BF16)
        qk = _dot3_nt(kc_ref[0][:, gsl[g]], qg[g])
        imp = jnp.zeros((nc, tq), F32)
        for r, h in enumerate(hs):
            s = jnp.where(mask_c, qk[:, rsl[r]] - slopes[h] * dist_c, NEG)
            m = jnp.max(s, axis=0, keepdims=True)
            p = jnp.where(mask_c, jnp.exp(s - m), 0.0)
            p = p / jnp.maximum(jnp.sum(p, axis=0, keepdims=True), 1e-30)
            o_cmp[h] = _dot_tn(p.astype(BF16), vc_bf)
            imp = imp + p
        score = _block_scores(_pair_sum(imp, 0), blk, q_pos_row, seq // SLC_BLOCK)
        sel_bf.append(_select_blocks(score, blk, n_sel).astype(BF16))

    span = WINDOW + tq
    w_start = pl.multiple_of(jnp.maximum(i * tq - WINDOW, 0), tq)
    dist_w = (q_pos_col - (w_start + lax.broadcasted_iota(I32, (tq, span), 1))).astype(F32)
    bias_w = jnp.where(jnp.logical_and(dist_w >= 0.0, dist_w <= float(WINDOW)), 0.0, NEG)
    o_win = [None] * N_HEADS
    kx_w = kx_ref[pl.ds(w_start, span), :]
    for g, hs in enumerate(group_heads):
        kw_bf = kw_ref[0, pl.ds(w_start, span), :][:, gsl[g]].astype(BF16)
        vw_bf = vw_ref[0, pl.ds(w_start, span), :][:, gsl[g]].astype(BF16)
        s_all = _dot_nt(qx_bf[g], jnp.concatenate([kw_bf, kx_w], axis=1))
        probs, sums = [], []
        for r, h in enumerate(hs):
            s = s_all[rsl[r]] + bias_w
            p = jnp.exp(s - jnp.max(s, axis=1, keepdims=True))
            sums.append(jnp.sum(p, axis=1, keepdims=True))
            probs.append(p.astype(BF16))
        o_all = _dot(jnp.concatenate(probs, axis=0), vw_bf)
        for r, h in enumerate(hs):
            o_win[h] = o_all[rsl[r]] / sums[r]

    n_tiles = ((i + 1) * tq + KEY_TILE - 1) // KEY_TILE

    def slc_step(t, carry):
        ms, ls, accs = (list(c) for c in carry)
        k0 = pl.multiple_of(t * KEY_TILE, KEY_TILE)
        dist = (q_pos_col - (k0 + lax.broadcasted_iota(I32, (tq, KEY_TILE), 1))).astype(F32)
        causal = dist >= 0.0
        key_cmp = (k0 + lax.broadcasted_iota(I32, (nc, KEY_TILE), 1)) // CMP_BLOCK
        expand = jnp.where(key_cmp == lax.broadcasted_iota(I32, (nc, KEY_TILE), 0), 1.0, 0.0).astype(BF16)
        kx_t = kx_ref[pl.ds(k0, KEY_TILE), :]
        for g, hs in enumerate(group_heads):
            kt_bf = ks_ref[0, pl.ds(k0, KEY_TILE), :][:, gsl[g]].astype(BF16)
            vt_bf = vs_ref[0, pl.ds(k0, KEY_TILE), :][:, gsl[g]].astype(BF16)
            chosen = _dot_tn(sel_bf[g], expand)
            bias = jnp.where(jnp.logical_and(causal, chosen > 0.5), 0.0, NEG)
            s_all = _dot_nt(qx_bf[g], jnp.concatenate([kt_bf, kx_t], axis=1))
            probs, alphas = [], []
            for r, h in enumerate(hs):
                s = s_all[rsl[r]] + bias
                m_new = jnp.maximum(ms[h], jnp.max(s, axis=1, keepdims=True))
                alpha = jnp.exp(ms[h] - m_new)
                p = jnp.exp(s - m_new)
                ls[h] = alpha * ls[h] + jnp.sum(p, axis=1, keepdims=True)
                ms[h] = m_new
                alphas.append(alpha)
                probs.append(p.astype(BF16))
            pv = _dot(jnp.concatenate(probs, axis=0), vt_bf)
            for r, h in enumerate(hs):
                accs[h] = alphas[r] * accs[h] + pv[rsl[r]]
        return tuple(ms), tuple(ls), tuple(accs)

    init = (tuple(jnp.full((tq, 1), NEG, F32) for _ in range(N_HEADS)),
            tuple(jnp.zeros((tq, 1), F32) for _ in range(N_HEADS)),
            tuple(jnp.zeros((tq, HEAD_DIM), F32) for _ in range(N_HEADS)))
    _, l_s, acc_s = lax.fori_loop(0, n_tiles, slc_step, init)

    pieces = []
    for h in range(N_HEADS):
        o_slc = acc_s[h] / jnp.maximum(l_s[h], 1e-30)
        pieces.append(o_cmp[h] * gates[:, 3 * h + 0:3 * h + 1] + o_slc * gates[:, 3 * h + 1:3 * h + 2]
                      + o_win[h] * gates[:, 3 * h + 2:3 * h + 3])
    o_ref[0] = jnp.concatenate(pieces, axis=1)


def _prompt_attention(q, gates, kc, vc, ks, vs, kw, vw):
    b, t, _ = q.shape
    nc = kc.shape[1]
    assert t % KEY_TILE == 0 and t >= WINDOW + Q_TILE
    n_sel = min(N_SEL, t // SLC_BLOCK)
    qspec = lambda n: pl.BlockSpec((1, Q_TILE, n), lambda bi, i: (bi, i, 0))
    full = lambda r: pl.BlockSpec((1, r, KV_COLS), lambda bi, i: (bi, 0, 0))
    assert t <= POS_RADIX * POS_RADIX
    kx = _position_features(t)
    return pl.pallas_call(
        functools.partial(_prompt_attn_kernel, seq=t, n_sel=n_sel),
        grid=(b, t // Q_TILE),
        in_specs=[qspec(ATTN_WIDTH), qspec(LANES), full(nc), full(nc),
                  full(t), full(t), full(t), full(t), pl.BlockSpec(kx.shape, lambda bi, i: (0, 0))],
        out_specs=qspec(ATTN_WIDTH),
        out_shape=jax.ShapeDtypeStruct((b, t, ATTN_WIDTH), F32),
        compiler_params=_cparams("arbitrary", "arbitrary"),
        name="prompt_attention",
    )(q, gates, kc, vc, ks, vs, kw, vw, kx)


def _merge_groups(per_group):
    row = lax.broadcasted_iota(I32, per_group[0].shape, 0) // Q_PER_KV
    out = per_group[0]
    for g in range(1, N_KV):
        out = jnp.where(row == g, per_group[g], out)
    return out


def _group_slice(x, g):
    return x[:, g * HEAD_DIM:(g + 1) * HEAD_DIM]


def _matmul3_kernel(a_ref, b_ref, o_ref):
    o_ref[...] = _dot3(a_ref[...], b_ref[...])


def _matmul3(a, b):
    return pl.pallas_call(
        _matmul3_kernel,
        out_shape=jax.ShapeDtypeStruct((a.shape[0], b.shape[1]), F32),
        compiler_params=pltpu.CompilerParams(vmem_limit_bytes=VMEM_LIMIT),
        name="matmul3",
    )(a, b)


def _page_specs(n_pages, page):
    def spec(o):
        return pl.BlockSpec((1, N_KV, HEAD_DIM, page),
                            lambda i, j, pt: (pt[i * n_pages + j * PAGES_PER_STEP + o], 0, 0, 0))
    return [spec(o) for o in range(PAGES_PER_STEP)]


def _sample_scores_kernel(pt_ref, ut_ref, *refs):
    k_refs, o_ref = refs[:-1], refs[-1]
    for h in range(N_HEADS):
        g = h // Q_PER_KV
        u = ut_ref[0, h]
        rows = [jnp.sum(k_ref[0, g] * u, axis=0, keepdims=True) for k_ref in k_refs]
        o_ref[0, h] = jnp.concatenate(rows, axis=0)


def _sample_scores(page_table, ut, k_pages):
    b, n_pages = page_table.shape
    page = k_pages.shape[-1]
    assert n_pages % PAGES_PER_STEP == 0
    return pl.pallas_call(
        _sample_scores_kernel,
        grid_spec=pltpu.PrefetchScalarGridSpec(
            num_scalar_prefetch=1,
            grid=(b, n_pages // PAGES_PER_STEP),
            in_specs=[pl.BlockSpec((1, N_HEADS, HEAD_DIM, page), lambda i, j, pt: (i, 0, 0, 0))]
                     + _page_specs(n_pages, page),
            out_specs=pl.BlockSpec((1, N_HEADS, PAGES_PER_STEP, page), lambda i, j, pt: (i, 0, j, 0)),
        ),
        out_shape=jax.ShapeDtypeStruct((b, N_HEADS, n_pages, page), F32),
        compiler_params=_cparams("arbitrary", "arbitrary"),
        name="sample_scores",
    )(page_table.reshape(-1), ut, *([k_pages] * PAGES_PER_STEP))


def _max_all(x):
    return jnp.max(jnp.max(x, axis=0, keepdims=True), axis=1, keepdims=True)


def _min_all(x):
    return jnp.min(jnp.min(x, axis=0, keepdims=True), axis=1, keepdims=True)


def _sum_all(x):
    return jnp.sum(jnp.sum(x, axis=0, keepdims=True), axis=1, keepdims=True)


def _sample_select_kernel(s_ref, q_ref, kcn_ref, pexp_ref, pnew_ref, sel_ref, *, past, n_new, n_sel):
    n_pages, page = s_ref.shape[2], s_ref.shape[3]
    cpp = page // CMP_BLOCK
    n_past = n_pages * cpp
    n_blocks_total = (n_past + n_new) // CMP_PER_SLC
    lane = lax.broadcasted_iota(I32, (n_pages, page), 1)
    prow = lax.broadcasted_iota(I32, (n_pages, page), 0)
    dist = (past - ((prow * cpp + lane // CMP_BLOCK) * CMP_BLOCK + (CMP_BLOCK - 1))).astype(F32)
    mask = jnp.logical_and(lane % CMP_BLOCK == 0, dist >= 0.0)

    q8 = q_ref[0] * ATTN_SCALE
    slope = _alibi_slope_col(N_HEADS, 1, 0, N_HEADS)
    kcn = jnp.concatenate([kcn_ref[0], jnp.zeros((LANES - n_new, KV_COLS), F32)], axis=0)
    new_lane = lax.broadcasted_iota(I32, (N_HEADS, LANES), 1)
    dist_n = (past - ((n_past + new_lane) * CMP_BLOCK + (CMP_BLOCK - 1))).astype(F32)
    mask_n = jnp.logical_and(dist_n >= 0.0, new_lane < n_new)
    qk_n = _merge_groups([_dot3_nt(q8, _group_slice(kcn, g)) for g in range(N_KV)])
    s_new = jnp.where(mask_n, qk_n - slope * dist_n, NEG)

    halvings = [CMP_BLOCK >> k for k in range(1, CMP_BLOCK.bit_length())]
    probs, probs_new = [], []
    for h in range(N_HEADS):
        x = s_ref[0, h]
        for sh in halvings:
            x = x + pltpu.roll(x, page - sh, 1)
        s = jnp.where(mask, x - 2.0 ** (-8.0 * (h + 1) / N_HEADS) * dist, NEG)
        sn = s_new[h:h + 1, :]
        mn = jnp.logical_and(dist_n[h:h + 1, :] >= 0.0, new_lane[h:h + 1, :] < n_new)
        m = jnp.maximum(_max_all(s), jnp.max(sn, axis=1, keepdims=True))
        p = jnp.where(mask, jnp.exp(s - m), 0.0)
        pn = jnp.where(mn, jnp.exp(sn - m), 0.0)
        den = jnp.maximum(_sum_all(p) + jnp.sum(pn, axis=1, keepdims=True), 1e-30)
        p = p / den
        probs.append(p)
        probs_new.append(pn / den)
        z = p
        for sh in reversed(halvings):
            z = z + pltpu.roll(z, sh, 1)
        pexp_ref[0, h] = z
    pnew_ref[0] = jnp.concatenate(probs_new, axis=0)

    row1 = lax.broadcasted_iota(I32, (1, LANES), 1)
    blk = jnp.where(lane % SLC_BLOCK == 0, prow * (page // SLC_BLOCK) + lane // SLC_BLOCK, -1)
    blk_n = jnp.where(row1 < n_new, n_past // CMP_PER_SLC + (row1 >> 1), -1)
    blk_f = blk.astype(F32)
    blk_nf = blk_n.astype(F32)
    out_lane = lax.broadcasted_iota(I32, (N_HEADS, LANES), 1)
    out_row = lax.broadcasted_iota(I32, (N_HEADS, LANES), 0)
    out = jnp.full((N_HEADS, LANES), -1, I32)
    for g in range(N_KV):
        imp = probs[g * Q_PER_KV]
        imp_n = probs_new[g * Q_PER_KV]
        for r in range(1, Q_PER_KV):
            imp = imp + probs[g * Q_PER_KV + r]
            imp_n = imp_n + probs_new[g * Q_PER_KV + r]
        imp = imp + pltpu.roll(imp, page - CMP_BLOCK, 1)
        s_m = jnp.where(blk >= 0, _block_scores(imp, blk, past, n_blocks_total), NOT_A_BLOCK)
        s_n = jnp.where(blk_n >= 0, _block_scores(_pair_sum(imp_n, 1), blk_n, past, n_blocks_total), NOT_A_BLOCK)
        for j in range(n_sel):
            top = jnp.maximum(_max_all(s_m), jnp.max(s_n, axis=1, keepdims=True))
            first = jnp.minimum(_min_all(jnp.where(s_m == top, blk_f, NO_INDEX)),
                                jnp.min(jnp.where(s_n == top, blk_nf, NO_INDEX), axis=1, keepdims=True))
            s_m = jnp.where(blk_f == first, PICKED, s_m)
            s_n = jnp.where(blk_nf == first, PICKED, s_n)
            pick = jnp.where(top >= 0.0, first.astype(I32), -1)
            out = jnp.where(jnp.logical_and(out_row == g, out_lane == j), pick, out)
    sel_ref[0] = out


def _sample_select(s_raw, q8, kc_new, past):
    b, _, n_pages, page = s_raw.shape
    n_new = kc_new.shape[1]
    assert CMP_PER_SLC == 2 and CMP_BLOCK == 32
    n_sel = min(N_SEL, (past // CMP_BLOCK + n_new) // CMP_PER_SLC)
    per_b = lambda *s: pl.BlockSpec((1,) + s, lambda i: (i,) + (0,) * len(s))
    return pl.pallas_call(
        functools.partial(_sample_select_kernel, past=past, n_new=n_new, n_sel=n_sel),
        grid=(b,),
        in_specs=[per_b(N_HEADS, n_pages, page), per_b(N_HEADS, HEAD_DIM), per_b(n_new, KV_COLS)],
        out_specs=[per_b(N_HEADS, n_pages, page), per_b(N_HEADS, LANES), per_b(N_HEADS, LANES)],
        out_shape=[jax.ShapeDtypeStruct((b, N_HEADS, n_pages, page), F32),
                   jax.ShapeDtypeStruct((b, N_HEADS, LANES), F32),
                   jax.ShapeDtypeStruct((b, N_HEADS, LANES), I32)],
        compiler_params=_cparams("arbitrary"),
        name="sample_select",
    )(s_raw, q8, kc_new)


def _sample_values_kernel(pt_ref, pe_ref, *refs):
    v_refs, y_ref = refs[:-1], refs[-1]

    @pl.when(pl.program_id(1) == 0)
    def _():
        y_ref[...] = jnp.zeros(y_ref.shape, F32)

    for g in range(N_KV):
        heads = range(g * Q_PER_KV, (g + 1) * Q_PER_KV)
        pe = [pe_ref[0, h] for h in heads]
        acc = [jnp.zeros(y_ref.shape[2:], F32) for _ in heads]
        for o, v_ref in enumerate(v_refs):
            v = v_ref[0, g]
            for r in range(Q_PER_KV):
                acc[r] = acc[r] + v * pe[r][o:o + 1, :]
        for r, h in enumerate(heads):
            y_ref[0, h] = y_ref[0, h] + acc[r]


def _sample_values(page_table, pexp, v_pages):
    b, n_pages = page_table.shape
    page = v_pages.shape[-1]
    return pl.pallas_call(
        _sample_values_kernel,
        grid_spec=pltpu.PrefetchScalarGridSpec(
            num_scalar_prefetch=1,
            grid=(b, n_pages // PAGES_PER_STEP),
            in_specs=[pl.BlockSpec((1, N_HEADS, PAGES_PER_STEP, page), lambda i, j, pt: (i, 0, j, 0))]
                     + _page_specs(n_pages, page),
            out_specs=pl.BlockSpec((1, N_HEADS, HEAD_DIM, page), lambda i, j, pt: (i, 0, 0, 0)),
        ),
        out_shape=jax.ShapeDtypeStruct((b, N_HEADS, HEAD_DIM, page), F32),
        compiler_params=_cparams("arbitrary", "arbitrary"),
        name="sample_values",
    )(page_table.reshape(-1), pexp, *([v_pages] * PAGES_PER_STEP))


def _new_token_terms(q8, k_row, v_row):
    s = _merge_groups([jnp.sum(q8 * _group_slice(k_row, g), axis=1, keepdims=True) for g in range(N_KV)])
    v = _merge_groups([jnp.broadcast_to(_group_slice(v_row, g), (N_HEADS, HEAD_DIM)) for g in range(N_KV)])
    return s, v


def _sample_attend_kernel(sel_ref, pt_ref, q_ref, *refs, past, n_sel, ns_past):
    page_refs = refs[:4 * n_sel]
    (ksn_ref, vsn_ref, kw_ref, vw_ref, kwn_ref, vwn_ref, gate_ref, ocmp_ref, pnew_ref, vcn_ref,
     o_ref) = refs[4 * n_sel:]
    b = pl.program_id(0)
    page = page_refs[0].shape[-1]
    spp = page // SLC_BLOCK
    q8 = q_ref[0] * ATTN_SCALE
    q8_bf = q8.astype(BF16)
    slope = _alibi_slope_col(N_HEADS, 1, 0, N_HEADS)
    lane = lax.broadcasted_iota(I32, (N_HEADS, page), 1)
    s_t, v_t = _new_token_terms(q8, ksn_ref[0], vsn_ref[0])

    scores, masks, new_scores = [], [], []
    for n in range(n_sel):
        k_refs = page_refs[4 * n:4 * n + N_KV]
        blks = [sel_ref[(b * N_KV + g) * n_sel + n] for g in range(N_KV)]
        blk_rows = _merge_groups([jnp.full((N_HEADS, page), blk, I32) for blk in blks])
        blk_col = _merge_groups([jnp.full((N_HEADS, 1), blk, I32) for blk in blks])
        qk = _merge_groups([_dot(q8_bf, k_ref[0, 0].astype(BF16)) for k_ref in k_refs])
        page_pos = blk_rows // spp
        dist = (past - (page_pos * page + lane)).astype(F32)
        in_block = (lane // SLC_BLOCK) == (blk_rows - page_pos * spp)
        cached = jnp.logical_and(blk_rows >= 0, blk_rows < ns_past)
        mask = jnp.logical_and(jnp.logical_and(in_block, cached), dist >= 0.0)
        scores.append(jnp.where(mask, qk - slope * dist, NEG))
        masks.append(mask)
        new_scores.append(jnp.where(blk_col >= ns_past, s_t, NEG))
    m = new_scores[0]
    for s, sn in zip(scores, new_scores):
        m = jnp.maximum(m, jnp.maximum(jnp.max(s, axis=1, keepdims=True), sn))
    l_tot = jnp.zeros((N_HEADS, 1), F32)
    p_new = jnp.zeros((N_HEADS, 1), F32)
    acc = jnp.zeros((N_HEADS, HEAD_DIM), F32)
    for n in range(n_sel):
        v_refs = page_refs[4 * n + N_KV:4 * n + 2 * N_KV]
        p = jnp.where(masks[n], jnp.exp(scores[n] - m), 0.0)
        p_bf = p.astype(BF16)
        l_tot = l_tot + jnp.sum(p, axis=1, keepdims=True)
        p_new = p_new + jnp.where(new_scores[n] > 0.5 * NEG, jnp.exp(new_scores[n] - m), 0.0)
        acc = acc + _merge_groups([_dot_nt(p_bf, v_ref[0, 0].astype(BF16)) for v_ref in v_refs])
    o_slc = (acc + p_new * v_t) / jnp.maximum(l_tot + p_new, 1e-30)

    w_buf = kw_ref.shape[-1]
    wl = lax.broadcasted_iota(I32, (N_HEADS, w_buf), 1)
    win_pos = past - w_buf + wl
    dist_w = (past - win_pos).astype(F32)
    mask_w = jnp.logical_and(jnp.logical_and(dist_w >= 0.0, dist_w <= float(WINDOW)), win_pos >= 0)
    qk_w = _merge_groups([_dot(q8_bf, kw_ref[0, g].astype(BF16)) for g in range(N_KV)])
    s_w = jnp.where(mask_w, qk_w - slope * dist_w, NEG)
    s_t, v_t = _new_token_terms(q8, kwn_ref[0], vwn_ref[0])
    m_w = jnp.maximum(jnp.max(s_w, axis=1, keepdims=True), s_t)
    p_w = jnp.where(mask_w, jnp.exp(s_w - m_w), 0.0)
    p_t = jnp.exp(s_t - m_w)
    den = jnp.maximum(jnp.sum(p_w, axis=1, keepdims=True) + p_t, 1e-30)
    pw_bf = p_w.astype(BF16)
    o_w = _merge_groups([_dot_nt(pw_bf, vw_ref[0, g].astype(BF16)) for g in range(N_KV)])
    o_win = (o_w + p_t * v_t) / den
    n_new = vcn_ref.shape[1]
    vcn = jnp.concatenate([vcn_ref[0], jnp.zeros((LANES - n_new, KV_COLS), F32)], axis=0).astype(BF16)
    pn_bf = pnew_ref[0].astype(BF16)
    o_cmp = ocmp_ref[0] + _merge_groups([_dot(pn_bf, _group_slice(vcn, g)) for g in range(N_KV)])
    gt = gate_ref[0]
    o_ref[0] = o_cmp * gt[:, 0:1] + o_slc * gt[:, 1:2] + o_win * gt[:, 2:3]


def _sample_attend(sel, page_table, q8, k_pages, v_pages, ks_new, vs_new, kw_state, vw_state,
                   kw_new, vw_new, gates8, o_cmp, p_new, vc_new, past):
    b, n_pages = page_table.shape
    n_sel = sel.shape[-1]
    page = k_pages.shape[-1]
    ns_past = past // SLC_BLOCK
    spp = page // SLC_BLOCK
    w_buf = kw_state.shape[-1]
    n_new = vc_new.shape[1]

    def cache_map(n, g):
        def index(i, sel_ref, pt_ref):
            blk = jnp.clip(sel_ref[(i * N_KV + g) * n_sel + n], 0, ns_past - 1)
            return (pt_ref[i * n_pages + blk // spp], g, 0, 0)
        return pl.BlockSpec((1, 1, HEAD_DIM, page), index)

    page_specs, page_args = [], []
    for n in range(n_sel):
        for arr in (k_pages, v_pages):
            for g in range(N_KV):
                page_specs.append(cache_map(n, g))
                page_args.append(arr)
    per_b = lambda *s: pl.BlockSpec((1,) + s, lambda i, sl, pt: (i,) + (0,) * len(s))
    return pl.pallas_call(
        functools.partial(_sample_attend_kernel, past=past, n_sel=n_sel, ns_past=ns_past),
        grid_spec=pltpu.PrefetchScalarGridSpec(
            num_scalar_prefetch=2,
            grid=(b,),
            in_specs=[per_b(N_HEADS, HEAD_DIM)] + page_specs
                     + [per_b(1, KV_COLS), per_b(1, KV_COLS),
                        per_b(N_KV, HEAD_DIM, w_buf), per_b(N_KV, HEAD_DIM, w_buf),
                        per_b(1, KV_COLS), per_b(1, KV_COLS),
                        per_b(N_HEADS, 3), per_b(N_HEADS, HEAD_DIM), per_b(N_HEADS, LANES),
                        per_b(n_new, KV_COLS)],
            out_specs=per_b(N_HEADS, HEAD_DIM),
        ),
        out_shape=jax.ShapeDtypeStruct((b, N_HEADS, HEAD_DIM), F32),
        compiler_params=_cparams("arbitrary"),
        name="sample_attend",
    )(sel.reshape(-1), page_table.reshape(-1), q8, *page_args,
      ks_new, vs_new, kw_state, vw_state, kw_new, vw_new, gates8, o_cmp, p_new, vc_new)


def _layernorm_silu(y, g, b):
    mu = jnp.mean(y, axis=-1, keepdims=True)
    var = jnp.mean(jnp.square(y - mu), axis=-1, keepdims=True)
    return _silu((y - mu) * lax.rsqrt(var + EPS) * g + b)


def _conv_prompt_kernel(u_ref, w_ref, b_ref, g_ref, beta_ref, o_ref, buf):
    j = pl.program_id(1)
    tt = u_ref.shape[1]
    c = buf.shape[1]
    kw = w_ref.shape[0] // SUBLANES

    @pl.when(j == 0)
    def _():
        buf[0:CONV_HALO, :] = jnp.zeros((CONV_HALO, c), F32)

    buf[CONV_HALO:CONV_HALO + tt, :] = u_ref[0]
    first = CONV_HALO - (kw - 1)
    rows = CONV_ROWS
    for r0 in range(0, tt, rows):
        acc = jnp.zeros((rows // SUBLANES, SUBLANES, c), F32)
        for r in range(SUBLANES):
            taps = range(r, kw, SUBLANES)
            win = buf[pl.ds(first + r + r0, rows + SUBLANES * (len(taps) - 1)), :]
            for t, k in enumerate(taps):
                wk = w_ref[SUBLANES * k:SUBLANES * (k + 1), :]
                tap = win[SUBLANES * t:SUBLANES * t + rows].reshape(rows // SUBLANES, SUBLANES, c)
                acc = acc + wk[None] * tap
        y = acc.reshape(rows, c) + b_ref[...]
        o_ref[0, r0:r0 + rows, :] = _layernorm_silu(y, g_ref[...], beta_ref[...])
    buf[0:CONV_HALO, :] = buf[tt:tt + CONV_HALO, :]


def _conv_prompt(u, w_dw, b_dw, ln_g, ln_b, tt):
    b, t, c = u.shape
    vec = pl.BlockSpec((1, c), lambda i, j: (0, 0))
    w_rep = jnp.repeat(w_dw, SUBLANES, axis=0)
    return pl.pallas_call(
        _conv_prompt_kernel,
        grid=(b, t // tt),
        in_specs=[pl.BlockSpec((1, tt, c), lambda i, j: (i, j, 0)),
                  pl.BlockSpec(w_rep.shape, lambda i, j: (0, 0)), vec, vec, vec],
        out_specs=pl.BlockSpec((1, tt, c), lambda i, j: (i, j, 0)),
        out_shape=jax.ShapeDtypeStruct((b, t, c), F32),
        scratch_shapes=[pltpu.VMEM((CONV_HALO + tt, c), F32)],
        compiler_params=_cparams("arbitrary", "arbitrary"),
        name="conv_prompt",
    )(u, w_rep, b_dw, ln_g, ln_b)


def _conv_sample_kernel(up_ref, w_ref, b_ref, g_ref, beta_ref, o_ref):
    y = jnp.sum(up_ref[...] * w_ref[...][None, :, :], axis=1)
    o_ref[...] = _layernorm_silu(y + b_ref[...], g_ref[...], beta_ref[...])


def _conv_sample(up, w_dw, b_dw, ln_g, ln_b):
    b, kw, c = up.shape
    return pl.pallas_call(
        _conv_sample_kernel,
        out_shape=jax.ShapeDtypeStruct((b, c), F32),
        name="conv_sample",
    )(up, w_dw, b_dw, ln_g, ln_b)


def _merge_router_kernel(oa_ref, oc_ref, x_ref, gate_ref, shift_ref, scale_ref, goa_ref, goc_ref,
                         wout_ref, g2_ref, rwh_ref, rwl_ref, rb_ref, cnt_in_ref,
                         x1_ref, h2_ref, eidx_ref, wts_ref, rank_ref, cnt_out_ref, run):
    first = jnp.logical_and(pl.program_id(0) == 0, pl.program_id(1) == 0)

    @pl.when(first)
    def _():
        run[...] = cnt_in_ref[...]

    a = _rms(oa_ref[0], goa_ref[...])
    c = _rms(oc_ref[0], goc_ref[...])
    cat = jnp.concatenate([a, c], axis=1).astype(BF16)
    x1 = x_ref[0] + gate_ref[0] * _dot(cat, wout_ref[...])
    x1_ref[0] = x1
    h2 = _rms(x1, g2_ref[...]) * (1.0 + scale_ref[0]) + shift_ref[0]
    h2_ref[0] = h2

    hh, hl = _split2(h2)
    logits = _dot_nt(rwh_ref[...], hh) + (_dot_nt(rwl_ref[...], hh) + _dot_nt(rwh_ref[...], hl))
    aff = _sigmoid(logits)
    n_exp, tm = aff.shape
    row_f = lax.broadcasted_iota(I32, (n_exp, tm), 0).astype(F32)
    s = aff + rb_ref[...]
    experts, weights = [], []
    for _ in range(TOP_K):
        m = jnp.max(s, axis=0, keepdims=True)
        e = jnp.min(jnp.where(s == m, row_f, NO_INDEX), axis=0, keepdims=True)
        pick = row_f == e
        experts.append(e)
        weights.append(jnp.sum(jnp.where(pick, aff, 0.0), axis=0, keepdims=True))
        s = jnp.where(pick, NEG, s)
    total = weights[0]
    for w in weights[1:]:
        total = total + w

    hot = jnp.where(s == NEG, 1.0, 0.0)
    r_i = lax.broadcasted_iota(I32, (tm, tm), 0)
    c_i = lax.broadcasted_iota(I32, (tm, tm), 1)
    earlier = jnp.where(r_i < c_i, 1.0, 0.0).astype(BF16)
    before = _dot(hot.astype(BF16), earlier) + run[...]
    ranks = [jnp.sum(jnp.where(row_f == e, before, 0.0), axis=0, keepdims=True) for e in experts]
    eidx_ref[0] = jnp.concatenate(experts, axis=0).astype(I32)
    wts_ref[0] = jnp.concatenate([ROUTE_SCALE * w / total for w in weights], axis=0)
    rank_ref[0] = jnp.concatenate(ranks, axis=0).astype(I32)
    run[...] = run[...] + jnp.sum(hot, axis=1, keepdims=True)
    cnt_out_ref[...] = run[...]


def _merge_router(o_attn, o_conv, x, gate, shift, scale, goa, goc, wout_bf, g2, rw_hi, rw_lo, rb,
                  cnt_in, tm):
    b, t, d = x.shape
    n_exp = rw_hi.shape[0]
    row = lambda n: pl.BlockSpec((1, tm, n), lambda i, j: (i, j, 0))
    pick = pl.BlockSpec((1, TOP_K, tm), lambda i, j: (i, 0, j))
    const = lambda shape: pl.BlockSpec(shape, lambda i, j: (0,) * len(shape))
    sds = lambda n, dt: jax.ShapeDtypeStruct((b, t, n), dt)
    picks = lambda dt: jax.ShapeDtypeStruct((b, TOP_K, t), dt)
    return pl.pallas_call(
        _merge_router_kernel,
        grid=(b, t // tm),
        in_specs=[row(o_attn.shape[-1]), row(o_conv.shape[-1]), row(d),
                  _mod_spec(gate, tm, d), _mod_spec(shift, tm, d), _mod_spec(scale, tm, d),
                  const(goa.shape), const(goc.shape), const(wout_bf.shape), const(g2.shape),
                  const(rw_hi.shape), const(rw_lo.shape), const(rb.shape), const(cnt_in.shape)],
        out_specs=[row(d), row(d), pick, pick, pick, const((n_exp, 1))],
        out_shape=[sds(d, F32), sds(d, F32), picks(I32), picks(F32), picks(I32),
                   jax.ShapeDtypeStruct((n_exp, 1), F32)],
        scratch_shapes=[pltpu.VMEM((n_exp, 1), F32)],
        compiler_params=_cparams("arbitrary", "arbitrary"),
        name="merge_router",
    )(o_attn, o_conv, x, gate, shift, scale, goa, goc, wout_bf, g2, rw_hi, rw_lo, rb, cnt_in)


def _row_copy(src_hbm, dst, src_row, dst_row, sem, chunks):
    return pltpu.make_async_copy(src_hbm.at[pl.ds(src_row * chunks, chunks)],
                                 dst.at[pl.ds(dst_row * chunks, chunks)], sem)


def _slot(start_ref, e_ref, r_ref, idx):
    return start_ref[e_ref[0, 0, idx]] + r_ref[0, 0, idx]


def _dispatch_kernel(cnt_ref, end_ref, start_ref, e_ref, r_ref, h_ref, xs_hbm, zbuf, zsem, sem,
                     *, tokens, rows, chunks, n_blocks):
    j = pl.program_id(0)
    n_exp = cnt_ref.shape[0]
    blk_rows = rows * chunks

    def zero_block(blk):
        return pltpu.make_async_copy(zbuf, xs_hbm.at[pl.ds(blk * blk_rows, blk_rows)], zsem)

    @pl.when(j == 0)
    def _():
        zbuf[...] = jnp.zeros(zbuf.shape, F32)
        n_active = end_ref[n_exp - 1] // rows

        def zero_tail(e, issued):
            partial = cnt_ref[e] % rows != 0

            @pl.when(partial)
            def _():
                zero_block(end_ref[e] // rows - 1).start()

            return issued + partial.astype(I32)

        def zero_unused(blk, _):
            zero_block(blk).start()
            return 0

        def drain_zero(_, c):
            zero_block(0).wait()
            return c

        issued = lax.fori_loop(0, n_exp, zero_tail, 0)
        lax.fori_loop(n_active, n_blocks, zero_unused, 0)
        lax.fori_loop(0, issued + (n_blocks - n_active), drain_zero, 0)

    def issue(r, _):
        for k in range(TOP_K):
            _row_copy(h_ref, xs_hbm, r, _slot(start_ref, e_ref, r_ref, r * TOP_K + k), sem, chunks).start(
                priority=k % DMA_PRIORITIES)
        return 0

    def drain(r, _):
        for k in range(TOP_K):
            _row_copy(h_ref, xs_hbm, 0, 0, sem, chunks).wait()
        return 0

    lax.fori_loop(0, tokens, issue, 0)
    lax.fori_loop(0, tokens, drain, 0)


def _dispatch(counts, pad_end, pad_start, e_idx, rank, h_rows, tokens, rows, chunks, n_blocks):
    n_tiles = e_idx.shape[0]
    picks = pl.BlockSpec((1, 1, tokens * TOP_K), lambda j, c, e, s: (j, 0, 0), memory_space=pltpu.SMEM)
    return pl.pallas_call(
        functools.partial(_dispatch_kernel, tokens=tokens, rows=rows, chunks=chunks, n_blocks=n_blocks),
        grid_spec=pltpu.PrefetchScalarGridSpec(
            num_scalar_prefetch=3,
            grid=(n_tiles,),
            in_specs=[picks, picks,
                      pl.BlockSpec((tokens * chunks, LANES), lambda j, c, e, s: (j, 0))],
            out_specs=pl.BlockSpec(memory_space=pl.ANY),
            scratch_shapes=[pltpu.VMEM((rows * chunks, LANES), F32),
                            pltpu.SemaphoreType.DMA(()), pltpu.SemaphoreType.DMA(())],
        ),
        out_shape=jax.ShapeDtypeStruct((n_blocks * rows * chunks, LANES), F32),
        compiler_params=_cparams("arbitrary"),
        name="moe_dispatch",
    )(counts, pad_end, pad_start, e_idx, rank, h_rows)


def _expert_kernel(be_ref, nact_ref, first_ref, slot_ref, next_ref, x_ref, wg_hbm, wu_hbm, wd_hbm, y_ref,
                   wg_buf, wu_buf, wd_buf, sems, *, rows, chunks):
    j = pl.program_id(0)

    def weight_copies(expert, slot):
        return [pltpu.make_async_copy(w_hbm.at[expert], buf.at[slot], sems.at[slot, i])
                for i, (w_hbm, buf) in enumerate(((wg_hbm, wg_buf), (wu_hbm, wu_buf), (wd_hbm, wd_buf)))]

    @pl.when(j < nact_ref[0])
    def _():
        slot = slot_ref[j]

        @pl.when(j == 0)
        def _():
            for cp in weight_copies(be_ref[0], 0):
                cp.start()

        @pl.when(first_ref[j] == 1)
        def _():
            for cp in weight_copies(be_ref[j], slot):
                cp.wait()

            @pl.when(next_ref[j] >= 0)
            def _():
                for cp in weight_copies(next_ref[j], 1 - slot):
                    cp.start()

        f = wg_buf.shape[2]
        gate = jnp.zeros((rows, f), F32)
        up = jnp.zeros((rows, f), F32)
        for c in range(0, chunks, 2):
            xc = jnp.concatenate([x_ref[pl.ds(c, rows, stride=chunks), :],
                                  x_ref[pl.ds(c + 1, rows, stride=chunks), :]], axis=1).astype(BF16)
            cs = pl.ds(c * LANES, 2 * LANES)
            gate = gate + _dot(xc, wg_buf[slot, cs, :].astype(BF16))
            up = up + _dot(xc, wu_buf[slot, cs, :].astype(BF16))
        h = (_silu(gate) * up).astype(BF16)
        y = _dot(h, wd_buf[slot].astype(BF16))
        for c in range(chunks):
            y_ref[pl.ds(c, rows, stride=chunks), :] = y[:, c * LANES:(c + 1) * LANES]


def _experts(blk_expert, n_active, pad_end, xs, wg, wu, wd, rows, chunks):
    n_blocks = blk_expert.shape[0]
    n_exp, d, f = wg.shape
    first = jnp.concatenate([jnp.ones((1,), I32), (blk_expert[1:] != blk_expert[:-1]).astype(I32)])
    slot = (jnp.cumsum(first) - 1) % 2
    run_end = pad_end[blk_expert] // rows
    nxt = jnp.where(run_end < n_active[0], blk_expert[jnp.minimum(run_end, n_blocks - 1)], -1)
    block = pl.BlockSpec((rows * chunks, LANES), lambda j, be, na, *_: (jnp.minimum(j, na[0] - 1), 0))
    hbm = pl.BlockSpec(memory_space=pl.ANY)
    n_prefetch = 5
    return pl.pallas_call(
        functools.partial(_expert_kernel, rows=rows, chunks=chunks),
        grid_spec=pltpu.PrefetchScalarGridSpec(
            num_scalar_prefetch=n_prefetch,
            grid=(n_blocks,),
            in_specs=[block, hbm, hbm, hbm],
            out_specs=block,
            scratch_shapes=[pltpu.VMEM((2, d, f), F32), pltpu.VMEM((2, d, f), F32),
                            pltpu.VMEM((2, f, d), F32), pltpu.SemaphoreType.DMA((2, 3))],
        ),
        out_shape=jax.ShapeDtypeStruct(xs.shape, F32),
        input_output_aliases={n_prefetch: 0},
        compiler_params=_cparams("arbitrary"),
        name="moe_experts",
    )(blk_expert, n_active, first, slot.astype(I32), nxt.astype(I32), xs, wg, wu, wd)


def _combine_kernel(start_ref, e_ref, r_ref, w_ref, x1_ref, h2_ref, gate_ref, wsg_ref, wsu_ref, wsd_ref,
                    gf_ref, ys_hbm, o_ref, buf, sem, *, chunks):
    tm = x1_ref.shape[1]

    def issue(r, _):
        for k in range(TOP_K):
            _row_copy(ys_hbm, buf.at[k], _slot(start_ref, e_ref, r_ref, r * TOP_K + k), r, sem, chunks).start(
                priority=k % DMA_PRIORITIES)
        return 0

    def drain(r, _):
        for k in range(TOP_K):
            _row_copy(ys_hbm, buf.at[k], 0, r, sem, chunks).wait()
        return 0

    lax.fori_loop(0, tm, issue, 0)
    h_bf = h2_ref[0].astype(BF16)
    hid = (_silu(_dot(h_bf, wsg_ref[...])) * _dot(h_bf, wsu_ref[...])).astype(BF16)
    shared = _dot(hid, wsd_ref[...])
    lax.fori_loop(0, tm, drain, 0)

    w = w_ref[0]
    cols = []
    for c in range(chunks):
        tot = jnp.zeros((tm, LANES), F32)
        for k in range(TOP_K):
            tot = tot + buf[k, pl.ds(c, tm, stride=chunks), :] * w[:, k:k + 1]
        cols.append(tot)
    routed = jnp.concatenate(cols, axis=1)
    x2 = x1_ref[0] + gate_ref[0] * (routed + shared)
    o_ref[0] = _rms(x2, gf_ref[...])


def _combine(pad_start, e_idx, rank, wts, x1, h2, gate, wsg_bf, wsu_bf, wsd_bf, gf, ys, tm, chunks):
    b, t, d = x1.shape
    nt = t // tm
    row = lambda n: pl.BlockSpec((1, tm, n), lambda i, j, *_: (i, j, 0))
    const = lambda shape: pl.BlockSpec(shape, lambda i, j, *_: (0,) * len(shape))
    picks = pl.BlockSpec((1, 1, tm * TOP_K), lambda i, j, *_: (i * nt + j, 0, 0), memory_space=pltpu.SMEM)
    return pl.pallas_call(
        functools.partial(_combine_kernel, chunks=chunks),
        grid_spec=pltpu.PrefetchScalarGridSpec(
            num_scalar_prefetch=1,
            grid=(b, nt),
            in_specs=[picks, picks, row(TOP_K), row(d), row(d), _mod_spec(gate, tm, d),
                      const(wsg_bf.shape), const(wsu_bf.shape), const(wsd_bf.shape), const(gf.shape),
                      pl.BlockSpec(memory_space=pl.ANY)],
            out_specs=row(d),
            scratch_shapes=[pltpu.VMEM((TOP_K, tm * chunks, LANES), F32), pltpu.SemaphoreType.DMA(())],
        ),
        out_shape=jax.ShapeDtypeStruct((b, t, d), F32),
        compiler_params=_cparams("arbitrary", "arbitrary"),
        name="moe_combine",
    )(pad_start, e_idx, rank, wts, x1, h2, gate, wsg_bf, wsu_bf, wsd_bf, gf, ys)


def _split_mod(mod, per_token):
    parts = jnp.split(mod, 6, axis=-1)
    if per_token:
        return [p[None] for p in parts]
    return [p[:, None, :] for p in parts]


def _padded_in_weight(w_in, conv_width):
    n_gate = 3 * N_HEADS
    o = ATTN_WIDTH + 6 * KV_COLS
    main = w_in[:, :o]
    gates = jnp.pad(w_in[:, o:o + n_gate], ((0, 0), (0, LANES - n_gate)))
    glu = w_in[:, o + n_gate:o + n_gate + 2 * conv_width]
    return jnp.concatenate([main, gates, glu], axis=1).astype(BF16)


def _cmp_rows(x):
    return x.reshape(x.shape[:-2] + (x.shape[-2] // CMP_BLOCK, CMP_BLOCK * KV_COLS))


def _largest_tile(n, cap):
    best = [k for k in range(SUBLANES, cap + 1, SUBLANES) if n % k == 0]
    assert best, (n, cap)
    return best[-1]


def _kv5(x):
    return x.reshape(x.shape[:-1] + (N_KV, HEAD_DIM))[None]


def kernel(x_prompt, x_sample, cache_k_cmp, cache_v_cmp, cache_k_slc, cache_v_slc, state_k_win, state_v_win, state_conv, page_table, c_prompt, c_sample, norm1_g, norm2_g, w_ada, b_ada, w_in, w_cmp_k, w_cmp_v, w_dw, b_dw, ln_conv_g, ln_conv_b, g_out_attn, g_out_conv, w_out, router_w, router_b, w_exp_gate, w_exp_up, w_exp_down, w_sh_gate, w_sh_up, w_sh_down, norm_f_g):
    assert w_ada.shape[0] == 1, "single layer"
    bp, t, d = x_prompt.shape
    bs, s_new, _ = x_sample.shape
    assert s_new == 1
    n_pool, page = cache_k_cmp.shape[1], cache_k_cmp.shape[2]
    n_pages = page_table.shape[1]
    past = n_pages * page
    conv_width = state_conv.shape[-1]
    n_exp = router_w.shape[-1]
    chunks = d // LANES
    tm = min(ROW_TILE, t)

    w_in_bf = _padded_in_weight(w_in[0], conv_width)
    wck = _compress_weight(w_cmp_k[0])
    wcv = _compress_weight(w_cmp_v[0])
    wout_bf = w_out[0].astype(BF16)
    rw_t = router_w[0].T
    rw_hi = rw_t.astype(BF16)
    rw_lo = (rw_t - rw_hi.astype(F32)).astype(BF16)
    wsg_bf, wsu_bf, wsd_bf = (w[0].astype(BF16) for w in (w_sh_gate, w_sh_up, w_sh_down))
    gf = norm_f_g[None, :]

    n_c = bp + bs
    c_all = jnp.concatenate([c_prompt, c_sample], axis=0)
    c_all = jnp.pad(c_all, ((0, (-n_c) % SUBLANES), (0, 0)))
    mod = _modulation(c_all, w_ada[0], b_ada)
    mp = _split_mod(mod[:bp], per_token=False)
    ms = _split_mod(mod[bp:n_c], per_token=True)

    (q_p, kc_p, vc_p, ks_p, vs_p, kw_p, vw_p, gate_p, u_p) = _in_proj(
        x_prompt, mp[0], mp[1], norm1_g, w_in_bf, tm)
    nc_p = t // CMP_BLOCK
    kcc, vcc = _compress(_cmp_rows(kc_p).reshape(bp * nc_p, -1), _cmp_rows(vc_p).reshape(bp * nc_p, -1),
                         wck, wcv)
    o_attn_p = _prompt_attention(q_p, gate_p, kcc.reshape(bp, nc_p, KV_COLS), vcc.reshape(bp, nc_p, KV_COLS),
                                 ks_p, vs_p, kw_p, vw_p)
    o_conv_p = _conv_prompt(u_p, w_dw[0], b_dw, ln_conv_g, ln_conv_b, tm)

    xs_row = x_sample.reshape(1, bs, d)
    (q_s, kc_s, vc_s, ks_s, vs_s, kw_s, vw_s, gate_s, u_s) = _in_proj(
        xs_row, ms[0], ms[1], norm1_g, w_in_bf, bs)
    q8 = q_s.reshape(bs, N_HEADS, HEAD_DIM)
    gates8 = gate_s[0, :, :3 * N_HEADS].reshape(bs, N_HEADS, 3)
    pages_t = lambda c: jnp.transpose(c[0], (0, 2, 3, 1))
    tail = (-(past + s_new)) % SLC_BLOCK
    n_new = (s_new + tail) // CMP_BLOCK
    tail_rows = lambda x: _cmp_rows(jnp.pad(x[0][:, None, :], ((0, 0), (0, tail), (0, 0)))).reshape(bs * n_new, -1)
    pad_rows = (-(bs * n_new)) % SUBLANES
    kc_new, vc_new = _compress(jnp.pad(tail_rows(kc_s), ((0, pad_rows), (0, 0))),
                               jnp.pad(tail_rows(vc_s), ((0, pad_rows), (0, 0))), wck, wcv)
    kc_new = kc_new[:bs * n_new].reshape(bs, n_new, KV_COLS)
    vc_new = vc_new[:bs * n_new].reshape(bs, n_new, KV_COLS)
    reps = page // CMP_BLOCK
    wk_fold = jnp.transpose(w_cmp_k[0], (2, 1, 0)).reshape(HEAD_DIM, HEAD_DIM * CMP_BLOCK)
    ut = _matmul3(q8.reshape(bs * N_HEADS, HEAD_DIM) * ATTN_SCALE, wk_fold)
    ut = jnp.tile(ut.reshape(bs, N_HEADS, HEAD_DIM, CMP_BLOCK), (1, 1, 1, reps))
    s_raw = _sample_scores(page_table, ut, pages_t(cache_k_cmp))
    p_exp, p_new, sel = _sample_select(s_raw, q8, kc_new, past)
    y_acc = _sample_values(page_table, p_exp, pages_t(cache_v_cmp))
    wv_fold = jnp.tile(jnp.transpose(w_cmp_v[0], (1, 0, 2)), (1, reps, 1)).reshape(HEAD_DIM * page, HEAD_DIM)
    o_cmp_s = _matmul3(y_acc.reshape(bs * N_HEADS, HEAD_DIM * page), wv_fold).reshape(bs, N_HEADS, HEAD_DIM)
    n_sel = min(N_SEL, (past // CMP_BLOCK + n_new) // CMP_PER_SLC)
    sel = sel[:, :N_KV, :n_sel]
    row3 = lambda x: x[0][:, None, :]
    o_attn_s = _sample_attend(
        sel, page_table, q8, pages_t(cache_k_slc), pages_t(cache_v_slc), row3(ks_s), row3(vs_s),
        pages_t(state_k_win), pages_t(state_v_win), row3(kw_s), row3(vw_s), gates8, o_cmp_s, p_new, vc_new,
        past)
    o_attn_s = o_attn_s.reshape(1, bs, ATTN_WIDTH)
    up_s = jnp.concatenate([state_conv[0], u_s[0][:, None, :]], axis=1)
    o_conv_s = _conv_sample(up_s, w_dw[0], b_dw, ln_conv_g, ln_conv_b)[None]

    router = functools.partial(_merge_router, goa=g_out_attn, goc=g_out_conv, wout_bf=wout_bf, g2=norm2_g,
                               rw_hi=rw_hi, rw_lo=rw_lo, rb=router_b[0][:, None])
    x1_p, h2_p, e_p, w_p, r_p, cnt = router(o_attn_p, o_conv_p, x_prompt, mp[2], mp[3], mp[4],
                                            cnt_in=jnp.zeros((n_exp, 1), F32), tm=tm)
    x1_s, h2_s, e_s, w_s, r_s, cnt = router(o_attn_s, o_conv_s, xs_row, ms[2], ms[3], ms[4],
                                            cnt_in=cnt, tm=bs)

    n_tok = bp * t + bs
    counts = cnt[:, 0].astype(I32)
    padded = (counts + MOE_ROWS - 1) // MOE_ROWS * MOE_ROWS
    pad_end = jnp.cumsum(padded)
    pad_end = pad_end.astype(I32)
    pad_start = pad_end - padded
    n_blocks = -(-(n_tok * TOP_K) // MOE_ROWS) + n_exp
    blk_first = jnp.arange(n_blocks, dtype=I32) * MOE_ROWS
    blk_expert = jnp.minimum(jnp.sum(pad_end[None, :] <= blk_first[:, None], axis=1), n_exp - 1).astype(I32)
    n_active = pad_end[-1:] // MOE_ROWS
    picks = lambda a: jnp.transpose(a, (0, 2, 1))
    w_p, w_s = picks(w_p), picks(w_s)
    e_all = jnp.concatenate([picks(e_p).reshape(-1, TOP_K), picks(e_s).reshape(-1, TOP_K)], axis=0)
    r_all = jnp.concatenate([picks(r_p).reshape(-1, TOP_K), picks(r_s).reshape(-1, TOP_K)], axis=0)

    tile = _largest_tile(n_tok, 512)
    h_rows = jnp.concatenate([h2_p.reshape(-1, d), h2_s.reshape(-1, d)], axis=0).reshape(n_tok * chunks, LANES)
    xs = _dispatch(counts, pad_end, pad_start, e_all.reshape(n_tok // tile, 1, tile * TOP_K),
                   r_all.reshape(n_tok // tile, 1, tile * TOP_K), h_rows, tile, MOE_ROWS, chunks, n_blocks)
    ys = _experts(blk_expert, n_active, pad_end, xs, w_exp_gate[0], w_exp_up[0], w_exp_down[0], MOE_ROWS, chunks)

    comb = functools.partial(_combine, pad_start, wsg_bf=wsg_bf, wsu_bf=wsu_bf, wsd_bf=wsd_bf, gf=gf, ys=ys,
                             chunks=chunks)
    tiles_p = (bp * (t // tm), 1, tm * TOP_K)
    y_prompt = comb(picks(e_p).reshape(tiles_p), picks(r_p).reshape(tiles_p), w_p, x1_p, h2_p, mp[5], tm=tm)
    y_sample = comb(picks(e_s).reshape(1, 1, bs * TOP_K), picks(r_s).reshape(1, 1, bs * TOP_K), w_s, x1_s, h2_s,
                    ms[5], tm=bs).reshape(bs, 1, d)

    win = min(WINDOW, t)
    hist = state_conv.shape[2]
    out_p = [_kv5(a) for a in (kc_p, vc_p, ks_p, vs_p, kw_p[:, t - win:], vw_p[:, t - win:])]
    conv_p = u_p[:, t - hist:][None]
    out_s = [_kv5(a[0][:, None, :]) for a in (kc_s, vc_s, ks_s, vs_s)]
    w_buf = state_k_win.shape[2]
    kw_buf = jnp.concatenate([state_k_win, _kv5(kw_s[0][:, None, :])], axis=2)[:, :, -w_buf:]
    vw_buf = jnp.concatenate([state_v_win, _kv5(vw_s[0][:, None, :])], axis=2)[:, :, -w_buf:]
    conv_s = up_s[:, -hist:][None]
    return (y_prompt, y_sample, *out_p, conv_p, *out_s, kw_buf, vw_buf, conv_s)
```

```python
import functools

import jax
import jax.numpy as jnp
from jax import lax
from jax.experimental import pallas as pl
from jax.experimental.pallas import tpu as pltpu

F32 = jnp.float32
BF16 = jnp.bfloat16
I32 = jnp.int32

N_HEADS = 8
HEAD_DIM = 64
N_KV = 2
Q_PER_KV = N_HEADS // N_KV
ATTN_WIDTH = N_HEADS * HEAD_DIM
KV_COLS = N_KV * HEAD_DIM
CMP_BLOCK = 32
SLC_BLOCK = 64
CMP_PER_SLC = SLC_BLOCK // CMP_BLOCK
N_SEL = 16
WINDOW = 512
TOP_K = 8
ROUTE_SCALE = 2.5
EPS = 1e-6
FORCED = 1e4
NEG = -1e30
ATTN_SCALE = HEAD_DIM ** -0.5
PICKED = -2.0
NOT_A_BLOCK = -4.0
NO_INDEX = 1e9

LANES = 128
SUBLANES = 8
VMEM_LIMIT = 56 * 1024 * 1024
DMA_PRIORITIES = 2

ROW_TILE = 512
Q_TILE = 256
KEY_TILE = 1024
CMP_ROW_TILE = 512
MOE_ROWS = 256
CONV_HALO = 32
POS_RADIX = 256
CONV_ROWS = 32
PAGES_PER_STEP = 32


def _cparams(*sem):
    return pltpu.CompilerParams(dimension_semantics=sem, vmem_limit_bytes=VMEM_LIMIT)


def _dot(a, b):
    return jnp.dot(a, b, preferred_element_type=F32)


def _dot_nt(a, b):
    return lax.dot_general(a, b, (((1,), (1,)), ((), ())), preferred_element_type=F32)


def _dot_tn(a, b):
    return lax.dot_general(a, b, (((0,), (0,)), ((), ())), preferred_element_type=F32)


def _split2(x):
    hi = x.astype(BF16)
    lo = (x - hi.astype(F32)).astype(BF16)
    return hi, lo


def _dot3(a, b):
    ah, al = _split2(a)
    bh, bl = _split2(b)
    return _dot(ah, bh) + (_dot(ah, bl) + _dot(al, bh))


def _dot3_nt(a, b):
    ah, al = _split2(a)
    bh, bl = _split2(b)
    return _dot_nt(ah, bh) + (_dot_nt(ah, bl) + _dot_nt(al, bh))


def _sigmoid(x):
    return 1.0 / (1.0 + jnp.exp(-x))


def _silu(x):
    return x * _sigmoid(x)


def _rms(x, g):
    return x * lax.rsqrt(jnp.mean(x * x, axis=-1, keepdims=True) + EPS) * g


def _alibi_slope_col(rows, rows_per_head, first_head, n_heads):
    r = lax.broadcasted_iota(I32, (rows, 1), 0) // rows_per_head
    out = jnp.zeros((rows, 1), F32)
    for k in range(n_heads):
        out = jnp.where(r == k, 2.0 ** (-8.0 * (first_head + k + 1) / N_HEADS), out)
    return out


def _modulation_kernel(c_ref, w_ref, b_ref, o_ref):
    o_ref[...] = _dot3(c_ref[...], w_ref[...]) + b_ref[...]


def _modulation(c, w, b):
    m, d = c.shape
    n = w.shape[1]
    tn = 768
    return pl.pallas_call(
        _modulation_kernel,
        grid=(n // tn,),
        in_specs=[pl.BlockSpec((m, d), lambda j: (0, 0)),
                  pl.BlockSpec((d, tn), lambda j: (0, j)),
                  pl.BlockSpec((1, tn), lambda j: (0, j))],
        out_specs=pl.BlockSpec((m, tn), lambda j: (0, j)),
        out_shape=jax.ShapeDtypeStruct((m, n), F32),
        compiler_params=_cparams("arbitrary"),
        name="modulation",
    )(c, w, b)


def _mod_spec(mod, tm, d):
    if mod.shape[1] == 1:
        return pl.BlockSpec((1, 1, d), lambda i, j, *_: (i, 0, 0))
    return pl.BlockSpec((1, tm, d), lambda i, j, *_: (i, j, 0))


def _in_proj_kernel(x_ref, shift_ref, scale_ref, g_ref, w_ref,
                    q_ref, kc_ref, vc_ref, ks_ref, vs_ref, kw_ref, vw_ref, gate_ref, u_ref):
    x = x_ref[0]
    h = _rms(x, g_ref[...]) * (1.0 + scale_ref[0]) + shift_ref[0]
    z = _dot(h.astype(BF16), w_ref[...])
    q_ref[0] = z[:, :ATTN_WIDTH]
    o = ATTN_WIDTH
    for ref in (kc_ref, vc_ref, ks_ref, vs_ref, kw_ref, vw_ref):
        ref[0] = z[:, o:o + KV_COLS]
        o += KV_COLS
    gate_ref[0] = _sigmoid(z[:, o:o + LANES])
    o += LANES
    cw = u_ref.shape[-1]
    u_ref[0] = z[:, o:o + cw] * _sigmoid(z[:, o + cw:o + 2 * cw])


def _in_proj(x, shift, scale, g, w_bf, tm):
    b, t, d = x.shape
    cw = (w_bf.shape[1] - ATTN_WIDTH - 6 * KV_COLS - LANES) // 2
    row = lambda n: pl.BlockSpec((1, tm, n), lambda i, j: (i, j, 0))
    sds = lambda n: jax.ShapeDtypeStruct((b, t, n), F32)
    return pl.pallas_call(
        _in_proj_kernel,
        grid=(b, t // tm),
        in_specs=[row(d), _mod_spec(shift, tm, d), _mod_spec(scale, tm, d),
                  pl.BlockSpec((1, d), lambda i, j: (0, 0)),
                  pl.BlockSpec(w_bf.shape, lambda i, j: (0, 0))],
        out_specs=[row(ATTN_WIDTH)] + [row(KV_COLS)] * 6 + [row(LANES), row(cw)],
        out_shape=[sds(ATTN_WIDTH)] + [sds(KV_COLS)] * 6 + [sds(LANES), sds(cw)],
        compiler_params=_cparams("arbitrary", "arbitrary"),
        name="in_proj",
    )(x, shift, scale, g, w_bf)


def _compress_kernel(k_ref, v_ref, wk_ref, wv_ref, ko_ref, vo_ref):
    ko_ref[...] = _dot3(k_ref[...], wk_ref[...])
    vo_ref[...] = _dot3(v_ref[...], wv_ref[...])


def _compress(k_rows, v_rows, wk, wv):
    r, kdim = k_rows.shape
    tr = min(CMP_ROW_TILE, r)
    assert r % tr == 0
    rows = pl.BlockSpec((tr, kdim), lambda i: (i, 0))
    wspec = pl.BlockSpec((kdim, KV_COLS), lambda i: (0, 0))
    ospec = pl.BlockSpec((tr, KV_COLS), lambda i: (i, 0))
    return pl.pallas_call(
        _compress_kernel,
        grid=(r // tr,),
        in_specs=[rows, rows, wspec, wspec],
        out_specs=[ospec, ospec],
        out_shape=[jax.ShapeDtypeStruct((r, KV_COLS), F32)] * 2,
        compiler_params=_cparams("arbitrary"),
        name="compress",
    )(k_rows, v_rows, wk, wv)


def _compress_weight(w):
    eye = jnp.eye(N_KV, dtype=w.dtype)
    big = jnp.einsum('lde,gh->lgdhe', w, eye)
    return big.reshape(CMP_BLOCK * KV_COLS, KV_COLS)


def _pair_sum(x, axis):
    n = x.shape[axis]
    idx = lax.broadcasted_iota(I32, x.shape, axis)
    nxt = pltpu.roll(x, n - 1, axis)
    prv = pltpu.roll(x, 1, axis)
    return x + jnp.where((idx & 1) == 0, nxt, prv)


def _block_scores(imp, blk, q_pos, n_blocks_total):
    cur = q_pos // SLC_BLOCK
    valid = jnp.logical_and(blk * SLC_BLOCK <= q_pos, blk < n_blocks_total)
    forced = jnp.logical_or(blk == 0, jnp.logical_or(blk == cur, blk == cur - 1))
    return jnp.where(valid, jnp.where(forced, FORCED, imp), -1.0)


def _select_blocks(score, blk, n_sel):
    blk_f = blk.astype(F32)
    s = score
    for _ in range(n_sel):
        m = jnp.max(s, axis=0, keepdims=True)
        first = jnp.min(jnp.where(s == m, blk_f, NO_INDEX), axis=0, keepdims=True)
        s = jnp.where(blk_f == first, PICKED, s)
    return jnp.where(jnp.logical_and(s == PICKED, score >= 0.0), 1.0, 0.0)


def _position_features(n):
    pos = jnp.arange(n, dtype=I32)[:, None]
    lane = jnp.arange(HEAD_DIM, dtype=I32)[None, :]
    feat = jnp.where(lane < 2, 1, jnp.where(lane == 2, pos // POS_RADIX, jnp.where(lane == 3, pos % POS_RADIX, 0)))
    return feat.astype(BF16)


def _query_position_features(q_pos, slope):
    lane = lax.broadcasted_iota(I32, (q_pos.shape[0], HEAD_DIM), 1)
    hi = (q_pos // POS_RADIX).astype(F32) * (-slope * POS_RADIX)
    lo = (q_pos % POS_RADIX).astype(F32) * (-slope)
    return jnp.where(lane == 0, hi, jnp.where(lane == 1, lo, jnp.where(
        lane == 2, slope * POS_RADIX, jnp.where(lane == 3, slope, 0.0))))


def _prompt_attn_kernel(q_ref, gate_ref, kc_ref, vc_ref, ks_ref, vs_ref, kw_ref, vw_ref, kx_ref, o_ref,
                        *, seq, n_sel):
    i = pl.program_id(1)
    tq = Q_TILE
    nc = kc_ref.shape[1]
    q_blk = q_ref[0] * ATTN_SCALE
    gates = gate_ref[0]
    slopes = [2.0 ** (-8.0 * (h + 1) / N_HEADS) for h in range(N_HEADS)]
    group_heads = [list(range(g * Q_PER_KV, (g + 1) * Q_PER_KV)) for g in range(N_KV)]
    gsl = [slice(g * HEAD_DIM, (g + 1) * HEAD_DIM) for g in range(N_KV)]
    rsl = [slice(r * tq, (r + 1) * tq) for r in range(Q_PER_KV)]
    q_pos_col = i * tq + lax.broadcasted_iota(I32, (tq, 1), 0)
    q_pos_row = i * tq + lax.broadcasted_iota(I32, (1, tq), 1)
    q_heads = [q_blk[:, h * HEAD_DIM:(h + 1) * HEAD_DIM] for h in range(N_HEADS)]
    qg = [jnp.concatenate([q_heads[h] for h in hs], axis=0) for hs in group_heads]
    qx_bf = [jnp.concatenate([jnp.concatenate([q_heads[h], _query_position_features(q_pos_col, slopes[h])],
                                              axis=1) for h in hs], axis=0).astype(BF16)
             for hs in group_heads]

    cmp_row = lax.broadcasted_iota(I32, (nc, tq), 0)
    dist_c = (q_pos_row - (cmp_row * CMP_BLOCK + (CMP_BLOCK - 1))).astype(F32)
    mask_c = dist_c >= 0.0
    blk = cmp_row >> 1
    o_cmp = [None] * N_HEADS
    sel_bf = []
    for g, hs in enumerate(group_heads):
        vc_bf = vc_ref[0][:, gsl[g]].astype(BF16)
        qk = _dot3_nt(kc_ref[0][:, gsl[g]], qg[g])
        imp = jnp.zeros((nc, tq), F32)
        for r, h in enumerate(hs):
            s = jnp.where(mask_c, qk[:, rsl[r]] - slopes[h] * dist_c, NEG)
            m = jnp.max(s, axis=0, keepdims=True)
            p = jnp.where(mask_c, jnp.exp(s - m), 0.0)
            p = p / jnp.maximum(jnp.sum(p, axis=0, keepdims=True), 1e-30)
            o_cmp[h] = _dot_tn(p.astype(BF16), vc_bf)
            imp = imp + p
        score = _block_scores(_pair_sum(imp, 0), blk, q_pos_row, seq // SLC_BLOCK)
        sel_bf.append(_select_blocks(score, blk, n_sel).astype(BF16))

    span = WINDOW + tq
    w_start = pl.multiple_of(jnp.maximum(i * tq - WINDOW, 0), tq)
    dist_w = (q_pos_col - (w_start + lax.broadcasted_iota(I32, (tq, span), 1))).astype(F32)
    bias_w = jnp.where(jnp.logical_and(dist_w >= 0.0, dist_w <= float(WINDOW)), 0.0, NEG)
    o_win = [None] * N_HEADS
    kx_w = kx_ref[pl.ds(w_start, span), :]
    for g, hs in enumerate(group_heads):
        kw_bf = kw_ref[0, pl.ds(w_start, span), :][:, gsl[g]].astype(BF16)
        vw_bf = vw_ref[0, pl.ds(w_start, span), :][:, gsl[g]].astype(BF16)
        s_all = _dot_nt(qx_bf[g], jnp.concatenate([kw_bf, kx_w], axis=1))
        probs, sums = [], []
        for r, h in enumerate(hs):
            s = s_all[rsl[r]] + bias_w
            p = jnp.exp(s - jnp.max(s, axis=1, keepdims=True))
            sums.append(jnp.sum(p, axis=1, keepdims=True))
            probs.append(p.astype(BF16))
        o_all = _dot(jnp.concatenate(probs, axis=0), vw_bf)
        for r, h in enumerate(hs):
            o_win[h] = o_all[rsl[r]] / sums[r]

    n_tiles = ((i + 1) * tq + KEY_TILE - 1) // KEY_TILE

    def slc_step(t, carry):
        ms, ls, accs = (list(c) for c in carry)
        k0 = pl.multiple_of(t * KEY_TILE, KEY_TILE)
        dist = (q_pos_col - (k0 + lax.broadcasted_iota(I32, (tq, KEY_TILE), 1))).astype(F32)
        causal = dist >= 0.0
        key_cmp = (k0 + lax.broadcasted_iota(I32, (nc, KEY_TILE), 1)) // CMP_BLOCK
        expand = jnp.where(key_cmp == lax.broadcasted_iota(I32, (nc, KEY_TILE), 0), 1.0, 0.0).astype(BF16)
        kx_t = kx_ref[pl.ds(k0, KEY_TILE), :]
        for g, hs in enumerate(group_heads):
            kt_bf = ks_ref[0, pl.ds(k0, KEY_TILE), :][:, gsl[g]].astype(BF16)
            vt_bf = vs_ref[0, pl.ds(k0, KEY_TILE), :][:, gsl[g]].astype(BF16)
            chosen = _dot_tn(sel_bf[g], expand)
            bias = jnp.where(jnp.logical_and(causal, chosen > 0.5), 0.0, NEG)
            s_all = _dot_nt(qx_bf[g], jnp.concatenate([kt_bf, kx_t], axis=1))
            probs, alphas = [], []
            for r, h in enumerate(hs):
                s = s_all[rsl[r]] + bias
                m_new = jnp.maximum(ms[h], jnp.max(s, axis=1, keepdims=True))
                alpha = jnp.exp(ms[h] - m_new)
                p = jnp.exp(s - m_new)
                ls[h] = alpha * ls[h] + jnp.sum(p, axis=1, keepdims=True)
                ms[h] = m_new
                alphas.append(alpha)
                probs.append(p.astype(BF16))
            pv = _dot(jnp.concatenate(probs, axis=0), vt_bf)
            for r, h in enumerate(hs):
                accs[h] = alphas[r] * accs[h] + pv[rsl[r]]
        return tuple(ms), tuple(ls), tuple(accs)

    init = (tuple(jnp.full((tq, 1), NEG, F32) for _ in range(N_HEADS)),
            tuple(jnp.zeros((tq, 1), F32) for _ in range(N_HEADS)),
            tuple(jnp.zeros((tq, HEAD_DIM), F32) for _ in range(N_HEADS)))
    _, l_s, acc_s = lax.fori_loop(0, n_tiles, slc_step, init)

    pieces = []
    for h in range(N_HEADS):
        o_slc = acc_s[h] / jnp.maximum(l_s[h], 1e-30)
        pieces.append(o_cmp[h] * gates[:, 3 * h + 0:3 * h + 1] + o_slc * gates[:, 3 * h + 1:3 * h + 2]
                      + o_win[h] * gates[:, 3 * h + 2:3 * h + 3])
    o_ref[0] = jnp.concatenate(pieces, axis=1)


def _prompt_attention(q, gates, kc, vc, ks, vs, kw, vw):
    b, t, _ = q.shape
    nc = kc.shape[1]
    assert t % KEY_TILE == 0 and t >= WINDOW + Q_TILE
    n_sel = min(N_SEL, t // SLC_BLOCK)
    qspec = lambda n: pl.BlockSpec((1, Q_TILE, n), lambda bi, i: (bi, i, 0))
    full = lambda r: pl.BlockSpec((1, r, KV_COLS), lambda bi, i: (bi, 0, 0))
    assert t <= POS_RADIX * POS_RADIX
    kx = _position_features(t)
    return pl.pallas_call(
        functools.partial(_prompt_attn_kernel, seq=t, n_sel=n_sel),
        grid=(b, t // Q_TILE),
        in_specs=[qspec(ATTN_WIDTH), qspec(LANES), full(nc), full(nc),
                  full(t), full(t), full(t), full(t), pl.BlockSpec(kx.shape, lambda bi, i: (0, 0))],
        out_specs=qspec(ATTN_WIDTH),
        out_shape=jax.ShapeDtypeStruct((b, t, ATTN_WIDTH), F32),
        compiler_params=_cparams("arbitrary", "arbitrary"),
        name="prompt_attention",
    )(q, gates, kc, vc, ks, vs, kw, vw, kx)


def _merge_groups(per_group):
    row = lax.broadcasted_iota(I32, per_group[0].shape, 0) // Q_PER_KV
    out = per_group[0]
    for g in range(1, N_KV):
        out = jnp.where(row == g, per_group[g], out)
    return out


def _group_slice(x, g):
    return x[:, g * HEAD_DIM:(g + 1) * HEAD_DIM]


def _matmul3_kernel(a_ref, b_ref, o_ref):
    o_ref[...] = _dot3(a_ref[...], b_ref[...])


def _matmul3(a, b):
    return pl.pallas_call(
        _matmul3_kernel,
        out_shape=jax.ShapeDtypeStruct((a.shape[0], b.shape[1]), F32),
        compiler_params=pltpu.CompilerParams(vmem_limit_bytes=VMEM_LIMIT),
        name="matmul3",
    )(a, b)


def _page_specs(n_pages, page):
    def spec(o):
        return pl.BlockSpec((1, N_KV, HEAD_DIM, page),
                            lambda i, j, pt: (pt[i * n_pages + j * PAGES_PER_STEP + o], 0, 0, 0))
    return [spec(o) for o in range(PAGES_PER_STEP)]


def _sample_scores_kernel(pt_ref, ut_ref, *refs):
    k_refs, o_ref = refs[:-1], refs[-1]
    for h in range(N_HEADS):
        g = h // Q_PER_KV
        u = ut_ref[0, h]
        rows = [jnp.sum(k_ref[0, g] * u, axis=0, keepdims=True) for k_ref in k_refs]
        o_ref[0, h] = jnp.concatenate(rows, axis=0)


def _sample_scores(page_table, ut, k_pages):
    b, n_pages = page_table.shape
    page = k_pages.shape[-1]
    assert n_pages % PAGES_PER_STEP == 0
    return pl.pallas_call(
        _sample_scores_kernel,
        grid_spec=pltpu.PrefetchScalarGridSpec(
            num_scalar_prefetch=1,
            grid=(b, n_pages // PAGES_PER_STEP),
            in_specs=[pl.BlockSpec((1, N_HEADS, HEAD_DIM, page), lambda i, j, pt: (i, 0, 0, 0))]
                     + _page_specs(n_pages, page),
            out_specs=pl.BlockSpec((1, N_HEADS, PAGES_PER_STEP, page), lambda i, j, pt: (i, 0, j, 0)),
        ),
        out_shape=jax.ShapeDtypeStruct((b, N_HEADS, n_pages, page), F32),
        compiler_params=_cparams("arbitrary", "arbitrary"),
        name="sample_scores",
    )(page_table.reshape(-1), ut, *([k_pages] * PAGES_PER_STEP))


def _max_all(x):
    return jnp.max(jnp.max(x, axis=0, keepdims=True), axis=1, keepdims=True)


def _min_all(x):
    return jnp.min(jnp.min(x, axis=0, keepdims=True), axis=1, keepdims=True)


def _sum_all(x):
    return jnp.sum(jnp.sum(x, axis=0, keepdims=True), axis=1, keepdims=True)


def _sample_select_kernel(s_ref, q_ref, kcn_ref, pexp_ref, pnew_ref, sel_ref, *, past, n_new, n_sel):
    n_pages, page = s_ref.shape[2], s_ref.shape[3]
    cpp = page // CMP_BLOCK
    n_past = n_pages * cpp
    n_blocks_total = (n_past + n_new) // CMP_PER_SLC
    lane = lax.broadcasted_iota(I32, (n_pages, page), 1)
    prow = lax.broadcasted_iota(I32, (n_pages, page), 0)
    dist = (past - ((prow * cpp + lane // CMP_BLOCK) * CMP_BLOCK + (CMP_BLOCK - 1))).astype(F32)
    mask = jnp.logical_and(lane % CMP_BLOCK == 0, dist >= 0.0)

    q8 = q_ref[0] * ATTN_SCALE
    slope = _alibi_slope_col(N_HEADS, 1, 0, N_HEADS)
    kcn = jnp.concatenate([kcn_ref[0], jnp.zeros((LANES - n_new, KV_COLS), F32)], axis=0)
    new_lane = lax.broadcasted_iota(I32, (N_HEADS, LANES), 1)
    dist_n = (past - ((n_past + new_lane) * CMP_BLOCK + (CMP_BLOCK - 1))).astype(F32)
    mask_n = jnp.logical_and(dist_n >= 0.0, new_lane < n_new)
    qk_n = _merge_groups([_dot3_nt(q8, _group_slice(kcn, g)) for g in range(N_KV)])
    s_new = jnp.where(mask_n, qk_n - slope * dist_n, NEG)

    halvings = [CMP_BLOCK >> k for k in range(1, CMP_BLOCK.bit_length())]
    probs, probs_new = [], []
    for h in range(N_HEADS):
        x = s_ref[0, h]
        for sh in halvings:
            x = x + pltpu.roll(x, page - sh, 1)
        s = jnp.where(mask, x - 2.0 ** (-8.0 * (h + 1) / N_HEADS) * dist, NEG)
        sn = s_new[h:h + 1, :]
        mn = jnp.logical_and(dist_n[h:h + 1, :] >= 0.0, new_lane[h:h + 1, :] < n_new)
        m = jnp.maximum(_max_all(s), jnp.max(sn, axis=1, keepdims=True))
        p = jnp.where(mask, jnp.exp(s - m), 0.0)
        pn = jnp.where(mn, jnp.exp(sn - m), 0.0)
        den = jnp.maximum(_sum_all(p) + jnp.sum(pn, axis=1, keepdims=True), 1e-30)
        p = p / den
        probs.append(p)
        probs_new.append(pn / den)
        z = p
        for sh in reversed(halvings):
            z = z + pltpu.roll(z, sh, 1)
        pexp_ref[0, h] = z
    pnew_ref[0] = jnp.concatenate(probs_new, axis=0)

    row1 = lax.broadcasted_iota(I32, (1, LANES), 1)
    blk = jnp.where(lane % SLC_BLOCK == 0, prow * (page // SLC_BLOCK) + lane // SLC_BLOCK, -1)
    blk_n = jnp.where(row1 < n_new, n_past // CMP_PER_SLC + (row1 >> 1), -1)
    blk_f = blk.astype(F32)
    blk_nf = blk_n.astype(F32)
    out_lane = lax.broadcasted_iota(I32, (N_HEADS, LANES), 1)
    out_row = lax.broadcasted_iota(I32, (N_HEADS, LANES), 0)
    out = jnp.full((N_HEADS, LANES), -1, I32)
    for g in range(N_KV):
        imp = probs[g * Q_PER_KV]
        imp_n = probs_new[g * Q_PER_KV]
        for r in range(1, Q_PER_KV):
            imp = imp + probs[g * Q_PER_KV + r]
            imp_n = imp_n + probs_new[g * Q_PER_KV + r]
        imp = imp + pltpu.roll(imp, page - CMP_BLOCK, 1)
        s_m = jnp.where(blk >= 0, _block_scores(imp, blk, past, n_blocks_total), NOT_A_BLOCK)
        s_n = jnp.where(blk_n >= 0, _block_scores(_pair_sum(imp_n, 1), blk_n, past, n_blocks_total), NOT_A_BLOCK)
        for j in range(n_sel):
            top = jnp.maximum(_max_all(s_m), jnp.max(s_n, axis=1, keepdims=True))
            first = jnp.minimum(_min_all(jnp.where(s_m == top, blk_f, NO_INDEX)),
                                jnp.min(jnp.where(s_n == top, blk_nf, NO_INDEX), axis=1, keepdims=True))
            s_m = jnp.where(blk_f == first, PICKED, s_m)
            s_n = jnp.where(blk_nf == first, PICKED, s_n)
            pick = jnp.where(top >= 0.0, first.astype(I32), -1)
            out = jnp.where(jnp.logical_and(out_row == g, out_lane == j), pick, out)
    sel_ref[0] = out


def _sample_select(s_raw, q8, kc_new, past):
    b, _, n_pages, page = s_raw.shape
    n_new = kc_new.shape[1]
    assert CMP_PER_SLC == 2 and CMP_BLOCK == 32
    n_sel = min(N_SEL, (past // CMP_BLOCK + n_new) // CMP_PER_SLC)
    per_b = lambda *s: pl.BlockSpec((1,) + s, lambda i: (i,) + (0,) * len(s))
    return pl.pallas_call(
        functools.partial(_sample_select_kernel, past=past, n_new=n_new, n_sel=n_sel),
        grid=(b,),
        in_specs=[per_b(N_HEADS, n_pages, page), per_b(N_HEADS, HEAD_DIM), per_b(n_new, KV_COLS)],
        out_specs=[per_b(N_HEADS, n_pages, page), per_b(N_HEADS, LANES), per_b(N_HEADS, LANES)],
        out_shape=[jax.ShapeDtypeStruct((b, N_HEADS, n_pages, page), F32),
                   jax.ShapeDtypeStruct((b, N_HEADS, LANES), F32),
                   jax.ShapeDtypeStruct((b, N_HEADS, LANES), I32)],
        compiler_params=_cparams("arbitrary"),
        name="sample_select",
    )(s_raw, q8, kc_new)


def _sample_values_kernel(pt_ref, pe_ref, *refs):
    v_refs, y_ref = refs[:-1], refs[-1]

    @pl.when(pl.program_id(1) == 0)
    def _():
        y_ref[...] = jnp.zeros(y_ref.shape, F32)

    for g in range(N_KV):
        heads = range(g * Q_PER_KV, (g + 1) * Q_PER_KV)
        pe = [pe_ref[0, h] for h in heads]
        acc = [jnp.zeros(y_ref.shape[2:], F32) for _ in heads]
        for o, v_ref in enumerate(v_refs):
            v = v_ref[0, g]
            for r in range(Q_PER_KV):
                acc[r] = acc[r] + v * pe[r][o:o + 1, :]
        for r, h in enumerate(heads):
            y_ref[0, h] = y_ref[0, h] + acc[r]


def _sample_values(page_table, pexp, v_pages):
    b, n_pages = page_table.shape
    page = v_pages.shape[-1]
    return pl.pallas_call(
        _sample_values_kernel,
        grid_spec=pltpu.PrefetchScalarGridSpec(
            num_scalar_prefetch=1,
            grid=(b, n_pages // PAGES_PER_STEP),
            in_specs=[pl.BlockSpec((1, N_HEADS, PAGES_PER_STEP, page), lambda i, j, pt: (i, 0, j, 0))]
                     + _page_specs(n_pages, page),
            out_specs=pl.BlockSpec((1, N_HEADS, HEAD_DIM, page), lambda i, j, pt: (i, 0, 0, 0)),
        ),
        out_shape=jax.ShapeDtypeStruct((b, N_HEADS, HEAD_DIM, page), F32),
        compiler_params=_cparams("arbitrary", "arbitrary"),
        name="sample_values",
    )(page_table.reshape(-1), pexp, *([v_pages] * PAGES_PER_STEP))


def _new_token_terms(q8, k_row, v_row):
    s = _merge_groups([jnp.sum(q8 * _group_slice(k_row, g), axis=1, keepdims=True) for g in range(N_KV)])
    v = _merge_groups([jnp.broadcast_to(_group_slice(v_row, g), (N_HEADS, HEAD_DIM)) for g in range(N_KV)])
    return s, v


def _sample_attend_kernel(sel_ref, pt_ref, q_ref, *refs, past, n_sel, ns_past):
    page_refs = refs[:4 * n_sel]
    (ksn_ref, vsn_ref, kw_ref, vw_ref, kwn_ref, vwn_ref, gate_ref, ocmp_ref, pnew_ref, vcn_ref,
     o_ref) = refs[4 * n_sel:]
    b = pl.program_id(0)
    page = page_refs[0].shape[-1]
    spp = page // SLC_BLOCK
    q8 = q_ref[0] * ATTN_SCALE
    q8_bf = q8.astype(BF16)
    slope = _alibi_slope_col(N_HEADS, 1, 0, N_HEADS)
    lane = lax.broadcasted_iota(I32, (N_HEADS, page), 1)
    s_t, v_t = _new_token_terms(q8, ksn_ref[0], vsn_ref[0])

    scores, masks, new_scores = [], [], []
    for n in range(n_sel):
        k_refs = page_refs[4 * n:4 * n + N_KV]
        blks = [sel_ref[(b * N_KV + g) * n_sel + n] for g in range(N_KV)]
        blk_rows = _merge_groups([jnp.full((N_HEADS, page), blk, I32) for blk in blks])
        blk_col = _merge_groups([jnp.full((N_HEADS, 1), blk, I32) for blk in blks])
        qk = _merge_groups([_dot(q8_bf, k_ref[0, 0].astype(BF16)) for k_ref in k_refs])
        page_pos = blk_rows // spp
        dist = (past - (page_pos * page + lane)).astype(F32)
        in_block = (lane // SLC_BLOCK) == (blk_rows - page_pos * spp)
        cached = jnp.logical_and(blk_rows >= 0, blk_rows < ns_past)
        mask = jnp.logical_and(jnp.logical_and(in_block, cached), dist >= 0.0)
        scores.append(jnp.where(mask, qk - slope * dist, NEG))
        masks.append(mask)
        new_scores.append(jnp.where(blk_col >= ns_past, s_t, NEG))
    m = new_scores[0]
    for s, sn in zip(scores, new_scores):
        m = jnp.maximum(m, jnp.maximum(jnp.max(s, axis=1, keepdims=True), sn))
    l_tot = jnp.zeros((N_HEADS, 1), F32)
    p_new = jnp.zeros((N_HEADS, 1), F32)
    acc = jnp.zeros((N_HEADS, HEAD_DIM), F32)
    for n in range(n_sel):
        v_refs = page_refs[4 * n + N_KV:4 * n + 2 * N_KV]
        p = jnp.where(masks[n], jnp.exp(scores[n] - m), 0.0)
        p_bf = p.astype(BF16)
        l_tot = l_tot + jnp.sum(p, axis=1, keepdims=True)
        p_new = p_new + jnp.where(new_scores[n] > 0.5 * NEG, jnp.exp(new_scores[n] - m), 0.0)
        acc = acc + _merge_groups([_dot_nt(p_bf, v_ref[0, 0].astype(BF16)) for v_ref in v_refs])
    o_slc = (acc + p_new * v_t) / jnp.maximum(l_tot + p_new, 1e-30)

    w_buf = kw_ref.shape[-1]
    wl = lax.broadcasted_iota(I32, (N_HEADS, w_buf), 1)
    win_pos = past - w_buf + wl
    dist_w = (past - win_pos).astype(F32)
    mask_w = jnp.logical_and(jnp.logical_and(dist_w >= 0.0, dist_w <= float(WINDOW)), win_pos >= 0)
    qk_w = _merge_groups([_dot(q8_bf, kw_ref[0, g].astype(BF16)) for g in range(N_KV)])
    s_w = jnp.where(mask_w, qk_w - slope * dist_w, NEG)
    s_t, v_t = _new_token_terms(q8, kwn_ref[0], vwn_ref[0])
    m_w = jnp.maximum(jnp.max(s_w, axis=1, keepdims=True), s_t)
    p_w = jnp.where(mask_w, jnp.exp(s_w - m_w), 0.0)
    p_t = jnp.exp(s_t - m_w)
    den = jnp.maximum(jnp.sum(p_w, axis=1, keepdims=True) + p_t, 1e-30)
    pw_bf = p_w.astype(BF16)
    o_w = _merge_groups([_dot_nt(pw_bf, vw_ref[0, g].astype(BF16)) for g in range(N_KV)])
    o_win = (o_w + p_t * v_t) / den
    n_new = vcn_ref.shape[1]
    vcn = jnp.concatenate([vcn_ref[0], jnp.zeros((LANES - n_new, KV_COLS), F32)], axis=0).astype(BF16)
    pn_bf = pnew_ref[0].astype(BF16)
    o_cmp = ocmp_ref[0] + _merge_groups([_dot(pn_bf, _group_slice(vcn, g)) for g in range(N_KV)])
    gt = gate_ref[0]
    o_ref[0] = o_cmp * gt[:, 0:1] + o_slc * gt[:, 1:2] + o_win * gt[:, 2:3]


def _sample_attend(sel, page_table, q8, k_pages, v_pages, ks_new, vs_new, kw_state, vw_state,
                   kw_new, vw_new, gates8, o_cmp, p_new, vc_new, past):
    b, n_pages = page_table.shape
    n_sel = sel.shape[-1]
    page = k_pages.shape[-1]
    ns_past = past // SLC_BLOCK
    spp = page // SLC_BLOCK
    w_buf = kw_state.shape[-1]
    n_new = vc_new.shape[1]

    def cache_map(n, g):
        def index(i, sel_ref, pt_ref):
            blk = jnp.clip(sel_ref[(i * N_KV + g) * n_sel + n], 0, ns_past - 1)
            return (pt_ref[i * n_pages + blk // spp], g, 0, 0)
        return pl.BlockSpec((1, 1, HEAD_DIM, page), index)

    page_specs, page_args = [], []
    for n in range(n_sel):
        for arr in (k_pages, v_pages):
            for g in range(N_KV):
                page_specs.append(cache_map(n, g))
                page_args.append(arr)
    per_b = lambda *s: pl.BlockSpec((1,) + s, lambda i, sl, pt: (i,) + (0,) * len(s))
    return pl.pallas_call(
        functools.partial(_sample_attend_kernel, past=past, n_sel=n_sel, ns_past=ns_past),
        grid_spec=pltpu.PrefetchScalarGridSpec(
            num_scalar_prefetch=2,
            grid=(b,),
            in_specs=[per_b(N_HEADS, HEAD_DIM)] + page_specs
                     + [per_b(1, KV_COLS), per_b(1, KV_COLS),
                        per_b(N_KV, HEAD_DIM, w_buf), per_b(N_KV, HEAD_DIM, w_buf),
                        per_b(1, KV_COLS), per_b(1, KV_COLS),
                        per_b(N_HEADS, 3), per_b(N_HEADS, HEAD_DIM), per_b(N_HEADS, LANES),
                        per_b(n_new, KV_COLS)],
            out_specs=per_b(N_HEADS, HEAD_DIM),
        ),
        out_shape=jax.ShapeDtypeStruct((b, N_HEADS, HEAD_DIM), F32),
        compiler_params=_cparams("arbitrary"),
        name="sample_attend",
    )(sel.reshape(-1), page_table.reshape(-1), q8, *page_args,
      ks_new, vs_new, kw_state, vw_state, kw_new, vw_new, gates8, o_cmp, p_new, vc_new)


def _layernorm_silu(y, g, b):
    mu = jnp.mean(y, axis=-1, keepdims=True)
    var = jnp.mean(jnp.square(y - mu), axis=-1, keepdims=True)
    return _silu((y - mu) * lax.rsqrt(var + EPS) * g + b)


def _conv_prompt_kernel(u_ref, w_ref, b_ref, g_ref, beta_ref, o_ref, buf):
    j = pl.program_id(1)
    tt = u_ref.shape[1]
    c = buf.shape[1]
    kw = w_ref.shape[0] // SUBLANES

    @pl.when(j == 0)
    def _():
        buf[0:CONV_HALO, :] = jnp.zeros((CONV_HALO, c), F32)

    buf[CONV_HALO:CONV_HALO + tt, :] = u_ref[0]
    first = CONV_HALO - (kw - 1)
    rows = CONV_ROWS
    for r0 in range(0, tt, rows):
        acc = jnp.zeros((rows // SUBLANES, SUBLANES, c), F32)
        for r in range(SUBLANES):
            taps = range(r, kw, SUBLANES)
            win = buf[pl.ds(first + r + r0, rows + SUBLANES * (len(taps) - 1)), :]
            for t, k in enumerate(taps):
                wk = w_ref[SUBLANES * k:SUBLANES * (k + 1), :]
                tap = win[SUBLANES * t:SUBLANES * t + rows].reshape(rows // SUBLANES, SUBLANES, c)
                acc = acc + wk[None] * tap
        y = acc.reshape(rows, c) + b_ref[...]
        o_ref[0, r0:r0 + rows, :] = _layernorm_silu(y, g_ref[...], beta_ref[...])
    buf[0:CONV_HALO, :] = buf[tt:tt + CONV_HALO, :]


def _conv_prompt(u, w_dw, b_dw, ln_g, ln_b, tt):
    b, t, c = u.shape
    vec = pl.BlockSpec((1, c), lambda i, j: (0, 0))
    w_rep = jnp.repeat(w_dw, SUBLANES, axis=0)
    return pl.pallas_call(
        _conv_prompt_kernel,
        grid=(b, t // tt),
        in_specs=[pl.BlockSpec((1, tt, c), lambda i, j: (i, j, 0)),
                  pl.BlockSpec(w_rep.shape, lambda i, j: (0, 0)), vec, vec, vec],
        out_specs=pl.BlockSpec((1, tt, c), lambda i, j: (i, j, 0)),
        out_shape=jax.ShapeDtypeStruct((b, t, c), F32),
        scratch_shapes=[pltpu.VMEM((CONV_HALO + tt, c), F32)],
        compiler_params=_cparams("arbitrary", "arbitrary"),
        name="conv_prompt",
    )(u, w_rep, b_dw, ln_g, ln_b)


def _conv_sample_kernel(up_ref, w_ref, b_ref, g_ref, beta_ref, o_ref):
    y = jnp.sum(up_ref[...] * w_ref[...][None, :, :], axis=1)
    o_ref[...] = _layernorm_silu(y + b_ref[...], g_ref[...], beta_ref[...])


def _conv_sample(up, w_dw, b_dw, ln_g, ln_b):
    b, kw, c = up.shape
    return pl.pallas_call(
        _conv_sample_kernel,
        out_shape=jax.ShapeDtypeStruct((b, c), F32),
        name="conv_sample",
    )(up, w_dw, b_dw, ln_g, ln_b)


def _merge_router_kernel(oa_ref, oc_ref, x_ref, gate_ref, shift_ref, scale_ref, goa_ref, goc_ref,
                         wout_ref, g2_ref, rwh_ref, rwl_ref, rb_ref, cnt_in_ref,
                         x1_ref, h2_ref, eidx_ref, wts_ref, rank_ref, cnt_out_ref, run):
    first = jnp.logical_and(pl.program_id(0) == 0, pl.program_id(1) == 0)

    @pl.when(first)
    def _():
        run[...] = cnt_in_ref[...]

    a = _rms(oa_ref[0], goa_ref[...])
    c = _rms(oc_ref[0], goc_ref[...])
    cat = jnp.concatenate([a, c], axis=1).astype(BF16)
    x1 = x_ref[0] + gate_ref[0] * _dot(cat, wout_ref[...])
    x1_ref[0] = x1
    h2 = _rms(x1, g2_ref[...]) * (1.0 + scale_ref[0]) + shift_ref[0]
    h2_ref[0] = h2

    hh, hl = _split2(h2)
    logits = _dot_nt(rwh_ref[...], hh) + (_dot_nt(rwl_ref[...], hh) + _dot_nt(rwh_ref[...], hl))
    aff = _sigmoid(logits)
    n_exp, tm = aff.shape
    row_f = lax.broadcasted_iota(I32, (n_exp, tm), 0).astype(F32)
    s = aff + rb_ref[...]
    experts, weights = [], []
    for _ in range(TOP_K):
        m = jnp.max(s, axis=0, keepdims=True)
        e = jnp.min(jnp.where(s == m, row_f, NO_INDEX), axis=0, keepdims=True)
        pick = row_f == e
        experts.append(e)
        weights.append(jnp.sum(jnp.where(pick, aff, 0.0), axis=0, keepdims=True))
        s = jnp.where(pick, NEG, s)
    total = weights[0]
    for w in weights[1:]:
        total = total + w

    hot = jnp.where(s == NEG, 1.0, 0.0)
    r_i = lax.broadcasted_iota(I32, (tm, tm), 0)
    c_i = lax.broadcasted_iota(I32, (tm, tm), 1)
    earlier = jnp.where(r_i < c_i, 1.0, 0.0).astype(BF16)
    before = _dot(hot.astype(BF16), earlier) + run[...]
    ranks = [jnp.sum(jnp.where(row_f == e, before, 0.0), axis=0, keepdims=True) for e in experts]
    eidx_ref[0] = jnp.concatenate(experts, axis=0).astype(I32)
    wts_ref[0] = jnp.concatenate([ROUTE_SCALE * w / total for w in weights], axis=0)
    rank_ref[0] = jnp.concatenate(ranks, axis=0).astype(I32)
    run[...] = run[...] + jnp.sum(hot, axis=1, keepdims=True)
    cnt_out_ref[...] = run[...]


def _merge_router(o_attn, o_conv, x, gate, shift, scale, goa, goc, wout_bf, g2, rw_hi, rw_lo, rb,
                  cnt_in, tm):
    b, t, d = x.shape
    n_exp = rw_hi.shape[0]
    row = lambda n: pl.BlockSpec((1, tm, n), lambda i, j: (i, j, 0))
    pick = pl.BlockSpec((1, TOP_K, tm), lambda i, j: (i, 0, j))
    const = lambda shape: pl.BlockSpec(shape, lambda i, j: (0,) * len(shape))
    sds = lambda n, dt: jax.ShapeDtypeStruct((b, t, n), dt)
    picks = lambda dt: jax.ShapeDtypeStruct((b, TOP_K, t), dt)
    return pl.pallas_call(
        _merge_router_kernel,
        grid=(b, t // tm),
        in_specs=[row(o_attn.shape[-1]), row(o_conv.shape[-1]), row(d),
                  _mod_spec(gate, tm, d), _mod_spec(shift, tm, d), _mod_spec(scale, tm, d),
                  const(goa.shape), const(goc.shape), const(wout_bf.shape), const(g2.shape),
                  const(rw_hi.shape), const(rw_lo.shape), const(rb.shape), const(cnt_in.shape)],
        out_specs=[row(d), row(d), pick, pick, pick, const((n_exp, 1))],
        out_shape=[sds(d, F32), sds(d, F32), picks(I32), picks(F32), picks(I32),
                   jax.ShapeDtypeStruct((n_exp, 1), F32)],
        scratch_shapes=[pltpu.VMEM((n_exp, 1), F32)],
        compiler_params=_cparams("arbitrary", "arbitrary"),
        name="merge_router",
    )(o_attn, o_conv, x, gate, shift, scale, goa, goc, wout_bf, g2, rw_hi, rw_lo, rb, cnt_in)


def _row_copy(src_hbm, dst, src_row, dst_row, sem, chunks):
    return pltpu.make_async_copy(src_hbm.at[pl.ds(src_row * chunks, chunks)],
                                 dst.at[pl.ds(dst_row * chunks, chunks)], sem)


def _slot(start_ref, e_ref, r_ref, idx):
    return start_ref[e_ref[0, 0, idx]] + r_ref[0, 0, idx]


def _dispatch_kernel(cnt_ref, end_ref, start_ref, e_ref, r_ref, h_ref, xs_hbm, zbuf, zsem, sem,
                     *, tokens, rows, chunks, n_blocks):
    j = pl.program_id(0)
    n_exp = cnt_ref.shape[0]
    blk_rows = rows * chunks

    def zero_block(blk):
        return pltpu.make_async_copy(zbuf, xs_hbm.at[pl.ds(blk * blk_rows, blk_rows)], zsem)

    @pl.when(j == 0)
    def _():
        zbuf[...] = jnp.zeros(zbuf.shape, F32)
        n_active = end_ref[n_exp - 1] // rows

        def zero_tail(e, issued):
            partial = cnt_ref[e] % rows != 0

            @pl.when(partial)
            def _():
                zero_block(end_ref[e] // rows - 1).start()

            return issued + partial.astype(I32)

        def zero_unused(blk, _):
            zero_block(blk).start()
            return 0

        def drain_zero(_, c):
            zero_block(0).wait()
            return c

        issued = lax.fori_loop(0, n_exp, zero_tail, 0)
        lax.fori_loop(n_active, n_blocks, zero_unused, 0)
        lax.fori_loop(0, issued + (n_blocks - n_active), drain_zero, 0)

    def issue(r, _):
        for k in range(TOP_K):
            _row_copy(h_ref, xs_hbm, r, _slot(start_ref, e_ref, r_ref, r * TOP_K + k), sem, chunks).start(
                priority=k % DMA_PRIORITIES)
        return 0

    def drain(r, _):
        for k in range(TOP_K):
            _row_copy(h_ref, xs_hbm, 0, 0, sem, chunks).wait()
        return 0

    lax.fori_loop(0, tokens, issue, 0)
    lax.fori_loop(0, tokens, drain, 0)


def _dispatch(counts, pad_end, pad_start, e_idx, rank, h_rows, tokens, rows, chunks, n_blocks):
    n_tiles = e_idx.shape[0]
    picks = pl.BlockSpec((1, 1, tokens * TOP_K), lambda j, c, e, s: (j, 0, 0), memory_space=pltpu.SMEM)
    return pl.pallas_call(
        functools.partial(_dispatch_kernel, tokens=tokens, rows=rows, chunks=chunks, n_blocks=n_blocks),
        grid_spec=pltpu.PrefetchScalarGridSpec(
            num_scalar_prefetch=3,
            grid=(n_tiles,),
            in_specs=[picks, picks,
                      pl.BlockSpec((tokens * chunks, LANES), lambda j, c, e, s: (j, 0))],
            out_specs=pl.BlockSpec(memory_space=pl.ANY),
            scratch_shapes=[pltpu.VMEM((rows * chunks, LANES), F32),
                            pltpu.SemaphoreType.DMA(()), pltpu.SemaphoreType.DMA(())],
        ),
        out_shape=jax.ShapeDtypeStruct((n_blocks * rows * chunks, LANES), F32),
        compiler_params=_cparams("arbitrary"),
        name="moe_dispatch",
    )(counts, pad_end, pad_start, e_idx, rank, h_rows)


def _expert_kernel(be_ref, nact_ref, first_ref, slot_ref, next_ref, x_ref, wg_hbm, wu_hbm, wd_hbm, y_ref,
                   wg_buf, wu_buf, wd_buf, sems, *, rows, chunks):
    j = pl.program_id(0)

    def weight_copies(expert, slot):
        return [pltpu.make_async_copy(w_hbm.at[expert], buf.at[slot], sems.at[slot, i])
                for i, (w_hbm, buf) in enumerate(((wg_hbm, wg_buf), (wu_hbm, wu_buf), (wd_hbm, wd_buf)))]

    @pl.when(j < nact_ref[0])
    def _():
        slot = slot_ref[j]

        @pl.when(j == 0)
        def _():
            for cp in weight_copies(be_ref[0], 0):
                cp.start()

        @pl.when(first_ref[j] == 1)
        def _():
            for cp in weight_copies(be_ref[j], slot):
                cp.wait()

            @pl.when(next_ref[j] >= 0)
            def _():
                for cp in weight_copies(next_ref[j], 1 - slot):
                    cp.start()

        f = wg_buf.shape[2]
        gate = jnp.zeros((rows, f), F32)
        up = jnp.zeros((rows, f), F32)
        for c in range(0, chunks, 2):
            xc = jnp.concatenate([x_ref[pl.ds(c, rows, stride=chunks), :],
                                  x_ref[pl.ds(c + 1, rows, stride=chunks), :]], axis=1).astype(BF16)
            cs = pl.ds(c * LANES, 2 * LANES)
            gate = gate + _dot(xc, wg_buf[slot, cs, :].astype(BF16))
            up = up + _dot(xc, wu_buf[slot, cs, :].astype(BF16))
        h = (_silu(gate) * up).astype(BF16)
        y = _dot(h, wd_buf[slot].astype(BF16))
        for c in range(chunks):
            y_ref[pl.ds(c, rows, stride=chunks), :] = y[:, c * LANES:(c + 1) * LANES]


def _experts(blk_expert, n_active, pad_end, xs, wg, wu, wd, rows, chunks):
    n_blocks = blk_expert.shape[0]
    n_exp, d, f = wg.shape
    first = jnp.concatenate([jnp.ones((1,), I32), (blk_expert[1:] != blk_expert[:-1]).astype(I32)])
    slot = (jnp.cumsum(first) - 1) % 2
    run_end = pad_end[blk_expert] // rows
    nxt = jnp.where(run_end < n_active[0], blk_expert[jnp.minimum(run_end, n_blocks - 1)], -1)
    block = pl.BlockSpec((rows * chunks, LANES), lambda j, be, na, *_: (jnp.minimum(j, na[0] - 1), 0))
    hbm = pl.BlockSpec(memory_space=pl.ANY)
    n_prefetch = 5
    return pl.pallas_call(
        functools.partial(_expert_kernel, rows=rows, chunks=chunks),
        grid_spec=pltpu.PrefetchScalarGridSpec(
            num_scalar_prefetch=n_prefetch,
            grid=(n_blocks,),
            in_specs=[block, hbm, hbm, hbm],
            out_specs=block,
            scratch_shapes=[pltpu.VMEM((2, d, f), F32), pltpu.VMEM((2, d, f), F32),
                            pltpu.VMEM((2, f, d), F32), pltpu.SemaphoreType.DMA((2, 3))],
        ),
        out_shape=jax.ShapeDtypeStruct(xs.shape, F32),
        input_output_aliases={n_prefetch: 0},
        compiler_params=_cparams("arbitrary"),
        name="moe_experts",
    )(blk_expert, n_active, first, slot.astype(I32), nxt.astype(I32), xs, wg, wu, wd)


def _combine_kernel(start_ref, e_ref, r_ref, w_ref, x1_ref, h2_ref, gate_ref, wsg_ref, wsu_ref, wsd_ref,
                    gf_ref, ys_hbm, o_ref, buf, sem, *, chunks):
    tm = x1_ref.shape[1]

    def issue(r, _):
        for k in range(TOP_K):
            _row_copy(ys_hbm, buf.at[k], _slot(start_ref, e_ref, r_ref, r * TOP_K + k), r, sem, chunks).start(
                priority=k % DMA_PRIORITIES)
        return 0

    def drain(r, _):
        for k in range(TOP_K):
            _row_copy(ys_hbm, buf.at[k], 0, r, sem, chunks).wait()
        return 0

    lax.fori_loop(0, tm, issue, 0)
    h_bf = h2_ref[0].astype(BF16)
    hid = (_silu(_dot(h_bf, wsg_ref[...])) * _dot(h_bf, wsu_ref[...])).astype(BF16)
    shared = _dot(hid, wsd_ref[...])
    lax.fori_loop(0, tm, drain, 0)

    w = w_ref[0]
    cols = []
    for c in range(chunks):
        tot = jnp.zeros((tm, LANES), F32)
        for k in range(TOP_K):
            tot = tot + buf[k, pl.ds(c, tm, stride=chunks), :] * w[:, k:k + 1]
        cols.append(tot)
    routed = jnp.concatenate(cols, axis=1)
    x2 = x1_ref[0] + gate_ref[0] * (routed + shared)
    o_ref[0] = _rms(x2, gf_ref[...])


def _combine(pad_start, e_idx, rank, wts, x1, h2, gate, wsg_bf, wsu_bf, wsd_bf, gf, ys, tm, chunks):
    b, t, d = x1.shape
    nt = t // tm
    row = lambda n: pl.BlockSpec((1, tm, n), lambda i, j, *_: (i, j, 0))
    const = lambda shape: pl.BlockSpec(shape, lambda i, j, *_: (0,) * len(shape))
    picks = pl.BlockSpec((1, 1, tm * TOP_K), lambda i, j, *_: (i * nt + j, 0, 0), memory_space=pltpu.SMEM)
    return pl.pallas_call(
        functools.partial(_combine_kernel, chunks=chunks),
        grid_spec=pltpu.PrefetchScalarGridSpec(
            num_scalar_prefetch=1,
            grid=(b, nt),
            in_specs=[picks, picks, row(TOP_K), row(d), row(d), _mod_spec(gate, tm, d),
                      const(wsg_bf.shape), const(wsu_bf.shape), const(wsd_bf.shape), const(gf.shape),
                      pl.BlockSpec(memory_space=pl.ANY)],
            out_specs=row(d),
            scratch_shapes=[pltpu.VMEM((TOP_K, tm * chunks, LANES), F32), pltpu.SemaphoreType.DMA(())],
        ),
        out_shape=jax.ShapeDtypeStruct((b, t, d), F32),
        compiler_params=_cparams("arbitrary", "arbitrary"),
        name="moe_combine",
    )(pad_start, e_idx, rank, wts, x1, h2, gate, wsg_bf, wsu_bf, wsd_bf, gf, ys)


def _split_mod(mod, per_token):
    parts = jnp.split(mod, 6, axis=-1)
    if per_token:
        return [p[None] for p in parts]
    return [p[:, None, :] for p in parts]


def _padded_in_weight(w_in, conv_width):
    n_gate = 3 * N_HEADS
    o = ATTN_WIDTH + 6 * KV_COLS
    main = w_in[:, :o]
    gates = jnp.pad(w_in[:, o:o + n_gate], ((0, 0), (0, LANES - n_gate)))
    glu = w_in[:, o + n_gate:o + n_gate + 2 * conv_width]
    return jnp.concatenate([main, gates, glu], axis=1).astype(BF16)


def _cmp_rows(x):
    return x.reshape(x.shape[:-2] + (x.shape[-2] // CMP_BLOCK, CMP_BLOCK * KV_COLS))


def _largest_tile(n, cap):
    best = [k for k in range(SUBLANES, cap + 1, SUBLANES) if n % k == 0]
    assert best, (n, cap)
    return best[-1]


def _kv5(x):
    return x.reshape(x.shape[:-1] + (N_KV, HEAD_DIM))[None]


def kernel(x_prompt, x_sample, cache_k_cmp, cache_v_cmp, cache_k_slc, cache_v_slc, state_k_win, state_v_win, state_conv, page_table, c_prompt, c_sample, norm1_g, norm2_g, w_ada, b_ada, w_in, w_cmp_k, w_cmp_v, w_dw, b_dw, ln_conv_g, ln_conv_b, g_out_attn, g_out_conv, w_out, router_w, router_b, w_exp_gate, w_exp_up, w_exp_down, w_sh_gate, w_sh_up, w_sh_down, norm_f_g):
    assert w_ada.shape[0] == 1, "single layer"
    bp, t, d = x_prompt.shape
    bs, s_new, _ = x_sample.shape
    assert s_new == 1
    n_pool, page = cache_k_cmp.shape[1], cache_k_cmp.shape[2]
    n_pages = page_table.shape[1]
    past = n_pages * page
    conv_width = state_conv.shape[-1]
    n_exp = router_w.shape[-1]
    chunks = d // LANES
    tm = min(ROW_TILE, t)

    w_in_bf = _padded_in_weight(w_in[0], conv_width)
    wck = _compress_weight(w_cmp_k[0])
    wcv = _compress_weight(w_cmp_v[0])
    wout_bf = w_out[0].astype(BF16)
    rw_t = router_w[0].T
    rw_hi = rw_t.astype(BF16)
    rw_lo = (rw_t - rw_hi.astype(F32)).astype(BF16)
    wsg_bf, wsu_bf, wsd_bf = (w[0].astype(BF16) for w in (w_sh_gate, w_sh_up, w_sh_down))
    gf = norm_f_g[None, :]

    n_c = bp + bs
    c_all = jnp.concatenate([c_prompt, c_sample], axis=0)
    c_all = jnp.pad(c_all, ((0, (-n_c) % SUBLANES), (0, 0)))
    mod = _modulation(c_all, w_ada[0], b_ada)
    mp = _split_mod(mod[:bp], per_token=False)
    ms = _split_mod(mod[bp:n_c], per_token=True)

    (q_p, kc_p, vc_p, ks_p, vs_p, kw_p, vw_p, gate_p, u_p) = _in_proj(
        x_prompt, mp[0], mp[1], norm1_g, w_in_bf, tm)
    nc_p = t // CMP_BLOCK
    kcc, vcc = _compress(_cmp_rows(kc_p).reshape(bp * nc_p, -1), _cmp_rows(vc_p).reshape(bp * nc_p, -1),
                         wck, wcv)
    o_attn_p = _prompt_attention(q_p, gate_p, kcc.reshape(bp, nc_p, KV_COLS), vcc.reshape(bp, nc_p, KV_COLS),
                                 ks_p, vs_p, kw_p, vw_p)
    o_conv_p = _conv_prompt(u_p, w_dw[0], b_dw, ln_conv_g, ln_conv_b, tm)

    xs_row = x_sample.reshape(1, bs, d)
    (q_s, kc_s, vc_s, ks_s, vs_s, kw_s, vw_s, gate_s, u_s) = _in_proj(
        xs_row, ms[0], ms[1], norm1_g, w_in_bf, bs)
    q8 = q_s.reshape(bs, N_HEADS, HEAD_DIM)
    gates8 = gate_s[0, :, :3 * N_HEADS].reshape(bs, N_HEADS, 3)
    pages_t = lambda c: jnp.transpose(c[0], (0, 2, 3, 1))
    tail = (-(past + s_new)) % SLC_BLOCK
    n_new = (s_new + tail) // CMP_BLOCK
    tail_rows = lambda x: _cmp_rows(jnp.pad(x[0][:, None, :], ((0, 0), (0, tail), (0, 0)))).reshape(bs * n_new, -1)
    pad_rows = (-(bs * n_new)) % SUBLANES
    kc_new, vc_new = _compress(jnp.pad(tail_rows(kc_s), ((0, pad_rows), (0, 0))),
                               jnp.pad(tail_rows(vc_s), ((0, pad_rows), (0, 0))), wck, wcv)
    kc_new = kc_new[:bs * n_new].reshape(bs, n_new, KV_COLS)
    vc_new = vc_new[:bs * n_new].reshape(bs, n_new, KV_COLS)
    reps = page // CMP_BLOCK
    wk_fold = jnp.transpose(w_cmp_k[0], (2, 1, 0)).reshape(HEAD_DIM, HEAD_DIM * CMP_BLOCK)
    ut = _matmul3(q8.reshape(bs * N_HEADS, HEAD_DIM) * ATTN_SCALE, wk_fold)
    ut = jnp.tile(ut.reshape(bs, N_HEADS, HEAD_DIM, CMP_BLOCK), (1, 1, 1, reps))
    s_raw = _sample_scores(page_table, ut, pages_t(cache_k_cmp))
    p_exp, p_new, sel = _sample_select(s_raw, q8, kc_new, past)
    y_acc = _sample_values(page_table, p_exp, pages_t(cache_v_cmp))
    wv_fold = jnp.tile(jnp.transpose(w_cmp_v[0], (1, 0, 2)), (1, reps, 1)).reshape(HEAD_DIM * page, HEAD_DIM)
    o_cmp_s = _matmul3(y_acc.reshape(bs * N_HEADS, HEAD_DIM * page), wv_fold).reshape(bs, N_HEADS, HEAD_DIM)
    n_sel = min(N_SEL, (past // CMP_BLOCK + n_new) // CMP_PER_SLC)
    sel = sel[:, :N_KV, :n_sel]
    row3 = lambda x: x[0][:, None, :]
    o_attn_s = _sample_attend(
        sel, page_table, q8, pages_t(cache_k_slc), pages_t(cache_v_slc), row3(ks_s), row3(vs_s),
        pages_t(state_k_win), pages_t(state_v_win), row3(kw_s), row3(vw_s), gates8, o_cmp_s, p_new, vc_new,
        past)
    o_attn_s = o_attn_s.reshape(1, bs, ATTN_WIDTH)
    up_s = jnp.concatenate([state_conv[0], u_s[0][:, None, :]], axis=1)
    o_conv_s = _conv_sample(up_s, w_dw[0], b_dw, ln_conv_g, ln_conv_b)[None]

    router = functools.partial(_merge_router, goa=g_out_attn, goc=g_out_conv, wout_bf=wout_bf, g2=norm2_g,
                               rw_hi=rw_hi, rw_lo=rw_lo, rb=router_b[0][:, None])
    x1_p, h2_p, e_p, w_p, r_p, cnt = router(o_attn_p, o_conv_p, x_prompt, mp[2], mp[3], mp[4],
                                            cnt_in=jnp.zeros((n_exp, 1), F32), tm=tm)
    x1_s, h2_s, e_s, w_s, r_s, cnt = router(o_attn_s, o_conv_s, xs_row, ms[2], ms[3], ms[4],
                                            cnt_in=cnt, tm=bs)

    n_tok = bp * t + bs
    counts = cnt[:, 0].astype(I32)
    padded = (counts + MOE_ROWS - 1) // MOE_ROWS * MOE_ROWS
    pad_end = jnp.cumsum(padded)
    pad_end = pad_end.astype(I32)
    pad_start = pad_end - padded
    n_blocks = -(-(n_tok * TOP_K) // MOE_ROWS) + n_exp
    blk_first = jnp.arange(n_blocks, dtype=I32) * MOE_ROWS
    blk_expert = jnp.minimum(jnp.sum(pad_end[None, :] <= blk_first[:, None], axis=1), n_exp - 1).astype(I32)
    n_active = pad_end[-1:] // MOE_ROWS
    picks = lambda a: jnp.transpose(a, (0, 2, 1))
    w_p, w_s = picks(w_p), picks(w_s)
    e_all = jnp.concatenate([picks(e_p).reshape(-1, TOP_K), picks(e_s).reshape(-1, TOP_K)], axis=0)
    r_all = jnp.concatenate([picks(r_p).reshape(-1, TOP_K), picks(r_s).reshape(-1, TOP_K)], axis=0)

    tile = _largest_tile(n_tok, 512)
    h_rows = jnp.concatenate([h2_p.reshape(-1, d), h2_s.reshape(-1, d)], axis=0).reshape(n_tok * chunks, LANES)
    xs = _dispatch(counts, pad_end, pad_start, e_all.reshape(n_tok // tile, 1, tile * TOP_K),
                   r_all.reshape(n_tok // tile, 1, tile * TOP_K), h_rows, tile, MOE_ROWS, chunks, n_blocks)
    ys = _experts(blk_expert, n_active, pad_end, xs, w_exp_gate[0], w_exp_up[0], w_exp_down[0], MOE_ROWS, chunks)

    comb = functools.partial(_combine, pad_start, wsg_bf=wsg_bf, wsu_bf=wsu_bf, wsd_bf=wsd_bf, gf=gf, ys=ys,
                             chunks=chunks)
    tiles_p = (bp * (t // tm), 1, tm * TOP_K)
    y_prompt = comb(picks(e_p).reshape(tiles_p), picks(r_p).reshape(tiles_p), w_p, x1_p, h2_p, mp[5], tm=tm)
    y_sample = comb(picks(e_s).reshape(1, 1, bs * TOP_K), picks(r_s).reshape(1, 1, bs * TOP_K), w_s, x1_s, h2_s,
                    ms[5], tm=bs).reshape(bs, 1, d)

    win = min(WINDOW, t)
    hist = state_conv.shape[2]
    out_p = [_kv5(a) for a in (kc_p, vc_p, ks_p, vs_p, kw_p[:, t - win:], vw_p[:, t - win:])]
    conv_p = u_p[:, t - hist:][None]
    out_s = [_kv5(a[0][:, None, :]) for a in (kc_s, vc_s, ks_s, vs_s)]
    w_buf = state_k_win.shape[2]
    kw_buf = jnp.concatenate([state_k_win, _kv5(kw_s[0][:, None, :])], axis=2)[:, :, -w_buf:]
    vw_buf = jnp.concatenate([state_v_win, _kv5(vw_s[0][:, None, :])], axis=2)[:, :, -w_buf:]
    conv_s = up_s[:, -hist:][None]
    return (y_prompt, y_sample, *out_p, conv_p, *out_s, kw_buf, vw_buf, conv_s)
```

```python
import functools

import jax
import jax.numpy as jnp
from jax import lax
from jax.experimental import pallas as pl
from jax.experimental.pallas import tpu as pltpu

F32 = jnp.float32
BF16 = jnp.bfloat16
I32 = jnp.int32

N_HEADS = 8
HEAD_DIM = 64
N_KV = 2
Q_PER_KV = N_HEADS // N_KV
ATTN_WIDTH = N_HEADS * HEAD_DIM
KV_COLS = N_KV * HEAD_DIM
CMP_BLOCK = 32
SLC_BLOCK = 64
CMP_PER_SLC = SLC_BLOCK // CMP_BLOCK
N_SEL = 16
WINDOW = 512
TOP_K = 8
ROUTE_SCALE = 2.5
EPS = 1e-6
FORCED = 1e4
NEG = -1e30
ATTN_SCALE = HEAD_DIM ** -0.5
PICKED = -2.0
NOT_A_BLOCK = -4.0
NO_INDEX = 1e9

LANES = 128
SUBLANES = 8
VMEM_LIMIT = 56 * 1024 * 1024
DMA_PRIORITIES = 2

ROW_TILE = 512
Q_TILE = 256
KEY_TILE = 1024
CMP_ROW_TILE = 512
MOE_ROWS = 256
CONV_HALO = 32
POS_RADIX = 256
CONV_ROWS = 32
PAGES_PER_STEP = 64


def _cparams(*sem):
    return pltpu.CompilerParams(dimension_semantics=sem, vmem_limit_bytes=VMEM_LIMIT)


def _dot(a, b):
    return jnp.dot(a, b, preferred_element_type=F32)


def _dot_nt(a, b):
    return lax.dot_general(a, b, (((1,), (1,)), ((), ())), preferred_element_type=F32)


def _dot_tn(a, b):
    return lax.dot_general(a, b, (((0,), (0,)), ((), ())), preferred_element_type=F32)


def _split2(x):
    hi = x.astype(BF16)
    lo = (x - hi.astype(F32)).astype(BF16)
    return hi, lo


def _dot3(a, b):
    ah, al = _split2(a)
    bh, bl = _split2(b)
    return _dot(ah, bh) + (_dot(ah, bl) + _dot(al, bh))


def _dot3_nt(a, b):
    ah, al = _split2(a)
    bh, bl = _split2(b)
    return _dot_nt(ah, bh) + (_dot_nt(ah, bl) + _dot_nt(al, bh))


def _sigmoid(x):
    return 1.0 / (1.0 + jnp.exp(-x))


def _silu(x):
    return x * _sigmoid(x)


def _rms(x, g):
    return x * lax.rsqrt(jnp.mean(x * x, axis=-1, keepdims=True) + EPS) * g


def _alibi_slope_col(rows, rows_per_head, first_head, n_heads):
    r = lax.broadcasted_iota(I32, (rows, 1), 0) // rows_per_head
    out = jnp.zeros((rows, 1), F32)
    for k in range(n_heads):
        out = jnp.where(r == k, 2.0 ** (-8.0 * (first_head + k + 1) / N_HEADS), out)
    return out


def _modulation_kernel(c_ref, w_ref, b_ref, o_ref):
    o_ref[...] = _dot3(c_ref[...], w_ref[...]) + b_ref[...]


def _modulation(c, w, b):
    m, d = c.shape
    n = w.shape[1]
    tn = 768
    return pl.pallas_call(
        _modulation_kernel,
        grid=(n // tn,),
        in_specs=[pl.BlockSpec((m, d), lambda j: (0, 0)),
                  pl.BlockSpec((d, tn), lambda j: (0, j)),
                  pl.BlockSpec((1, tn), lambda j: (0, j))],
        out_specs=pl.BlockSpec((m, tn), lambda j: (0, j)),
        out_shape=jax.ShapeDtypeStruct((m, n), F32),
        compiler_params=_cparams("arbitrary"),
        name="modulation",
    )(c, w, b)


def _mod_spec(mod, tm, d):
    if mod.shape[1] == 1:
        return pl.BlockSpec((1, 1, d), lambda i, j, *_: (i, 0, 0))
    return pl.BlockSpec((1, tm, d), lambda i, j, *_: (i, j, 0))


def _in_proj_kernel(x_ref, shift_ref, scale_ref, g_ref, w_ref,
                    q_ref, kc_ref, vc_ref, ks_ref, vs_ref, kw_ref, vw_ref, gate_ref, u_ref):
    x = x_ref[0]
    h = _rms(x, g_ref[...]) * (1.0 + scale_ref[0]) + shift_ref[0]
    z = _dot(h.astype(BF16), w_ref[...])
    q_ref[0] = z[:, :ATTN_WIDTH]
    o = ATTN_WIDTH
    for ref in (kc_ref, vc_ref, ks_ref, vs_ref, kw_ref, vw_ref):
        ref[0] = z[:, o:o + KV_COLS]
        o += KV_COLS
    gate_ref[0] = _sigmoid(z[:, o:o + LANES])
    o += LANES
    cw = u_ref.shape[-1]
    u_ref[0] = z[:, o:o + cw] * _sigmoid(z[:, o + cw:o + 2 * cw])


def _in_proj(x, shift, scale, g, w_bf, tm):
    b, t, d = x.shape
    cw = (w_bf.shape[1] - ATTN_WIDTH - 6 * KV_COLS - LANES) // 2
    row = lambda n: pl.BlockSpec((1, tm, n), lambda i, j: (i, j, 0))
    sds = lambda n: jax.ShapeDtypeStruct((b, t, n), F32)
    return pl.pallas_call(
        _in_proj_kernel,
        grid=(b, t // tm),
        in_specs=[row(d), _mod_spec(shift, tm, d), _mod_spec(scale, tm, d),
                  pl.BlockSpec((1, d), lambda i, j: (0, 0)),
                  pl.BlockSpec(w_bf.shape, lambda i, j: (0, 0))],
        out_specs=[row(ATTN_WIDTH)] + [row(KV_COLS)] * 6 + [row(LANES), row(cw)],
        out_shape=[sds(ATTN_WIDTH)] + [sds(KV_COLS)] * 6 + [sds(LANES), sds(cw)],
        compiler_params=_cparams("arbitrary", "arbitrary"),
        name="in_proj",
    )(x, shift, scale, g, w_bf)


def _compress_kernel(k_ref, v_ref, wk_ref, wv_ref, ko_ref, vo_ref):
    ko_ref[...] = _dot3(k_ref[...], wk_ref[...])
    vo_ref[...] = _dot3(v_ref[...], wv_ref[...])


def _compress(k_rows, v_rows, wk, wv):
    r, kdim = k_rows.shape
    tr = min(CMP_ROW_TILE, r)
    assert r % tr == 0
    rows = pl.BlockSpec((tr, kdim), lambda i: (i, 0))
    wspec = pl.BlockSpec((kdim, KV_COLS), lambda i: (0, 0))
    ospec = pl.BlockSpec((tr, KV_COLS), lambda i: (i, 0))
    return pl.pallas_call(
        _compress_kernel,
        grid=(r // tr,),
        in_specs=[rows, rows, wspec, wspec],
        out_specs=[ospec, ospec],
        out_shape=[jax.ShapeDtypeStruct((r, KV_COLS), F32)] * 2,
        compiler_params=_cparams("arbitrary"),
        name="compress",
    )(k_rows, v_rows, wk, wv)


def _compress_weight(w):
    eye = jnp.eye(N_KV, dtype=w.dtype)
    big = jnp.einsum('lde,gh->lgdhe', w, eye)
    return big.reshape(CMP_BLOCK * KV_COLS, KV_COLS)


def _pair_sum(x, axis):
    n = x.shape[axis]
    idx = lax.broadcasted_iota(I32, x.shape, axis)
    nxt = pltpu.roll(x, n - 1, axis)
    prv = pltpu.roll(x, 1, axis)
    return x + jnp.where((idx & 1) == 0, nxt, prv)


def _block_scores(imp, blk, q_pos, n_blocks_total):
    cur = q_pos // SLC_BLOCK
    valid = jnp.logical_and(blk * SLC_BLOCK <= q_pos, blk < n_blocks_total)
    forced = jnp.logical_or(blk == 0, jnp.logical_or(blk == cur, blk == cur - 1))
    return jnp.where(valid, jnp.where(forced, FORCED, imp), -1.0)


def _select_blocks(score, blk, n_sel):
    blk_f = blk.astype(F32)
    s = score
    for _ in range(n_sel):
        m = jnp.max(s, axis=0, keepdims=True)
        first = jnp.min(jnp.where(s == m, blk_f, NO_INDEX), axis=0, keepdims=True)
        s = jnp.where(blk_f == first, PICKED, s)
    return jnp.where(jnp.logical_and(s == PICKED, score >= 0.0), 1.0, 0.0)


def _position_features(n):
    pos = jnp.arange(n, dtype=I32)[:, None]
    lane = jnp.arange(HEAD_DIM, dtype=I32)[None, :]
    feat = jnp.where(lane < 2, 1, jnp.where(lane == 2, pos // POS_RADIX, jnp.where(lane == 3, pos % POS_RADIX, 0)))
    return feat.astype(BF16)


def _query_position_features(q_pos, slope):
    lane = lax.broadcasted_iota(I32, (q_pos.shape[0], HEAD_DIM), 1)
    hi = (q_pos // POS_RADIX).astype(F32) * (-slope * POS_RADIX)
    lo = (q_pos % POS_RADIX).astype(F32) * (-slope)
    return jnp.where(lane == 0, hi, jnp.where(lane == 1, lo, jnp.where(
        lane == 2, slope * POS_RADIX, jnp.where(lane == 3, slope, 0.0))))


def _prompt_attn_kernel(q_ref, gate_ref, kc_ref, vc_ref, ks_ref, vs_ref, kw_ref, vw_ref, kx_ref, o_ref,
                        *, seq, n_sel):
    i = pl.program_id(1)
    tq = Q_TILE
    nc = kc_ref.shape[1]
    q_blk = q_ref[0] * ATTN_SCALE
    gates = gate_ref[0]
    slopes = [2.0 ** (-8.0 * (h + 1) / N_HEADS) for h in range(N_HEADS)]
    group_heads = [list(range(g * Q_PER_KV, (g + 1) * Q_PER_KV)) for g in range(N_KV)]
    gsl = [slice(g * HEAD_DIM, (g + 1) * HEAD_DIM) for g in range(N_KV)]
    rsl = [slice(r * tq, (r + 1) * tq) for r in range(Q_PER_KV)]
    q_pos_col = i * tq + lax.broadcasted_iota(I32, (tq, 1), 0)
    q_pos_row = i * tq + lax.broadcasted_iota(I32, (1, tq), 1)
    q_heads = [q_blk[:, h * HEAD_DIM:(h + 1) * HEAD_DIM] for h in range(N_HEADS)]
    qg = [jnp.concatenate([q_heads[h] for h in hs], axis=0) for hs in group_heads]
    qx_bf = [jnp.concatenate([jnp.concatenate([q_heads[h], _query_position_features(q_pos_col, slopes[h])],
                                              axis=1) for h in hs], axis=0).astype(BF16)
             for hs in group_heads]

    cmp_row = lax.broadcasted_iota(I32, (nc, tq), 0)
    dist_c = (q_pos_row - (cmp_row * CMP_BLOCK + (CMP_BLOCK - 1))).astype(F32)
    mask_c = dist_c >= 0.0
    blk = cmp_row >> 1
    o_cmp = [None] * N_HEADS
    sel_bf = []
    for g, hs in enumerate(group_heads):
        vc_bf = vc_ref[0][:, gsl[g]].astype(BF16)
        qk = _dot3_nt(kc_ref[0][:, gsl[g]], qg[g])
        imp = jnp.zeros((nc, tq), F32)
        for r, h in enumerate(hs):
            s = jnp.where(mask_c, qk[:, rsl[r]] - slopes[h] * dist_c, NEG)
            m = jnp.max(s, axis=0, keepdims=True)
            p = jnp.where(mask_c, jnp.exp(s - m), 0.0)
            p = p / jnp.maximum(jnp.sum(p, axis=0, keepdims=True), 1e-30)
            o_cmp[h] = _dot_tn(p.astype(BF16), vc_bf)
            imp = imp + p
        score = _block_scores(_pair_sum(imp, 0), blk, q_pos_row, seq // SLC_BLOCK)
        sel_bf.append(_select_blocks(score, blk, n_sel).astype(BF16))

    span = WINDOW + tq
    w_start = pl.multiple_of(jnp.maximum(i * tq - WINDOW, 0), tq)
    dist_w = (q_pos_col - (w_start + lax.broadcasted_iota(I32, (tq, span), 1))).astype(F32)
    bias_w = jnp.where(jnp.logical_and(dist_w >= 0.0, dist_w <= float(WINDOW)), 0.0, NEG)
    o_win = [None] * N_HEADS
    kx_w = kx_ref[pl.ds(w_start, span), :]
    for g, hs in enumerate(group_heads):
        kw_bf = kw_ref[0, pl.ds(w_start, span), :][:, gsl[g]].astype(BF16)
        vw_bf = vw_ref[0, pl.ds(w_start, span), :][:, gsl[g]].astype(BF16)
        s_all = _dot_nt(qx_bf[g], jnp.concatenate([kw_bf, kx_w], axis=1))
        probs, sums = [], []
        for r, h in enumerate(hs):
            s = s_all[rsl[r]] + bias_w
            p = jnp.exp(s - jnp.max(s, axis=1, keepdims=True))
            sums.append(jnp.sum(p, axis=1, keepdims=True))
            probs.append(p.astype(BF16))
        o_all = _dot(jnp.concatenate(probs, axis=0), vw_bf)
        for r, h in enumerate(hs):
            o_win[h] = o_all[rsl[r]] / sums[r]

    n_tiles = ((i + 1) * tq + KEY_TILE - 1) // KEY_TILE

    def slc_step(t, carry):
        ms, ls, accs = (list(c) for c in carry)
        k0 = pl.multiple_of(t * KEY_TILE, KEY_TILE)
        dist = (q_pos_col - (k0 + lax.broadcasted_iota(I32, (tq, KEY_TILE), 1))).astype(F32)
        causal = dist >= 0.0
        key_cmp = (k0 + lax.broadcasted_iota(I32, (nc, KEY_TILE), 1)) // CMP_BLOCK
        expand = jnp.where(key_cmp == lax.broadcasted_iota(I32, (nc, KEY_TILE), 0), 1.0, 0.0).astype(BF16)
        kx_t = kx_ref[pl.ds(k0, KEY_TILE), :]
        for g, hs in enumerate(group_heads):
            kt_bf = ks_ref[0, pl.ds(k0, KEY_TILE), :][:, gsl[g]].astype(BF16)
            vt_bf = vs_ref[0, pl.ds(k0, KEY_TILE), :][:, gsl[g]].astype(BF16)
            chosen = _dot_tn(sel_bf[g], expand)
            bias = jnp.where(jnp.logical_and(causal, chosen > 0.5), 0.0, NEG)
            s_all = _dot_nt(qx_bf[g], jnp.concatenate([kt_bf, kx_t], axis=1))
            probs, alphas = [], []
            for r, h in enumerate(hs):
                s = s_all[rsl[r]] + bias
                m_new = jnp.maximum(ms[h], jnp.max(s, axis=1, keepdims=True))
                alpha = jnp.exp(ms[h] - m_new)
                p = jnp.exp(s - m_new)
                ls[h] = alpha * ls[h] + jnp.sum(p, axis=1, keepdims=True)
                ms[h] = m_new
                alphas.append(alpha)
                probs.append(p.astype(BF16))
            pv = _dot(jnp.concatenate(probs, axis=0), vt_bf)
            for r, h in enumerate(hs):
                accs[h] = alphas[r] * accs[h] + pv[rsl[r]]
        return tuple(ms), tuple(ls), tuple(accs)

    init = (tuple(jnp.full((tq, 1), NEG, F32) for _ in range(N_HEADS)),
            tuple(jnp.zeros((tq, 1), F32) for _ in range(N_HEADS)),
            tuple(jnp.zeros((tq, HEAD_DIM), F32) for _ in range(N_HEADS)))
    _, l_s, acc_s = lax.fori_loop(0, n_tiles, slc_step, init)

    pieces = []
    for h in range(N_HEADS):
        o_slc = acc_s[h] / jnp.maximum(l_s[h], 1e-30)
        pieces.append(o_cmp[h] * gates[:, 3 * h + 0:3 * h + 1] + o_slc * gates[:, 3 * h + 1:3 * h + 2]
                      + o_win[h] * gates[:, 3 * h + 2:3 * h + 3])
    o_ref[0] = jnp.concatenate(pieces, axis=1)


def _prompt_attention(q, gates, kc, vc, ks, vs, kw, vw):
    b, t, _ = q.shape
    nc = kc.shape[1]
    assert t % KEY_TILE == 0 and t >= WINDOW + Q_TILE
    n_sel = min(N_SEL, t // SLC_BLOCK)
    qspec = lambda n: pl.BlockSpec((1, Q_TILE, n), lambda bi, i: (bi, i, 0))
    full = lambda r: pl.BlockSpec((1, r, KV_COLS), lambda bi, i: (bi, 0, 0))
    assert t <= POS_RADIX * POS_RADIX
    kx = _position_features(t)
    return pl.pallas_call(
        functools.partial(_prompt_attn_kernel, seq=t, n_sel=n_sel),
        grid=(b, t // Q_TILE),
        in_specs=[qspec(ATTN_WIDTH), qspec(LANES), full(nc), full(nc),
                  full(t), full(t), full(t), full(t), pl.BlockSpec(kx.shape, lambda bi, i: (0, 0))],
        out_specs=qspec(ATTN_WIDTH),
        out_shape=jax.ShapeDtypeStruct((b, t, ATTN_WIDTH), F32),
        compiler_params=_cparams("arbitrary", "arbitrary"),
        name="prompt_attention",
    )(q, gates, kc, vc, ks, vs, kw, vw, kx)


def _merge_groups(per_group):
    row = lax.broadcasted_iota(I32, per_group[0].shape, 0) // Q_PER_KV
    out = per_group[0]
    for g in range(1, N_KV):
        out = jnp.where(row == g, per_group[g], out)
    return out


def _group_slice(x, g):
    return x[:, g * HEAD_DIM:(g + 1) * HEAD_DIM]


def _matmul3_kernel(a_ref, b_ref, o_ref):
    o_ref[...] = _dot3(a_ref[...], b_ref[...])


def _matmul3(a, b):
    return pl.pallas_call(
        _matmul3_kernel,
        out_shape=jax.ShapeDtypeStruct((a.shape[0], b.shape[1]), F32),
        compiler_params=pltpu.CompilerParams(vmem_limit_bytes=VMEM_LIMIT),
        name="matmul3",
    )(a, b)


def _page_specs(n_pages, page):
    def spec(o):
        return pl.BlockSpec((1, N_KV, HEAD_DIM, page),
                            lambda i, j, pt: (pt[i * n_pages + j * PAGES_PER_STEP + o], 0, 0, 0))
    return [spec(o) for o in range(PAGES_PER_STEP)]


def _sample_scores_kernel(pt_ref, ut_ref, *refs):
    k_refs, o_ref = refs[:-1], refs[-1]
    for h in range(N_HEADS):
        g = h // Q_PER_KV
        u = ut_ref[0, h]
        rows = [jnp.sum(k_ref[0, g] * u, axis=0, keepdims=True) for k_ref in k_refs]
        o_ref[0, h] = jnp.concatenate(rows, axis=0)


def _sample_scores(page_table, ut, k_pages):
    b, n_pages = page_table.shape
    page = k_pages.shape[-1]
    assert n_pages % PAGES_PER_STEP == 0
    return pl.pallas_call(
        _sample_scores_kernel,
        grid_spec=pltpu.PrefetchScalarGridSpec(
            num_scalar_prefetch=1,
            grid=(b, n_pages // PAGES_PER_STEP),
            in_specs=[pl.BlockSpec((1, N_HEADS, HEAD_DIM, page), lambda i, j, pt: (i, 0, 0, 0))]
                     + _page_specs(n_pages, page),
            out_specs=pl.BlockSpec((1, N_HEADS, PAGES_PER_STEP, page), lambda i, j, pt: (i, 0, j, 0)),
        ),
        out_shape=jax.ShapeDtypeStruct((b, N_HEADS, n_pages, page), F32),
        compiler_params=_cparams("arbitrary", "arbitrary"),
        name="sample_scores",
    )(page_table.reshape(-1), ut, *([k_pages] * PAGES_PER_STEP))


def _max_all(x):
    return jnp.max(jnp.max(x, axis=0, keepdims=True), axis=1, keepdims=True)


def _min_all(x):
    return jnp.min(jnp.min(x, axis=0, keepdims=True), axis=1, keepdims=True)


def _sum_all(x):
    return jnp.sum(jnp.sum(x, axis=0, keepdims=True), axis=1, keepdims=True)


def _sample_select_kernel(s_ref, q_ref, kcn_ref, pexp_ref, pnew_ref, sel_ref, *, past, n_new, n_sel):
    n_pages, page = s_ref.shape[2], s_ref.shape[3]
    cpp = page // CMP_BLOCK
    n_past = n_pages * cpp
    n_blocks_total = (n_past + n_new) // CMP_PER_SLC
    lane = lax.broadcasted_iota(I32, (n_pages, page), 1)
    prow = lax.broadcasted_iota(I32, (n_pages, page), 0)
    dist = (past - ((prow * cpp + lane // CMP_BLOCK) * CMP_BLOCK + (CMP_BLOCK - 1))).astype(F32)
    mask = jnp.logical_and(lane % CMP_BLOCK == 0, dist >= 0.0)

    q8 = q_ref[0] * ATTN_SCALE
    slope = _alibi_slope_col(N_HEADS, 1, 0, N_HEADS)
    kcn = jnp.concatenate([kcn_ref[0], jnp.zeros((LANES - n_new, KV_COLS), F32)], axis=0)
    new_lane = lax.broadcasted_iota(I32, (N_HEADS, LANES), 1)
    dist_n = (past - ((n_past + new_lane) * CMP_BLOCK + (CMP_BLOCK - 1))).astype(F32)
    mask_n = jnp.logical_and(dist_n >= 0.0, new_lane < n_new)
    qk_n = _merge_groups([_dot3_nt(q8, _group_slice(kcn, g)) for g in range(N_KV)])
    s_new = jnp.where(mask_n, qk_n - slope * dist_n, NEG)

    halvings = [CMP_BLOCK >> k for k in range(1, CMP_BLOCK.bit_length())]
    probs, probs_new = [], []
    for h in range(N_HEADS):
        x = s_ref[0, h]
        for sh in halvings:
            x = x + pltpu.roll(x, page - sh, 1)
        s = jnp.where(mask, x - 2.0 ** (-8.0 * (h + 1) / N_HEADS) * dist, NEG)
        sn = s_new[h:h + 1, :]
        mn = jnp.logical_and(dist_n[h:h + 1, :] >= 0.0, new_lane[h:h + 1, :] < n_new)
        m = jnp.maximum(_max_all(s), jnp.max(sn, axis=1, keepdims=True))
        p = jnp.where(mask, jnp.exp(s - m), 0.0)
        pn = jnp.where(mn, jnp.exp(sn - m), 0.0)
        den = jnp.maximum(_sum_all(p) + jnp.sum(pn, axis=1, keepdims=True), 1e-30)
        p = p / den
        probs.append(p)
        probs_new.append(pn / den)
        z = p
        for sh in reversed(halvings):
            z = z + pltpu.roll(z, sh, 1)
        pexp_ref[0, h] = z
    pnew_ref[0] = jnp.concatenate(probs_new, axis=0)

    row1 = lax.broadcasted_iota(I32, (1, LANES), 1)
    blk = jnp.where(lane % SLC_BLOCK == 0, prow * (page // SLC_BLOCK) + lane // SLC_BLOCK, -1)
    blk_n = jnp.where(row1 < n_new, n_past // CMP_PER_SLC + (row1 >> 1), -1)
    blk_f = blk.astype(F32)
    blk_nf = blk_n.astype(F32)
    out_lane = lax.broadcasted_iota(I32, (N_HEADS, LANES), 1)
    out_row = lax.broadcasted_iota(I32, (N_HEADS, LANES), 0)
    out = jnp.full((N_HEADS, LANES), -1, I32)
    for g in range(N_KV):
        imp = probs[g * Q_PER_KV]
        imp_n = probs_new[g * Q_PER_KV]
        for r in range(1, Q_PER_KV):
            imp = imp + probs[g * Q_PER_KV + r]
            imp_n = imp_n + probs_new[g * Q_PER_KV + r]
        imp = imp + pltpu.roll(imp, page - CMP_BLOCK, 1)
        s_m = jnp.where(blk >= 0, _block_scores(imp, blk, past, n_blocks_total), NOT_A_BLOCK)
        s_n = jnp.where(blk_n >= 0, _block_scores(_pair_sum(imp_n, 1), blk_n, past, n_blocks_total), NOT_A_BLOCK)
        for j in range(n_sel):
            top = jnp.maximum(_max_all(s_m), jnp.max(s_n, axis=1, keepdims=True))
            first = jnp.minimum(_min_all(jnp.where(s_m == top, blk_f, NO_INDEX)),
                                jnp.min(jnp.where(s_n == top, blk_nf, NO_INDEX), axis=1, keepdims=True))
            s_m = jnp.where(blk_f == first, PICKED, s_m)
            s_n = jnp.where(blk_nf == first, PICKED, s_n)
            pick = jnp.where(top >= 0.0, first.astype(I32), -1)
            out = jnp.where(jnp.logical_and(out_row == g, out_lane == j), pick, out)
    sel_ref[0] = out


def _sample_select(s_raw, q8, kc_new, past):
    b, _, n_pages, page = s_raw.shape
    n_new = kc_new.shape[1]
    assert CMP_PER_SLC == 2 and CMP_BLOCK == 32
    n_sel = min(N_SEL, (past // CMP_BLOCK + n_new) // CMP_PER_SLC)
    per_b = lambda *s: pl.BlockSpec((1,) + s, lambda i: (i,) + (0,) * len(s))
    return pl.pallas_call(
        functools.partial(_sample_select_kernel, past=past, n_new=n_new, n_sel=n_sel),
        grid=(b,),
        in_specs=[per_b(N_HEADS, n_pages, page), per_b(N_HEADS, HEAD_DIM), per_b(n_new, KV_COLS)],
        out_specs=[per_b(N_HEADS, n_pages, page), per_b(N_HEADS, LANES), per_b(N_HEADS, LANES)],
        out_shape=[jax.ShapeDtypeStruct((b, N_HEADS, n_pages, page), F32),
                   jax.ShapeDtypeStruct((b, N_HEADS, LANES), F32),
                   jax.ShapeDtypeStruct((b, N_HEADS, LANES), I32)],
        compiler_params=_cparams("arbitrary"),
        name="sample_select",
    )(s_raw, q8, kc_new)


def _sample_values_kernel(pt_ref, pe_ref, *refs):
    v_refs, y_ref = refs[:-1], refs[-1]

    @pl.when(pl.program_id(1) == 0)
    def _():
        y_ref[...] = jnp.zeros(y_ref.shape, F32)

    for g in range(N_KV):
        heads = range(g * Q_PER_KV, (g + 1) * Q_PER_KV)
        pe = [pe_ref[0, h] for h in heads]
        acc = [jnp.zeros(y_ref.shape[2:], F32) for _ in heads]
        for o, v_ref in enumerate(v_refs):
            v = v_ref[0, g]
            for r in range(Q_PER_KV):
                acc[r] = acc[r] + v * pe[r][o:o + 1, :]
        for r, h in enumerate(heads):
            y_ref[0, h] = y_ref[0, h] + acc[r]


def _sample_values(page_table, pexp, v_pages):
    b, n_pages = page_table.shape
    page = v_pages.shape[-1]
    return pl.pallas_call(
        _sample_values_kernel,
        grid_spec=pltpu.PrefetchScalarGridSpec(
            num_scalar_prefetch=1,
            grid=(b, n_pages // PAGES_PER_STEP),
            in_specs=[pl.BlockSpec((1, N_HEADS, PAGES_PER_STEP, page), lambda i, j, pt: (i, 0, j, 0))]
                     + _page_specs(n_pages, page),
            out_specs=pl.BlockSpec((1, N_HEADS, HEAD_DIM, page), lambda i, j, pt: (i, 0, 0, 0)),
        ),
        out_shape=jax.ShapeDtypeStruct((b, N_HEADS, HEAD_DIM, page), F32),
        compiler_params=_cparams("arbitrary", "arbitrary"),
        name="sample_values",
    )(page_table.reshape(-1), pexp, *([v_pages] * PAGES_PER_STEP))


def _new_token_terms(q8, k_row, v_row):
    s = _merge_groups([jnp.sum(q8 * _group_slice(k_row, g), axis=1, keepdims=True) for g in range(N_KV)])
    v = _merge_groups([jnp.broadcast_to(_group_slice(v_row, g), (N_HEADS, HEAD_DIM)) for g in range(N_KV)])
    return s, v


def _sample_attend_kernel(sel_ref, pt_ref, q_ref, *refs, past, n_sel, ns_past):
    page_refs = refs[:4 * n_sel]
    (ksn_ref, vsn_ref, kw_ref, vw_ref, kwn_ref, vwn_ref, gate_ref, ocmp_ref, pnew_ref, vcn_ref,
     o_ref) = refs[4 * n_sel:]
    b = pl.program_id(0)
    page = page_refs[0].shape[-1]
    spp = page // SLC_BLOCK
    q8 = q_ref[0] * ATTN_SCALE
    q8_bf = q8.astype(BF16)
    slope = _alibi_slope_col(N_HEADS, 1, 0, N_HEADS)
    lane = lax.broadcasted_iota(I32, (N_HEADS, page), 1)
    s_t, v_t = _new_token_terms(q8, ksn_ref[0], vsn_ref[0])

    scores, masks, new_scores = [], [], []
    for n in range(n_sel):
        k_refs = page_refs[4 * n:4 * n + N_KV]
        blks = [sel_ref[(b * N_KV + g) * n_sel + n] for g in range(N_KV)]
        blk_rows = _merge_groups([jnp.full((N_HEADS, page), blk, I32) for blk in blks])
        blk_col = _merge_groups([jnp.full((N_HEADS, 1), blk, I32) for blk in blks])
        qk = _merge_groups([_dot(q8_bf, k_ref[0, 0].astype(BF16)) for k_ref in k_refs])
        page_pos = blk_rows // spp
        dist = (past - (page_pos * page + lane)).astype(F32)
        in_block = (lane // SLC_BLOCK) == (blk_rows - page_pos * spp)
        cached = jnp.logical_and(blk_rows >= 0, blk_rows < ns_past)
        mask = jnp.logical_and(jnp.logical_and(in_block, cached), dist >= 0.0)
        scores.append(jnp.where(mask, qk - slope * dist, NEG))
        masks.append(mask)
        new_scores.append(jnp.where(blk_col >= ns_past, s_t, NEG))
    m = new_scores[0]
    for s, sn in zip(scores, new_scores):
        m = jnp.maximum(m, jnp.maximum(jnp.max(s, axis=1, keepdims=True), sn))
    l_tot = jnp.zeros((N_HEADS, 1), F32)
    p_new = jnp.zeros((N_HEADS, 1), F32)
    acc = jnp.zeros((N_HEADS, HEAD_DIM), F32)
    for n in range(n_sel):
        v_refs = page_refs[4 * n + N_KV:4 * n + 2 * N_KV]
        p = jnp.where(masks[n], jnp.exp(scores[n] - m), 0.0)
        p_bf = p.astype(BF16)
        l_tot = l_tot + jnp.sum(p, axis=1, keepdims=True)
        p_new = p_new + jnp.where(new_scores[n] > 0.5 * NEG, jnp.exp(new_scores[n] - m), 0.0)
        acc = acc + _merge_groups([_dot_nt(p_bf, v_ref[0, 0].astype(BF16)) for v_ref in v_refs])
    o_slc = (acc + p_new * v_t) / jnp.maximum(l_tot + p_new, 1e-30)

    w_buf = kw_ref.shape[-1]
    wl = lax.broadcasted_iota(I32, (N_HEADS, w_buf), 1)
    win_pos = past - w_buf + wl
    dist_w = (past - win_pos).astype(F32)
    mask_w = jnp.logical_and(jnp.logical_and(dist_w >= 0.0, dist_w <= float(WINDOW)), win_pos >= 0)
    qk_w = _merge_groups([_dot(q8_bf, kw_ref[0, g].astype(BF16)) for g in range(N_KV)])
    s_w = jnp.where(mask_w, qk_w - slope * dist_w, NEG)
    s_t, v_t = _new_token_terms(q8, kwn_ref[0], vwn_ref[0])
    m_w = jnp.maximum(jnp.max(s_w, axis=1, keepdims=True), s_t)
    p_w = jnp.where(mask_w, jnp.exp(s_w - m_w), 0.0)
    p_t = jnp.exp(s_t - m_w)
    den = jnp.maximum(jnp.sum(p_w, axis=1, keepdims=True) + p_t, 1e-30)
    pw_bf = p_w.astype(BF16)
    o_w = _merge_groups([_dot_nt(pw_bf, vw_ref[0, g].astype(BF16)) for g in range(N_KV)])
    o_win = (o_w + p_t * v_t) / den
    n_new = vcn_ref.shape[1]
    vcn = jnp.concatenate([vcn_ref[0], jnp.zeros((LANES - n_new, KV_COLS), F32)], axis=0).astype(BF16)
    pn_bf = pnew_ref[0].astype(BF16)
    o_cmp = ocmp_ref[0] + _merge_groups([_dot(pn_bf, _group_slice(vcn, g)) for g in range(N_KV)])
    gt = gate_ref[0]
    o_ref[0] = o_cmp * gt[:, 0:1] + o_slc * gt[:, 1:2] + o_win * gt[:, 2:3]


def _sample_attend(sel, page_table, q8, k_pages, v_pages, ks_new, vs_new, kw_state, vw_state,
                   kw_new, vw_new, gates8, o_cmp, p_new, vc_new, past):
    b, n_pages = page_table.shape
    n_sel = sel.shape[-1]
    page = k_pages.shape[-1]
    ns_past = past // SLC_BLOCK
    spp = page // SLC_BLOCK
    w_buf = kw_state.shape[-1]
    n_new = vc_new.shape[1]

    def cache_map(n, g):
        def index(i, sel_ref, pt_ref):
            blk = jnp.clip(sel_ref[(i * N_KV + g) * n_sel + n], 0, ns_past - 1)
            return (pt_ref[i * n_pages + blk // spp], g, 0, 0)
        return pl.BlockSpec((1, 1, HEAD_DIM, page), index)

    page_specs, page_args = [], []
    for n in range(n_sel):
        for arr in (k_pages, v_pages):
            for g in range(N_KV):
                page_specs.append(cache_map(n, g))
                page_args.append(arr)
    per_b = lambda *s: pl.BlockSpec((1,) + s, lambda i, sl, pt: (i,) + (0,) * len(s))
    return pl.pallas_call(
        functools.partial(_sample_attend_kernel, past=past, n_sel=n_sel, ns_past=ns_past),
        grid_spec=pltpu.PrefetchScalarGridSpec(
            num_scalar_prefetch=2,
            grid=(b,),
            in_specs=[per_b(N_HEADS, HEAD_DIM)] + page_specs
                     + [per_b(1, KV_COLS), per_b(1, KV_COLS),
                        per_b(N_KV, HEAD_DIM, w_buf), per_b(N_KV, HEAD_DIM, w_buf),
                        per_b(1, KV_COLS), per_b(1, KV_COLS),
                        per_b(N_HEADS, 3), per_b(N_HEADS, HEAD_DIM), per_b(N_HEADS, LANES),
                        per_b(n_new, KV_COLS)],
            out_specs=per_b(N_HEADS, HEAD_DIM),
        ),
        out_shape=jax.ShapeDtypeStruct((b, N_HEADS, HEAD_DIM), F32),
        compiler_params=_cparams("arbitrary"),
        name="sample_attend",
    )(sel.reshape(-1), page_table.reshape(-1), q8, *page_args,
      ks_new, vs_new, kw_state, vw_state, kw_new, vw_new, gates8, o_cmp, p_new, vc_new)


def _layernorm_silu(y, g, b):
    mu = jnp.mean(y, axis=-1, keepdims=True)
    var = jnp.mean(jnp.square(y - mu), axis=-1, keepdims=True)
    return _silu((y - mu) * lax.rsqrt(var + EPS) * g + b)


def _conv_prompt_kernel(u_ref, w_ref, b_ref, g_ref, beta_ref, o_ref, buf):
    j = pl.program_id(1)
    tt = u_ref.shape[1]
    c = buf.shape[1]
    kw = w_ref.shape[0] // SUBLANES

    @pl.when(j == 0)
    def _():
        buf[0:CONV_HALO, :] = jnp.zeros((CONV_HALO, c), F32)

    buf[CONV_HALO:CONV_HALO + tt, :] = u_ref[0]
    first = CONV_HALO - (kw - 1)
    rows = CONV_ROWS
    for r0 in range(0, tt, rows):
        acc = jnp.zeros((rows // SUBLANES, SUBLANES, c), F32)
        for r in range(SUBLANES):
            taps = range(r, kw, SUBLANES)
            win = buf[pl.ds(first + r + r0, rows + SUBLANES * (len(taps) - 1)), :]
            for t, k in enumerate(taps):
                wk = w_ref[SUBLANES * k:SUBLANES * (k + 1), :]
                tap = win[SUBLANES * t:SUBLANES * t + rows].reshape(rows // SUBLANES, SUBLANES, c)
                acc = acc + wk[None] * tap
        y = acc.reshape(rows, c) + b_ref[...]
        o_ref[0, r0:r0 + rows, :] = _layernorm_silu(y, g_ref[...], beta_ref[...])
    buf[0:CONV_HALO, :] = buf[tt:tt + CONV_HALO, :]


def _conv_prompt(u, w_dw, b_dw, ln_g, ln_b, tt):
    b, t, c = u.shape
    vec = pl.BlockSpec((1, c), lambda i, j: (0, 0))
    w_rep = jnp.repeat(w_dw, SUBLANES, axis=0)
    return pl.pallas_call(
        _conv_prompt_kernel,
        grid=(b, t // tt),
        in_specs=[pl.BlockSpec((1, tt, c), lambda i, j: (i, j, 0)),
                  pl.BlockSpec(w_rep.shape, lambda i, j: (0, 0)), vec, vec, vec],
        out_specs=pl.BlockSpec((1, tt, c), lambda i, j: (i, j, 0)),
        out_shape=jax.ShapeDtypeStruct((b, t, c), F32),
        scratch_shapes=[pltpu.VMEM((CONV_HALO + tt, c), F32)],
        compiler_params=_cparams("arbitrary", "arbitrary"),
        name="conv_prompt",
    )(u, w_rep, b_dw, ln_g, ln_b)


def _conv_sample_kernel(up_ref, w_ref, b_ref, g_ref, beta_ref, o_ref):
    y = jnp.sum(up_ref[...] * w_ref[...][None, :, :], axis=1)
    o_ref[...] = _layernorm_silu(y + b_ref[...], g_ref[...], beta_ref[...])


def _conv_sample(up, w_dw, b_dw, ln_g, ln_b):
    b, kw, c = up.shape
    return pl.pallas_call(
        _conv_sample_kernel,
        out_shape=jax.ShapeDtypeStruct((b, c), F32),
        name="conv_sample",
    )(up, w_dw, b_dw, ln_g, ln_b)


def _merge_router_kernel(oa_ref, oc_ref, x_ref, gate_ref, shift_ref, scale_ref, goa_ref, goc_ref,
                         wout_ref, g2_ref, rwh_ref, rwl_ref, rb_ref, cnt_in_ref,
                         x1_ref, h2_ref, eidx_ref, wts_ref, rank_ref, cnt_out_ref, run):
    first = jnp.logical_and(pl.program_id(0) == 0, pl.program_id(1) == 0)

    @pl.when(first)
    def _():
        run[...] = cnt_in_ref[...]

    a = _rms(oa_ref[0], goa_ref[...])
    c = _rms(oc_ref[0], goc_ref[...])
    cat = jnp.concatenate([a, c], axis=1).astype(BF16)
    x1 = x_ref[0] + gate_ref[0] * _dot(cat, wout_ref[...])
    x1_ref[0] = x1
    h2 = _rms(x1, g2_ref[...]) * (1.0 + scale_ref[0]) + shift_ref[0]
    h2_ref[0] = h2

    hh, hl = _split2(h2)
    logits = _dot_nt(rwh_ref[...], hh) + (_dot_nt(rwl_ref[...], hh) + _dot_nt(rwh_ref[...], hl))
    aff = _sigmoid(logits)
    n_exp, tm = aff.shape
    row_f = lax.broadcasted_iota(I32, (n_exp, tm), 0).astype(F32)
    s = aff + rb_ref[...]
    experts, weights = [], []
    for _ in range(TOP_K):
        m = jnp.max(s, axis=0, keepdims=True)
        e = jnp.min(jnp.where(s == m, row_f, NO_INDEX), axis=0, keepdims=True)
        pick = row_f == e
        experts.append(e)
        weights.append(jnp.sum(jnp.where(pick, aff, 0.0), axis=0, keepdims=True))
        s = jnp.where(pick, NEG, s)
    total = weights[0]
    for w in weights[1:]:
        total = total + w

    hot = jnp.where(s == NEG, 1.0, 0.0)
    r_i = lax.broadcasted_iota(I32, (tm, tm), 0)
    c_i = lax.broadcasted_iota(I32, (tm, tm), 1)
    earlier = jnp.where(r_i < c_i, 1.0, 0.0).astype(BF16)
    before = _dot(hot.astype(BF16), earlier) + run[...]
    ranks = [jnp.sum(jnp.where(row_f == e, before, 0.0), axis=0, keepdims=True) for e in experts]
    eidx_ref[0] = jnp.concatenate(experts, axis=0).astype(I32)
    wts_ref[0] = jnp.concatenate([ROUTE_SCALE * w / total for w in weights], axis=0)
    rank_ref[0] = jnp.concatenate(ranks, axis=0).astype(I32)
    run[...] = run[...] + jnp.sum(hot, axis=1, keepdims=True)
    cnt_out_ref[...] = run[...]


def _merge_router(o_attn, o_conv, x, gate, shift, scale, goa, goc, wout_bf, g2, rw_hi, rw_lo, rb,
                  cnt_in, tm):
    b, t, d = x.shape
    n_exp = rw_hi.shape[0]
    row = lambda n: pl.BlockSpec((1, tm, n), lambda i, j: (i, j, 0))
    pick = pl.BlockSpec((1, TOP_K, tm), lambda i, j: (i, 0, j))
    const = lambda shape: pl.BlockSpec(shape, lambda i, j: (0,) * len(shape))
    sds = lambda n, dt: jax.ShapeDtypeStruct((b, t, n), dt)
    picks = lambda dt: jax.ShapeDtypeStruct((b, TOP_K, t), dt)
    return pl.pallas_call(
        _merge_router_kernel,
        grid=(b, t // tm),
        in_specs=[row(o_attn.shape[-1]), row(o_conv.shape[-1]), row(d),
                  _mod_spec(gate, tm, d), _mod_spec(shift, tm, d), _mod_spec(scale, tm, d),
                  const(goa.shape), const(goc.shape), const(wout_bf.shape), const(g2.shape),
                  const(rw_hi.shape), const(rw_lo.shape), const(rb.shape), const(cnt_in.shape)],
        out_specs=[row(d), row(d), pick, pick, pick, const((n_exp, 1))],
        out_shape=[sds(d, F32), sds(d, F32), picks(I32), picks(F32), picks(I32),
                   jax.ShapeDtypeStruct((n_exp, 1), F32)],
        scratch_shapes=[pltpu.VMEM((n_exp, 1), F32)],
        compiler_params=_cparams("arbitrary", "arbitrary"),
        name="merge_router",
    )(o_attn, o_conv, x, gate, shift, scale, goa, goc, wout_bf, g2, rw_hi, rw_lo, rb, cnt_in)


def _row_copy(src_hbm, dst, src_row, dst_row, sem, chunks):
    return pltpu.make_async_copy(src_hbm.at[pl.ds(src_row * chunks, chunks)],
                                 dst.at[pl.ds(dst_row * chunks, chunks)], sem)


def _slot(start_ref, e_ref, r_ref, idx):
    return start_ref[e_ref[0, 0, idx]] + r_ref[0, 0, idx]


def _dispatch_kernel(cnt_ref, end_ref, start_ref, e_ref, r_ref, h_ref, xs_hbm, zbuf, zsem, sem,
                     *, tokens, rows, chunks, n_blocks):
    j = pl.program_id(0)
    n_exp = cnt_ref.shape[0]
    blk_rows = rows * chunks

    def zero_block(blk):
        return pltpu.make_async_copy(zbuf, xs_hbm.at[pl.ds(blk * blk_rows, blk_rows)], zsem)

    @pl.when(j == 0)
    def _():
        zbuf[...] = jnp.zeros(zbuf.shape, F32)
        n_active = end_ref[n_exp - 1] // rows

        def zero_tail(e, issued):
            partial = cnt_ref[e] % rows != 0

            @pl.when(partial)
            def _():
                zero_block(end_ref[e] // rows - 1).start()

            return issued + partial.astype(I32)

        def zero_unused(blk, _):
            zero_block(blk).start()
            return 0

        def drain_zero(_, c):
            zero_block(0).wait()
            return c

        issued = lax.fori_loop(0, n_exp, zero_tail, 0)
        lax.fori_loop(n_active, n_blocks, zero_unused, 0)
        lax.fori_loop(0, issued + (n_blocks - n_active), drain_zero, 0)

    def issue(r, _):
        for k in range(TOP_K):
            _row_copy(h_ref, xs_hbm, r, _slot(start_ref, e_ref, r_ref, r * TOP_K + k), sem, chunks).start(
                priority=k % DMA_PRIORITIES)
        return 0

    def drain(r, _):
        for k in range(TOP_K):
            _row_copy(h_ref, xs_hbm, 0, 0, sem, chunks).wait()
        return 0

    lax.fori_loop(0, tokens, issue, 0)
    lax.fori_loop(0, tokens, drain, 0)


def _dispatch(counts, pad_end, pad_start, e_idx, rank, h_rows, tokens, rows, chunks, n_blocks):
    n_tiles = e_idx.shape[0]
    picks = pl.BlockSpec((1, 1, tokens * TOP_K), lambda j, c, e, s: (j, 0, 0), memory_space=pltpu.SMEM)
    return pl.pallas_call(
        functools.partial(_dispatch_kernel, tokens=tokens, rows=rows, chunks=chunks, n_blocks=n_blocks),
        grid_spec=pltpu.PrefetchScalarGridSpec(
            num_scalar_prefetch=3,
            grid=(n_tiles,),
            in_specs=[picks, picks,
                      pl.BlockSpec((tokens * chunks, LANES), lambda j, c, e, s: (j, 0))],
            out_specs=pl.BlockSpec(memory_space=pl.ANY),
            scratch_shapes=[pltpu.VMEM((rows * chunks, LANES), F32),
                            pltpu.SemaphoreType.DMA(()), pltpu.SemaphoreType.DMA(())],
        ),
        out_shape=jax.ShapeDtypeStruct((n_blocks * rows * chunks, LANES), F32),
        compiler_params=_cparams("arbitrary"),
        name="moe_dispatch",
    )(counts, pad_end, pad_start, e_idx, rank, h_rows)


def _expert_kernel(be_ref, nact_ref, first_ref, slot_ref, next_ref, x_ref, wg_hbm, wu_hbm, wd_hbm, y_ref,
                   wg_buf, wu_buf, wd_buf, sems, *, rows, chunks):
    j = pl.program_id(0)

    def weight_copies(expert, slot):
        return [pltpu.make_async_copy(w_hbm.at[expert], buf.at[slot], sems.at[slot, i])
                for i, (w_hbm, buf) in enumerate(((wg_hbm, wg_buf), (wu_hbm, wu_buf), (wd_hbm, wd_buf)))]

    @pl.when(j < nact_ref[0])
    def _():
        slot = slot_ref[j]

        @pl.when(j == 0)
        def _():
            for cp in weight_copies(be_ref[0], 0):
                cp.start()

        @pl.when(first_ref[j] == 1)
        def _():
            for cp in weight_copies(be_ref[j], slot):
                cp.wait()

            @pl.when(next_ref[j] >= 0)
            def _():
                for cp in weight_copies(next_ref[j], 1 - slot):
                    cp.start()

        f = wg_buf.shape[2]
        gate = jnp.zeros((rows, f), F32)
        up = jnp.zeros((rows, f), F32)
        for c in range(0, chunks, 2):
            xc = jnp.concatenate([x_ref[pl.ds(c, rows, stride=chunks), :],
                                  x_ref[pl.ds(c + 1, rows, stride=chunks), :]], axis=1).astype(BF16)
            cs = pl.ds(c * LANES, 2 * LANES)
            gate = gate + _dot(xc, wg_buf[slot, cs, :].astype(BF16))
            up = up + _dot(xc, wu_buf[slot, cs, :].astype(BF16))
        h = (_silu(gate) * up).astype(BF16)
        y = _dot(h, wd_buf[slot].astype(BF16))
        for c in range(chunks):
            y_ref[pl.ds(c, rows, stride=chunks), :] = y[:, c * LANES:(c + 1) * LANES]


def _experts(blk_expert, n_active, pad_end, xs, wg, wu, wd, rows, chunks):
    n_blocks = blk_expert.shape[0]
    n_exp, d, f = wg.shape
    first = jnp.concatenate([jnp.ones((1,), I32), (blk_expert[1:] != blk_expert[:-1]).astype(I32)])
    slot = (jnp.cumsum(first) - 1) % 2
    run_end = pad_end[blk_expert] // rows
    nxt = jnp.where(run_end < n_active[0], blk_expert[jnp.minimum(run_end, n_blocks - 1)], -1)
    block = pl.BlockSpec((rows * chunks, LANES), lambda j, be, na, *_: (jnp.minimum(j, na[0] - 1), 0))
    hbm = pl.BlockSpec(memory_space=pl.ANY)
    n_prefetch = 5
    return pl.pallas_call(
        functools.partial(_expert_kernel, rows=rows, chunks=chunks),
        grid_spec=pltpu.PrefetchScalarGridSpec(
            num_scalar_prefetch=n_prefetch,
            grid=(n_blocks,),
            in_specs=[block, hbm, hbm, hbm],
            out_specs=block,
            scratch_shapes=[pltpu.VMEM((2, d, f), F32), pltpu.VMEM((2, d, f), F32),
                            pltpu.VMEM((2, f, d), F32), pltpu.SemaphoreType.DMA((2, 3))],
        ),
        out_shape=jax.ShapeDtypeStruct(xs.shape, F32),
        input_output_aliases={n_prefetch: 0},
        compiler_params=_cparams("arbitrary"),
        name="moe_experts",
    )(blk_expert, n_active, first, slot.astype(I32), nxt.astype(I32), xs, wg, wu, wd)


def _combine_kernel(start_ref, e_ref, r_ref, w_ref, x1_ref, h2_ref, gate_ref, wsg_ref, wsu_ref, wsd_ref,
                    gf_ref, ys_hbm, o_ref, buf, sem, *, chunks):
    tm = x1_ref.shape[1]

    def issue(r, _):
        for k in range(TOP_K):
            _row_copy(ys_hbm, buf.at[k], _slot(start_ref, e_ref, r_ref, r * TOP_K + k), r, sem, chunks).start(
                priority=k % DMA_PRIORITIES)
        return 0

    def drain(r, _):
        for k in range(TOP_K):
            _row_copy(ys_hbm, buf.at[k], 0, r, sem, chunks).wait()
        return 0

    lax.fori_loop(0, tm, issue, 0)
    h_bf = h2_ref[0].astype(BF16)
    hid = (_silu(_dot(h_bf, wsg_ref[...])) * _dot(h_bf, wsu_ref[...])).astype(BF16)
    shared = _dot(hid, wsd_ref[...])
    lax.fori_loop(0, tm, drain, 0)

    w = w_ref[0]
    cols = []
    for c in range(chunks):
        tot = jnp.zeros((tm, LANES), F32)
        for k in range(TOP_K):
            tot = tot + buf[k, pl.ds(c, tm, stride=chunks), :] * w[:, k:k + 1]
        cols.append(tot)
    routed = jnp.concatenate(cols, axis=1)
    x2 = x1_ref[0] + gate_ref[0] * (routed + shared)
    o_ref[0] = _rms(x2, gf_ref[...])


def _combine(pad_start, e_idx, rank, wts, x1, h2, gate, wsg_bf, wsu_bf, wsd_bf, gf, ys, tm, chunks):
    b, t, d = x1.shape
    nt = t // tm
    row = lambda n: pl.BlockSpec((1, tm, n), lambda i, j, *_: (i, j, 0))
    const = lambda shape: pl.BlockSpec(shape, lambda i, j, *_: (0,) * len(shape))
    picks = pl.BlockSpec((1, 1, tm * TOP_K), lambda i, j, *_: (i * nt + j, 0, 0), memory_space=pltpu.SMEM)
    return pl.pallas_call(
        functools.partial(_combine_kernel, chunks=chunks),
        grid_spec=pltpu.PrefetchScalarGridSpec(
            num_scalar_prefetch=1,
            grid=(b, nt),
            in_specs=[picks, picks, row(TOP_K), row(d), row(d), _mod_spec(gate, tm, d),
                      const(wsg_bf.shape), const(wsu_bf.shape), const(wsd_bf.shape), const(gf.shape),
                      pl.BlockSpec(memory_space=pl.ANY)],
            out_specs=row(d),
            scratch_shapes=[pltpu.VMEM((TOP_K, tm * chunks, LANES), F32), pltpu.SemaphoreType.DMA(())],
        ),
        out_shape=jax.ShapeDtypeStruct((b, t, d), F32),
        compiler_params=_cparams("arbitrary", "arbitrary"),
        name="moe_combine",
    )(pad_start, e_idx, rank, wts, x1, h2, gate, wsg_bf, wsu_bf, wsd_bf, gf, ys)


def _split_mod(mod, per_token):
    parts = jnp.split(mod, 6, axis=-1)
    if per_token:
        return [p[None] for p in parts]
    return [p[:, None, :] for p in parts]


def _padded_in_weight(w_in, conv_width):
    n_gate = 3 * N_HEADS
    o = ATTN_WIDTH + 6 * KV_COLS
    main = w_in[:, :o]
    gates = jnp.pad(w_in[:, o:o + n_gate], ((0, 0), (0, LANES - n_gate)))
    glu = w_in[:, o + n_gate:o + n_gate + 2 * conv_width]
    return jnp.concatenate([main, gates, glu], axis=1).astype(BF16)


def _cmp_rows(x):
    return x.reshape(x.shape[:-2] + (x.shape[-2] // CMP_BLOCK, CMP_BLOCK * KV_COLS))


def _largest_tile(n, cap):
    best = [k for k in range(SUBLANES, cap + 1, SUBLANES) if n % k == 0]
    assert best, (n, cap)
    return best[-1]


def _kv5(x):
    return x.reshape(x.shape[:-1] + (N_KV, HEAD_DIM))[None]


def kernel(x_prompt, x_sample, cache_k_cmp, cache_v_cmp, cache_k_slc, cache_v_slc, state_k_win, state_v_win, state_conv, page_table, c_prompt, c_sample, norm1_g, norm2_g, w_ada, b_ada, w_in, w_cmp_k, w_cmp_v, w_dw, b_dw, ln_conv_g, ln_conv_b, g_out_attn, g_out_conv, w_out, router_w, router_b, w_exp_gate, w_exp_up, w_exp_down, w_sh_gate, w_sh_up, w_sh_down, norm_f_g):
    assert w_ada.shape[0] == 1, "single layer"
    bp, t, d = x_prompt.shape
    bs, s_new, _ = x_sample.shape
    assert s_new == 1
    n_pool, page = cache_k_cmp.shape[1], cache_k_cmp.shape[2]
    n_pages = page_table.shape[1]
    past = n_pages * page
    conv_width = state_conv.shape[-1]
    n_exp = router_w.shape[-1]
    chunks = d // LANES
    tm = min(ROW_TILE, t)

    w_in_bf = _padded_in_weight(w_in[0], conv_width)
    wck = _compress_weight(w_cmp_k[0])
    wcv = _compress_weight(w_cmp_v[0])
    wout_bf = w_out[0].astype(BF16)
    rw_t = router_w[0].T
    rw_hi = rw_t.astype(BF16)
    rw_lo = (rw_t - rw_hi.astype(F32)).astype(BF16)
    wsg_bf, wsu_bf, wsd_bf = (w[0].astype(BF16) for w in (w_sh_gate, w_sh_up, w_sh_down))
    gf = norm_f_g[None, :]

    n_c = bp + bs
    c_all = jnp.concatenate([c_prompt, c_sample], axis=0)
    c_all = jnp.pad(c_all, ((0, (-n_c) % SUBLANES), (0, 0)))
    mod = _modulation(c_all, w_ada[0], b_ada)
    mp = _split_mod(mod[:bp], per_token=False)
    ms = _split_mod(mod[bp:n_c], per_token=True)

    (q_p, kc_p, vc_p, ks_p, vs_p, kw_p, vw_p, gate_p, u_p) = _in_proj(
        x_prompt, mp[0], mp[1], norm1_g, w_in_bf, tm)
    nc_p = t // CMP_BLOCK
    kcc, vcc = _compress(_cmp_rows(kc_p).reshape(bp * nc_p, -1), _cmp_rows(vc_p).reshape(bp * nc_p, -1),
                         wck, wcv)
    o_attn_p = _prompt_attention(q_p, gate_p, kcc.reshape(bp, nc_p, KV_COLS), vcc.reshape(bp, nc_p, KV_COLS),
                                 ks_p, vs_p, kw_p, vw_p)
    o_conv_p = _conv_prompt(u_p, w_dw[0], b_dw, ln_conv_g, ln_conv_b, tm)

    xs_row = x_sample.reshape(1, bs, d)
    (q_s, kc_s, vc_s, ks_s, vs_s, kw_s, vw_s, gate_s, u_s) = _in_proj(
        xs_row, ms[0], ms[1], norm1_g, w_in_bf, bs)
    q8 = q_s.reshape(bs, N_HEADS, HEAD_DIM)
    gates8 = gate_s[0, :, :3 * N_HEADS].reshape(bs, N_HEADS, 3)
    pages_t = lambda c: jnp.transpose(c[0], (0, 2, 3, 1))
    tail = (-(past + s_new)) % SLC_BLOCK
    n_new = (s_new + tail) // CMP_BLOCK
    tail_rows = lambda x: _cmp_rows(jnp.pad(x[0][:, None, :], ((0, 0), (0, tail), (0, 0)))).reshape(bs * n_new, -1)
    pad_rows = (-(bs * n_new)) % SUBLANES
    kc_new, vc_new = _compress(jnp.pad(tail_rows(kc_s), ((0, pad_rows), (0, 0))),
                               jnp.pad(tail_rows(vc_s), ((0, pad_rows), (0, 0))), wck, wcv)
    kc_new = kc_new[:bs * n_new].reshape(bs, n_new, KV_COLS)
    vc_new = vc_new[:bs * n_new].reshape(bs, n_new, KV_COLS)
    reps = page // CMP_BLOCK
    wk_fold = jnp.transpose(w_cmp_k[0], (2, 1, 0)).reshape(HEAD_DIM, HEAD_DIM * CMP_BLOCK)
    ut = _matmul3(q8.reshape(bs * N_HEADS, HEAD_DIM) * ATTN_SCALE, wk_fold)
    ut = jnp.tile(ut.reshape(bs, N_HEADS, HEAD_DIM, CMP_BLOCK), (1, 1, 1, reps))
    s_raw = _sample_scores(page_table, ut, pages_t(cache_k_cmp))
    p_exp, p_new, sel = _sample_select(s_raw, q8, kc_new, past)
    y_acc = _sample_values(page_table, p_exp, pages_t(cache_v_cmp))
    wv_fold = jnp.tile(jnp.transpose(w_cmp_v[0], (1, 0, 2)), (1, reps, 1)).reshape(HEAD_DIM * page, HEAD_DIM)
    o_cmp_s = _matmul3(y_acc.reshape(bs * N_HEADS, HEAD_DIM * page), wv_fold).reshape(bs, N_HEADS, HEAD_DIM)
    n_sel = min(N_SEL, (past // CMP_BLOCK + n_new) // CMP_PER_SLC)
    sel = sel[:, :N_KV, :n_sel]
    row3 = lambda x: x[0][:, None, :]
    o_attn_s = _sample_attend(
        sel, page_table, q8, pages_t(cache_k_slc), pages_t(cache_v_slc), row3(ks_s), row3(vs_s),
        pages_t(state_k_win), pages_t(state_v_win), row3(kw_s), row3(vw_s), gates8, o_cmp_s, p_new, vc_new,
        past)
    o_attn_s = o_attn_s.reshape(1, bs, ATTN_WIDTH)
    up_s = jnp.concatenate([state_conv[0], u_s[0][:, None, :]], axis=1)
    o_conv_s = _conv_sample(up_s, w_dw[0], b_dw, ln_conv_g, ln_conv_b)[None]

    router = functools.partial(_merge_router, goa=g_out_attn, goc=g_out_conv, wout_bf=wout_bf, g2=norm2_g,
                               rw_hi=rw_hi, rw_lo=rw_lo, rb=router_b[0][:, None])
    x1_p, h2_p, e_p, w_p, r_p, cnt = router(o_attn_p, o_conv_p, x_prompt, mp[2], mp[3], mp[4],
                                            cnt_in=jnp.zeros((n_exp, 1), F32), tm=tm)
    x1_s, h2_s, e_s, w_s, r_s, cnt = router(o_attn_s, o_conv_s, xs_row, ms[2], ms[3], ms[4],
                                            cnt_in=cnt, tm=bs)

    n_tok = bp * t + bs
    counts = cnt[:, 0].astype(I32)
    padded = (counts + MOE_ROWS - 1) // MOE_ROWS * MOE_ROWS
    pad_end = jnp.cumsum(padded)
    pad_end = pad_end.astype(I32)
    pad_start = pad_end - padded
    n_blocks = -(-(n_tok * TOP_K) // MOE_ROWS) + n_exp
    blk_first = jnp.arange(n_blocks, dtype=I32) * MOE_ROWS
    blk_expert = jnp.minimum(jnp.sum(pad_end[None, :] <= blk_first[:, None], axis=1), n_exp - 1).astype(I32)
    n_active = pad_end[-1:] // MOE_ROWS
    picks = lambda a: jnp.transpose(a, (0, 2, 1))
    w_p, w_s = picks(w_p), picks(w_s)
    e_all = jnp.concatenate([picks(e_p).reshape(-1, TOP_K), picks(e_s).reshape(-1, TOP_K)], axis=0)
    r_all = jnp.concatenate([picks(r_p).reshape(-1, TOP_K), picks(r_s).reshape(-1, TOP_K)], axis=0)

    tile = _largest_tile(n_tok, 512)
    h_rows = jnp.concatenate([h2_p.reshape(-1, d), h2_s.reshape(-1, d)], axis=0).reshape(n_tok * chunks, LANES)
    xs = _dispatch(counts, pad_end, pad_start, e_all.reshape(n_tok // tile, 1, tile * TOP_K),
                   r_all.reshape(n_tok // tile, 1, tile * TOP_K), h_rows, tile, MOE_ROWS, chunks, n_blocks)
    ys = _experts(blk_expert, n_active, pad_end, xs, w_exp_gate[0], w_exp_up[0], w_exp_down[0], MOE_ROWS, chunks)

    comb = functools.partial(_combine, pad_start, wsg_bf=wsg_bf, wsu_bf=wsu_bf, wsd_bf=wsd_bf, gf=gf, ys=ys,
                             chunks=chunks)
    tiles_p = (bp * (t // tm), 1, tm * TOP_K)
    y_prompt = comb(picks(e_p).reshape(tiles_p), picks(r_p).reshape(tiles_p), w_p, x1_p, h2_p, mp[5], tm=tm)
    y_sample = comb(picks(e_s).reshape(1, 1, bs * TOP_K), picks(r_s).reshape(1, 1, bs * TOP_K), w_s, x1_s, h2_s,
                    ms[5], tm=bs).reshape(bs, 1, d)

    win = min(WINDOW, t)
    hist = state_conv.shape[2]
    out_p = [_kv5(a) for a in (kc_p, vc_p, ks_p, vs_p, kw_p[:, t - win:], vw_p[:, t - win:])]
    conv_p = u_p[:, t - hist:][None]
    out_s = [_kv5(a[0][:, None, :]) for a in (kc_s, vc_s, ks_s, vs_s)]
    w_buf = state_k_win.shape[2]
    kw_buf = jnp.concatenate([state_k_win, _kv5(kw_s[0][:, None, :])], axis=2)[:, :, -w_buf:]
    vw_buf = jnp.concatenate([state_v_win, _kv5(vw_s[0][:, None, :])], axis=2)[:, :, -w_buf:]
    conv_s = up_s[:, -hist:][None]
    return (y_prompt, y_sample, *out_p, conv_p, *out_s, kw_buf, vw_buf, conv_s)
```
